```python
import math
import jax, jax.numpy as jnp
from jax import lax
import numpy as np

D_MODEL = 1024
BATCH = 2
SEQ = 8192
DEPTH = 2
DEC_BATCH = 32
DEC_SEQ = 4
PAST_LEN = 16384
PAGE_SIZE = 128

N_MIXERS = 2
N_ATT_LAYERS = (DEPTH + 1) // 2
N_LIN_LAYERS = DEPTH // 2
RMS_EPS = 1e-6
D_FF = -(-8 * D_MODEL // (3 * 256)) * 256

NSA_HEADS = 16
NSA_HEAD_DIM = 64
NSA_KV_GROUPS = 4
NSA_HPG = NSA_HEADS // NSA_KV_GROUPS
CMP_BLOCK = 32
CMP_STRIDE = 16
CMP_HID = 256
SLC_BLOCK = 64
N_SELECT = 16
WINDOW = 512
Q_BLOCK = 128
NSA_Q_W = NSA_HEADS * NSA_HEAD_DIM
NSA_KV_W = 2 * NSA_KV_GROUPS * NSA_HEAD_DIM
NSA_IN_W = NSA_Q_W + 3 * NSA_KV_W + 3 * NSA_HEADS

N_BUCKETS = 32
BUCKET_MAX_EXACT = 16
BUCKET_MAX_DIST = 128

GDN_QK_HEADS = 8
GDN_V_HEADS = 16
GDN_HEAD_DIM = 128
CONV_W = 4
GDN_CHUNK = 64
GDN_QK_W = GDN_QK_HEADS * GDN_HEAD_DIM
GDN_V_W = GDN_V_HEADS * GDN_HEAD_DIM
C_CONV = 2 * GDN_QK_W + GDN_V_W
GDN_IN_W = C_CONV + GDN_V_W + 2 * GDN_V_HEADS

kernel_name = "nsa_gdn_hybrid_decode_step"


def _rms_norm(x, gain):
    xf = x.astype(jnp.float32)
    y = xf * lax.rsqrt(jnp.mean(xf * xf, axis=-1, keepdims=True) + RMS_EPS)
    return (y * gain.astype(jnp.float32)).astype(x.dtype)


def _modulate(x, gain, shift, scale):
    return _rms_norm(x, gain) * (1.0 + scale) + shift


def _adaln(c, w, b):
    m = (jax.nn.silu(c) @ w + b).reshape(c.shape[0], 6, 1, D_MODEL)
    return [m[:, k] for k in range(6)]


def _swiglu(h, w_in, w_out):
    gu = h @ w_in
    return (jax.nn.silu(gu[..., :D_FF]) * gu[..., D_FF:]) @ w_out


def _masked_softmax(logits, mask):
    l = jnp.where(mask, logits, -1e30)
    m = jnp.max(l, axis=-1, keepdims=True)
    e = jnp.where(mask, jnp.exp(l - m), 0.0)
    s = jnp.sum(e, axis=-1, keepdims=True)
    return e / jnp.where(s > 0, s, 1.0)


def _t5_bucket(dist):
    n = jnp.maximum(dist, 0)
    nf = jnp.maximum(n, 1).astype(jnp.float32)
    large = BUCKET_MAX_EXACT + (jnp.log(nf / BUCKET_MAX_EXACT) / math.log(BUCKET_MAX_DIST / BUCKET_MAX_EXACT)
                                * (N_BUCKETS - BUCKET_MAX_EXACT)).astype(jnp.int32)
    large = jnp.minimum(large, N_BUCKETS - 1)
    return jnp.where(n < BUCKET_MAX_EXACT, n, large)


def _grouped_bias(rel_bias, dist):
    q, k = dist.shape
    b = rel_bias.astype(jnp.float32)[_t5_bucket(dist)].reshape(q, k, NSA_KV_GROUPS, NSA_HPG)
    return jnp.transpose(b, (2, 3, 0, 1))


def _nsa_project(h, w_in):
    b, t = h.shape[:2]
    p = h @ w_in
    q = p[..., :NSA_Q_W].reshape(b, t, NSA_HEADS, NSA_HEAD_DIM) * (NSA_HEAD_DIM ** -0.5)
    kv = p[..., NSA_Q_W:NSA_Q_W + 3 * NSA_KV_W].reshape(b, t, 3, 2, NSA_KV_GROUPS, NSA_HEAD_DIM)
    gates = jax.nn.sigmoid(p[..., NSA_Q_W + 3 * NSA_KV_W:].reshape(b, t, NSA_HEADS, 3))
    return q, kv[:, :, 0], kv[:, :, 1], kv[:, :, 2], gates


def _compress_kv(kv_rows, pe, w1, b1, w2):
    b, t = kv_rows.shape[:2]
    n_cmp = (t - CMP_BLOCK) // CMP_STRIDE + 1
    segs = kv_rows[:, :(n_cmp + 1) * CMP_STRIDE].reshape(b, n_cmp + 1, CMP_STRIDE, 2, NSA_KV_GROUPS, NSA_HEAD_DIM)
    pa = jnp.einsum('bnskgd,ksdh->bnkgh', segs, w1[:, :CMP_STRIDE])
    pb = jnp.einsum('bnskgd,ksdh->bnkgh', segs, w1[:, CMP_STRIDE:])
    pe_term = jnp.einsum('ksd,ksdh->kh', pe, w1) + b1
    hid = jax.nn.gelu(pa[:, :-1] + pb[:, 1:] + pe_term[:, None, :])
    return jnp.einsum('bnkgh,khd->bnkgd', hid, w2)


def _nsa_attend(q, gates, q_pos, kv_cmp_c, fetch, n_slc_blocks, kv_w, w_pos, rel_bias):
    b, nq = q.shape[:2]
    f32 = jnp.float32
    qg = q.reshape(b, nq, NSA_KV_GROUPS, NSA_HPG, NSA_HEAD_DIM)
    n_cmp = kv_cmp_c.shape[1]
    c_end = jnp.arange(n_cmp, dtype=jnp.int32) * CMP_STRIDE + CMP_BLOCK - 1
    dist_c = q_pos[:, None] - c_end[None, :]
    lc = jnp.einsum('bqghd,bngd->bghqn', qg, kv_cmp_c[:, :, 0], preferred_element_type=f32) + _grouped_bias(rel_bias, dist_c)
    pc = _masked_softmax(lc, dist_c >= 0)
    o_c = jnp.einsum('bghqn,bngd->bqghd', pc, kv_cmp_c[:, :, 1].astype(f32))
    j = jnp.arange(n_slc_blocks, dtype=jnp.int32)
    s_start = j * SLC_BLOCK
    c_start = c_end - CMP_BLOCK + 1
    overlap = ((c_start[:, None] < s_start[None, :] + SLC_BLOCK) & (c_end[:, None] >= s_start[None, :])).astype(f32)
    imp = jnp.einsum('bghqn,nj->bgqj', pc, overlap)
    q_blk = q_pos // SLC_BLOCK
    forced = (j[None, :] == 0) | (j[None, :] == q_blk[:, None]) | (j[None, :] == q_blk[:, None] - 1)
    valid = j[None, :] <= q_blk[:, None]
    score = jnp.where(forced, 1e4, jnp.where(valid, imp, -1e4))
    _, idx = lax.top_k(score, min(N_SELECT, n_slc_blocks))
    kv_sel = fetch(idx)
    key_pos = idx[..., None] * SLC_BLOCK + jnp.arange(SLC_BLOCK, dtype=jnp.int32)
    dist_s = q_pos[None, None, :, None, None] - key_pos
    tbl = rel_bias.astype(f32).reshape(N_BUCKETS, NSA_KV_GROUPS, NSA_HPG)
    g_i = jnp.arange(NSA_KV_GROUPS)[None, :, None, None, None]
    bias_s = jnp.moveaxis(tbl[_t5_bucket(dist_s), g_i], -1, 2)
    ls = jnp.einsum('bqghd,bgqnsd->bghqns', qg, kv_sel[..., 0, :], preferred_element_type=f32) + bias_s
    mask_s = (dist_s >= 0).reshape(b, NSA_KV_GROUPS, nq, -1)[:, :, None]
    ps = _masked_softmax(ls.reshape(b, NSA_KV_GROUPS, NSA_HPG, nq, -1), mask_s)
    v_sel = kv_sel[..., 1, :].reshape(b, NSA_KV_GROUPS, nq, -1, NSA_HEAD_DIM).astype(f32)
    o_s = jnp.einsum('bghqk,bgqkd->bqghd', ps, v_sel)
    dist_w = q_pos[:, None] - w_pos[None, :]
    lw = jnp.einsum('bqghd,bwgd->bghqw', qg, kv_w[:, :, 0], preferred_element_type=f32) + _grouped_bias(rel_bias, dist_w)
    mask_w = (dist_w >= 0) & (dist_w < WINDOW) & (w_pos[None, :] >= 0)
    pw = _masked_softmax(lw, mask_w)
    o_w = jnp.einsum('bghqw,bwgd->bqghd', pw, kv_w[:, :, 1].astype(f32))
    gt = gates.astype(f32).reshape(b, nq, NSA_KV_GROUPS, NSA_HPG, 3)
    o = gt[..., 0:1] * o_c + gt[..., 1:2] * o_s + gt[..., 2:3] * o_w
    return o.reshape(b, nq, NSA_Q_W).astype(q.dtype)


def _nsa_prompt(h, w_in, cmp_w, w_out, rel_bias):
    b, t = h.shape[:2]
    q, kv_c, kv_s, kv_w, gates = _nsa_project(h, w_in)
    kvc = _compress_kv(kv_c, *cmp_w)
    n_sb = t // SLC_BLOCK
    slc_blocks = kv_s.reshape(b, n_sb, SLC_BLOCK, 2, NSA_KV_GROUPS, NSA_HEAD_DIM)
    b_i = jnp.arange(b)[:, None, None, None]
    g_i = jnp.arange(NSA_KV_GROUPS)[None, :, None, None]

    def fetch(idx):
        return slc_blocks[b_i, idx, :, :, g_i]

    kvw_pad = jnp.pad(kv_w, ((0, 0), (WINDOW, 0), (0, 0), (0, 0), (0, 0)))
    nqb = t // Q_BLOCK
    q_blocks = jnp.moveaxis(q.reshape(b, nqb, Q_BLOCK, NSA_HEADS, NSA_HEAD_DIM), 1, 0)
    g_blocks = jnp.moveaxis(gates.reshape(b, nqb, Q_BLOCK, NSA_HEADS, 3), 1, 0)
    starts = jnp.arange(nqb, dtype=jnp.int32) * Q_BLOCK

    def body(args):
        qb, gb, s0 = args
        q_pos = s0 + jnp.arange(Q_BLOCK, dtype=jnp.int32)
        kvw = lax.dynamic_slice_in_dim(kvw_pad, s0, Q_BLOCK + WINDOW, axis=1)
        w_pos = s0 - WINDOW + jnp.arange(Q_BLOCK + WINDOW, dtype=jnp.int32)
        return _nsa_attend(qb, gb, q_pos, kvc, fetch, n_sb, kvw, w_pos, rel_bias)

    o = lax.map(body, (q_blocks, g_blocks, starts))
    o = jnp.moveaxis(o, 0, 1).reshape(b, t, NSA_Q_W)
    return o @ w_out, kv_c, kv_s, kv_w[:, -min(WINDOW, t):]


def _nsa_sample(h, l, cache_kv_cmp, cache_kv_slc, cache_kv_win, page_table, w_in, cmp_w, w_out, rel_bias):
    db, nq = h.shape[:2]
    n_pages = page_table.shape[1]
    past_len = n_pages * PAGE_SIZE
    q, kv_c, kv_s, kv_w, gates = _nsa_project(h, w_in)
    past_c = cache_kv_cmp[l, page_table].reshape(db, past_len, 2, NSA_KV_GROUPS, NSA_HEAD_DIM)
    kvc = _compress_kv(jnp.concatenate([past_c, kv_c.astype(past_c.dtype)], axis=1), *cmp_w)
    n_past_blk = past_len // SLC_BLOCK
    nb_new = -(-nq // SLC_BLOCK)
    n_sb = n_past_blk + nb_new
    bpp = PAGE_SIZE // SLC_BLOCK
    new_blocks = jnp.pad(kv_s, ((0, 0), (0, nb_new * SLC_BLOCK - nq), (0, 0), (0, 0), (0, 0))
                         ).reshape(db, nb_new, SLC_BLOCK, 2, NSA_KV_GROUPS, NSA_HEAD_DIM)
    b_i = jnp.arange(db)[:, None, None, None]
    g_i = jnp.arange(NSA_KV_GROUPS)[None, :, None, None]

    def fetch(idx):
        phys = page_table[b_i, jnp.clip(idx // bpp, 0, n_pages - 1)]
        rows = (idx % bpp)[..., None] * SLC_BLOCK + jnp.arange(SLC_BLOCK, dtype=jnp.int32)
        past = cache_kv_slc[l, phys[..., None], rows, :, g_i[..., None]]
        new = new_blocks[b_i, jnp.clip(idx - n_past_blk, 0, nb_new - 1), :, :, g_i]
        return jnp.where((idx >= n_past_blk)[..., None, None, None], new.astype(past.dtype), past)

    w_buf = cache_kv_win.shape[2]
    kv_w_all = jnp.concatenate([cache_kv_win[l], kv_w.astype(cache_kv_win.dtype)], axis=1)
    w_pos = past_len - w_buf + jnp.arange(w_buf + nq, dtype=jnp.int32)
    q_pos = past_len + jnp.arange(nq, dtype=jnp.int32)
    o = _nsa_attend(q, gates, q_pos, kvc, fetch, n_sb, kv_w_all, w_pos, rel_bias)
    return o @ w_out, kv_c, kv_s, kv_w_all[:, -w_buf:]


def _l2norm(x):
    xf = x.astype(jnp.float32)
    return xf * lax.rsqrt(jnp.sum(xf * xf, axis=-1, keepdims=True) + 1e-6)


def _gated_delta_chunked(q, k, v, beta, g, s0):
    f32 = jnp.float32
    b, t, nh, dk = q.shape
    dv = v.shape[-1]
    c = GDN_CHUNK
    tp = -(-t // c) * c
    n = tp // c

    def blk(x):
        x = jnp.pad(x.astype(f32), ((0, 0), (0, tp - t)) + ((0, 0),) * (x.ndim - 2))
        x = x.reshape((b, n, c) + x.shape[2:])
        return jnp.moveaxis(x, 3, 1)

    q, k, v, beta, g = blk(q), blk(k), blk(v), blk(beta), blk(g)
    gc = jnp.cumsum(g, axis=-1)
    kb = k * beta[..., None]
    vb = v * beta[..., None]
    tri_incl = jnp.tril(jnp.ones((c, c), dtype=bool))
    tri_strict = jnp.tril(jnp.ones((c, c), dtype=bool), -1)
    diff = gc[..., :, None] - gc[..., None, :]
    decay = jnp.where(tri_incl, jnp.exp(jnp.where(tri_incl, diff, 0.0)), 0.0)
    lmat = jnp.where(tri_strict, jnp.einsum('bhnid,bhnjd->bhnij', kb, k) * decay, 0.0)
    eye = jnp.eye(c, dtype=f32)
    tinv = lax.linalg.triangular_solve(eye + lmat, jnp.broadcast_to(eye, lmat.shape), left_side=True, lower=True)
    u = jnp.einsum('bhnij,bhnjd->bhnid', tinv, vb)
    w = jnp.einsum('bhnij,bhnjd->bhnid', tinv, kb * jnp.exp(gc)[..., None])
    a_in = jnp.einsum('bhnid,bhnjd->bhnij', q, k) * decay
    q_e = q * jnp.exp(gc)[..., None]
    k_d = k * jnp.exp(gc[..., -1:] - gc)[..., None]
    d_last = jnp.exp(gc[..., -1])
    xs = tuple(jnp.moveaxis(x, 2, 0) for x in (q_e, k_d, u, w, a_in, d_last))

    def step(s, xn):
        qe_n, kd_n, u_n, w_n, a_n, dl_n = xn
        v_new = u_n - jnp.einsum('bhcd,bhde->bhce', w_n, s)
        o = jnp.einsum('bhcd,bhde->bhce', qe_n, s) + jnp.einsum('bhij,bhje->bhie', a_n, v_new)
        s = s * dl_n[..., None, None] + jnp.einsum('bhcd,bhce->bhde', kd_n, v_new)
        return s, o

    s_fin, o = lax.scan(step, s0.astype(f32), xs)
    o = jnp.transpose(o, (1, 0, 3, 2, 4)).reshape(b, tp, nh, dv)[:, :t]
    return o, s_fin.astype(s0.dtype)


def _gdn(h, conv_buf, s0, w_in, conv_w, a_log, dt_bias, norm_w, w_out):
    b, t = h.shape[:2]
    p = h @ w_in
    qkv = p[..., :C_CONV]
    z = p[..., C_CONV:C_CONV + GDN_V_W]
    bb = p[..., C_CONV + GDN_V_W:C_CONV + GDN_V_W + GDN_V_HEADS]
    aa = p[..., C_CONV + GDN_V_W + GDN_V_HEADS:]
    xp = jnp.concatenate([conv_buf.astype(qkv.dtype), qkv], axis=1)
    conv = xp[:, 0:t] * conv_w[0]
    for j in range(1, CONV_W):
        conv = conv + xp[:, j:j + t] * conv_w[j]
    act = jax.nn.silu(conv)
    rep = GDN_V_HEADS // GDN_QK_HEADS
    q = _l2norm(act[..., :GDN_QK_W].reshape(b, t, GDN_QK_HEADS, GDN_HEAD_DIM)) * (GDN_HEAD_DIM ** -0.5)
    k = _l2norm(act[..., GDN_QK_W:2 * GDN_QK_W].reshape(b, t, GDN_QK_HEADS, GDN_HEAD_DIM))
    q = jnp.repeat(q, rep, axis=2)
    k = jnp.repeat(k, rep, axis=2)
    v = act[..., 2 * GDN_QK_W:].reshape(b, t, GDN_V_HEADS, GDN_HEAD_DIM)
    beta = jax.nn.sigmoid(bb.astype(jnp.float32))
    g = -jnp.exp(a_log.astype(jnp.float32)) * jax.nn.softplus(aa.astype(jnp.float32) + dt_bias.astype(jnp.float32))
    o, s_fin = _gated_delta_chunked(q, k, v, beta, g, s0)
    o = _rms_norm(o.astype(h.dtype), norm_w) * jax.nn.silu(z.reshape(b, t, GDN_V_HEADS, GDN_HEAD_DIM))
    return o.reshape(b, t, GDN_V_W) @ w_out, xp[:, -(CONV_W - 1):], s_fin


def setup_inputs(seed: int = 0) -> dict:
    key = jax.random.key(seed)
    ks = jax.random.split(key, 32)
    nrm = jax.random.normal
    n_pages = PAST_LEN // PAGE_SIZE
    n_used = DEC_BATCH * n_pages
    n_pool = n_used + n_used // 4
    w_buf = min(WINDOW, PAST_LEN)
    g, hd = NSA_KV_GROUPS, NSA_HEAD_DIM
    dt = jnp.exp(jax.random.uniform(ks[25], (N_LIN_LAYERS, GDN_V_HEADS)) * (math.log(0.1) - math.log(0.001)) + math.log(0.001))
    return {
        "x_prompt": nrm(ks[0], (BATCH, SEQ, D_MODEL), jnp.float32),
        "x_sample": nrm(ks[1], (DEC_BATCH, DEC_SEQ, D_MODEL), jnp.float32),
        "c_prompt": nrm(ks[2], (BATCH, D_MODEL), jnp.float32),
        "c_sample": nrm(ks[3], (DEC_BATCH, D_MODEL), jnp.float32),
        "cache_kv_cmp": nrm(ks[4], (N_ATT_LAYERS, n_pool, PAGE_SIZE, 2, g, hd), jnp.float32),
        "cache_kv_slc": nrm(ks[5], (N_ATT_LAYERS, n_pool, PAGE_SIZE, 2, g, hd), jnp.float32),
        "cache_kv_win": nrm(ks[6], (N_ATT_LAYERS, DEC_BATCH, w_buf, 2, g, hd), jnp.float32),
        "state_conv": nrm(ks[7], (N_LIN_LAYERS, DEC_BATCH, CONV_W - 1, C_CONV), jnp.float32),
        "state_ssm": 0.5 * nrm(ks[8], (N_LIN_LAYERS, DEC_BATCH, GDN_V_HEADS, GDN_HEAD_DIM, GDN_HEAD_DIM), jnp.float32),
        "page_table": jax.random.permutation(ks[9], n_pool)[:n_used].reshape(DEC_BATCH, n_pages).astype(jnp.int32),
        "rel_bias": 0.5 * nrm(ks[10], (N_BUCKETS, NSA_HEADS), jnp.float32),
        "norm_gains": 1.0 + 0.05 * nrm(ks[11], (DEPTH, 4, D_MODEL), jnp.float32),
        "w_ada": nrm(ks[12], (DEPTH, D_MODEL, 6 * D_MODEL), jnp.float32) * D_MODEL ** -0.5,
        "b_ada": 0.02 * nrm(ks[13], (DEPTH, 6 * D_MODEL), jnp.float32),
        "w_ffn_in": nrm(ks[14], (DEPTH, D_MODEL, 2 * D_FF), jnp.float32) * D_MODEL ** -0.5,
        "w_ffn_out": nrm(ks[15], (DEPTH, D_FF, D_MODEL), jnp.float32) * D_FF ** -0.5,
        "nsa_w_in": nrm(ks[16], (N_ATT_LAYERS, D_MODEL, NSA_IN_W), jnp.float32) * D_MODEL ** -0.5,
        "nsa_cmp_pe": 0.1 * nrm(ks[17], (N_ATT_LAYERS, 2, CMP_BLOCK, hd), jnp.float32),
        "nsa_cmp_w1": nrm(ks[18], (N_ATT_LAYERS, 2, CMP_BLOCK, hd, CMP_HID), jnp.float32) * (CMP_BLOCK * hd) ** -0.5,
        "nsa_cmp_b1": 0.02 * nrm(ks[19], (N_ATT_LAYERS, 2, CMP_HID), jnp.float32),
        "nsa_cmp_w2": nrm(ks[20], (N_ATT_LAYERS, 2, CMP_HID, hd), jnp.float32) * CMP_HID ** -0.5,
        "nsa_w_out": nrm(ks[21], (N_ATT_LAYERS, NSA_Q_W, D_MODEL), jnp.float32) * NSA_Q_W ** -0.5,
        "gdn_w_in": nrm(ks[22], (N_LIN_LAYERS, D_MODEL, GDN_IN_W), jnp.float32) * D_MODEL ** -0.5,
        "gdn_conv_w": nrm(ks[23], (N_LIN_LAYERS, CONV_W, C_CONV), jnp.float32) * CONV_W ** -0.5,
        "gdn_a_log": jnp.log(jax.random.uniform(ks[24], (N_LIN_LAYERS, GDN_V_HEADS), minval=1.0, maxval=16.0)),
        "gdn_dt_bias": dt + jnp.log(-jnp.expm1(-dt)),
        "gdn_norm_w": 1.0 + 0.05 * nrm(ks[26], (N_LIN_LAYERS, GDN_HEAD_DIM), jnp.float32),
        "gdn_w_out": nrm(ks[27], (N_LIN_LAYERS, GDN_V_W, D_MODEL), jnp.float32) * GDN_V_W ** -0.5,
    }


def reference(x_prompt, x_sample, c_prompt, c_sample, cache_kv_cmp, cache_kv_slc, cache_kv_win, state_conv, state_ssm,
              page_table, rel_bias, norm_gains, w_ada, b_ada, w_ffn_in, w_ffn_out, nsa_w_in, nsa_cmp_pe, nsa_cmp_w1,
              nsa_cmp_b1, nsa_cmp_w2, nsa_w_out, gdn_w_in, gdn_conv_w, gdn_a_log, gdn_dt_bias, gdn_norm_w, gdn_w_out):
    xp, xs = x_prompt, x_sample
    kvc_p, kvc_s, kvs_p, kvs_s, kvw_p, kvw_s = [], [], [], [], [], []
    cv_p, cv_s, ss_p, ss_s = [], [], [], []
    for i in range(DEPTH):
        mp = _adaln(c_prompt, w_ada[i], b_ada[i])
        ms = _adaln(c_sample, w_ada[i], b_ada[i])
        hp = _modulate(xp, norm_gains[i, 0], mp[0], mp[1])
        hs = _modulate(xs, norm_gains[i, 0], ms[0], ms[1])
        l = i // N_MIXERS
        if i % N_MIXERS == 0:
            cmp_w = (nsa_cmp_pe[l], nsa_cmp_w1[l], nsa_cmp_b1[l], nsa_cmp_w2[l])
            yp, a, bq, cq = _nsa_prompt(hp, nsa_w_in[l], cmp_w, nsa_w_out[l], rel_bias)
            kvc_p.append(a); kvs_p.append(bq); kvw_p.append(cq)
            ys, a, bq, cq = _nsa_sample(hs, l, cache_kv_cmp, cache_kv_slc, cache_kv_win, page_table,
                                        nsa_w_in[l], cmp_w, nsa_w_out[l], rel_bias)
            kvc_s.append(a); kvs_s.append(bq); kvw_s.append(cq)
        else:
            gdn_w = (gdn_w_in[l], gdn_conv_w[l], gdn_a_log[l], gdn_dt_bias[l], gdn_norm_w[l], gdn_w_out[l])
            b = xp.shape[0]
            yp, a, bq = _gdn(hp, jnp.zeros((b, CONV_W - 1, C_CONV), xp.dtype),
                             jnp.zeros((b, GDN_V_HEADS, GDN_HEAD_DIM, GDN_HEAD_DIM), jnp.float32), *gdn_w)
            cv_p.append(a); ss_p.append(bq)
            ys, a, bq = _gdn(hs, state_conv[l], state_ssm[l], *gdn_w)
            cv_s.append(a); ss_s.append(bq)
        xp = xp + mp[2] * _rms_norm(yp, norm_gains[i, 1])
        xs = xs + ms[2] * _rms_norm(ys, norm_gains[i, 1])
        hp = _modulate(xp, norm_gains[i, 2], mp[3], mp[4])
        hs = _modulate(xs, norm_gains[i, 2], ms[3], ms[4])
        xp = xp + mp[5] * _rms_norm(_swiglu(hp, w_ffn_in[i], w_ffn_out[i]), norm_gains[i, 3])
        xs = xs + ms[5] * _rms_norm(_swiglu(hs, w_ffn_in[i], w_ffn_out[i]), norm_gains[i, 3])
    return (xp, xs, jnp.stack(kvc_p), jnp.stack(kvc_s), jnp.stack(kvs_p), jnp.stack(kvs_s),
            jnp.stack(kvw_p), jnp.stack(kvw_s), jnp.stack(cv_p), jnp.stack(cv_s), jnp.stack(ss_p), jnp.stack(ss_s))
```

```python
import functools
import math

import numpy as np
import jax
import jax.numpy as jnp
from jax import lax
from jax.experimental import pallas as pl
from jax.experimental.pallas import tpu as pltpu

F32 = jnp.float32
BF16 = jnp.bfloat16

D_MODEL = 1024
RMS_EPS = 1e-6
D_FF = 2816
NSA_HEADS = 16
NSA_HEAD_DIM = 64
NSA_KV_GROUPS = 4
NSA_HPG = 4
CMP_BLOCK = 32
CMP_STRIDE = 16
CMP_HID = 256
SLC_BLOCK = 64
N_SELECT = 16
WINDOW = 512
Q_BLOCK = 128
PAGE_SIZE = 128
N_BUCKETS = 32
GDN_QK_HEADS = 8
GDN_V_HEADS = 16
GDN_HEAD_DIM = 128
CONV_W = 4
GDN_CHUNK = 64
NSA_Q_W = 1024
NSA_KV_W = 512
C_CONV = 4096
GDN_V_W = 2048

LANES = 128
SEG_W = CMP_STRIDE * NSA_KV_W
SEGS_PER_PAGE = PAGE_SIZE // CMP_STRIDE
NEG = -1e30
VMEM_LIMIT = 48 * 1024 * 1024

_BUCKET_THR = (19, 21, 24, 27, 31, 35, 40, 46, 52, 59, 67, 77, 87, 99, 113)
FAR_DIST = 128


def _nt(a, b):
    return lax.dot_general(a, b, (((1,), (1,)), ((), ())), preferred_element_type=F32)


def _split2(x):
    hi = x.astype(BF16)
    lo = (x - hi.astype(F32)).astype(BF16)
    return hi, lo


def _split3(x):
    hi = x.astype(BF16)
    r = x - hi.astype(F32)
    mid = r.astype(BF16)
    lo = (r - mid.astype(F32)).astype(BF16)
    return hi, mid, lo


def _pick_tn(n):
    units = n // LANES
    best = 1
    for d in range(1, units + 1):
        if units % d == 0 and d * LANES <= 1536:
            best = d
    return best * LANES


def _adaln_kernel(c_ref, w_ref, b_ref, o_ref):
    c = c_ref[...]
    a = (c * jax.nn.sigmoid(c)).astype(BF16)
    o_ref[...] = jnp.dot(a, w_ref[...].astype(BF16), preferred_element_type=F32) + b_ref[...]


def _adaln(c_all, w_ada, b_ada):
    depth, d, n = w_ada.shape
    rows = c_all.shape[0]
    tn = 768
    return pl.pallas_call(
        _adaln_kernel,
        grid=(depth, n // tn),
        in_specs=[pl.BlockSpec((rows, d), lambda l, j: (0, 0)),
                  pl.BlockSpec((None, d, tn), lambda l, j: (l, 0, j)),
                  pl.BlockSpec((None, 1, tn), lambda l, j: (l, 0, j))],
        out_specs=pl.BlockSpec((None, rows, tn), lambda l, j: (l, 0, j)),
        out_shape=jax.ShapeDtypeStruct((depth, rows, n), F32),
        compiler_params=pltpu.CompilerParams(dimension_semantics=("parallel", "parallel")),
        name="adaln",
    )(c_all, w_ada, b_ada.reshape(depth, 1, n))


def _mod_norm(x, gain, shift, scale):
    ms = jnp.mean(x * x, axis=-1, keepdims=True)
    y = x * lax.rsqrt(ms + RMS_EPS) * gain
    return y * (1.0 + scale) + shift


def _nml_kernel(x_ref, g_ref, sh_ref, sc_ref, w_ref, o_ref, h_ref):
    @pl.when(pl.program_id(2) == 0)
    def _():
        h_ref[...] = _mod_norm(x_ref[...], g_ref[...], sh_ref[...], sc_ref[...]).astype(BF16)

    o_ref[...] = jnp.dot(h_ref[...], w_ref[...], preferred_element_type=F32).astype(o_ref.dtype)


def _nml_swiglu_kernel(x_ref, g_ref, sh_ref, sc_ref, wg_ref, wu_ref, o_ref, h_ref):
    @pl.when(pl.program_id(2) == 0)
    def _():
        h_ref[...] = _mod_norm(x_ref[...], g_ref[...], sh_ref[...], sc_ref[...]).astype(BF16)

    h = h_ref[...]
    gate = jnp.dot(h, wg_ref[...], preferred_element_type=F32)
    up = jnp.dot(h, wu_ref[...], preferred_element_type=F32)
    o_ref[...] = (gate * jax.nn.sigmoid(gate) * up).astype(o_ref.dtype)


def _mod_spec(mod, tm):
    if mod.shape[1] == 1:
        return pl.BlockSpec((None, 1, mod.shape[2]), lambda b, i, *_: (b, 0, 0))
    return pl.BlockSpec((None, tm, mod.shape[2]), lambda b, i, *_: (b, i, 0))


def _row_tile(t):
    return 512 if t % 512 == 0 else t


def _norm_mod_linear(x, gain, shift, scale, w, out_dtype=F32):
    b, t, d = x.shape
    n = w.shape[1]
    tm, tn = _row_tile(t), _pick_tn(n)
    return pl.pallas_call(
        _nml_kernel,
        grid=(b, t // tm, n // tn),
        in_specs=[pl.BlockSpec((None, tm, d), lambda b, i, j: (b, i, 0)),
                  pl.BlockSpec((1, d), lambda b, i, j: (0, 0)),
                  _mod_spec(shift, tm), _mod_spec(scale, tm),
                  pl.BlockSpec((d, tn), lambda b, i, j: (0, j))],
        out_specs=pl.BlockSpec((None, tm, tn), lambda b, i, j: (b, i, j)),
        out_shape=jax.ShapeDtypeStruct((b, t, n), out_dtype),
        scratch_shapes=[pltpu.VMEM((tm, d), BF16)],
        compiler_params=pltpu.CompilerParams(
            dimension_semantics=("parallel", "parallel", "arbitrary"), vmem_limit_bytes=VMEM_LIMIT),
        name="norm_mod_linear",
    )(x, gain.reshape(1, d), shift, scale, w)


def _norm_mod_swiglu(x, gain, shift, scale, w_in):
    b, t, d = x.shape
    nf = w_in.shape[1] // 2
    tm, tn = _row_tile(t), _pick_tn(nf)
    nj = nf // tn
    return pl.pallas_call(
        _nml_swiglu_kernel,
        grid=(b, t // tm, nj),
        in_specs=[pl.BlockSpec((None, tm, d), lambda b, i, j: (b, i, 0)),
                  pl.BlockSpec((1, d), lambda b, i, j: (0, 0)),
                  _mod_spec(shift, tm), _mod_spec(scale, tm),
                  pl.BlockSpec((d, tn), lambda b, i, j: (0, j)),
                  pl.BlockSpec((d, tn), lambda b, i, j: (0, j + nj))],
        out_specs=pl.BlockSpec((None, tm, tn), lambda b, i, j: (b, i, j)),
        out_shape=jax.ShapeDtypeStruct((b, t, nf), BF16),
        scratch_shapes=[pltpu.VMEM((tm, d), BF16)],
        compiler_params=pltpu.CompilerParams(
            dimension_semantics=("parallel", "parallel", "arbitrary"), vmem_limit_bytes=VMEM_LIMIT),
        name="norm_mod_swiglu",
    )(x, gain.reshape(1, d), shift, scale, w_in, w_in)


def _rms_gated_residual(y, x, gate, gain):
    ms = jnp.mean(y * y, axis=-1, keepdims=True)
    return x + gate * (y * lax.rsqrt(ms + RMS_EPS) * gain)


def _mrr_kernel(a_ref, w_ref, x_ref, gate_ref, gain_ref, o_ref):
    y = jnp.dot(a_ref[...].astype(BF16), w_ref[...], preferred_element_type=F32)
    o_ref[...] = _rms_gated_residual(y, x_ref[...], gate_ref[...], gain_ref[...])


def _matmul_rms_residual(a, w, x, gate, gain):
    b, t, k = a.shape
    d = w.shape[1]
    tm = _row_tile(t)
    return pl.pallas_call(
        _mrr_kernel,
        grid=(b, t // tm),
        in_specs=[pl.BlockSpec((None, tm, k), lambda b, i: (b, i, 0)),
                  pl.BlockSpec((k, d), lambda b, i: (0, 0)),
                  pl.BlockSpec((None, tm, d), lambda b, i: (b, i, 0)),
                  _mod_spec(gate, tm),
                  pl.BlockSpec((1, d), lambda b, i: (0, 0))],
        out_specs=pl.BlockSpec((None, tm, d), lambda b, i: (b, i, 0)),
        out_shape=jax.ShapeDtypeStruct((b, t, d), F32),
        compiler_params=pltpu.CompilerParams(
            dimension_semantics=("parallel", "parallel"), vmem_limit_bytes=VMEM_LIMIT),
        name="matmul_rms_residual",
    )(a, w, x, gate, gain.reshape(1, d))


def _gdn_out_kernel(o_ref, z_ref, nw_ref, w_ref, x_ref, gate_ref, gain_ref, out_ref, a_ref):
    nw = nw_ref[...]
    for h in range(GDN_V_HEADS):
        sl = slice(h * GDN_HEAD_DIM, (h + 1) * GDN_HEAD_DIM)
        o = o_ref[:, sl]
        z = z_ref[:, sl]
        ms = jnp.mean(o * o, axis=-1, keepdims=True)
        a_ref[:, sl] = ((o * lax.rsqrt(ms + RMS_EPS) * nw) * (z * jax.nn.sigmoid(z))).astype(BF16)
    y = jnp.dot(a_ref[...], w_ref[...], preferred_element_type=F32)
    out_ref[...] = _rms_gated_residual(y, x_ref[...], gate_ref[...], gain_ref[...])


def _gdn_out(o, p, z_col_block, norm_w, w, x, gate, gain):
    b, t, k = o.shape
    d = w.shape[1]
    tm = _row_tile(t)
    return pl.pallas_call(
        _gdn_out_kernel,
        grid=(b, t // tm),
        in_specs=[pl.BlockSpec((None, tm, k), lambda b, i: (b, i, 0)),
                  pl.BlockSpec((None, tm, k), lambda b, i: (b, i, z_col_block)),
                  pl.BlockSpec((1, GDN_HEAD_DIM), lambda b, i: (0, 0)),
                  pl.BlockSpec((k, d), lambda b, i: (0, 0)),
                  pl.BlockSpec((None, tm, d), lambda b, i: (b, i, 0)),
                  _mod_spec(gate, tm),
                  pl.BlockSpec((1, d), lambda b, i: (0, 0))],
        out_specs=pl.BlockSpec((None, tm, d), lambda b, i: (b, i, 0)),
        out_shape=jax.ShapeDtypeStruct((b, t, d), F32),
        scratch_shapes=[pltpu.VMEM((tm, k), BF16)],
        compiler_params=pltpu.CompilerParams(
            dimension_semantics=("parallel", "parallel"), vmem_limit_bytes=VMEM_LIMIT),
        name="gdn_out",
    )(o, p, norm_w.reshape(1, GDN_HEAD_DIM), w, x, gate, gain.reshape(1, d))


def _bucket_of(n):
    big = jnp.full(n.shape, 16, jnp.int32)
    for thr in _BUCKET_THR:
        big = big + (n >= thr).astype(jnp.int32)
    return jnp.where(n < 16, n, big)


def _bias_tab_kernel(tbl_ref, tb_ref, lut_ref):
    h = pl.program_id(0)
    far = tbl_ref[N_BUCKETS - 1, h]

    def lookup(dist):
        bkt = _bucket_of(jnp.maximum(dist, 0))
        out = jnp.zeros(dist.shape, F32)
        for bb in range(N_BUCKETS):
            out = jnp.where(bkt == bb, tbl_ref[bb, h], out)
        return out - far

    qi = lax.broadcasted_iota(jnp.int32, (Q_BLOCK, 2 * Q_BLOCK), 0)
    kj = lax.broadcasted_iota(jnp.int32, (Q_BLOCK, 2 * Q_BLOCK), 1)
    dist = Q_BLOCK + qi - kj
    tb_ref[...] = jnp.where(dist >= 0, lookup(dist), NEG)
    lut_ref[...] = lookup(lax.broadcasted_iota(jnp.int32, (8, LANES), 1))


def _bias_tables(rel_bias):
    return pl.pallas_call(
        _bias_tab_kernel,
        grid=(NSA_HEADS,),
        in_specs=[pl.BlockSpec(memory_space=pltpu.SMEM)],
        out_specs=[pl.BlockSpec((None, Q_BLOCK, 2 * Q_BLOCK), lambda h: (h, 0, 0)),
                   pl.BlockSpec((None, 8, LANES), lambda h: (h, 0, 0))],
        out_shape=[jax.ShapeDtypeStruct((NSA_HEADS, Q_BLOCK, 2 * Q_BLOCK), F32),
                   jax.ShapeDtypeStruct((NSA_HEADS, 8, LANES), F32)],
        compiler_params=pltpu.CompilerParams(dimension_semantics=("parallel",)),
        name="bias_tables",
    )(rel_bias)


def _lut_gather(lut_rows, dist):
    idx = jnp.clip(dist, 0, LANES - 1)
    val = jnp.take_along_axis(lut_rows, idx, axis=1)
    return jnp.where((dist >= 0) & (dist < FAR_DIST), val, 0.0)


def _pe_term_kernel(pe_ref, wbd_ref, b1_ref, o_ref):
    y = jnp.dot(pe_ref[...], wbd_ref[...], preferred_element_type=F32)
    o_ref[...] = y[:, 0:CMP_HID] + y[:, 3 * CMP_HID:4 * CMP_HID] + b1_ref[...]


def _pe_term(pe_x, wbd, b1):
    return pl.pallas_call(
        _pe_term_kernel,
        grid=(2,),
        in_specs=[pl.BlockSpec((None, 8, 2048), lambda k: (k, 0, 0)),
                  pl.BlockSpec((None, 2048, 1024), lambda k: (k, 0, 0)),
                  pl.BlockSpec((None, 1, CMP_HID), lambda k: (k, 0, 0))],
        out_specs=pl.BlockSpec((None, 8, CMP_HID), lambda k: (k, 0, 0)),
        out_shape=jax.ShapeDtypeStruct((2, 8, CMP_HID), F32),
        compiler_params=pltpu.CompilerParams(dimension_semantics=("parallel",), vmem_limit_bytes=VMEM_LIMIT),
        name="cmp_pe_term",
    )(pe_x, wbd, b1.reshape(2, 1, CMP_HID))


def _compress_kernel(tab_ref, *refs, npg):
    del tab_ref
    pages = refs[:npg]
    wbd_ref, w2_ref, pe_ref, cc_ref, kc_ref, vc_ref, xs_ref, carry_ref = refs[npg:]
    ts = SEGS_PER_PAGE * npg

    @pl.when(pl.program_id(1) == 0)
    def _():
        carry_ref[...] = jnp.zeros(carry_ref.shape, F32)

    row0 = lax.broadcasted_iota(jnp.int32, (ts, CMP_HID), 0) == 0
    for k in range(2):
        out_ref = kc_ref if k == 0 else vc_ref
        for gp in range(2):
            off = k * 256 + gp * LANES
            for s in range(CMP_STRIDE):
                lo = s * NSA_KV_W + off
                piece = jnp.concatenate([pg[:, lo:lo + LANES] for pg in pages], axis=0)
                xs_ref[:, s * LANES:(s + 1) * LANES] = piece.astype(BF16)
            y = jnp.dot(xs_ref[...], wbd_ref[k], preferred_element_type=F32)
            hs = []
            for g2 in range(2):
                pa = y[:, g2 * 512:g2 * 512 + CMP_HID]
                pb = y[:, g2 * 512 + CMP_HID:g2 * 512 + 2 * CMP_HID]
                ci = (k * 2 + gp) * 2 + g2
                prev = carry_ref[ci]
                pa_prev = jnp.where(row0, prev[7:8, :], pltpu.roll(pa, 1, axis=0))
                carry_ref[ci] = pa[ts - 8:ts, :]
                hs.append(jax.nn.gelu(pa_prev + pb + pe_ref[k, 0:1, :]))
            hid = jnp.concatenate(hs, axis=-1).astype(BF16)
            o = jnp.dot(hid, w2_ref[k], preferred_element_type=F32) + cc_ref[k]
            out_ref[2 * gp] = o[:, :LANES].astype(BF16)
            out_ref[2 * gp + 1] = o[:, LANES:].astype(BF16)


def _compress(pages, table, wbd, w2bd, pe_term, ccols):
    bc, n_pages = table.shape
    npg = min(32, n_pages)
    ts = SEGS_PER_PAGE * npg
    nseg = n_pages * SEGS_PER_PAGE

    def page_spec(j):
        return pl.BlockSpec((None, SEGS_PER_PAGE, SEG_W), lambda b, i, tab: (tab[b, i * npg + j], 0, 0))

    const = lambda *shape: pl.BlockSpec(shape, lambda b, i, tab: (0,) * len(shape), pipeline_mode=pl.Buffered(1))
    out_spec = pl.BlockSpec((None, NSA_KV_GROUPS, ts, LANES), lambda b, i, tab: (b, 0, i, 0))
    grid_spec = pltpu.PrefetchScalarGridSpec(
        num_scalar_prefetch=1,
        grid=(bc, n_pages // npg),
        in_specs=[page_spec(j) for j in range(npg)] + [
            const(2, 2048, 1024), const(2, 512, 256), const(2, 8, CMP_HID), const(2, 1, 256)],
        out_specs=[out_spec, out_spec],
        scratch_shapes=[pltpu.VMEM((ts, 2048), BF16), pltpu.VMEM((8, 8, CMP_HID), F32)],
    )
    out_sds = jax.ShapeDtypeStruct((bc, NSA_KV_GROUPS, nseg, LANES), BF16)
    return pl.pallas_call(
        functools.partial(_compress_kernel, npg=npg),
        grid_spec=grid_spec,
        out_shape=[out_sds, out_sds],
        compiler_params=pltpu.CompilerParams(
            dimension_semantics=("parallel", "arbitrary"), vmem_limit_bytes=VMEM_LIMIT),
        name="kv_compress",
    )(table, *([pages] * npg), wbd, w2bd, pe_term, ccols)


def _compress_weights(pe, w1, b1, w2):
    eye2 = jnp.eye(2, dtype=F32)
    w = w1.reshape(2, 2, CMP_STRIDE, NSA_HEAD_DIM, CMP_HID)
    wbd = jnp.einsum('kasdh,gj->ksgdjah', w, eye2).reshape(2, 2048, 1024).astype(BF16)
    w2p = jnp.pad(w2, ((0, 0), (0, 0), (0, LANES - NSA_HEAD_DIM)))
    w2bd = jnp.einsum('khd,gj->kghjd', w2p, eye2).reshape(2, 512, 256).astype(BF16)
    pe_x = pe.reshape(2, 2, CMP_STRIDE, NSA_HEAD_DIM).transpose(0, 2, 1, 3).reshape(2, 1, 2048)
    pe_x = jnp.broadcast_to(pe_x, (2, 8, 2048)).astype(BF16)
    pe_term = _pe_term(pe_x, wbd, b1)
    cc = np.zeros((2, 1, 256), np.float32)
    for g2 in range(2):
        cc[0, 0, g2 * LANES + 64] = 1.0
        cc[0, 0, g2 * LANES + 65] = 1.0
        cc[1, 0, g2 * LANES + 64] = 1.0
    return wbd, w2bd, pe_term, jnp.asarray(cc)


def _topk_rows_mask(score, k):
    blk = lax.broadcasted_iota(jnp.int32, score.shape, 0).astype(F32)
    sel = jnp.zeros(score.shape, F32)
    for _ in range(k):
        mx = jnp.max(score, axis=0, keepdims=True)
        idx = jnp.min(jnp.where(score == mx, blk, 1e9), axis=0, keepdims=True)
        hit = blk == idx
        sel = jnp.where(hit, 1.0, sel)
        score = jnp.where(hit, -jnp.inf, score)
    return sel


def _softmax_av(s, v):
    m = jnp.max(s, axis=-1, keepdims=True)
    p = jnp.exp(s - m).astype(BF16)
    acc = jnp.dot(p, v, preferred_element_type=F32)
    return acc[:, :NSA_HEAD_DIM] / acc[:, NSA_HEAD_DIM:NSA_HEAD_DIM + 1]


def _nsa_prompt_kernel(q_ref, gt_ref, kc_ref, vc_ref, ks_ref, vs_ref, kw_ref, vw_ref, tb_ref, lut_ref, ovt_ref,
                       o_ref, qaug_ref, *, nseg, n_sb):
    g = pl.program_id(1)
    qb = pl.program_id(2)
    s0 = qb * Q_BLOCK
    rows = NSA_HPG * Q_BLOCK
    q = q_ref[...].reshape(rows, LANES)
    tb = tb_ref[...]

    sc = _nt(q, kc_ref[...]).reshape(NSA_HPG, Q_BLOCK, nseg)
    qi = lax.broadcasted_iota(jnp.int32, (Q_BLOCK, nseg), 0)
    mi = lax.broadcasted_iota(jnp.int32, (Q_BLOCK, nseg), 1)
    dist_c = s0 + qi - CMP_STRIDE * mi - (CMP_STRIDE - 1)
    valid_c = (dist_c >= 0) & (mi >= 1)
    corr = []
    for hh in range(NSA_HPG):
        lut = jnp.broadcast_to(lut_ref[hh, 0:1, :], (Q_BLOCK, LANES))
        corr.append(jnp.concatenate(
            [_lut_gather(lut, dist_c[:, c * LANES:(c + 1) * LANES]) for c in range(nseg // LANES)], axis=-1))
    sc = jnp.where(valid_c[None], sc + jnp.stack(corr), NEG)
    mx = jnp.max(sc, axis=-1, keepdims=True)
    e = jnp.where(valid_c[None], jnp.exp(sc - mx), 0.0)
    ssum = jnp.sum(e, axis=-1, keepdims=True)
    pc = e / jnp.where(ssum > 0, ssum, 1.0)
    o_c = jnp.dot(pc.reshape(rows, nseg).astype(BF16), vc_ref[...], preferred_element_type=F32)

    pcs = jnp.sum(pc, axis=0)
    hi, lo = _split2(pcs)
    ovt = ovt_ref[...]
    imp_t = _nt(ovt, hi) + _nt(ovt, lo)
    jb = lax.broadcasted_iota(jnp.int32, (LANES, Q_BLOCK), 0)
    q_blk = (s0 + lax.broadcasted_iota(jnp.int32, (LANES, Q_BLOCK), 1)) // SLC_BLOCK
    forced = (jb == 0) | (jb == q_blk) | (jb == q_blk - 1)
    score = jnp.where(forced, 1e4, jnp.where(jb <= q_blk, imp_t, -1e4))
    score = jnp.where(jb < n_sb, score, -3e38)
    sel_t = _topk_rows_mask(score, min(N_SELECT, n_sb))
    unsel = 1.0 - sel_t.T
    blk_lane = lax.broadcasted_iota(jnp.int32, (Q_BLOCK, LANES), 1)
    near_blk0 = 2 * qb - 2
    unsel_far = jnp.where(blk_lane >= near_blk0, 1.0, unsel)

    qaug_ref[:, LANES:] = q
    for hh in range(NSA_HPG):
        qaug_ref[hh * Q_BLOCK:(hh + 1) * Q_BLOCK, :LANES] = unsel.astype(BF16)
    a0 = pl.multiple_of(jnp.maximum(s0 - Q_BLOCK, 0), Q_BLOCK)
    b0 = pl.multiple_of(s0, Q_BLOCK)
    kn = jnp.concatenate([ks_ref[pl.ds(a0, Q_BLOCK), :], ks_ref[pl.ds(b0, Q_BLOCK), :]], axis=0)
    vn = jnp.concatenate([vs_ref[pl.ds(a0, Q_BLOCK), :], vs_ref[pl.ds(b0, Q_BLOCK), :]], axis=0)
    kcol = lax.broadcasted_iota(jnp.int32, (Q_BLOCK, 2 * Q_BLOCK), 1)
    no_prev = jnp.where((kcol < Q_BLOCK) & (qb == 0), NEG, 0.0)
    s_n = _nt(qaug_ref[...], kn).reshape(NSA_HPG, Q_BLOCK, 2 * Q_BLOCK) + (tb + no_prev[None])
    s_n = s_n.reshape(rows, 2 * Q_BLOCK)
    m_run = jnp.max(s_n, axis=-1, keepdims=True)
    acc = jnp.dot(jnp.exp(s_n - m_run).astype(BF16), vn, preferred_element_type=F32)

    for hh in range(NSA_HPG):
        qaug_ref[hh * Q_BLOCK:(hh + 1) * Q_BLOCK, :LANES] = unsel_far.astype(BF16)
    kc_far = 4 * Q_BLOCK
    n_far = (jnp.maximum(qb - 1, 0) + 3) // 4

    def far_step(c, carry):
        m_old, acc_old = carry
        k0 = pl.multiple_of(c * kc_far, kc_far)
        s = _nt(qaug_ref[...], ks_ref[pl.ds(k0, kc_far), :])
        m_new = jnp.maximum(m_old, jnp.max(s, axis=-1, keepdims=True))
        p = jnp.exp(s - m_new).astype(BF16)
        pv = jnp.dot(p, vs_ref[pl.ds(k0, kc_far), :], preferred_element_type=F32)
        return m_new, jnp.exp(m_old - m_new) * acc_old + pv

    m_run, acc = lax.fori_loop(0, n_far, far_step, (m_run, acc))
    o_s = acc[:, :NSA_HEAD_DIM] / acc[:, NSA_HEAD_DIM:NSA_HEAD_DIM + 1]

    w0 = pl.multiple_of(s0, Q_BLOCK)
    kw = kw_ref[pl.ds(w0, WINDOW + Q_BLOCK), :]
    vw = vw_ref[pl.ds(w0, WINDOW + Q_BLOCK), :]
    s_w = _nt(q, kw).reshape(NSA_HPG, Q_BLOCK, WINDOW + Q_BLOCK)
    qi_w = lax.broadcasted_iota(jnp.int32, (Q_BLOCK, WINDOW + Q_BLOCK), 0)
    kk_w = lax.broadcasted_iota(jnp.int32, (Q_BLOCK, WINDOW + Q_BLOCK), 1)
    ok_w = (kk_w > qi_w) & (s0 + kk_w >= WINDOW)
    s_w = jnp.where(ok_w[None], s_w, NEG)
    s_w = jnp.concatenate([s_w[:, :, :WINDOW - Q_BLOCK], s_w[:, :, WINDOW - Q_BLOCK:] + tb], axis=-1)
    o_w = _softmax_av(s_w.reshape(rows, WINDOW + Q_BLOCK), vw)

    gt = jax.nn.sigmoid(gt_ref[...])
    glane = lax.broadcasted_iota(jnp.int32, gt.shape, 1)
    for hh in range(NSA_HPG):
        col = 3 * (NSA_HPG * g + hh)
        gate = [jnp.sum(jnp.where(glane == col + br, gt, 0.0), axis=-1, keepdims=True) for br in range(3)]
        rs = slice(hh * Q_BLOCK, (hh + 1) * Q_BLOCK)
        o = gate[0] * o_c[rs, :NSA_HEAD_DIM] + gate[1] * o_s[rs] + gate[2] * o_w[rs]
        o_ref[hh] = o.astype(o_ref.dtype)


def _overlap_t(n_blk_pad, nseg):
    m = np.arange(nseg)[None, :]
    j = np.arange(n_blk_pad)[:, None]
    c_start = CMP_STRIDE * m - CMP_STRIDE
    c_end = CMP_STRIDE * m + CMP_STRIDE - 1
    ov = (c_start < j * SLC_BLOCK + SLC_BLOCK) & (c_end >= j * SLC_BLOCK) & (m >= 1)
    return ov.astype(np.float32)


def _nsa_prompt_attention(q128, p, gate_col_block, kc, vc, ks, vs, kw, vw, tb, lut):
    b, _, t, _ = q128.shape
    nseg = kc.shape[2]
    n_sb = t // SLC_BLOCK
    assert n_sb <= LANES and t % (4 * Q_BLOCK) == 0 and nseg % LANES == 0
    ovt = jnp.asarray(_overlap_t(LANES, nseg), dtype=BF16)
    per_bg = lambda rows, cols: pl.BlockSpec((None, None, rows, cols), lambda b, g, i: (b, g, 0, 0))
    return pl.pallas_call(
        functools.partial(_nsa_prompt_kernel, nseg=nseg, n_sb=n_sb),
        grid=(b, NSA_KV_GROUPS, t // Q_BLOCK),
        in_specs=[pl.BlockSpec((None, NSA_HPG, Q_BLOCK, LANES), lambda b, g, i: (b, g, i, 0)),
                  pl.BlockSpec((None, Q_BLOCK, LANES), lambda b, g, i: (b, i, gate_col_block)),
                  per_bg(nseg, LANES), per_bg(nseg, LANES),
                  per_bg(t, 2 * LANES), per_bg(t, LANES),
                  per_bg(t + WINDOW, LANES), per_bg(t + WINDOW, LANES),
                  pl.BlockSpec((NSA_HPG, Q_BLOCK, 2 * Q_BLOCK), lambda b, g, i: (g, 0, 0)),
                  pl.BlockSpec((NSA_HPG, 8, LANES), lambda b, g, i: (g, 0, 0)),
                  pl.BlockSpec((LANES, nseg), lambda b, g, i: (0, 0))],
        out_specs=pl.BlockSpec((None, NSA_HPG, Q_BLOCK, NSA_HEAD_DIM), lambda b, g, i: (b, g, i, 0)),
        out_shape=jax.ShapeDtypeStruct((b, NSA_HEADS, t, NSA_HEAD_DIM), BF16),
        scratch_shapes=[pltpu.VMEM((NSA_HPG * Q_BLOCK, 2 * LANES), BF16)],
        compiler_params=pltpu.CompilerParams(
            dimension_semantics=("parallel", "parallel", "arbitrary"), vmem_limit_bytes=VMEM_LIMIT),
        name="nsa_prompt_attention",
    )(q128, p, kc, vc, ks, vs, kw, vw, tb, lut, ovt)


def _gdn_conv_kernel(x_ref, w_ref, o_ref, carry_ref, *, tm, tc):
    j = pl.program_id(1)

    @pl.when(pl.program_id(2) == 0)
    def _():
        carry_ref[...] = jnp.zeros(carry_ref.shape, F32)

    x = x_ref[...]
    w = w_ref[...]
    prev = carry_ref[...]
    row8 = lax.broadcasted_iota(jnp.int32, (8, tc), 0)
    conv = x * w[CONV_W - 1:CONV_W, :]
    for sft in range(1, CONV_W):
        xs = pltpu.roll(x, sft, axis=0)
        top = jnp.where(row8 < sft, pltpu.roll(prev, sft, axis=0), xs[0:8])
        xs = top if tm == 8 else jnp.concatenate([top, xs[8:]], axis=0)
        conv = conv + xs * w[CONV_W - 1 - sft:CONV_W - sft, :]
    carry_ref[...] = x[tm - 8:tm, :]
    act = conv * jax.nn.sigmoid(conv)
    for hd in range(tc // GDN_HEAD_DIM):
        sl = slice(hd * GDN_HEAD_DIM, (hd + 1) * GDN_HEAD_DIM)
        a = act[:, sl]
        col0 = j * tc + hd * GDN_HEAD_DIM
        nrm = a * lax.rsqrt(jnp.sum(a * a, axis=-1, keepdims=True) + 1e-6)
        nrm = nrm * jnp.where(col0 < 1024, GDN_HEAD_DIM ** -0.5, 1.0)
        o_ref[:, sl] = jnp.where(col0 < 2048, nrm, a)


def _gdn_conv(p, conv_w):
    b, t, _ = p.shape
    tm = _row_tile(t)
    tc = 512
    return pl.pallas_call(
        functools.partial(_gdn_conv_kernel, tm=tm, tc=tc),
        grid=(b, C_CONV // tc, t // tm),
        in_specs=[pl.BlockSpec((None, tm, tc), lambda b, j, i: (b, i, j)),
                  pl.BlockSpec((CONV_W, tc), lambda b, j, i: (0, j))],
        out_specs=pl.BlockSpec((None, tm, tc), lambda b, j, i: (b, i, j)),
        out_shape=jax.ShapeDtypeStruct((b, t, C_CONV), F32),
        scratch_shapes=[pltpu.VMEM((8, tc), F32)],
        compiler_params=pltpu.CompilerParams(
            dimension_semantics=("parallel", "parallel", "arbitrary"), vmem_limit_bytes=VMEM_LIMIT),
        name="gdn_conv",
    )(p, conv_w)


def _gdn_gate_kernel(ba_ref, alog_ref, dtb_ref, o_ref):
    x = ba_ref[...]
    y = x + dtb_ref[...]
    softplus = jnp.maximum(y, 0.0) + jnp.log1p(jnp.exp(-jnp.abs(y)))
    g = -jnp.exp(alog_ref[...]) * softplus
    lane = lax.broadcasted_iota(jnp.int32, x.shape, 1)
    o_ref[...] = jnp.where(lane < GDN_V_HEADS, jax.nn.sigmoid(x), g)


def _gdn_gates(p, ba_col_block, a_log, dt_bias):
    b, t, _ = p.shape
    tm = _row_tile(t)
    pad = lambda v: jnp.pad(v.reshape(1, GDN_V_HEADS), ((0, 0), (GDN_V_HEADS, LANES - 2 * GDN_V_HEADS)))
    return pl.pallas_call(
        _gdn_gate_kernel,
        grid=(b, t // tm),
        in_specs=[pl.BlockSpec((None, tm, LANES), lambda b, i: (b, i, ba_col_block)),
                  pl.BlockSpec((1, LANES), lambda b, i: (0, 0)),
                  pl.BlockSpec((1, LANES), lambda b, i: (0, 0))],
        out_specs=pl.BlockSpec((None, tm, LANES), lambda b, i: (b, i, 0)),
        out_shape=jax.ShapeDtypeStruct((b, t, LANES), F32),
        compiler_params=pltpu.CompilerParams(dimension_semantics=("parallel", "parallel")),
        name="gdn_gates",
    )(p, pad(a_log), pad(dt_bias))


def _bdot(a, b):
    return jnp.dot(a.astype(BF16), b.astype(BF16), preferred_element_type=F32)


def _unit_lower_inverse(l, row, col):
    eye = (row == col).astype(F32)
    same16 = (row // 16) == (col // 16)
    same32 = (row // 32) == (col // 32)
    m = jnp.where(same16, -l, 0.0)
    p = eye + m
    for _ in range(3):
        m = _bdot(m, m)
        p = p + _bdot(p, m)
    c1 = jnp.where(same32 & jnp.logical_not(same16), l, 0.0)
    p = p - _bdot(p, _bdot(c1, p))
    c2 = jnp.where(same32, 0.0, l)
    return p - _bdot(p, _bdot(c2, p))


def _gdn_delta_kernel(act_ref, bg_ref, gt_ref, s0_ref, ltri_ref, o_ref, s_ref):
    c = GDN_CHUNK

    @pl.when(pl.program_id(1) == 0)
    def _():
        s_ref[...] = s0_ref[...]

    bg = bg_ref[...]
    ltri = ltri_ref[...]
    gc_all = sum(jnp.dot(ltri, part, preferred_element_type=F32) for part in _split3(bg))
    gc_rows = sum(_nt(part, ltri) for part in _split3(gt_ref[...]))
    row = lax.broadcasted_iota(jnp.int32, (c, c), 0)
    col = lax.broadcasted_iota(jnp.int32, (c, c), 1)
    incl = row >= col
    strict = row > col
    zpad = jnp.zeros((c, GDN_HEAD_DIM), F32)
    for h in range(GDN_V_HEADS):
        hq = h // (GDN_V_HEADS // GDN_QK_HEADS)
        qh = act_ref[:, hq * GDN_HEAD_DIM:(hq + 1) * GDN_HEAD_DIM]
        kh = act_ref[:, 1024 + hq * GDN_HEAD_DIM:1024 + (hq + 1) * GDN_HEAD_DIM]
        vh = act_ref[:, 2048 + h * GDN_HEAD_DIM:2048 + (h + 1) * GDN_HEAD_DIM]
        beta = bg[:, h:h + 1]
        gc = gc_all[:, GDN_V_HEADS + h:GDN_V_HEADS + h + 1]
        gcr = gc_rows[h:h + 1, :]
        decay = jnp.where(incl, jnp.exp(jnp.where(incl, gc - gcr, 0.0)), 0.0)
        kb = kh * beta
        vb = vh * beta
        lmat = jnp.where(strict, _nt(kb.astype(BF16), kh.astype(BF16)) * decay, 0.0)
        tinv = _unit_lower_inverse(lmat, row, col)
        eg = jnp.exp(gc)
        uw = _bdot(tinv, jnp.concatenate([vb, kb * eg], axis=-1))
        a_in = _nt(qh.astype(BF16), kh.astype(BF16)) * decay
        g_last = gc[c - 1:c, :]
        k_d = kh * jnp.exp(g_last - gc)
        s_old = s_ref[h]
        ws = _bdot(jnp.concatenate([uw[:, GDN_HEAD_DIM:], qh * eg], axis=0), s_old)
        v_new = uw[:, :GDN_HEAD_DIM] - ws[:c]
        o_ref[:, h * GDN_HEAD_DIM:(h + 1) * GDN_HEAD_DIM] = ws[c:] + _bdot(a_in, v_new)
        kd_t = jnp.concatenate([k_d, zpad], axis=0).T
        s_ref[h] = s_old * jnp.exp(g_last) + _bdot(kd_t, jnp.concatenate([v_new, zpad], axis=0))


def _gdn_delta(act, bg, s0):
    b, t, _ = act.shape
    nc = t // GDN_CHUNK
    g_rows = bg[:, :, GDN_V_HEADS:2 * GDN_V_HEADS].reshape(b, nc, GDN_CHUNK, GDN_V_HEADS).transpose(0, 1, 3, 2)
    ltri = jnp.asarray(np.tril(np.ones((GDN_CHUNK, GDN_CHUNK), np.float32)), dtype=BF16)
    state_spec = pl.BlockSpec((None, GDN_V_HEADS, GDN_HEAD_DIM, GDN_HEAD_DIM), lambda b, n: (b, 0, 0, 0))
    return pl.pallas_call(
        _gdn_delta_kernel,
        grid=(b, nc),
        in_specs=[pl.BlockSpec((None, GDN_CHUNK, C_CONV), lambda b, n: (b, n, 0)),
                  pl.BlockSpec((None, GDN_CHUNK, LANES), lambda b, n: (b, n, 0)),
                  pl.BlockSpec((None, None, GDN_V_HEADS, GDN_CHUNK), lambda b, n: (b, n, 0, 0)),
                  state_spec,
                  pl.BlockSpec((GDN_CHUNK, GDN_CHUNK), lambda b, n: (0, 0))],
        out_specs=[pl.BlockSpec((None, GDN_CHUNK, GDN_V_W), lambda b, n: (b, n, 0)), state_spec],
        out_shape=[jax.ShapeDtypeStruct((b, t, GDN_V_W), F32),
                   jax.ShapeDtypeStruct(s0.shape, F32)],
        compiler_params=pltpu.CompilerParams(
            dimension_semantics=("parallel", "arbitrary"), vmem_limit_bytes=VMEM_LIMIT),
        name="gdn_delta_rule",
    )(act, bg, g_rows, s0, ltri)


SAMPLE_ROWS = NSA_HEADS * 4


def _sample_cmp_kernel(q_ref, kc_ref, vc_ref, lut_ref, ov_ref, oc_ref, un_ref, *, nseg, past_len, n_sb, nq):
    rg = NSA_HPG * nq
    ri = lax.broadcasted_iota(jnp.int32, (rg, nseg), 0)
    mi = lax.broadcasted_iota(jnp.int32, (rg, nseg), 1)
    dist = past_len + ri % nq - CMP_STRIDE * mi - (CMP_STRIDE - 1)
    valid = (dist >= 0) & (mi >= 1)
    jl = lax.broadcasted_iota(jnp.int32, (8, un_ref.shape[-1]), 1)
    q_blk = (past_len + lax.broadcasted_iota(jnp.int32, jl.shape, 0) % nq) // SLC_BLOCK
    forced = (jl == 0) | (jl == q_blk) | (jl == q_blk - 1)
    jf = jl.astype(F32)
    for g in range(NSA_KV_GROUPS):
        sc = _nt(q_ref[g], kc_ref[g])
        tail = sc[:, nseg - LANES:] + _lut_gather(lut_ref[g], dist[:, nseg - LANES:])
        sc = jnp.where(valid, jnp.concatenate([sc[:, :nseg - LANES], tail], axis=-1), NEG)
        mx = jnp.max(sc, axis=-1, keepdims=True)
        e = jnp.where(valid, jnp.exp(sc - mx), 0.0)
        ssum = jnp.sum(e, axis=-1, keepdims=True)
        pc = e / jnp.where(ssum > 0, ssum, 1.0)
        oc_ref[g] = jnp.dot(pc.astype(BF16), vc_ref[g], preferred_element_type=F32)
        pcs = pc
        for hh in range(1, NSA_HPG):
            pcs = pcs + pltpu.roll(pc, hh * nq, axis=0)
        hi, lo = _split2(pcs[0:8])
        imp = jnp.dot(hi, ov_ref[...], preferred_element_type=F32) + jnp.dot(lo, ov_ref[...],
                                                                              preferred_element_type=F32)
        score = jnp.where(forced, 1e4, jnp.where(jl <= q_blk, imp, -1e4))
        score = jnp.where(jl < n_sb, score, -3e38)
        sel = jnp.zeros(score.shape, F32)
        for _ in range(min(N_SELECT, n_sb)):
            mxs = jnp.max(score, axis=-1, keepdims=True)
            idx = jnp.min(jnp.where(score == mxs, jf, 1e9), axis=-1, keepdims=True)
            hit = jf == idx
            sel = jnp.where(hit, 1.0, sel)
            score = jnp.where(hit, -jnp.inf, score)
        un_ref[g] = 1.0 - sel


def _sample_cmp(q16, kc, vc, lut16, past_len, nq):
    b = q16.shape[0]
    nseg = kc.shape[2]
    rg = NSA_HPG * nq
    n_sb = past_len // SLC_BLOCK + 1
    n_sb_pad = -(-n_sb // LANES) * LANES
    assert nq == 4 and nseg * CMP_STRIDE == past_len
    m = np.arange(nseg)[:, None]
    j = np.arange(n_sb_pad)[None, :]
    ov = ((CMP_STRIDE * m - CMP_STRIDE < j * SLC_BLOCK + SLC_BLOCK) & (CMP_STRIDE * m + CMP_STRIDE - 1 >= j * SLC_BLOCK)
          & (m >= 1) & (j < n_sb)).astype(np.float32)
    whole = lambda *shape: pl.BlockSpec((None,) + shape, lambda b: (b,) + (0,) * len(shape))
    return pl.pallas_call(
        functools.partial(_sample_cmp_kernel, nseg=nseg, past_len=past_len, n_sb=n_sb, nq=nq),
        grid=(b,),
        in_specs=[whole(NSA_KV_GROUPS, rg, LANES), whole(NSA_KV_GROUPS, nseg, LANES), whole(NSA_KV_GROUPS, nseg, LANES),
                  pl.BlockSpec((NSA_KV_GROUPS, rg, LANES), lambda b: (0, 0, 0)),
                  pl.BlockSpec((nseg, n_sb_pad), lambda b: (0, 0))],
        out_specs=[whole(NSA_KV_GROUPS, rg, LANES), whole(NSA_KV_GROUPS, 8, n_sb_pad)],
        out_shape=[jax.ShapeDtypeStruct((b, NSA_KV_GROUPS, rg, LANES), F32),
                   jax.ShapeDtypeStruct((b, NSA_KV_GROUPS, 8, n_sb_pad), F32)],
        compiler_params=pltpu.CompilerParams(dimension_semantics=("parallel",), vmem_limit_bytes=VMEM_LIMIT),
        name="nsa_sample_cmp_topk",
    )(q16, kc, vc, lut16, jnp.asarray(ov, dtype=BF16))


def _sample_sel_kernel(tab_ref, *refs, npg, past_len, nq):
    del tab_ref
    pages = refs[:npg]
    qbd_ref, un_ref, ee_ref, far_ref, lut_ref, m_ref, l_ref, acc_ref = refs[npg:]
    c = pl.program_id(1)
    kc = npg * PAGE_SIZE

    @pl.when(c == 0)
    def _():
        m_ref[...] = jnp.full(m_ref.shape, NEG, F32)
        l_ref[...] = jnp.zeros(l_ref.shape, F32)
        acc_ref[...] = jnp.zeros(acc_ref.shape, F32)

    kv = jnp.concatenate([pg[...] for pg in pages], axis=0)
    kb = kv[:, :256].astype(BF16)
    vb = kv[:, 256:].astype(BF16)
    s = _nt(qbd_ref[...], kb) + far_ref[...][:, 0:1] + jnp.dot(un_ref[...], ee_ref[...], preferred_element_type=F32)
    ri = lax.broadcasted_iota(jnp.int32, (SAMPLE_ROWS, LANES), 0)
    li = lax.broadcasted_iota(jnp.int32, (SAMPLE_ROWS, LANES), 1)
    dist = past_len + ri % nq - (c * kc + kc - LANES + li)
    s = jnp.concatenate([s[:, :kc - LANES], s[:, kc - LANES:] + _lut_gather(lut_ref[...], dist)], axis=-1)
    m_old = m_ref[...][:, 0:1]
    m_new = jnp.maximum(m_old, jnp.max(s, axis=-1, keepdims=True))
    alpha = jnp.exp(m_old - m_new)
    p = jnp.exp(s - m_new)
    l_ref[...] = alpha * l_ref[...] + jnp.sum(p, axis=-1, keepdims=True)
    acc_ref[...] = alpha * acc_ref[...] + jnp.dot(p.astype(BF16), vb, preferred_element_type=F32)
    m_ref[...] = jnp.broadcast_to(m_new, m_ref.shape)


def _sample_sel(pages, table, qbd, unsel_c, farcol, lut64, past_len, nq):
    b, n_pages = table.shape
    npg = min(16, n_pages)
    kc = npg * PAGE_SIZE
    nch = n_pages // npg
    blk_per_chunk = kc // SLC_BLOCK
    ee = np.zeros((LANES, kc), np.float32)
    ee[np.arange(kc) // SLC_BLOCK, np.arange(kc)] = NEG
    assert blk_per_chunk <= LANES

    def page_spec(j):
        return pl.BlockSpec((None, PAGE_SIZE, NSA_KV_W), lambda b, c, tab: (tab[b, c * npg + j], 0, 0))

    const = lambda *shape: pl.BlockSpec(shape, lambda b, c, tab: (0,) * len(shape))
    acc_spec = lambda cols: pl.BlockSpec((None, SAMPLE_ROWS, cols), lambda b, c, tab: (b, 0, 0))
    grid_spec = pltpu.PrefetchScalarGridSpec(
        num_scalar_prefetch=1,
        grid=(b, nch),
        in_specs=[page_spec(j) for j in range(npg)] + [
            pl.BlockSpec((None, SAMPLE_ROWS, 2 * LANES), lambda b, c, tab: (b, 0, 0)),
            pl.BlockSpec((None, None, SAMPLE_ROWS, LANES), lambda b, c, tab: (b, c, 0, 0)),
            const(LANES, kc), const(SAMPLE_ROWS, LANES), const(SAMPLE_ROWS, LANES)],
        out_specs=[acc_spec(LANES), acc_spec(LANES), acc_spec(2 * LANES)],
    )
    return pl.pallas_call(
        functools.partial(_sample_sel_kernel, npg=npg, past_len=past_len, nq=nq),
        grid_spec=grid_spec,
        out_shape=[jax.ShapeDtypeStruct((b, SAMPLE_ROWS, LANES), F32),
                   jax.ShapeDtypeStruct((b, SAMPLE_ROWS, LANES), F32),
                   jax.ShapeDtypeStruct((b, SAMPLE_ROWS, 2 * LANES), F32)],
        compiler_params=pltpu.CompilerParams(
            dimension_semantics=("parallel", "arbitrary"), vmem_limit_bytes=VMEM_LIMIT),
        name="nsa_sample_selected",
    )(table, *([pages] * npg), qbd, unsel_c, jnp.asarray(ee, dtype=BF16), farcol, lut64)


def _own_group_cols(x, grp):
    out = jnp.zeros((x.shape[0], NSA_HEAD_DIM), F32)
    for g in range(NSA_KV_GROUPS):
        out = jnp.where(grp == g, x[:, g * NSA_HEAD_DIM:(g + 1) * NSA_HEAD_DIM], out)
    return out


def _sample_final_kernel(qbd_ref, m_ref, l_ref, acc_ref, snew_ref, wc_ref, wnew_ref, oc_ref, gr_ref, far_ref, lut_ref,
                         o_ref, *, nq, w_buf):
    rows = SAMPLE_ROWS
    qbd = qbd_ref[...]
    far = far_ref[...][:, 0:1]
    lut = lut_ref[...]
    ri = lax.broadcasted_iota(jnp.int32, (rows, LANES), 0)
    li = lax.broadcasted_iota(jnp.int32, (rows, LANES), 1)
    tok = ri % nq
    grp = lax.broadcasted_iota(jnp.int32, (rows, NSA_HEAD_DIM), 0) // (NSA_HPG * nq)

    knew = snew_ref[...]
    s_new = _nt(qbd, knew[:, :256].astype(BF16)) + far
    d_new = tok - li
    s_new = jnp.where((d_new >= 0) & (li < nq), s_new + _lut_gather(lut, d_new), NEG)
    m_old = m_ref[...][:, 0:1]
    m_new = jnp.maximum(m_old, jnp.max(s_new, axis=-1, keepdims=True))
    alpha = jnp.exp(m_old - m_new)
    p_new = jnp.exp(s_new - m_new)
    l_s = alpha * l_ref[...][:, 0:1] + jnp.sum(p_new, axis=-1, keepdims=True)
    acc_s = alpha * acc_ref[...] + jnp.dot(p_new.astype(BF16), knew[:, 256:].astype(BF16), preferred_element_type=F32)
    o_s = _own_group_cols(acc_s, grp) / l_s

    kv_w = jnp.concatenate([wc_ref[...], wnew_ref[...]], axis=0)
    s_w = _nt(qbd, kv_w[:, :256].astype(BF16)) + far
    n_w = w_buf + LANES
    idx = lax.broadcasted_iota(jnp.int32, (rows, n_w), 1)
    d_w = w_buf + lax.broadcasted_iota(jnp.int32, (rows, n_w), 0) % nq - idx
    ok_w = (d_w >= 0) & (d_w < WINDOW) & (idx < w_buf + nq)
    corr = [jnp.zeros((rows, n_w - 2 * LANES), F32)]
    for cidx in range(2):
        lo = n_w - 2 * LANES + cidx * LANES
        corr.append(_lut_gather(lut, d_w[:, lo:lo + LANES]))
    s_w = jnp.where(ok_w, s_w + jnp.concatenate(corr, axis=-1), NEG)
    m_w = jnp.max(s_w, axis=-1, keepdims=True)
    p_w = jnp.exp(s_w - m_w)
    l_w = jnp.sum(p_w, axis=-1, keepdims=True)
    acc_w = jnp.dot(p_w.astype(BF16), kv_w[:, 256:].astype(BF16), preferred_element_type=F32)
    o_w = _own_group_cols(acc_w, grp) / l_w

    gt = jax.nn.sigmoid(gr_ref[...])
    o_ref[...] = gt[:, 0:1] * oc_ref[...][:, :NSA_HEAD_DIM] + gt[:, 1:2] * o_s + gt[:, 2:3] * o_w


def _sample_final(qbd, m, l, acc, snew, wcache, wnew, o_c, graw, farcol, lut64, nq):
    b = qbd.shape[0]
    w_buf = wcache.shape[1]
    assert w_buf == WINDOW
    whole = lambda *shape: pl.BlockSpec((None,) + shape, lambda b: (b,) + (0,) * len(shape))
    const = lambda *shape: pl.BlockSpec(shape, lambda b: (0,) * len(shape))
    return pl.pallas_call(
        functools.partial(_sample_final_kernel, nq=nq, w_buf=w_buf),
        grid=(b,),
        in_specs=[whole(SAMPLE_ROWS, 2 * LANES), whole(SAMPLE_ROWS, LANES), whole(SAMPLE_ROWS, LANES),
                  whole(SAMPLE_ROWS, 2 * LANES), whole(LANES, NSA_KV_W), whole(w_buf, NSA_KV_W), whole(LANES, NSA_KV_W),
                  whole(SAMPLE_ROWS, LANES), whole(SAMPLE_ROWS, LANES),
                  const(SAMPLE_ROWS, LANES), const(SAMPLE_ROWS, LANES)],
        out_specs=whole(SAMPLE_ROWS, NSA_HEAD_DIM),
        out_shape=jax.ShapeDtypeStruct((b, SAMPLE_ROWS, NSA_HEAD_DIM), F32),
        compiler_params=pltpu.CompilerParams(dimension_semantics=("parallel",), vmem_limit_bytes=VMEM_LIMIT),
        name="nsa_sample_final",
    )(qbd, m, l, acc, snew, wcache, wnew, o_c, graw, farcol, lut64)


def _ffn(x, mod, gains, w_in, w_out):
    hid = _norm_mod_swiglu(x, gains[2], mod[3], mod[4], w_in)
    return _matmul_rms_residual(hid, w_out, x, mod[5], gains[3])


def _heads_major(x, n_heads):
    b, t, _ = x.shape
    return x.reshape(b, t, n_heads, NSA_HEAD_DIM).transpose(0, 2, 1, 3)


def _const_cols(vals, lead_shape):
    cols = np.zeros((NSA_HEAD_DIM,), np.float32)
    cols[:len(vals)] = vals
    return jnp.broadcast_to(jnp.asarray(cols, dtype=BF16), tuple(lead_shape) + (NSA_HEAD_DIM,))


def _nsa_prompt(x, mod, gains, w_in, cmp_w, w_out, tb, lut, bias_cols):
    b, t, _ = x.shape
    p = _norm_mod_linear(x, gains[0], mod[0], mod[1], w_in)
    kvc, kvs, kvw = (p[..., 1024 + i * NSA_KV_W:1024 + (i + 1) * NSA_KV_W] for i in range(3))
    q128 = jnp.concatenate([_heads_major(p[..., :NSA_Q_W].astype(BF16), NSA_HEADS),
                            jnp.broadcast_to(bias_cols[None, :, None, :], (b, NSA_HEADS, t, NSA_HEAD_DIM))], axis=-1)
    lead = (b, NSA_KV_GROUPS, t)
    blk_onehot = np.zeros((t, LANES), np.float32)
    blk_onehot[np.arange(t), np.arange(t) // SLC_BLOCK] = NEG
    ks = jnp.concatenate([jnp.broadcast_to(jnp.asarray(blk_onehot, dtype=BF16), lead + (LANES,)),
                          _heads_major(kvs[..., :256].astype(BF16), NSA_KV_GROUPS),
                          _const_cols([1.0, 1.0], lead)], axis=-1)
    vs = jnp.concatenate([_heads_major(kvs[..., 256:].astype(BF16), NSA_KV_GROUPS), _const_cols([1.0], lead)], axis=-1)
    front = ((0, 0), (0, 0), (WINDOW, 0), (0, 0))
    kw = jnp.pad(jnp.concatenate([_heads_major(kvw[..., :256].astype(BF16), NSA_KV_GROUPS),
                                  _const_cols([1.0, 1.0], lead)], axis=-1), front)
    vw = jnp.pad(jnp.concatenate([_heads_major(kvw[..., 256:].astype(BF16), NSA_KV_GROUPS),
                                  _const_cols([1.0], lead)], axis=-1), front)
    n_pages = t // PAGE_SIZE
    table = jnp.arange(b * n_pages, dtype=jnp.int32).reshape(b, n_pages)
    kc, vc = _compress(kvc.reshape(b * n_pages, SEGS_PER_PAGE, SEG_W), table, *cmp_w)
    o = _nsa_prompt_attention(q128, p, (NSA_Q_W + 3 * NSA_KV_W) // LANES, kc, vc, ks, vs, kw, vw, tb, lut)
    o = o.transpose(0, 2, 1, 3).reshape(b, t, NSA_Q_W)
    x = _matmul_rms_residual(o, w_out, x, mod[2], gains[1])
    shape5 = (b, t, 2, NSA_KV_GROUPS, NSA_HEAD_DIM)
    return x, kvc.reshape(shape5), kvs.reshape(shape5), kvw.reshape(shape5)[:, -min(WINDOW, t):]


def _nsa_sample(x, mod, gains, w_in, cmp_w, w_out, lut, rel_bias, bias_cols, cache_cmp, cache_slc, cache_win,
                page_table, db, nq):
    n_pages = page_table.shape[1]
    past_len = n_pages * PAGE_SIZE
    rows = db * nq
    p = _norm_mod_linear(x, gains[0], mod[0], mod[1], w_in)[0]
    kvc, kvs, kvw = (p[:, 1024 + i * NSA_KV_W:1024 + (i + 1) * NSA_KV_W] for i in range(3))
    qh = p[:, :NSA_Q_W].astype(BF16).reshape(db, nq, NSA_KV_GROUPS, NSA_HPG, NSA_HEAD_DIM).transpose(0, 2, 3, 1, 4)
    q16 = jnp.concatenate([qh, jnp.broadcast_to(bias_cols.reshape(1, NSA_KV_GROUPS, NSA_HPG, 1, NSA_HEAD_DIM), qh.shape)],
                          axis=-1).reshape(db, NSA_KV_GROUPS, NSA_HPG * nq, LANES)
    eye_g = jnp.eye(NSA_KV_GROUPS, dtype=BF16)
    qbd = jnp.einsum('bghtd,gj->bghtjd', qh, eye_g).reshape(db, SAMPLE_ROWS, NSA_KV_GROUPS * NSA_HEAD_DIM)
    row_head = np.repeat(np.arange(NSA_HEADS), nq)
    lut64 = lut[:, 0, :][row_head]
    farcol = jnp.broadcast_to(rel_bias[N_BUCKETS - 1][row_head][:, None], (SAMPLE_ROWS, LANES))
    kc, vc = _compress(cache_cmp.reshape(-1, SEGS_PER_PAGE, SEG_W), page_table, *cmp_w)
    o_c, unsel = _sample_cmp(q16, kc, vc, lut64.reshape(NSA_KV_GROUPS, NSA_HPG * nq, LANES), past_len, nq)
    npg = min(16, n_pages)
    nch = n_pages // npg
    bpc = npg * PAGE_SIZE // SLC_BLOCK
    un = unsel[:, :, :nq, :past_len // SLC_BLOCK].reshape(db, NSA_KV_GROUPS, 1, nq, nch, bpc)
    un = jnp.broadcast_to(un, (db, NSA_KV_GROUPS, NSA_HPG, nq, nch, bpc)).transpose(0, 4, 1, 2, 3, 5)
    un = jnp.pad(un.reshape(db, nch, SAMPLE_ROWS, bpc), ((0, 0), (0, 0), (0, 0), (0, LANES - bpc))).astype(BF16)
    m, l, acc = _sample_sel(cache_slc.reshape(-1, PAGE_SIZE, NSA_KV_W), page_table, qbd, un, farcol, lut64, past_len, nq)
    pad_rows = lambda a: jnp.pad(a.reshape(db, nq, NSA_KV_W), ((0, 0), (0, LANES - nq), (0, 0)))
    wcache = cache_win.reshape(db, -1, NSA_KV_W)
    graw = p[:, NSA_Q_W + 3 * NSA_KV_W:NSA_Q_W + 3 * NSA_KV_W + 3 * NSA_HEADS]
    graw = graw.reshape(db, nq, NSA_HEADS, 3).transpose(0, 2, 1, 3).reshape(db, SAMPLE_ROWS, 3)
    graw = jnp.pad(graw, ((0, 0), (0, 0), (0, LANES - 3)))
    o = _sample_final(qbd, m, l, acc, pad_rows(kvs), wcache, pad_rows(kvw), o_c.reshape(db, SAMPLE_ROWS, LANES), graw,
                      farcol, lut64, nq)
    o = o.reshape(db, NSA_HEADS, nq, NSA_HEAD_DIM).transpose(0, 2, 1, 3).reshape(1, rows, NSA_Q_W)
    x = _matmul_rms_residual(o, w_out, x, mod[2], gains[1])
    shape5 = (db, nq, 2, NSA_KV_GROUPS, NSA_HEAD_DIM)
    kv_win = jnp.concatenate([cache_win, kvw.reshape(shape5)], axis=1)[:, -cache_win.shape[1]:]
    return x, kvc.reshape(shape5), kvs.reshape(shape5), kv_win


def _gdn_prompt(x, mod, gains, w_in, conv_w, a_log, dt_bias, norm_w, w_out):
    b, t, _ = x.shape
    p = _norm_mod_linear(x, gains[0], mod[0], mod[1], w_in)
    act = _gdn_conv(p, conv_w)
    bg = _gdn_gates(p, (C_CONV + GDN_V_W) // LANES, a_log, dt_bias)
    s0 = jnp.zeros((b, GDN_V_HEADS, GDN_HEAD_DIM, GDN_HEAD_DIM), F32)
    o, s_fin = _gdn_delta(act, bg, s0)
    x = _gdn_out(o, p, C_CONV // GDN_V_W, norm_w, w_out, x, mod[2], gains[1])
    return x, p[:, t - (CONV_W - 1):, :C_CONV], s_fin


def _gdn_sample(x, mod, gains, w_in, conv_w, a_log, dt_bias, norm_w, w_out, conv_buf, s0, db, nq):
    p = _norm_mod_linear(x, gains[0], mod[0], mod[1], w_in)
    qkv = p[0, :, :C_CONV].reshape(db, nq, C_CONV)
    xp = jnp.concatenate([conv_buf, qkv], axis=1)
    act = _gdn_conv(jnp.pad(xp, ((0, 0), (0, 8 - xp.shape[1]), (0, 0))), conv_w)[:, CONV_W - 1:CONV_W - 1 + nq]
    bg = _gdn_gates(p, (C_CONV + GDN_V_W) // LANES, a_log, dt_bias).reshape(db, nq, LANES)
    pad_t = ((0, 0), (0, GDN_CHUNK - nq), (0, 0))
    o, s_fin = _gdn_delta(jnp.pad(act, pad_t), jnp.pad(bg, pad_t), s0)
    o = o[:, :nq].reshape(1, db * nq, GDN_V_W)
    x = _gdn_out(o, p, C_CONV // GDN_V_W, norm_w, w_out, x, mod[2], gains[1])
    return x, xp[:, -(CONV_W - 1):], s_fin


def kernel(x_prompt, x_sample, c_prompt, c_sample, cache_kv_cmp, cache_kv_slc, cache_kv_win, state_conv, state_ssm,
           page_table, rel_bias, norm_gains, w_ada, b_ada, w_ffn_in, w_ffn_out, nsa_w_in, nsa_cmp_pe, nsa_cmp_w1,
           nsa_cmp_b1, nsa_cmp_w2, nsa_w_out, gdn_w_in, gdn_conv_w, gdn_a_log, gdn_dt_bias, gdn_norm_w, gdn_w_out):
    depth = w_ada.shape[0]
    bp, t, d = x_prompt.shape
    db, nq, _ = x_sample.shape
    assert nq + CONV_W - 1 <= 8 and nq <= GDN_CHUNK

    c_all = jnp.concatenate([c_prompt, c_sample], axis=0)
    rows_pad = -(-c_all.shape[0] // 8) * 8
    ada = _adaln(jnp.pad(c_all, ((0, rows_pad - c_all.shape[0]), (0, 0))), w_ada, b_ada)
    ada = ada.reshape(depth, rows_pad, 6, d)
    tb, lut = _bias_tables(rel_bias)
    far_hi, far_lo = _split2(rel_bias[N_BUCKETS - 1])
    bias_cols = jnp.zeros((NSA_HEADS, NSA_HEAD_DIM), BF16).at[:, 0].set(far_hi).at[:, 1].set(far_lo)

    xp = x_prompt
    xs = x_sample.reshape(1, db * nq, d)
    kvc_p, kvc_s, kvs_p, kvs_s, kvw_p, kvw_s, cv_p, cv_s, ss_p, ss_s = ([] for _ in range(10))
    for i in range(depth):
        mod_p = [ada[i, :bp, k][:, None, :] for k in range(6)]
        mod_s = [jnp.repeat(ada[i, bp:bp + db, k], nq, axis=0)[None] for k in range(6)]
        gains = norm_gains[i]
        l = i // 2
        if i % 2 == 0:
            w_in = jnp.concatenate([nsa_w_in[l][:, :NSA_Q_W] * (NSA_HEAD_DIM ** -0.5), nsa_w_in[l][:, NSA_Q_W:]], axis=1)
            w_in = jnp.pad(w_in, ((0, 0), (0, -w_in.shape[1] % LANES))).astype(BF16)
            cmp_w = _compress_weights(nsa_cmp_pe[l], nsa_cmp_w1[l], nsa_cmp_b1[l], nsa_cmp_w2[l])
            w_out = nsa_w_out[l].astype(BF16)
            xp, a, bq, cq = _nsa_prompt(xp, mod_p, gains, w_in, cmp_w, w_out, tb, lut, bias_cols)
            kvc_p.append(a); kvs_p.append(bq); kvw_p.append(cq)
            xs, a, bq, cq = _nsa_sample(xs, mod_s, gains, w_in, cmp_w, w_out, lut, rel_bias, bias_cols, cache_kv_cmp[l],
                                        cache_kv_slc[l], cache_kv_win[l], page_table, db, nq)
            kvc_s.append(a); kvs_s.append(bq); kvw_s.append(cq)
        else:
            w_in = jnp.pad(gdn_w_in[l], ((0, 0), (0, -gdn_w_in.shape[2] % LANES))).astype(BF16)
            gdn_w = (w_in, gdn_conv_w[l], gdn_a_log[l], gdn_dt_bias[l], gdn_norm_w[l], gdn_w_out[l].astype(BF16))
            xp, a, bq = _gdn_prompt(xp, mod_p, gains, *gdn_w)
            cv_p.append(a); ss_p.append(bq)
            xs, a, bq = _gdn_sample(xs, mod_s, gains, *gdn_w, state_conv[l], state_ssm[l], db, nq)
            cv_s.append(a); ss_s.append(bq)
        w_ffn = (w_ffn_in[i].astype(BF16), w_ffn_out[i].astype(BF16))
        xp = _ffn(xp, mod_p, gains, *w_ffn)
        xs = _ffn(xs, mod_s, gains, *w_ffn)
    return (xp, xs.reshape(db, nq, d), jnp.stack(kvc_p), jnp.stack(kvc_s), jnp.stack(kvs_p), jnp.stack(kvs_s),
            jnp.stack(kvw_p), jnp.stack(kvw_s), jnp.stack(cv_p), jnp.stack(cv_s), jnp.stack(ss_p), jnp.stack(ss_s))
```

```python
import functools
import math

import numpy as np
import jax
import jax.numpy as jnp
from jax import lax
from jax.experimental import pallas as pl
from jax.experimental.pallas import tpu as pltpu

F32 = jnp.float32
BF16 = jnp.bfloat16

D_MODEL = 1024
RMS_EPS = 1e-6
D_FF = 2816
NSA_HEADS = 16
NSA_HEAD_DIM = 64
NSA_KV_GROUPS = 4
NSA_HPG = 4
CMP_BLOCK = 32
CMP_STRIDE = 16
CMP_HID = 256
SLC_BLOCK = 64
N_SELECT = 16
WINDOW = 512
Q_BLOCK = 128
PAGE_SIZE = 128
N_BUCKETS = 32
GDN_QK_HEADS = 8
GDN_V_HEADS = 16
GDN_HEAD_DIM = 128
CONV_W = 4
GDN_CHUNK = 64
NSA_Q_W = 1024
NSA_KV_W = 512
C_CONV = 4096
GDN_V_W = 2048

LANES = 128
SEG_W = CMP_STRIDE * NSA_KV_W
SEGS_PER_PAGE = PAGE_SIZE // CMP_STRIDE
NEG = -1e30
VMEM_LIMIT = 48 * 1024 * 1024

_BUCKET_THR = (19, 21, 24, 27, 31, 35, 40, 46, 52, 59, 67, 77, 87, 99, 113)
FAR_DIST = 128


def _nt(a, b):
    return lax.dot_general(a, b, (((1,), (1,)), ((), ())), preferred_element_type=F32)


def _split2(x):
    hi = x.astype(BF16)
    lo = (x - hi.astype(F32)).astype(BF16)
    return hi, lo


def _split3(x):
    hi = x.astype(BF16)
    r = x - hi.astype(F32)
    mid = r.astype(BF16)
    lo = (r - mid.astype(F32)).astype(BF16)
    return hi, mid, lo


def _pick_tn(n):
    units = n // LANES
    best = 1
    for d in range(1, units + 1):
        if units % d == 0 and d * LANES <= 1536:
            best = d
    return best * LANES


def _adaln_kernel(c_ref, w_ref, b_ref, o_ref):
    c = c_ref[...]
    a = (c * jax.nn.sigmoid(c)).astype(BF16)
    o_ref[...] = jnp.dot(a, w_ref[...].astype(BF16), preferred_element_type=F32) + b_ref[...]


def _adaln(c_all, w_ada, b_ada):
    depth, d, n = w_ada.shape
    rows = c_all.shape[0]
    tn = 768
    return pl.pallas_call(
        _adaln_kernel,
        grid=(depth, n // tn),
        in_specs=[pl.BlockSpec((rows, d), lambda l, j: (0, 0)),
                  pl.BlockSpec((None, d, tn), lambda l, j: (l, 0, j)),
                  pl.BlockSpec((None, 1, tn), lambda l, j: (l, 0, j))],
        out_specs=pl.BlockSpec((None, rows, tn), lambda l, j: (l, 0, j)),
        out_shape=jax.ShapeDtypeStruct((depth, rows, n), F32),
        compiler_params=pltpu.CompilerParams(dimension_semantics=("parallel", "parallel")),
        name="adaln",
    )(c_all, w_ada, b_ada.reshape(depth, 1, n))


def _mod_norm(x, gain, shift, scale):
    ms = jnp.mean(x * x, axis=-1, keepdims=True)
    y = x * lax.rsqrt(ms + RMS_EPS) * gain
    return y * (1.0 + scale) + shift


def _nml_kernel(x_ref, g_ref, sh_ref, sc_ref, w_ref, o_ref, h_ref):
    @pl.when(pl.program_id(2) == 0)
    def _():
        h_ref[...] = _mod_norm(x_ref[...], g_ref[...], sh_ref[...], sc_ref[...]).astype(BF16)

    o_ref[...] = jnp.dot(h_ref[...], w_ref[...], preferred_element_type=F32).astype(o_ref.dtype)


def _nml_swiglu_kernel(x_ref, g_ref, sh_ref, sc_ref, wg_ref, wu_ref, o_ref, h_ref):
    @pl.when(pl.program_id(2) == 0)
    def _():
        h_ref[...] = _mod_norm(x_ref[...], g_ref[...], sh_ref[...], sc_ref[...]).astype(BF16)

    h = h_ref[...]
    gate = jnp.dot(h, wg_ref[...], preferred_element_type=F32)
    up = jnp.dot(h, wu_ref[...], preferred_element_type=F32)
    o_ref[...] = (gate * jax.nn.sigmoid(gate) * up).astype(o_ref.dtype)


def _mod_spec(mod, tm):
    if mod.shape[1] == 1:
        return pl.BlockSpec((None, 1, mod.shape[2]), lambda b, i, *_: (b, 0, 0))
    return pl.BlockSpec((None, tm, mod.shape[2]), lambda b, i, *_: (b, i, 0))


def _row_tile(t):
    return 512 if t % 512 == 0 else t


def _norm_mod_linear(x, gain, shift, scale, w, out_dtype=F32):
    b, t, d = x.shape
    n = w.shape[1]
    tm, tn = _row_tile(t), _pick_tn(n)
    return pl.pallas_call(
        _nml_kernel,
        grid=(b, t // tm, n // tn),
        in_specs=[pl.BlockSpec((None, tm, d), lambda b, i, j: (b, i, 0)),
                  pl.BlockSpec((1, d), lambda b, i, j: (0, 0)),
                  _mod_spec(shift, tm), _mod_spec(scale, tm),
                  pl.BlockSpec((d, tn), lambda b, i, j: (0, j))],
        out_specs=pl.BlockSpec((None, tm, tn), lambda b, i, j: (b, i, j)),
        out_shape=jax.ShapeDtypeStruct((b, t, n), out_dtype),
        scratch_shapes=[pltpu.VMEM((tm, d), BF16)],
        compiler_params=pltpu.CompilerParams(
            dimension_semantics=("parallel", "parallel", "arbitrary"), vmem_limit_bytes=VMEM_LIMIT),
        name="norm_mod_linear",
    )(x, gain.reshape(1, d), shift, scale, w)


def _norm_mod_swiglu(x, gain, shift, scale, w_in):
    b, t, d = x.shape
    nf = w_in.shape[1] // 2
    tm, tn = _row_tile(t), _pick_tn(nf)
    nj = nf // tn
    return pl.pallas_call(
        _nml_swiglu_kernel,
        grid=(b, t // tm, nj),
        in_specs=[pl.BlockSpec((None, tm, d), lambda b, i, j: (b, i, 0)),
                  pl.BlockSpec((1, d), lambda b, i, j: (0, 0)),
                  _mod_spec(shift, tm), _mod_spec(scale, tm),
                  pl.BlockSpec((d, tn), lambda b, i, j: (0, j)),
                  pl.BlockSpec((d, tn), lambda b, i, j: (0, j + nj))],
        out_specs=pl.BlockSpec((None, tm, tn), lambda b, i, j: (b, i, j)),
        out_shape=jax.ShapeDtypeStruct((b, t, nf), BF16),
        scratch_shapes=[pltpu.VMEM((tm, d), BF16)],
        compiler_params=pltpu.CompilerParams(
            dimension_semantics=("parallel", "parallel", "arbitrary"), vmem_limit_bytes=VMEM_LIMIT),
        name="norm_mod_swiglu",
    )(x, gain.reshape(1, d), shift, scale, w_in, w_in)


def _rms_gated_residual(y, x, gate, gain):
    ms = jnp.mean(y * y, axis=-1, keepdims=True)
    return x + gate * (y * lax.rsqrt(ms + RMS_EPS) * gain)


def _mrr_kernel(a_ref, w_ref, x_ref, gate_ref, gain_ref, o_ref):
    y = jnp.dot(a_ref[...].astype(BF16), w_ref[...], preferred_element_type=F32)
    o_ref[...] = _rms_gated_residual(y, x_ref[...], gate_ref[...], gain_ref[...])


def _matmul_rms_residual(a, w, x, gate, gain):
    b, t, k = a.shape
    d = w.shape[1]
    tm = _row_tile(t)
    return pl.pallas_call(
        _mrr_kernel,
        grid=(b, t // tm),
        in_specs=[pl.BlockSpec((None, tm, k), lambda b, i: (b, i, 0)),
                  pl.BlockSpec((k, d), lambda b, i: (0, 0)),
                  pl.BlockSpec((None, tm, d), lambda b, i: (b, i, 0)),
                  _mod_spec(gate, tm),
                  pl.BlockSpec((1, d), lambda b, i: (0, 0))],
        out_specs=pl.BlockSpec((None, tm, d), lambda b, i: (b, i, 0)),
        out_shape=jax.ShapeDtypeStruct((b, t, d), F32),
        compiler_params=pltpu.CompilerParams(
            dimension_semantics=("parallel", "parallel"), vmem_limit_bytes=VMEM_LIMIT),
        name="matmul_rms_residual",
    )(a, w, x, gate, gain.reshape(1, d))


def _gdn_out_kernel(o_ref, z_ref, nw_ref, w_ref, x_ref, gate_ref, gain_ref, out_ref, a_ref):
    nw = nw_ref[...]
    for h in range(GDN_V_HEADS):
        sl = slice(h * GDN_HEAD_DIM, (h + 1) * GDN_HEAD_DIM)
        o = o_ref[:, sl]
        z = z_ref[:, sl]
        ms = jnp.mean(o * o, axis=-1, keepdims=True)
        a_ref[:, sl] = ((o * lax.rsqrt(ms + RMS_EPS) * nw) * (z * jax.nn.sigmoid(z))).astype(BF16)
    y = jnp.dot(a_ref[...], w_ref[...], preferred_element_type=F32)
    out_ref[...] = _rms_gated_residual(y, x_ref[...], gate_ref[...], gain_ref[...])


def _gdn_out(o, p, z_col_block, norm_w, w, x, gate, gain):
    b, t, k = o.shape
    d = w.shape[1]
    tm = _row_tile(t)
    return pl.pallas_call(
        _gdn_out_kernel,
        grid=(b, t // tm),
        in_specs=[pl.BlockSpec((None, tm, k), lambda b, i: (b, i, 0)),
                  pl.BlockSpec((None, tm, k), lambda b, i: (b, i, z_col_block)),
                  pl.BlockSpec((1, GDN_HEAD_DIM), lambda b, i: (0, 0)),
                  pl.BlockSpec((k, d), lambda b, i: (0, 0)),
                  pl.BlockSpec((None, tm, d), lambda b, i: (b, i, 0)),
                  _mod_spec(gate, tm),
                  pl.BlockSpec((1, d), lambda b, i: (0, 0))],
        out_specs=pl.BlockSpec((None, tm, d), lambda b, i: (b, i, 0)),
        out_shape=jax.ShapeDtypeStruct((b, t, d), F32),
        scratch_shapes=[pltpu.VMEM((tm, k), BF16)],
        compiler_params=pltpu.CompilerParams(
            dimension_semantics=("parallel", "parallel"), vmem_limit_bytes=VMEM_LIMIT),
        name="gdn_out",
    )(o, p, norm_w.reshape(1, GDN_HEAD_DIM), w, x, gate, gain.reshape(1, d))


def _bucket_of(n):
    big = jnp.full(n.shape, 16, jnp.int32)
    for thr in _BUCKET_THR:
        big = big + (n >= thr).astype(jnp.int32)
    return jnp.where(n < 16, n, big)


def _bias_tab_kernel(tbl_ref, tb_ref, lut_ref):
    h = pl.program_id(0)
    far = tbl_ref[N_BUCKETS - 1, h]

    def lookup(dist):
        bkt = _bucket_of(jnp.maximum(dist, 0))
        out = jnp.zeros(dist.shape, F32)
        for bb in range(N_BUCKETS):
            out = jnp.where(bkt == bb, tbl_ref[bb, h], out)
        return out - far

    qi = lax.broadcasted_iota(jnp.int32, (Q_BLOCK, 2 * Q_BLOCK), 0)
    kj = lax.broadcasted_iota(jnp.int32, (Q_BLOCK, 2 * Q_BLOCK), 1)
    dist = Q_BLOCK + qi - kj
    tb_ref[...] = jnp.where(dist >= 0, lookup(dist), NEG)
    lut_ref[...] = lookup(lax.broadcasted_iota(jnp.int32, (8, LANES), 1))


def _bias_tables(rel_bias):
    return pl.pallas_call(
        _bias_tab_kernel,
        grid=(NSA_HEADS,),
        in_specs=[pl.BlockSpec(memory_space=pltpu.SMEM)],
        out_specs=[pl.BlockSpec((None, Q_BLOCK, 2 * Q_BLOCK), lambda h: (h, 0, 0)),
                   pl.BlockSpec((None, 8, LANES), lambda h: (h, 0, 0))],
        out_shape=[jax.ShapeDtypeStruct((NSA_HEADS, Q_BLOCK, 2 * Q_BLOCK), F32),
                   jax.ShapeDtypeStruct((NSA_HEADS, 8, LANES), F32)],
        compiler_params=pltpu.CompilerParams(dimension_semantics=("parallel",)),
        name="bias_tables",
    )(rel_bias)


def _lut_gather(lut_rows, dist):
    idx = jnp.clip(dist, 0, LANES - 1)
    val = jnp.take_along_axis(lut_rows, idx, axis=1)
    return jnp.where((dist >= 0) & (dist < FAR_DIST), val, 0.0)


def _pe_term_kernel(pe_ref, wbd_ref, b1_ref, o_ref):
    y = jnp.dot(pe_ref[...], wbd_ref[...], preferred_element_type=F32)
    o_ref[...] = y[:, 0:CMP_HID] + y[:, 3 * CMP_HID:4 * CMP_HID] + b1_ref[...]


def _pe_term(pe_x, wbd, b1):
    return pl.pallas_call(
        _pe_term_kernel,
        grid=(2,),
        in_specs=[pl.BlockSpec((None, 8, 2048), lambda k: (k, 0, 0)),
                  pl.BlockSpec((None, 2048, 1024), lambda k: (k, 0, 0)),
                  pl.BlockSpec((None, 1, CMP_HID), lambda k: (k, 0, 0))],
        out_specs=pl.BlockSpec((None, 8, CMP_HID), lambda k: (k, 0, 0)),
        out_shape=jax.ShapeDtypeStruct((2, 8, CMP_HID), F32),
        compiler_params=pltpu.CompilerParams(dimension_semantics=("parallel",), vmem_limit_bytes=VMEM_LIMIT),
        name="cmp_pe_term",
    )(pe_x, wbd, b1.reshape(2, 1, CMP_HID))


def _compress_kernel(tab_ref, *refs, npg, rows_minor):
    del tab_ref
    pages = refs[:npg]
    wbd_ref, w2_ref, pe_ref, cc_ref, kc_ref, vc_ref, xs_ref, carry_ref = refs[npg:npg + 8]
    ts = SEGS_PER_PAGE * npg

    @pl.when(pl.program_id(1) == 0)
    def _():
        carry_ref[...] = jnp.zeros(carry_ref.shape, F32)

    row0 = lax.broadcasted_iota(jnp.int32, (ts, CMP_HID), 0) == 0
    for k in range(2):
        out_ref = kc_ref if k == 0 else vc_ref
        for gp in range(2):
            if rows_minor:
                rt_ref = refs[npg + 8]
                for j, pg in enumerate(pages):
                    rt_ref[j * PAGE_SIZE:(j + 1) * PAGE_SIZE, :] = pg[k, gp].T
                for s in range(CMP_STRIDE):
                    xs_ref[:, s * LANES:(s + 1) * LANES] = rt_ref[pl.ds(s, ts, stride=CMP_STRIDE), :].astype(BF16)
            else:
                off = k * 256 + gp * LANES
                for s in range(CMP_STRIDE):
                    lo = s * NSA_KV_W + off
                    piece = jnp.concatenate([pg[:, lo:lo + LANES] for pg in pages], axis=0)
                    xs_ref[:, s * LANES:(s + 1) * LANES] = piece.astype(BF16)
            y = jnp.dot(xs_ref[...], wbd_ref[k], preferred_element_type=F32)
            hs = []
            for g2 in range(2):
                pa = y[:, g2 * 512:g2 * 512 + CMP_HID]
                pb = y[:, g2 * 512 + CMP_HID:g2 * 512 + 2 * CMP_HID]
                ci = (k * 2 + gp) * 2 + g2
                prev = carry_ref[ci]
                pa_prev = jnp.where(row0, prev[7:8, :], pltpu.roll(pa, 1, axis=0))
                carry_ref[ci] = pa[ts - 8:ts, :]
                hs.append(jax.nn.gelu(pa_prev + pb + pe_ref[k, 0:1, :]))
            hid = jnp.concatenate(hs, axis=-1).astype(BF16)
            o = jnp.dot(hid, w2_ref[k], preferred_element_type=F32) + cc_ref[k]
            out_ref[2 * gp] = o[:, :LANES].astype(BF16)
            out_ref[2 * gp + 1] = o[:, LANES:].astype(BF16)


def _compress(pages, table, wbd, w2bd, pe_term, ccols):
    bc, n_pages = table.shape
    npg = min(32, n_pages)
    ts = SEGS_PER_PAGE * npg
    nseg = n_pages * SEGS_PER_PAGE
    rows_minor = pages.ndim == 5
    page_block = (None,) + pages.shape[1:]

    def page_spec(j):
        return pl.BlockSpec(page_block, lambda b, i, tab: (tab[b, i * npg + j],) + (0,) * (pages.ndim - 1))

    const = lambda *shape: pl.BlockSpec(shape, lambda b, i, tab: (0,) * len(shape), pipeline_mode=pl.Buffered(1))
    out_spec = pl.BlockSpec((None, NSA_KV_GROUPS, ts, LANES), lambda b, i, tab: (b, 0, i, 0))
    grid_spec = pltpu.PrefetchScalarGridSpec(
        num_scalar_prefetch=1,
        grid=(bc, n_pages // npg),
        in_specs=[page_spec(j) for j in range(npg)] + [
            const(2, 2048, 1024), const(2, 512, 256), const(2, 8, CMP_HID), const(2, 1, 256)],
        out_specs=[out_spec, out_spec],
        scratch_shapes=[pltpu.VMEM((ts, 2048), BF16), pltpu.VMEM((8, 8, CMP_HID), F32)] + (
            [pltpu.VMEM((npg * PAGE_SIZE, LANES), F32)] if rows_minor else []),
    )
    out_sds = jax.ShapeDtypeStruct((bc, NSA_KV_GROUPS, nseg, LANES), BF16)
    return pl.pallas_call(
        functools.partial(_compress_kernel, npg=npg, rows_minor=rows_minor),
        grid_spec=grid_spec,
        out_shape=[out_sds, out_sds],
        compiler_params=pltpu.CompilerParams(
            dimension_semantics=("parallel", "arbitrary"), vmem_limit_bytes=VMEM_LIMIT),
        name="kv_compress",
    )(table, *([pages] * npg), wbd, w2bd, pe_term, ccols)


def _compress_weights(pe, w1, b1, w2):
    eye2 = jnp.eye(2, dtype=F32)
    w = w1.reshape(2, 2, CMP_STRIDE, NSA_HEAD_DIM, CMP_HID)
    wbd = jnp.einsum('kasdh,gj->ksgdjah', w, eye2).reshape(2, 2048, 1024).astype(BF16)
    w2p = jnp.pad(w2, ((0, 0), (0, 0), (0, LANES - NSA_HEAD_DIM)))
    w2bd = jnp.einsum('khd,gj->kghjd', w2p, eye2).reshape(2, 512, 256).astype(BF16)
    pe_x = pe.reshape(2, 2, CMP_STRIDE, NSA_HEAD_DIM).transpose(0, 2, 1, 3).reshape(2, 1, 2048)
    pe_x = jnp.broadcast_to(pe_x, (2, 8, 2048)).astype(BF16)
    pe_term = _pe_term(pe_x, wbd, b1)
    cc = np.zeros((2, 1, 256), np.float32)
    for g2 in range(2):
        cc[0, 0, g2 * LANES + 64] = 1.0
        cc[0, 0, g2 * LANES + 65] = 1.0
        cc[1, 0, g2 * LANES + 64] = 1.0
    return wbd, w2bd, pe_term, jnp.asarray(cc)


def _topk_rows_mask(score, k):
    blk = lax.broadcasted_iota(jnp.int32, score.shape, 0).astype(F32)
    sel = jnp.zeros(score.shape, F32)
    for _ in range(k):
        mx = jnp.max(score, axis=0, keepdims=True)
        idx = jnp.min(jnp.where(score == mx, blk, 1e9), axis=0, keepdims=True)
        hit = blk == idx
        sel = jnp.where(hit, 1.0, sel)
        score = jnp.where(hit, -jnp.inf, score)
    return sel


def _softmax_av(s, v):
    m = jnp.max(s, axis=-1, keepdims=True)
    p = jnp.exp(s - m).astype(BF16)
    acc = jnp.dot(p, v, preferred_element_type=F32)
    return acc[:, :NSA_HEAD_DIM] / acc[:, NSA_HEAD_DIM:NSA_HEAD_DIM + 1]


def _nsa_prompt_kernel(q_ref, gt_ref, kc_ref, vc_ref, ks_ref, vs_ref, kw_ref, vw_ref, tb_ref, lut_ref, ovt_ref,
                       o_ref, qaug_ref, *, nseg, n_sb):
    g = pl.program_id(1)
    qb = pl.program_id(2)
    s0 = qb * Q_BLOCK
    rows = NSA_HPG * Q_BLOCK
    q = q_ref[...].reshape(rows, LANES)
    tb = tb_ref[...]

    sc = _nt(q, kc_ref[...]).reshape(NSA_HPG, Q_BLOCK, nseg)
    qi = lax.broadcasted_iota(jnp.int32, (Q_BLOCK, nseg), 0)
    mi = lax.broadcasted_iota(jnp.int32, (Q_BLOCK, nseg), 1)
    dist_c = s0 + qi - CMP_STRIDE * mi - (CMP_STRIDE - 1)
    valid_c = (dist_c >= 0) & (mi >= 1)
    corr = []
    for hh in range(NSA_HPG):
        lut = jnp.broadcast_to(lut_ref[hh, 0:1, :], (Q_BLOCK, LANES))
        corr.append(jnp.concatenate(
            [_lut_gather(lut, dist_c[:, c * LANES:(c + 1) * LANES]) for c in range(nseg // LANES)], axis=-1))
    sc = jnp.where(valid_c[None], sc + jnp.stack(corr), NEG)
    mx = jnp.max(sc, axis=-1, keepdims=True)
    e = jnp.where(valid_c[None], jnp.exp(sc - mx), 0.0)
    ssum = jnp.sum(e, axis=-1, keepdims=True)
    pc = e / jnp.where(ssum > 0, ssum, 1.0)
    o_c = jnp.dot(pc.reshape(rows, nseg).astype(BF16), vc_ref[...], preferred_element_type=F32)

    pcs = jnp.sum(pc, axis=0)
    hi, lo = _split2(pcs)
    ovt = ovt_ref[...]
    imp_t = _nt(ovt, hi) + _nt(ovt, lo)
    jb = lax.broadcasted_iota(jnp.int32, (LANES, Q_BLOCK), 0)
    q_blk = (s0 + lax.broadcasted_iota(jnp.int32, (LANES, Q_BLOCK), 1)) // SLC_BLOCK
    forced = (jb == 0) | (jb == q_blk) | (jb == q_blk - 1)
    score = jnp.where(forced, 1e4, jnp.where(jb <= q_blk, imp_t, -1e4))
    score = jnp.where(jb < n_sb, score, -3e38)
    sel_t = _topk_rows_mask(score, min(N_SELECT, n_sb))
    unsel = 1.0 - sel_t.T
    blk_lane = lax.broadcasted_iota(jnp.int32, (Q_BLOCK, LANES), 1)
    near_blk0 = 2 * qb - 2
    unsel_far = jnp.where(blk_lane >= near_blk0, 1.0, unsel)

    qaug_ref[:, LANES:] = q
    for hh in range(NSA_HPG):
        qaug_ref[hh * Q_BLOCK:(hh + 1) * Q_BLOCK, :LANES] = unsel.astype(BF16)
    a0 = pl.multiple_of(jnp.maximum(s0 - Q_BLOCK, 0), Q_BLOCK)
    b0 = pl.multiple_of(s0, Q_BLOCK)
    kn = jnp.concatenate([ks_ref[pl.ds(a0, Q_BLOCK), :], ks_ref[pl.ds(b0, Q_BLOCK), :]], axis=0)
    vn = jnp.concatenate([vs_ref[pl.ds(a0, Q_BLOCK), :], vs_ref[pl.ds(b0, Q_BLOCK), :]], axis=0)
    kcol = lax.broadcasted_iota(jnp.int32, (Q_BLOCK, 2 * Q_BLOCK), 1)
    no_prev = jnp.where((kcol < Q_BLOCK) & (qb == 0), NEG, 0.0)
    s_n = _nt(qaug_ref[...], kn).reshape(NSA_HPG, Q_BLOCK, 2 * Q_BLOCK) + (tb + no_prev[None])
    s_n = s_n.reshape(rows, 2 * Q_BLOCK)
    m_run = jnp.max(s_n, axis=-1, keepdims=True)
    acc = jnp.dot(jnp.exp(s_n - m_run).astype(BF16), vn, preferred_element_type=F32)

    for hh in range(NSA_HPG):
        qaug_ref[hh * Q_BLOCK:(hh + 1) * Q_BLOCK, :LANES] = unsel_far.astype(BF16)
    kc_far = 4 * Q_BLOCK
    n_far = (jnp.maximum(qb - 1, 0) + 3) // 4

    def far_step(c, carry):
        m_old, acc_old = carry
        k0 = pl.multiple_of(c * kc_far, kc_far)
        s = _nt(qaug_ref[...], ks_ref[pl.ds(k0, kc_far), :])
        m_new = jnp.maximum(m_old, jnp.max(s, axis=-1, keepdims=True))
        p = jnp.exp(s - m_new).astype(BF16)
        pv = jnp.dot(p, vs_ref[pl.ds(k0, kc_far), :], preferred_element_type=F32)
        return m_new, jnp.exp(m_old - m_new) * acc_old + pv

    m_run, acc = lax.fori_loop(0, n_far, far_step, (m_run, acc))
    o_s = acc[:, :NSA_HEAD_DIM] / acc[:, NSA_HEAD_DIM:NSA_HEAD_DIM + 1]

    w0 = pl.multiple_of(s0, Q_BLOCK)
    kw = kw_ref[pl.ds(w0, WINDOW + Q_BLOCK), :]
    vw = vw_ref[pl.ds(w0, WINDOW + Q_BLOCK), :]
    s_w = _nt(q, kw).reshape(NSA_HPG, Q_BLOCK, WINDOW + Q_BLOCK)
    qi_w = lax.broadcasted_iota(jnp.int32, (Q_BLOCK, WINDOW + Q_BLOCK), 0)
    kk_w = lax.broadcasted_iota(jnp.int32, (Q_BLOCK, WINDOW + Q_BLOCK), 1)
    ok_w = (kk_w > qi_w) & (s0 + kk_w >= WINDOW)
    s_w = jnp.where(ok_w[None], s_w, NEG)
    s_w = jnp.concatenate([s_w[:, :, :WINDOW - Q_BLOCK], s_w[:, :, WINDOW - Q_BLOCK:] + tb], axis=-1)
    o_w = _softmax_av(s_w.reshape(rows, WINDOW + Q_BLOCK), vw)

    gt = jax.nn.sigmoid(gt_ref[...])
    glane = lax.broadcasted_iota(jnp.int32, gt.shape, 1)
    for hh in range(NSA_HPG):
        col = 3 * (NSA_HPG * g + hh)
        gate = [jnp.sum(jnp.where(glane == col + br, gt, 0.0), axis=-1, keepdims=True) for br in range(3)]
        rs = slice(hh * Q_BLOCK, (hh + 1) * Q_BLOCK)
        o = gate[0] * o_c[rs, :NSA_HEAD_DIM] + gate[1] * o_s[rs] + gate[2] * o_w[rs]
        o_ref[hh] = o.astype(o_ref.dtype)


def _overlap_t(n_blk_pad, nseg):
    m = np.arange(nseg)[None, :]
    j = np.arange(n_blk_pad)[:, None]
    c_start = CMP_STRIDE * m - CMP_STRIDE
    c_end = CMP_STRIDE * m + CMP_STRIDE - 1
    ov = (c_start < j * SLC_BLOCK + SLC_BLOCK) & (c_end >= j * SLC_BLOCK) & (m >= 1)
    return ov.astype(np.float32)


def _nsa_prompt_attention(q128, p, gate_col_block, kc, vc, ks, vs, kw, vw, tb, lut):
    b, _, t, _ = q128.shape
    nseg = kc.shape[2]
    n_sb = t // SLC_BLOCK
    assert n_sb <= LANES and t % (4 * Q_BLOCK) == 0 and nseg % LANES == 0
    ovt = jnp.asarray(_overlap_t(LANES, nseg), dtype=BF16)
    per_bg = lambda rows, cols: pl.BlockSpec((None, None, rows, cols), lambda b, g, i: (b, g, 0, 0))
    return pl.pallas_call(
        functools.partial(_nsa_prompt_kernel, nseg=nseg, n_sb=n_sb),
        grid=(b, NSA_KV_GROUPS, t // Q_BLOCK),
        in_specs=[pl.BlockSpec((None, NSA_HPG, Q_BLOCK, LANES), lambda b, g, i: (b, g, i, 0)),
                  pl.BlockSpec((None, Q_BLOCK, LANES), lambda b, g, i: (b, i, gate_col_block)),
                  per_bg(nseg, LANES), per_bg(nseg, LANES),
                  per_bg(t, 2 * LANES), per_bg(t, LANES),
                  per_bg(t + WINDOW, LANES), per_bg(t + WINDOW, LANES),
                  pl.BlockSpec((NSA_HPG, Q_BLOCK, 2 * Q_BLOCK), lambda b, g, i: (g, 0, 0)),
                  pl.BlockSpec((NSA_HPG, 8, LANES), lambda b, g, i: (g, 0, 0)),
                  pl.BlockSpec((LANES, nseg), lambda b, g, i: (0, 0))],
        out_specs=pl.BlockSpec((None, NSA_HPG, Q_BLOCK, NSA_HEAD_DIM), lambda b, g, i: (b, g, i, 0)),
        out_shape=jax.ShapeDtypeStruct((b, NSA_HEADS, t, NSA_HEAD_DIM), BF16),
        scratch_shapes=[pltpu.VMEM((NSA_HPG * Q_BLOCK, 2 * LANES), BF16)],
        compiler_params=pltpu.CompilerParams(
            dimension_semantics=("parallel", "parallel", "arbitrary"), vmem_limit_bytes=VMEM_LIMIT),
        name="nsa_prompt_attention",
    )(q128, p, kc, vc, ks, vs, kw, vw, tb, lut, ovt)


def _gdn_conv_kernel(x_ref, w_ref, o_ref, carry_ref, *, tm, tc):
    j = pl.program_id(1)

    @pl.when(pl.program_id(2) == 0)
    def _():
        carry_ref[...] = jnp.zeros(carry_ref.shape, F32)

    x = x_ref[...]
    w = w_ref[...]
    prev = carry_ref[...]
    row8 = lax.broadcasted_iota(jnp.int32, (8, tc), 0)
    conv = x * w[CONV_W - 1:CONV_W, :]
    for sft in range(1, CONV_W):
        xs = pltpu.roll(x, sft, axis=0)
        top = jnp.where(row8 < sft, pltpu.roll(prev, sft, axis=0), xs[0:8])
        xs = top if tm == 8 else jnp.concatenate([top, xs[8:]], axis=0)
        conv = conv + xs * w[CONV_W - 1 - sft:CONV_W - sft, :]
    carry_ref[...] = x[tm - 8:tm, :]
    act = conv * jax.nn.sigmoid(conv)
    for hd in range(tc // GDN_HEAD_DIM):
        sl = slice(hd * GDN_HEAD_DIM, (hd + 1) * GDN_HEAD_DIM)
        a = act[:, sl]
        col0 = j * tc + hd * GDN_HEAD_DIM
        nrm = a * lax.rsqrt(jnp.sum(a * a, axis=-1, keepdims=True) + 1e-6)
        nrm = nrm * jnp.where(col0 < 1024, GDN_HEAD_DIM ** -0.5, 1.0)
        o_ref[:, sl] = jnp.where(col0 < 2048, nrm, a)


def _gdn_conv(p, conv_w):
    b, t, _ = p.shape
    tm = _row_tile(t)
    tc = 512
    return pl.pallas_call(
        functools.partial(_gdn_conv_kernel, tm=tm, tc=tc),
        grid=(b, C_CONV // tc, t // tm),
        in_specs=[pl.BlockSpec((None, tm, tc), lambda b, j, i: (b, i, j)),
                  pl.BlockSpec((CONV_W, tc), lambda b, j, i: (0, j))],
        out_specs=pl.BlockSpec((None, tm, tc), lambda b, j, i: (b, i, j)),
        out_shape=jax.ShapeDtypeStruct((b, t, C_CONV), F32),
        scratch_shapes=[pltpu.VMEM((8, tc), F32)],
        compiler_params=pltpu.CompilerParams(
            dimension_semantics=("parallel", "parallel", "arbitrary"), vmem_limit_bytes=VMEM_LIMIT),
        name="gdn_conv",
    )(p, conv_w)


def _gdn_gate_kernel(ba_ref, alog_ref, dtb_ref, o_ref):
    x = ba_ref[...]
    y = x + dtb_ref[...]
    softplus = jnp.maximum(y, 0.0) + jnp.log1p(jnp.exp(-jnp.abs(y)))
    g = -jnp.exp(alog_ref[...]) * softplus
    lane = lax.broadcasted_iota(jnp.int32, x.shape, 1)
    o_ref[...] = jnp.where(lane < GDN_V_HEADS, jax.nn.sigmoid(x), g)


def _gdn_gates(p, ba_col_block, a_log, dt_bias):
    b, t, _ = p.shape
    tm = _row_tile(t)
    pad = lambda v: jnp.pad(v.reshape(1, GDN_V_HEADS), ((0, 0), (GDN_V_HEADS, LANES - 2 * GDN_V_HEADS)))
    return pl.pallas_call(
        _gdn_gate_kernel,
        grid=(b, t // tm),
        in_specs=[pl.BlockSpec((None, tm, LANES), lambda b, i: (b, i, ba_col_block)),
                  pl.BlockSpec((1, LANES), lambda b, i: (0, 0)),
                  pl.BlockSpec((1, LANES), lambda b, i: (0, 0))],
        out_specs=pl.BlockSpec((None, tm, LANES), lambda b, i: (b, i, 0)),
        out_shape=jax.ShapeDtypeStruct((b, t, LANES), F32),
        compiler_params=pltpu.CompilerParams(dimension_semantics=("parallel", "parallel")),
        name="gdn_gates",
    )(p, pad(a_log), pad(dt_bias))


def _bdot(a, b):
    return jnp.dot(a.astype(BF16), b.astype(BF16), preferred_element_type=F32)


GDN_PACK = 4
_PACK_ORDER = (0, 2, 1, 3)
_PACK_HEADS = tuple(GDN_PACK * p + o for p in range(GDN_V_HEADS // GDN_PACK) for o in _PACK_ORDER)


def _iota2(shape, axis):
    return lax.broadcasted_iota(jnp.int32, shape, axis)


def _packed_mm(a_cat, b_cat, bd_mask):
    b_bd = jnp.where(bd_mask, jnp.concatenate([b_cat] * GDN_PACK, axis=0), 0.0)
    return _bdot(a_cat, b_bd)


def _unit_lower_inverse_packed(ls, row, col, bd_mask):
    eye = (row == col).astype(F32)
    same16 = (row // 16) == (col // 16)
    same32 = (row // 32) == (col // 32)
    ms = [jnp.where(same16, -l, 0.0) for l in ls]
    ps = [eye + m for m in ms]
    for _ in range(3):
        ms = [_packed_mm(m, m, bd_mask) for m in ms]
        ps = [p + _packed_mm(p, m, bd_mask) for p, m in zip(ps, ms)]
    for level in (same32 & jnp.logical_not(same16), jnp.logical_not(same32)):
        ts = [_packed_mm(jnp.where(level, l, 0.0), p, bd_mask) for l, p in zip(ls, ps)]
        ps = [p - _packed_mm(p, t, bd_mask) for p, t in zip(ps, ts)]
    return ps


def _gdn_delta_kernel(act_ref, bg_ref, gt_ref, s0_ref, ltri_ref, lbd_ref, o_ref, s_ref, sbd_ref):
    c, hd = GDN_CHUNK, GDN_HEAD_DIM
    n_packs = GDN_V_HEADS // GDN_PACK
    zero_hd = jnp.zeros((hd, hd), F32)

    @pl.when(pl.program_id(1) == 0)
    def _():
        for pr in range(GDN_V_HEADS // 2):
            h0, h1 = _PACK_HEADS[2 * pr], _PACK_HEADS[2 * pr + 1]
            sbd_ref[pr] = jnp.concatenate([jnp.concatenate([s0_ref[h0], zero_hd], axis=-1),
                                           jnp.concatenate([zero_hd, s0_ref[h1]], axis=-1)], axis=0)

    bg = bg_ref[...]
    cum = sum(jnp.dot(ltri_ref[...], part, preferred_element_type=F32) for part in _split3(bg))
    gcr_all = sum(_nt(part, lbd_ref[...]) for part in _split3(gt_ref[...]))
    row = _iota2((c, GDN_PACK * c), 0)
    lane = _iota2((c, GDN_PACK * c), 1)
    col, slot = lane % c, lane // c
    incl, strict = row >= col, row > col
    bd_mask = (_iota2((4 * c, 4 * c), 0) // c) == (_iota2((4 * c, 4 * c), 1) // c)
    pair_mask = (_iota2((2 * hd, 2 * hd), 0) // hd) == (_iota2((2 * hd, 2 * hd), 1) // hd)
    k_mask = (_iota2((2 * hd, hd), 0) // hd) == (_iota2((2 * hd, hd), 1) // c)
    row_pair = _iota2((2 * hd, 1), 0)

    def slot_cat(cols):
        out = jnp.broadcast_to(cols[3], (c, GDN_PACK * c))
        for x in (2, 1, 0):
            out = jnp.where(slot == x, cols[x], out)
        return out

    def side_by_side(a, b):
        return jnp.concatenate([a, b], axis=-1)

    qs, ks, betas, gcs, lmats, a_ins = [], [], [], [], [], []
    for p in range(n_packs):
        heads = _PACK_HEADS[GDN_PACK * p:GDN_PACK * (p + 1)]
        qa, qb = (act_ref[:, (2 * p + i) * hd:(2 * p + i + 1) * hd] for i in (0, 1))
        ka, kb = (act_ref[:, 1024 + (2 * p + i) * hd:1024 + (2 * p + i + 1) * hd] for i in (0, 1))
        kt = jnp.concatenate([ka, kb], axis=0).T
        k_bd = jnp.where(k_mask, jnp.concatenate([kt, kt], axis=0), 0.0)
        kq = _bdot(jnp.concatenate([side_by_side(ka, kb), side_by_side(qa, qb)], axis=0), k_bd)
        kk = side_by_side(kq[:c], kq[:c])
        qk = side_by_side(kq[c:], kq[c:])
        beta = [bg[:, h:h + 1] for h in heads]
        gc = [cum[:, GDN_V_HEADS + h:GDN_V_HEADS + h + 1] for h in heads]
        decay = jnp.where(incl, jnp.exp(jnp.where(incl, slot_cat(gc) - gcr_all[p:p + 1, :], 0.0)), 0.0)
        lmats.append(jnp.where(strict, slot_cat(beta) * kk * decay, 0.0))
        a_ins.append(qk * decay)
        qs.append((qa, qb, qa, qb)); ks.append((ka, kb, ka, kb)); betas.append(beta); gcs.append(gc)

    tinvs = _unit_lower_inverse_packed(lmats, row, col, bd_mask)

    uws, egs = [], []
    for p in range(n_packs):
        bands = []
        eg = [jnp.exp(g) for g in gcs[p]]
        for x in range(GDN_PACK):
            h = _PACK_HEADS[GDN_PACK * p + x]
            vh = act_ref[:, 2048 + h * hd:2048 + (h + 1) * hd]
            rhs = betas[p][x] * side_by_side(vh, ks[p][x] * eg[x])
            pieces = [jnp.zeros((c, 2 * hd * x), F32)] * (x > 0) + [rhs] + [jnp.zeros((c, 2 * hd * (3 - x)), F32)] * (x < 3)
            bands.append(jnp.concatenate(pieces, axis=-1))
        uws.append(_bdot(tinvs[p], jnp.concatenate(bands, axis=0)))
        egs.append(eg)

    wss, s_olds = [], []
    for p in range(n_packs):
        for pr in range(2):
            x0, x1 = 2 * pr, 2 * pr + 1
            w0, w1 = (uws[p][:, 2 * hd * x + hd:2 * hd * (x + 1)] for x in (x0, x1))
            lhs = jnp.concatenate([side_by_side(w0, w1),
                                   side_by_side(qs[p][x0] * egs[p][x0], qs[p][x1] * egs[p][x1])], axis=0)
            s_old = sbd_ref[2 * p + pr]
            s_olds.append(s_old)
            wss.append(_bdot(lhs, s_old))

    v_news = []
    for p in range(n_packs):
        vn = []
        for x in range(GDN_PACK):
            ws = wss[2 * p + x // 2]
            vn.append(uws[p][:, 2 * hd * x:2 * hd * x + hd] - ws[:c, hd * (x % 2):hd * (x % 2 + 1)])
        v_news.append(vn)
    for p in range(n_packs):
        bands = []
        for x in range(GDN_PACK):
            pieces = [jnp.zeros((c, hd * x), F32)] * (x > 0) + [v_news[p][x]] + [jnp.zeros((c, hd * (3 - x)), F32)] * (x < 3)
            bands.append(jnp.concatenate(pieces, axis=-1))
        av = _bdot(a_ins[p], jnp.concatenate(bands, axis=0))
        for x in range(GDN_PACK):
            h = _PACK_HEADS[GDN_PACK * p + x]
            ws = wss[2 * p + x // 2]
            o_ref[:, h * hd:(h + 1) * hd] = ws[c:, hd * (x % 2):hd * (x % 2 + 1)] + av[:, hd * x:hd * (x + 1)]
    zrows = jnp.zeros((c, 2 * hd), F32)
    for p in range(n_packs):
        for pr in range(2):
            x0, x1 = 2 * pr, 2 * pr + 1
            gl0, gl1 = gcs[p][x0][c - 1:c, :], gcs[p][x1][c - 1:c, :]
            kd = jnp.concatenate([side_by_side(ks[p][x0] * jnp.exp(gl0 - gcs[p][x0]),
                                               ks[p][x1] * jnp.exp(gl1 - gcs[p][x1])), zrows], axis=0)
            kd_t = jnp.concatenate([kd[:, :hd].T, kd[:, hd:].T], axis=0)
            vn = jnp.concatenate([side_by_side(v_news[p][x0], v_news[p][x1]), zrows], axis=0)
            d_last = jnp.where(row_pair < hd, jnp.exp(gl0), jnp.exp(gl1))
            sbd_ref[2 * p + pr] = jnp.where(pair_mask, s_olds[2 * p + pr] * d_last + _bdot(kd_t, vn), 0.0)

    @pl.when(pl.program_id(1) == pl.num_programs(1) - 1)
    def _():
        for pr in range(GDN_V_HEADS // 2):
            s_pair = sbd_ref[pr]
            s_ref[_PACK_HEADS[2 * pr]] = s_pair[:hd, :hd]
            s_ref[_PACK_HEADS[2 * pr + 1]] = s_pair[hd:, hd:]


def _gdn_delta(act, bg, s0):
    b, t, _ = act.shape
    nc = t // GDN_CHUNK
    n_packs = GDN_V_HEADS // GDN_PACK
    wp = GDN_PACK * GDN_CHUNK
    g_rows = bg[:, :, GDN_V_HEADS:2 * GDN_V_HEADS][:, :, np.asarray(_PACK_HEADS)]
    g_rows = g_rows.reshape(b, nc, GDN_CHUNK, n_packs, GDN_PACK).transpose(0, 1, 3, 4, 2).reshape(b, nc, n_packs, wp)
    g_rows = jnp.pad(g_rows, ((0, 0), (0, 0), (0, 8 - n_packs), (0, 0)))
    tri = np.tril(np.ones((GDN_CHUNK, GDN_CHUNK), np.float32))
    ltri = jnp.asarray(tri, dtype=BF16)
    lbd = jnp.asarray(np.kron(np.eye(GDN_PACK, dtype=np.float32), tri), dtype=BF16)
    state_spec = pl.BlockSpec((None, GDN_V_HEADS, GDN_HEAD_DIM, GDN_HEAD_DIM), lambda b, n: (b, 0, 0, 0))
    return pl.pallas_call(
        _gdn_delta_kernel,
        grid=(b, nc),
        in_specs=[pl.BlockSpec((None, GDN_CHUNK, C_CONV), lambda b, n: (b, n, 0)),
                  pl.BlockSpec((None, GDN_CHUNK, LANES), lambda b, n: (b, n, 0)),
                  pl.BlockSpec((None, None, 8, wp), lambda b, n: (b, n, 0, 0)),
                  state_spec,
                  pl.BlockSpec((GDN_CHUNK, GDN_CHUNK), lambda b, n: (0, 0)),
                  pl.BlockSpec((wp, wp), lambda b, n: (0, 0))],
        out_specs=[pl.BlockSpec((None, GDN_CHUNK, GDN_V_W), lambda b, n: (b, n, 0)), state_spec],
        out_shape=[jax.ShapeDtypeStruct((b, t, GDN_V_W), F32),
                   jax.ShapeDtypeStruct(s0.shape, F32)],
        scratch_shapes=[pltpu.VMEM((GDN_V_HEADS // 2, 2 * GDN_HEAD_DIM, 2 * GDN_HEAD_DIM), F32)],
        compiler_params=pltpu.CompilerParams(
            dimension_semantics=("parallel", "arbitrary"), vmem_limit_bytes=VMEM_LIMIT),
        name="gdn_delta_rule",
    )(act, bg, g_rows, s0, ltri, lbd)


SAMPLE_ROWS = NSA_HEADS * 4


def _sample_cmp_kernel(q_ref, kc_ref, vc_ref, lut_ref, ov_ref, oc_ref, un_ref, *, nseg, past_len, n_sb, nq):
    rg = NSA_HPG * nq
    ri = lax.broadcasted_iota(jnp.int32, (rg, nseg), 0)
    mi = lax.broadcasted_iota(jnp.int32, (rg, nseg), 1)
    dist = past_len + ri % nq - CMP_STRIDE * mi - (CMP_STRIDE - 1)
    valid = (dist >= 0) & (mi >= 1)
    jl = lax.broadcasted_iota(jnp.int32, (8, un_ref.shape[-1]), 1)
    q_blk = (past_len + lax.broadcasted_iota(jnp.int32, jl.shape, 0) % nq) // SLC_BLOCK
    forced = (jl == 0) | (jl == q_blk) | (jl == q_blk - 1)
    jf = jl.astype(F32)
    for g in range(NSA_KV_GROUPS):
        sc = _nt(q_ref[g], kc_ref[g])
        tail = sc[:, nseg - LANES:] + _lut_gather(lut_ref[g], dist[:, nseg - LANES:])
        sc = jnp.where(valid, jnp.concatenate([sc[:, :nseg - LANES], tail], axis=-1), NEG)
        mx = jnp.max(sc, axis=-1, keepdims=True)
        e = jnp.where(valid, jnp.exp(sc - mx), 0.0)
        ssum = jnp.sum(e, axis=-1, keepdims=True)
        pc = e / jnp.where(ssum > 0, ssum, 1.0)
        oc_ref[g] = jnp.dot(pc.astype(BF16), vc_ref[g], preferred_element_type=F32)
        pcs = pc
        for hh in range(1, NSA_HPG):
            pcs = pcs + pltpu.roll(pc, hh * nq, axis=0)
        hi, lo = _split2(pcs[0:8])
        imp = jnp.dot(hi, ov_ref[...], preferred_element_type=F32) + jnp.dot(lo, ov_ref[...],
                                                                              preferred_element_type=F32)
        score = jnp.where(forced, 1e4, jnp.where(jl <= q_blk, imp, -1e4))
        score = jnp.where(jl < n_sb, score, -3e38)
        sel = jnp.zeros(score.shape, F32)
        for _ in range(min(N_SELECT, n_sb)):
            mxs = jnp.max(score, axis=-1, keepdims=True)
            idx = jnp.min(jnp.where(score == mxs, jf, 1e9), axis=-1, keepdims=True)
            hit = jf == idx
            sel = jnp.where(hit, 1.0, sel)
            score = jnp.where(hit, -jnp.inf, score)
        un_ref[g] = 1.0 - sel


def _sample_cmp(q16, kc, vc, lut16, past_len, nq):
    b = q16.shape[0]
    nseg = kc.shape[2]
    rg = NSA_HPG * nq
    n_sb = past_len // SLC_BLOCK + 1
    n_sb_pad = -(-n_sb // LANES) * LANES
    assert nq == 4 and nseg * CMP_STRIDE == past_len
    m = np.arange(nseg)[:, None]
    j = np.arange(n_sb_pad)[None, :]
    ov = ((CMP_STRIDE * m - CMP_STRIDE < j * SLC_BLOCK + SLC_BLOCK) & (CMP_STRIDE * m + CMP_STRIDE - 1 >= j * SLC_BLOCK)
          & (m >= 1) & (j < n_sb)).astype(np.float32)
    whole = lambda *shape: pl.BlockSpec((None,) + shape, lambda b: (b,) + (0,) * len(shape))
    return pl.pallas_call(
        functools.partial(_sample_cmp_kernel, nseg=nseg, past_len=past_len, n_sb=n_sb, nq=nq),
        grid=(b,),
        in_specs=[whole(NSA_KV_GROUPS, rg, LANES), whole(NSA_KV_GROUPS, nseg, LANES), whole(NSA_KV_GROUPS, nseg, LANES),
                  pl.BlockSpec((NSA_KV_GROUPS, rg, LANES), lambda b: (0, 0, 0)),
                  pl.BlockSpec((nseg, n_sb_pad), lambda b: (0, 0))],
        out_specs=[whole(NSA_KV_GROUPS, rg, LANES), whole(NSA_KV_GROUPS, 8, n_sb_pad)],
        out_shape=[jax.ShapeDtypeStruct((b, NSA_KV_GROUPS, rg, LANES), F32),
                   jax.ShapeDtypeStruct((b, NSA_KV_GROUPS, 8, n_sb_pad), F32)],
        compiler_params=pltpu.CompilerParams(dimension_semantics=("parallel",), vmem_limit_bytes=VMEM_LIMIT),
        name="nsa_sample_cmp_topk",
    )(q16, kc, vc, lut16, jnp.asarray(ov, dtype=BF16))


def _sample_sel_kernel(tab_ref, *refs, npg, past_len, nq):
    del tab_ref
    pages = refs[:npg]
    qbd_ref, un_ref, ee_ref, far_ref, lut_ref, m_ref, l_ref, acc_ref = refs[npg:]
    c = pl.program_id(1)
    kc = npg * PAGE_SIZE

    @pl.when(c == 0)
    def _():
        m_ref[...] = jnp.full(m_ref.shape, NEG, F32)
        l_ref[...] = jnp.zeros(l_ref.shape, F32)
        acc_ref[...] = jnp.zeros(acc_ref.shape, F32)

    kt = jnp.concatenate([pg[0] for pg in pages], axis=1).astype(BF16)
    vt = jnp.concatenate([pg[1] for pg in pages], axis=1).astype(BF16)
    s = (jnp.dot(qbd_ref[...], kt, preferred_element_type=F32) + far_ref[...][:, 0:1]
         + jnp.dot(un_ref[...], ee_ref[...], preferred_element_type=F32))
    ri = lax.broadcasted_iota(jnp.int32, (SAMPLE_ROWS, LANES), 0)
    li = lax.broadcasted_iota(jnp.int32, (SAMPLE_ROWS, LANES), 1)
    dist = past_len + ri % nq - (c * kc + kc - LANES + li)
    s = jnp.concatenate([s[:, :kc - LANES], s[:, kc - LANES:] + _lut_gather(lut_ref[...], dist)], axis=-1)
    m_old = m_ref[...][:, 0:1]
    m_new = jnp.maximum(m_old, jnp.max(s, axis=-1, keepdims=True))
    alpha = jnp.exp(m_old - m_new)
    p = jnp.exp(s - m_new)
    l_ref[...] = alpha * l_ref[...] + jnp.sum(p, axis=-1, keepdims=True)
    acc_ref[...] = alpha * acc_ref[...] + _nt(p.astype(BF16), vt)
    m_ref[...] = jnp.broadcast_to(m_new, m_ref.shape)


def _sample_sel(pages, table, qbd, unsel_c, farcol, lut64, past_len, nq):
    b, n_pages = table.shape
    npg = min(16, n_pages)
    kc = npg * PAGE_SIZE
    nch = n_pages // npg
    blk_per_chunk = kc // SLC_BLOCK
    ee = np.zeros((LANES, kc), np.float32)
    ee[np.arange(kc) // SLC_BLOCK, np.arange(kc)] = NEG
    assert blk_per_chunk <= LANES

    def page_spec(j):
        return pl.BlockSpec((None, 2, NSA_KV_W // 2, PAGE_SIZE), lambda b, c, tab: (tab[b, c * npg + j], 0, 0, 0))

    const = lambda *shape: pl.BlockSpec(shape, lambda b, c, tab: (0,) * len(shape))
    acc_spec = lambda cols: pl.BlockSpec((None, SAMPLE_ROWS, cols), lambda b, c, tab: (b, 0, 0))
    grid_spec = pltpu.PrefetchScalarGridSpec(
        num_scalar_prefetch=1,
        grid=(b, nch),
        in_specs=[page_spec(j) for j in range(npg)] + [
            pl.BlockSpec((None, SAMPLE_ROWS, 2 * LANES), lambda b, c, tab: (b, 0, 0)),
            pl.BlockSpec((None, None, SAMPLE_ROWS, LANES), lambda b, c, tab: (b, c, 0, 0)),
            const(LANES, kc), const(SAMPLE_ROWS, LANES), const(SAMPLE_ROWS, LANES)],
        out_specs=[acc_spec(LANES), acc_spec(LANES), acc_spec(2 * LANES)],
    )
    return pl.pallas_call(
        functools.partial(_sample_sel_kernel, npg=npg, past_len=past_len, nq=nq),
        grid_spec=grid_spec,
        out_shape=[jax.ShapeDtypeStruct((b, SAMPLE_ROWS, LANES), F32),
                   jax.ShapeDtypeStruct((b, SAMPLE_ROWS, LANES), F32),
                   jax.ShapeDtypeStruct((b, SAMPLE_ROWS, 2 * LANES), F32)],
        compiler_params=pltpu.CompilerParams(
            dimension_semantics=("parallel", "arbitrary"), vmem_limit_bytes=VMEM_LIMIT),
        name="nsa_sample_selected",
    )(table, *([pages] * npg), qbd, unsel_c, jnp.asarray(ee, dtype=BF16), farcol, lut64)


def _own_group_cols(x, grp):
    out = jnp.zeros((x.shape[0], NSA_HEAD_DIM), F32)
    for g in range(NSA_KV_GROUPS):
        out = jnp.where(grp == g, x[:, g * NSA_HEAD_DIM:(g + 1) * NSA_HEAD_DIM], out)
    return out


def _sample_final_kernel(qbd_ref, m_ref, l_ref, acc_ref, snew_ref, wc_ref, wnew_ref, oc_ref, gr_ref, far_ref, lut_ref,
                         o_ref, *, nq, w_buf):
    rows = SAMPLE_ROWS
    qbd = qbd_ref[...]
    far = far_ref[...][:, 0:1]
    lut = lut_ref[...]
    ri = lax.broadcasted_iota(jnp.int32, (rows, LANES), 0)
    li = lax.broadcasted_iota(jnp.int32, (rows, LANES), 1)
    tok = ri % nq
    grp = lax.broadcasted_iota(jnp.int32, (rows, NSA_HEAD_DIM), 0) // (NSA_HPG * nq)

    knew = snew_ref[...]
    s_new = _nt(qbd, knew[:, :256].astype(BF16)) + far
    d_new = tok - li
    s_new = jnp.where((d_new >= 0) & (li < nq), s_new + _lut_gather(lut, d_new), NEG)
    m_old = m_ref[...][:, 0:1]
    m_new = jnp.maximum(m_old, jnp.max(s_new, axis=-1, keepdims=True))
    alpha = jnp.exp(m_old - m_new)
    p_new = jnp.exp(s_new - m_new)
    l_s = alpha * l_ref[...][:, 0:1] + jnp.sum(p_new, axis=-1, keepdims=True)
    acc_s = alpha * acc_ref[...] + jnp.dot(p_new.astype(BF16), knew[:, 256:].astype(BF16), preferred_element_type=F32)
    o_s = _own_group_cols(acc_s, grp) / l_s

    kv_w = jnp.concatenate([wc_ref[...], wnew_ref[...]], axis=0)
    s_w = _nt(qbd, kv_w[:, :256].astype(BF16)) + far
    n_w = w_buf + LANES
    idx = lax.broadcasted_iota(jnp.int32, (rows, n_w), 1)
    d_w = w_buf + lax.broadcasted_iota(jnp.int32, (rows, n_w), 0) % nq - idx
    ok_w = (d_w >= 0) & (d_w < WINDOW) & (idx < w_buf + nq)
    corr = [jnp.zeros((rows, n_w - 2 * LANES), F32)]
    for cidx in range(2):
        lo = n_w - 2 * LANES + cidx * LANES
        corr.append(_lut_gather(lut, d_w[:, lo:lo + LANES]))
    s_w = jnp.where(ok_w, s_w + jnp.concatenate(corr, axis=-1), NEG)
    m_w = jnp.max(s_w, axis=-1, keepdims=True)
    p_w = jnp.exp(s_w - m_w)
    l_w = jnp.sum(p_w, axis=-1, keepdims=True)
    acc_w = jnp.dot(p_w.astype(BF16), kv_w[:, 256:].astype(BF16), preferred_element_type=F32)
    o_w = _own_group_cols(acc_w, grp) / l_w

    gt = jax.nn.sigmoid(gr_ref[...])
    o_ref[...] = gt[:, 0:1] * oc_ref[...][:, :NSA_HEAD_DIM] + gt[:, 1:2] * o_s + gt[:, 2:3] * o_w


def _sample_final(qbd, m, l, acc, snew, wcache, wnew, o_c, graw, farcol, lut64, nq):
    b = qbd.shape[0]
    w_buf = wcache.shape[1]
    assert w_buf == WINDOW
    whole = lambda *shape: pl.BlockSpec((None,) + shape, lambda b: (b,) + (0,) * len(shape))
    const = lambda *shape: pl.BlockSpec(shape, lambda b: (0,) * len(shape))
    return pl.pallas_call(
        functools.partial(_sample_final_kernel, nq=nq, w_buf=w_buf),
        grid=(b,),
        in_specs=[whole(SAMPLE_ROWS, 2 * LANES), whole(SAMPLE_ROWS, LANES), whole(SAMPLE_ROWS, LANES),
                  whole(SAMPLE_ROWS, 2 * LANES), whole(LANES, NSA_KV_W), whole(w_buf, NSA_KV_W), whole(LANES, NSA_KV_W),
                  whole(SAMPLE_ROWS, LANES), whole(SAMPLE_ROWS, LANES),
                  const(SAMPLE_ROWS, LANES), const(SAMPLE_ROWS, LANES)],
        out_specs=whole(SAMPLE_ROWS, NSA_HEAD_DIM),
        out_shape=jax.ShapeDtypeStruct((b, SAMPLE_ROWS, NSA_HEAD_DIM), F32),
        compiler_params=pltpu.CompilerParams(dimension_semantics=("parallel",), vmem_limit_bytes=VMEM_LIMIT),
        name="nsa_sample_final",
    )(qbd, m, l, acc, snew, wcache, wnew, o_c, graw, farcol, lut64)


def _ffn(x, mod, gains, w_in, w_out):
    hid = _norm_mod_swiglu(x, gains[2], mod[3], mod[4], w_in)
    return _matmul_rms_residual(hid, w_out, x, mod[5], gains[3])


def _heads_major(x, n_heads):
    b, t, _ = x.shape
    return x.reshape(b, t, n_heads, NSA_HEAD_DIM).transpose(0, 2, 1, 3)


def _const_cols(vals, lead_shape):
    cols = np.zeros((NSA_HEAD_DIM,), np.float32)
    cols[:len(vals)] = vals
    return jnp.broadcast_to(jnp.asarray(cols, dtype=BF16), tuple(lead_shape) + (NSA_HEAD_DIM,))


def _nsa_prompt(x, mod, gains, w_in, cmp_w, w_out, tb, lut, bias_cols):
    b, t, _ = x.shape
    p = _norm_mod_linear(x, gains[0], mod[0], mod[1], w_in)
    kvc, kvs, kvw = (p[..., 1024 + i * NSA_KV_W:1024 + (i + 1) * NSA_KV_W] for i in range(3))
    q128 = jnp.concatenate([_heads_major(p[..., :NSA_Q_W].astype(BF16), NSA_HEADS),
                            jnp.broadcast_to(bias_cols[None, :, None, :], (b, NSA_HEADS, t, NSA_HEAD_DIM))], axis=-1)
    lead = (b, NSA_KV_GROUPS, t)
    blk_onehot = np.zeros((t, LANES), np.float32)
    blk_onehot[np.arange(t), np.arange(t) // SLC_BLOCK] = NEG
    ks = jnp.concatenate([jnp.broadcast_to(jnp.asarray(blk_onehot, dtype=BF16), lead + (LANES,)),
                          _heads_major(kvs[..., :256].astype(BF16), NSA_KV_GROUPS),
                          _const_cols([1.0, 1.0], lead)], axis=-1)
    vs = jnp.concatenate([_heads_major(kvs[..., 256:].astype(BF16), NSA_KV_GROUPS), _const_cols([1.0], lead)], axis=-1)
    front = ((0, 0), (0, 0), (WINDOW, 0), (0, 0))
    kw = jnp.pad(jnp.concatenate([_heads_major(kvw[..., :256].astype(BF16), NSA_KV_GROUPS),
                                  _const_cols([1.0, 1.0], lead)], axis=-1), front)
    vw = jnp.pad(jnp.concatenate([_heads_major(kvw[..., 256:].astype(BF16), NSA_KV_GROUPS),
                                  _const_cols([1.0], lead)], axis=-1), front)
    n_pages = t // PAGE_SIZE
    table = jnp.arange(b * n_pages, dtype=jnp.int32).reshape(b, n_pages)
    kc, vc = _compress(kvc.reshape(b * n_pages, SEGS_PER_PAGE, SEG_W), table, *cmp_w)
    o = _nsa_prompt_attention(q128, p, (NSA_Q_W + 3 * NSA_KV_W) // LANES, kc, vc, ks, vs, kw, vw, tb, lut)
    o = o.transpose(0, 2, 1, 3).reshape(b, t, NSA_Q_W)
    x = _matmul_rms_residual(o, w_out, x, mod[2], gains[1])
    shape5 = (b, t, 2, NSA_KV_GROUPS, NSA_HEAD_DIM)
    return x, kvc.reshape(shape5), kvs.reshape(shape5), kvw.reshape(shape5)[:, -min(WINDOW, t):]


def _nsa_sample(x, mod, gains, w_in, cmp_w, w_out, lut, rel_bias, bias_cols, cache_cmp, cache_slc, cache_win,
                page_table, db, nq):
    n_pages = page_table.shape[1]
    past_len = n_pages * PAGE_SIZE
    rows = db * nq
    p = _norm_mod_linear(x, gains[0], mod[0], mod[1], w_in)[0]
    kvc, kvs, kvw = (p[:, 1024 + i * NSA_KV_W:1024 + (i + 1) * NSA_KV_W] for i in range(3))
    qh = p[:, :NSA_Q_W].astype(BF16).reshape(db, nq, NSA_KV_GROUPS, NSA_HPG, NSA_HEAD_DIM).transpose(0, 2, 3, 1, 4)
    q16 = jnp.concatenate([qh, jnp.broadcast_to(bias_cols.reshape(1, NSA_KV_GROUPS, NSA_HPG, 1, NSA_HEAD_DIM), qh.shape)],
                          axis=-1).reshape(db, NSA_KV_GROUPS, NSA_HPG * nq, LANES)
    eye_g = jnp.eye(NSA_KV_GROUPS, dtype=BF16)
    qbd = jnp.einsum('bghtd,gj->bghtjd', qh, eye_g).reshape(db, SAMPLE_ROWS, NSA_KV_GROUPS * NSA_HEAD_DIM)
    row_head = np.repeat(np.arange(NSA_HEADS), nq)
    lut64 = lut[:, 0, :][row_head]
    farcol = jnp.broadcast_to(rel_bias[N_BUCKETS - 1][row_head][:, None], (SAMPLE_ROWS, LANES))
    rows_minor = lambda cache: jnp.transpose(cache, (0, 2, 3, 4, 1))
    kc, vc = _compress(rows_minor(cache_cmp).reshape(-1, 2, 2, LANES, PAGE_SIZE), page_table, *cmp_w)
    o_c, unsel = _sample_cmp(q16, kc, vc, lut64.reshape(NSA_KV_GROUPS, NSA_HPG * nq, LANES), past_len, nq)
    npg = min(16, n_pages)
    nch = n_pages // npg
    bpc = npg * PAGE_SIZE // SLC_BLOCK
    un = unsel[:, :, :nq, :past_len // SLC_BLOCK].reshape(db, NSA_KV_GROUPS, 1, nq, nch, bpc)
    un = jnp.broadcast_to(un, (db, NSA_KV_GROUPS, NSA_HPG, nq, nch, bpc)).transpose(0, 4, 1, 2, 3, 5)
    un = jnp.pad(un.reshape(db, nch, SAMPLE_ROWS, bpc), ((0, 0), (0, 0), (0, 0), (0, LANES - bpc))).astype(BF16)
    m, l, acc = _sample_sel(rows_minor(cache_slc).reshape(-1, 2, NSA_KV_W // 2, PAGE_SIZE), page_table, qbd, un, farcol,
                            lut64, past_len, nq)
    pad_rows = lambda a: jnp.pad(a.reshape(db, nq, NSA_KV_W), ((0, 0), (0, LANES - nq), (0, 0)))
    wcache = cache_win.reshape(db, -1, NSA_KV_W)
    graw = p[:, NSA_Q_W + 3 * NSA_KV_W:NSA_Q_W + 3 * NSA_KV_W + 3 * NSA_HEADS]
    graw = graw.reshape(db, nq, NSA_HEADS, 3).transpose(0, 2, 1, 3).reshape(db, SAMPLE_ROWS, 3)
    graw = jnp.pad(graw, ((0, 0), (0, 0), (0, LANES - 3)))
    o = _sample_final(qbd, m, l, acc, pad_rows(kvs), wcache, pad_rows(kvw), o_c.reshape(db, SAMPLE_ROWS, LANES), graw,
                      farcol, lut64, nq)
    o = o.reshape(db, NSA_HEADS, nq, NSA_HEAD_DIM).transpose(0, 2, 1, 3).reshape(1, rows, NSA_Q_W)
    x = _matmul_rms_residual(o, w_out, x, mod[2], gains[1])
    shape5 = (db, nq, 2, NSA_KV_GROUPS, NSA_HEAD_DIM)
    kv_win = jnp.concatenate([cache_win, kvw.reshape(shape5)], axis=1)[:, -cache_win.shape[1]:]
    return x, kvc.reshape(shape5), kvs.reshape(shape5), kv_win


def _gdn_prompt(x, mod, gains, w_in, conv_w, a_log, dt_bias, norm_w, w_out):
    b, t, _ = x.shape
    p = _norm_mod_linear(x, gains[0], mod[0], mod[1], w_in)
    act = _gdn_conv(p, conv_w)
    bg = _gdn_gates(p, (C_CONV + GDN_V_W) // LANES, a_log, dt_bias)
    s0 = jnp.zeros((b, GDN_V_HEADS, GDN_HEAD_DIM, GDN_HEAD_DIM), F32)
    o, s_fin = _gdn_delta(act, bg, s0)
    x = _gdn_out(o, p, C_CONV // GDN_V_W, norm_w, w_out, x, mod[2], gains[1])
    return x, p[:, t - (CONV_W - 1):, :C_CONV], s_fin


def _gdn_sample(x, mod, gains, w_in, conv_w, a_log, dt_bias, norm_w, w_out, conv_buf, s0, db, nq):
    p = _norm_mod_linear(x, gains[0], mod[0], mod[1], w_in)
    qkv = p[0, :, :C_CONV].reshape(db, nq, C_CONV)
    xp = jnp.concatenate([conv_buf, qkv], axis=1)
    act = _gdn_conv(jnp.pad(xp, ((0, 0), (0, 8 - xp.shape[1]), (0, 0))), conv_w)[:, CONV_W - 1:CONV_W - 1 + nq]
    bg = _gdn_gates(p, (C_CONV + GDN_V_W) // LANES, a_log, dt_bias).reshape(db, nq, LANES)
    pad_t = ((0, 0), (0, GDN_CHUNK - nq), (0, 0))
    o, s_fin = _gdn_delta(jnp.pad(act, pad_t), jnp.pad(bg, pad_t), s0)
    o = o[:, :nq].reshape(1, db * nq, GDN_V_W)
    x = _gdn_out(o, p, C_CONV // GDN_V_W, norm_w, w_out, x, mod[2], gains[1])
    return x, xp[:, -(CONV_W - 1):], s_fin


def kernel(x_prompt, x_sample, c_prompt, c_sample, cache_kv_cmp, cache_kv_slc, cache_kv_win, state_conv, state_ssm,
           page_table, rel_bias, norm_gains, w_ada, b_ada, w_ffn_in, w_ffn_out, nsa_w_in, nsa_cmp_pe, nsa_cmp_w1,
           nsa_cmp_b1, nsa_cmp_w2, nsa_w_out, gdn_w_in, gdn_conv_w, gdn_a_log, gdn_dt_bias, gdn_norm_w, gdn_w_out):
    depth = w_ada.shape[0]
    bp, t, d = x_prompt.shape
    db, nq, _ = x_sample.shape
    assert nq + CONV_W - 1 <= 8 and nq <= GDN_CHUNK

    c_all = jnp.concatenate([c_prompt, c_sample], axis=0)
    rows_pad = -(-c_all.shape[0] // 8) * 8
    ada = _adaln(jnp.pad(c_all, ((0, rows_pad - c_all.shape[0]), (0, 0))), w_ada, b_ada)
    ada = ada.reshape(depth, rows_pad, 6, d)
    tb, lut = _bias_tables(rel_bias)
    far_hi, far_lo = _split2(rel_bias[N_BUCKETS - 1])
    bias_cols = jnp.zeros((NSA_HEADS, NSA_HEAD_DIM), BF16).at[:, 0].set(far_hi).at[:, 1].set(far_lo)

    xp = x_prompt
    xs = x_sample.reshape(1, db * nq, d)
    kvc_p, kvc_s, kvs_p, kvs_s, kvw_p, kvw_s, cv_p, cv_s, ss_p, ss_s = ([] for _ in range(10))
    for i in range(depth):
        mod_p = [ada[i, :bp, k][:, None, :] for k in range(6)]
        mod_s = [jnp.repeat(ada[i, bp:bp + db, k], nq, axis=0)[None] for k in range(6)]
        gains = norm_gains[i]
        l = i // 2
        if i % 2 == 0:
            w_in = jnp.concatenate([nsa_w_in[l][:, :NSA_Q_W] * (NSA_HEAD_DIM ** -0.5), nsa_w_in[l][:, NSA_Q_W:]], axis=1)
            w_in = jnp.pad(w_in, ((0, 0), (0, -w_in.shape[1] % LANES))).astype(BF16)
            cmp_w = _compress_weights(nsa_cmp_pe[l], nsa_cmp_w1[l], nsa_cmp_b1[l], nsa_cmp_w2[l])
            w_out = nsa_w_out[l].astype(BF16)
            xp, a, bq, cq = _nsa_prompt(xp, mod_p, gains, w_in, cmp_w, w_out, tb, lut, bias_cols)
            kvc_p.append(a); kvs_p.append(bq); kvw_p.append(cq)
            xs, a, bq, cq = _nsa_sample(xs, mod_s, gains, w_in, cmp_w, w_out, lut, rel_bias, bias_cols, cache_kv_cmp[l],
                                        cache_kv_slc[l], cache_kv_win[l], page_table, db, nq)
            kvc_s.append(a); kvs_s.append(bq); kvw_s.append(cq)
        else:
            w_in = jnp.pad(gdn_w_in[l], ((0, 0), (0, -gdn_w_in.shape[2] % LANES))).astype(BF16)
            gdn_w = (w_in, gdn_conv_w[l], gdn_a_log[l], gdn_dt_bias[l], gdn_norm_w[l], gdn_w_out[l].astype(BF16))
            xp, a, bq = _gdn_prompt(xp, mod_p, gains, *gdn_w)
            cv_p.append(a); ss_p.append(bq)
            xs, a, bq = _gdn_sample(xs, mod_s, gains, *gdn_w, state_conv[l], state_ssm[l], db, nq)
            cv_s.append(a); ss_s.append(bq)
        w_ffn = (w_ffn_in[i].astype(BF16), w_ffn_out[i].astype(BF16))
        xp = _ffn(xp, mod_p, gains, *w_ffn)
        xs = _ffn(xs, mod_s, gains, *w_ffn)
    return (xp, xs.reshape(db, nq, d), jnp.stack(kvc_p), jnp.stack(kvc_s), jnp.stack(kvs_p), jnp.stack(kvs_s),
            jnp.stack(kvw_p), jnp.stack(kvw_s), jnp.stack(cv_p), jnp.stack(cv_s), jnp.stack(ss_p), jnp.stack(ss_s))
```

```python
import functools
import math

import numpy as np
import jax
import jax.numpy as jnp
from jax import lax
from jax.experimental import pallas as pl
from jax.experimental.pallas import tpu as pltpu

F32 = jnp.float32
BF16 = jnp.bfloat16

D_MODEL = 1024
RMS_EPS = 1e-6
D_FF = 2816
NSA_HEADS = 16
NSA_HEAD_DIM = 64
NSA_KV_GROUPS = 4
NSA_HPG = 4
CMP_BLOCK = 32
CMP_STRIDE = 16
CMP_HID = 256
SLC_BLOCK = 64
N_SELECT = 16
WINDOW = 512
Q_BLOCK = 128
PAGE_SIZE = 128
N_BUCKETS = 32
GDN_QK_HEADS = 8
GDN_V_HEADS = 16
GDN_HEAD_DIM = 128
CONV_W = 4
GDN_CHUNK = 64
NSA_Q_W = 1024
NSA_KV_W = 512
C_CONV = 4096
GDN_V_W = 2048

LANES = 128
SEG_W = CMP_STRIDE * NSA_KV_W
SEGS_PER_PAGE = PAGE_SIZE // CMP_STRIDE
NEG = -1e30
VMEM_LIMIT = 48 * 1024 * 1024

_BUCKET_THR = (19, 21, 24, 27, 31, 35, 40, 46, 52, 59, 67, 77, 87, 99, 113)
FAR_DIST = 128


def _nt(a, b):
    return lax.dot_general(a, b, (((1,), (1,)), ((), ())), preferred_element_type=F32)


def _split2(x):
    hi = x.astype(BF16)
    lo = (x - hi.astype(F32)).astype(BF16)
    return hi, lo


def _split3(x):
    hi = x.astype(BF16)
    r = x - hi.astype(F32)
    mid = r.astype(BF16)
    lo = (r - mid.astype(F32)).astype(BF16)
    return hi, mid, lo


def _pick_tn(n):
    units = n // LANES
    best = 1
    for d in range(1, units + 1):
        if units % d == 0 and d * LANES <= 1536:
            best = d
    return best * LANES


def _adaln_kernel(c_ref, w_ref, b_ref, o_ref):
    c = c_ref[...]
    a = (c * jax.nn.sigmoid(c)).astype(BF16)
    o_ref[...] = jnp.dot(a, w_ref[...].astype(BF16), preferred_element_type=F32) + b_ref[...]


def _adaln(c_all, w_ada, b_ada):
    depth, d, n = w_ada.shape
    rows = c_all.shape[0]
    tn = 768
    return pl.pallas_call(
        _adaln_kernel,
        grid=(depth, n // tn),
        in_specs=[pl.BlockSpec((rows, d), lambda l, j: (0, 0)),
                  pl.BlockSpec((None, d, tn), lambda l, j: (l, 0, j)),
                  pl.BlockSpec((None, 1, tn), lambda l, j: (l, 0, j))],
        out_specs=pl.BlockSpec((None, rows, tn), lambda l, j: (l, 0, j)),
        out_shape=jax.ShapeDtypeStruct((depth, rows, n), F32),
        compiler_params=pltpu.CompilerParams(dimension_semantics=("parallel", "parallel")),
        name="adaln",
    )(c_all, w_ada, b_ada.reshape(depth, 1, n))


def _mod_norm(x, gain, shift, scale):
    ms = jnp.mean(x * x, axis=-1, keepdims=True)
    y = x * lax.rsqrt(ms + RMS_EPS) * gain
    return y * (1.0 + scale) + shift


def _nml_kernel(x_ref, g_ref, sh_ref, sc_ref, w_ref, o_ref, h_ref):
    @pl.when(pl.program_id(2) == 0)
    def _():
        h_ref[...] = _mod_norm(x_ref[...], g_ref[...], sh_ref[...], sc_ref[...]).astype(BF16)

    o_ref[...] = jnp.dot(h_ref[...], w_ref[...], preferred_element_type=F32).astype(o_ref.dtype)


def _nml_swiglu_kernel(x_ref, g_ref, sh_ref, sc_ref, wg_ref, wu_ref, o_ref, h_ref):
    @pl.when(pl.program_id(2) == 0)
    def _():
        h_ref[...] = _mod_norm(x_ref[...], g_ref[...], sh_ref[...], sc_ref[...]).astype(BF16)

    h = h_ref[...]
    gate = jnp.dot(h, wg_ref[...], preferred_element_type=F32)
    up = jnp.dot(h, wu_ref[...], preferred_element_type=F32)
    o_ref[...] = (gate * jax.nn.sigmoid(gate) * up).astype(o_ref.dtype)


def _mod_spec(mod, tm):
    if mod.shape[1] == 1:
        return pl.BlockSpec((None, 1, mod.shape[2]), lambda b, i, *_: (b, 0, 0))
    return pl.BlockSpec((None, tm, mod.shape[2]), lambda b, i, *_: (b, i, 0))


def _row_tile(t):
    return 512 if t % 512 == 0 else t


def _norm_mod_linear(x, gain, shift, scale, w, out_dtype=F32):
    b, t, d = x.shape
    n = w.shape[1]
    tm, tn = _row_tile(t), _pick_tn(n)
    return pl.pallas_call(
        _nml_kernel,
        grid=(b, t // tm, n // tn),
        in_specs=[pl.BlockSpec((None, tm, d), lambda b, i, j: (b, i, 0)),
                  pl.BlockSpec((1, d), lambda b, i, j: (0, 0)),
                  _mod_spec(shift, tm), _mod_spec(scale, tm),
                  pl.BlockSpec((d, tn), lambda b, i, j: (0, j))],
        out_specs=pl.BlockSpec((None, tm, tn), lambda b, i, j: (b, i, j)),
        out_shape=jax.ShapeDtypeStruct((b, t, n), out_dtype),
        scratch_shapes=[pltpu.VMEM((tm, d), BF16)],
        compiler_params=pltpu.CompilerParams(
            dimension_semantics=("parallel", "parallel", "arbitrary"), vmem_limit_bytes=VMEM_LIMIT),
        name="norm_mod_linear",
    )(x, gain.reshape(1, d), shift, scale, w)


def _norm_mod_swiglu(x, gain, shift, scale, w_in):
    b, t, d = x.shape
    nf = w_in.shape[1] // 2
    tm, tn = _row_tile(t), _pick_tn(nf)
    nj = nf // tn
    return pl.pallas_call(
        _nml_swiglu_kernel,
        grid=(b, t // tm, nj),
        in_specs=[pl.BlockSpec((None, tm, d), lambda b, i, j: (b, i, 0)),
                  pl.BlockSpec((1, d), lambda b, i, j: (0, 0)),
                  _mod_spec(shift, tm), _mod_spec(scale, tm),
                  pl.BlockSpec((d, tn), lambda b, i, j: (0, j)),
                  pl.BlockSpec((d, tn), lambda b, i, j: (0, j + nj))],
        out_specs=pl.BlockSpec((None, tm, tn), lambda b, i, j: (b, i, j)),
        out_shape=jax.ShapeDtypeStruct((b, t, nf), BF16),
        scratch_shapes=[pltpu.VMEM((tm, d), BF16)],
        compiler_params=pltpu.CompilerParams(
            dimension_semantics=("parallel", "parallel", "arbitrary"), vmem_limit_bytes=VMEM_LIMIT),
        name="norm_mod_swiglu",
    )(x, gain.reshape(1, d), shift, scale, w_in, w_in)


def _rms_gated_residual(y, x, gate, gain):
    ms = jnp.mean(y * y, axis=-1, keepdims=True)
    return x + gate * (y * lax.rsqrt(ms + RMS_EPS) * gain)


def _mrr_kernel(a_ref, w_ref, x_ref, gate_ref, gain_ref, o_ref):
    y = jnp.dot(a_ref[...].astype(BF16), w_ref[...], preferred_element_type=F32)
    o_ref[...] = _rms_gated_residual(y, x_ref[...], gate_ref[...], gain_ref[...])


def _matmul_rms_residual(a, w, x, gate, gain):
    b, t, k = a.shape
    d = w.shape[1]
    tm = _row_tile(t)
    return pl.pallas_call(
        _mrr_kernel,
        grid=(b, t // tm),
        in_specs=[pl.BlockSpec((None, tm, k), lambda b, i: (b, i, 0)),
                  pl.BlockSpec((k, d), lambda b, i: (0, 0)),
                  pl.BlockSpec((None, tm, d), lambda b, i: (b, i, 0)),
                  _mod_spec(gate, tm),
                  pl.BlockSpec((1, d), lambda b, i: (0, 0))],
        out_specs=pl.BlockSpec((None, tm, d), lambda b, i: (b, i, 0)),
        out_shape=jax.ShapeDtypeStruct((b, t, d), F32),
        compiler_params=pltpu.CompilerParams(
            dimension_semantics=("parallel", "parallel"), vmem_limit_bytes=VMEM_LIMIT),
        name="matmul_rms_residual",
    )(a, w, x, gate, gain.reshape(1, d))


def _gdn_out_kernel(o_ref, z_ref, nw_ref, w_ref, x_ref, gate_ref, gain_ref, out_ref, a_ref):
    nw = nw_ref[...]
    for h in range(GDN_V_HEADS):
        sl = slice(h * GDN_HEAD_DIM, (h + 1) * GDN_HEAD_DIM)
        o = o_ref[:, sl]
        z = z_ref[:, sl]
        ms = jnp.mean(o * o, axis=-1, keepdims=True)
        a_ref[:, sl] = ((o * lax.rsqrt(ms + RMS_EPS) * nw) * (z * jax.nn.sigmoid(z))).astype(BF16)
    y = jnp.dot(a_ref[...], w_ref[...], preferred_element_type=F32)
    out_ref[...] = _rms_gated_residual(y, x_ref[...], gate_ref[...], gain_ref[...])


def _gdn_out(o, p, z_col_block, norm_w, w, x, gate, gain):
    b, t, k = o.shape
    d = w.shape[1]
    tm = _row_tile(t)
    return pl.pallas_call(
        _gdn_out_kernel,
        grid=(b, t // tm),
        in_specs=[pl.BlockSpec((None, tm, k), lambda b, i: (b, i, 0)),
                  pl.BlockSpec((None, tm, k), lambda b, i: (b, i, z_col_block)),
                  pl.BlockSpec((1, GDN_HEAD_DIM), lambda b, i: (0, 0)),
                  pl.BlockSpec((k, d), lambda b, i: (0, 0)),
                  pl.BlockSpec((None, tm, d), lambda b, i: (b, i, 0)),
                  _mod_spec(gate, tm),
                  pl.BlockSpec((1, d), lambda b, i: (0, 0))],
        out_specs=pl.BlockSpec((None, tm, d), lambda b, i: (b, i, 0)),
        out_shape=jax.ShapeDtypeStruct((b, t, d), F32),
        scratch_shapes=[pltpu.VMEM((tm, k), BF16)],
        compiler_params=pltpu.CompilerParams(
            dimension_semantics=("parallel", "parallel"), vmem_limit_bytes=VMEM_LIMIT),
        name="gdn_out",
    )(o, p, norm_w.reshape(1, GDN_HEAD_DIM), w, x, gate, gain.reshape(1, d))


def _bucket_of(n):
    big = jnp.full(n.shape, 16, jnp.int32)
    for thr in _BUCKET_THR:
        big = big + (n >= thr).astype(jnp.int32)
    return jnp.where(n < 16, n, big)


def _bias_tab_kernel(tbl_ref, tb_ref, lut_ref):
    h = pl.program_id(0)
    far = tbl_ref[N_BUCKETS - 1, h]

    def lookup(dist):
        bkt = _bucket_of(jnp.maximum(dist, 0))
        out = jnp.zeros(dist.shape, F32)
        for bb in range(N_BUCKETS):
            out = jnp.where(bkt == bb, tbl_ref[bb, h], out)
        return out - far

    qi = lax.broadcasted_iota(jnp.int32, (Q_BLOCK, 2 * Q_BLOCK), 0)
    kj = lax.broadcasted_iota(jnp.int32, (Q_BLOCK, 2 * Q_BLOCK), 1)
    dist = Q_BLOCK + qi - kj
    tb_ref[...] = jnp.where(dist >= 0, lookup(dist), NEG)
    lut_ref[...] = lookup(lax.broadcasted_iota(jnp.int32, (8, LANES), 1))


def _bias_tables(rel_bias):
    return pl.pallas_call(
        _bias_tab_kernel,
        grid=(NSA_HEADS,),
        in_specs=[pl.BlockSpec(memory_space=pltpu.SMEM)],
        out_specs=[pl.BlockSpec((None, Q_BLOCK, 2 * Q_BLOCK), lambda h: (h, 0, 0)),
                   pl.BlockSpec((None, 8, LANES), lambda h: (h, 0, 0))],
        out_shape=[jax.ShapeDtypeStruct((NSA_HEADS, Q_BLOCK, 2 * Q_BLOCK), F32),
                   jax.ShapeDtypeStruct((NSA_HEADS, 8, LANES), F32)],
        compiler_params=pltpu.CompilerParams(dimension_semantics=("parallel",)),
        name="bias_tables",
    )(rel_bias)


def _lut_gather(lut_rows, dist):
    idx = jnp.clip(dist, 0, LANES - 1)
    val = jnp.take_along_axis(lut_rows, idx, axis=1)
    return jnp.where((dist >= 0) & (dist < FAR_DIST), val, 0.0)


def _pe_term_kernel(pe_ref, wbd_ref, b1_ref, o_ref):
    y = jnp.dot(pe_ref[...], wbd_ref[...], preferred_element_type=F32)
    o_ref[...] = y[:, 0:CMP_HID] + y[:, 3 * CMP_HID:4 * CMP_HID] + b1_ref[...]


def _pe_term(pe_x, wbd, b1):
    return pl.pallas_call(
        _pe_term_kernel,
        grid=(2,),
        in_specs=[pl.BlockSpec((None, 8, 2048), lambda k: (k, 0, 0)),
                  pl.BlockSpec((None, 2048, 1024), lambda k: (k, 0, 0)),
                  pl.BlockSpec((None, 1, CMP_HID), lambda k: (k, 0, 0))],
        out_specs=pl.BlockSpec((None, 8, CMP_HID), lambda k: (k, 0, 0)),
        out_shape=jax.ShapeDtypeStruct((2, 8, CMP_HID), F32),
        compiler_params=pltpu.CompilerParams(dimension_semantics=("parallel",), vmem_limit_bytes=VMEM_LIMIT),
        name="cmp_pe_term",
    )(pe_x, wbd, b1.reshape(2, 1, CMP_HID))


def _compress_kernel(tab_ref, *refs, npg, rows_minor):
    del tab_ref
    pages = refs[:npg]
    wbd_ref, w2_ref, pe_ref, cc_ref, kc_ref, vc_ref, xs_ref, carry_ref = refs[npg:npg + 8]
    ts = SEGS_PER_PAGE * npg

    @pl.when(pl.program_id(1) == 0)
    def _():
        carry_ref[...] = jnp.zeros(carry_ref.shape, F32)

    row0 = lax.broadcasted_iota(jnp.int32, (ts, CMP_HID), 0) == 0
    low_half = lax.broadcasted_iota(jnp.int32, (ts, LANES), 1) < NSA_HEAD_DIM
    half = CMP_STRIDE * NSA_HEAD_DIM
    for k in range(2):
        out_ref = kc_ref if k == 0 else vc_ref
        for gp in range(2):
            if rows_minor:
                rt_ref = refs[npg + 8].at[gp]
                xb_ref = xs_ref.at[gp]
                for j, pg in enumerate(pages):
                    rt_ref[j * PAGE_SIZE:(j + 1) * PAGE_SIZE, :] = pg[k, gp].astype(BF16).T.astype(F32)
                for j in range(CMP_STRIDE // 2):
                    ra = rt_ref[pl.ds(2 * j, ts, stride=CMP_STRIDE), :]
                    rb = rt_ref[pl.ds(2 * j + 1, ts, stride=CMP_STRIDE), :]
                    lanes = slice(j * LANES, (j + 1) * LANES)
                    xb_ref[:, lanes] = jnp.where(low_half, ra, pltpu.roll(rb, NSA_HEAD_DIM, axis=1)).astype(BF16)
                    xb_ref[:, half + j * LANES:half + (j + 1) * LANES] = jnp.where(
                        low_half, pltpu.roll(ra, NSA_HEAD_DIM, axis=1), rb).astype(BF16)
                ys = [jnp.dot(xb_ref[:, g2 * half:(g2 + 1) * half], wbd_ref[k], preferred_element_type=F32)
                      for g2 in range(2)]
            else:
                xb_ref = xs_ref.at[gp]
                off = k * 256 + gp * LANES
                for s in range(CMP_STRIDE):
                    lo = s * NSA_KV_W + off
                    piece = jnp.concatenate([pg[:, lo:lo + LANES] for pg in pages], axis=0)
                    xb_ref[:, s * LANES:(s + 1) * LANES] = piece.astype(BF16)
                y = jnp.dot(xb_ref[...], wbd_ref[k], preferred_element_type=F32)
                ys = [y[:, :2 * CMP_HID], y[:, 2 * CMP_HID:]]
            hs = []
            for g2 in range(2):
                pa = ys[g2][:, :CMP_HID]
                pb = ys[g2][:, CMP_HID:]
                ci = (k * 2 + gp) * 2 + g2
                prev = carry_ref[ci]
                pa_prev = jnp.where(row0, prev[7:8, :], pltpu.roll(pa, 1, axis=0))
                carry_ref[ci] = pa[ts - 8:ts, :]
                hs.append(jax.nn.gelu(pa_prev + pb + pe_ref[k, 0:1, :]))
            hid = jnp.concatenate(hs, axis=-1).astype(BF16)
            o = jnp.dot(hid, w2_ref[k], preferred_element_type=F32) + cc_ref[k]
            out_ref[2 * gp] = o[:, :LANES].astype(BF16)
            out_ref[2 * gp + 1] = o[:, LANES:].astype(BF16)


def _compress(pages, table, wbd, w1cat, w2bd, pe_term, ccols):
    bc, n_pages = table.shape
    npg = min(32, n_pages)
    ts = SEGS_PER_PAGE * npg
    nseg = n_pages * SEGS_PER_PAGE
    rows_minor = pages.ndim == 5
    w_first = w1cat if rows_minor else wbd
    page_block = (None,) + pages.shape[1:]

    def page_spec(j):
        return pl.BlockSpec(page_block, lambda b, i, tab: (tab[b, i * npg + j],) + (0,) * (pages.ndim - 1))

    const = lambda *shape: pl.BlockSpec(shape, lambda b, i, tab: (0,) * len(shape), pipeline_mode=pl.Buffered(1))
    out_spec = pl.BlockSpec((None, NSA_KV_GROUPS, ts, LANES), lambda b, i, tab: (b, 0, i, 0))
    grid_spec = pltpu.PrefetchScalarGridSpec(
        num_scalar_prefetch=1,
        grid=(bc, n_pages // npg),
        in_specs=[page_spec(j) for j in range(npg)] + [
            const(*w_first.shape), const(2, 512, 256), const(2, 8, CMP_HID), const(2, 1, 256)],
        out_specs=[out_spec, out_spec],
        scratch_shapes=[pltpu.VMEM((2, ts, 2048), BF16), pltpu.VMEM((8, 8, CMP_HID), F32)] + (
            [pltpu.VMEM((2, npg * PAGE_SIZE, LANES), F32)] if rows_minor else []),
    )
    out_sds = jax.ShapeDtypeStruct((bc, NSA_KV_GROUPS, nseg, LANES), BF16)
    return pl.pallas_call(
        functools.partial(_compress_kernel, npg=npg, rows_minor=rows_minor),
        grid_spec=grid_spec,
        out_shape=[out_sds, out_sds],
        compiler_params=pltpu.CompilerParams(
            dimension_semantics=("parallel", "arbitrary"), vmem_limit_bytes=VMEM_LIMIT),
        name="kv_compress",
    )(table, *([pages] * npg), w_first, w2bd, pe_term, ccols)


def _compress_weights(pe, w1, b1, w2):
    eye2 = jnp.eye(2, dtype=F32)
    w = w1.reshape(2, 2, CMP_STRIDE, NSA_HEAD_DIM, CMP_HID)
    wbd = jnp.einsum('kasdh,gj->ksgdjah', w, eye2).reshape(2, 2048, 1024).astype(BF16)
    w1cat = w.transpose(0, 2, 3, 1, 4).reshape(2, CMP_STRIDE * NSA_HEAD_DIM, 2 * CMP_HID).astype(BF16)
    w2p = jnp.pad(w2, ((0, 0), (0, 0), (0, LANES - NSA_HEAD_DIM)))
    w2bd = jnp.einsum('khd,gj->kghjd', w2p, eye2).reshape(2, 512, 256).astype(BF16)
    pe_x = pe.reshape(2, 2, CMP_STRIDE, NSA_HEAD_DIM).transpose(0, 2, 1, 3).reshape(2, 1, 2048)
    pe_x = jnp.broadcast_to(pe_x, (2, 8, 2048)).astype(BF16)
    pe_term = _pe_term(pe_x, wbd, b1)
    cc = np.zeros((2, 1, 256), np.float32)
    for g2 in range(2):
        cc[0, 0, g2 * LANES + 64] = 1.0
        cc[0, 0, g2 * LANES + 65] = 1.0
        cc[1, 0, g2 * LANES + 64] = 1.0
    return wbd, w1cat, w2bd, pe_term, jnp.asarray(cc)


def _topk_rows_mask(score, k, fillers=()):
    blk = lax.broadcasted_iota(jnp.int32, score.shape, 0).astype(F32)
    sel = jnp.zeros(score.shape, F32)
    fillers = list(fillers)
    for it in range(k):
        mx = jnp.max(score, axis=0, keepdims=True)
        idx = jnp.min(jnp.where(score == mx, blk, 1e9), axis=0, keepdims=True)
        hit = blk == idx
        sel = jnp.where(hit, 1.0, sel)
        score = jnp.where(hit, -jnp.inf, score)
        for f in fillers[it * len(fillers) // k:(it + 1) * len(fillers) // k]:
            f()
    return sel


def _nsa_prompt_kernel(q_ref, gt_ref, kc_ref, vc_ref, ks_ref, vs_ref, kw_ref, vw_ref, tb_ref, lut_ref, ovt_ref,
                       o_ref, qaug_ref, sc_ref, pcb_ref, pcs_ref, sw_ref, pw_ref, pfar_ref, *, nseg, n_sb):
    g = pl.program_id(1)
    qb = pl.program_id(2)
    s0 = qb * Q_BLOCK
    rows = NSA_HPG * Q_BLOCK
    q = q_ref[...].reshape(rows, LANES)
    tb = tb_ref[...]

    sc_ref[...] = _nt(q, kc_ref[...])
    qi128 = lax.broadcasted_iota(jnp.int32, (Q_BLOCK, LANES), 0)
    li128 = lax.broadcasted_iota(jnp.int32, (Q_BLOCK, LANES), 1)

    def add_near_bias(chunk):
        l0 = pl.multiple_of(chunk * LANES, LANES)
        dist = s0 + qi128 - CMP_STRIDE * (chunk * LANES + li128) - (CMP_STRIDE - 1)
        for hh in range(NSA_HPG):
            lut = jnp.broadcast_to(lut_ref[hh, 0:1, :], (Q_BLOCK, LANES))
            rs = slice(hh * Q_BLOCK, (hh + 1) * Q_BLOCK)
            sc_ref[rs, pl.ds(l0, LANES)] = sc_ref[rs, pl.ds(l0, LANES)] + _lut_gather(lut, dist)

    chunk_lo = jnp.maximum(8 * qb - 8, 0) // LANES
    chunk_hi = (8 * qb + 7) // LANES
    add_near_bias(chunk_lo)

    @pl.when(chunk_hi != chunk_lo)
    def _():
        add_near_bias(chunk_hi)

    rb_n = 32
    qi_b = lax.broadcasted_iota(jnp.int32, (rb_n, nseg), 0)
    mi_b = lax.broadcasted_iota(jnp.int32, (rb_n, nseg), 1)
    for rb in range(Q_BLOCK // rb_n):
        dist_b = s0 + rb * rb_n + qi_b - CMP_STRIDE * mi_b - (CMP_STRIDE - 1)
        valid_b = (dist_b >= 0) & (mi_b >= 1)
        pcs_b = jnp.zeros((rb_n, nseg), F32)
        for hh in range(NSA_HPG):
            rs = slice(hh * Q_BLOCK + rb * rb_n, hh * Q_BLOCK + (rb + 1) * rb_n)
            s_b = jnp.where(valid_b, sc_ref[rs, :], NEG)
            e = jnp.where(valid_b, jnp.exp(s_b - jnp.max(s_b, axis=-1, keepdims=True)), 0.0)
            ssum = jnp.sum(e, axis=-1, keepdims=True)
            pc = e * (1.0 / jnp.where(ssum > 0, ssum, 1.0))
            pcb_ref[rs, :] = pc.astype(BF16)
            pcs_b = pcs_b + pc
        pcs_ref[rb * rb_n:(rb + 1) * rb_n, :] = pcs_b
    o_c = jnp.dot(pcb_ref[...], vc_ref[...], preferred_element_type=F32)

    pcs = pcs_ref[...]
    hi, lo = _split2(pcs)
    ovt = ovt_ref[...]
    imp_t = _nt(ovt, hi) + _nt(ovt, lo)
    jb = lax.broadcasted_iota(jnp.int32, (LANES, Q_BLOCK), 0)
    q_blk = (s0 + lax.broadcasted_iota(jnp.int32, (LANES, Q_BLOCK), 1)) // SLC_BLOCK
    forced = (jb == 0) | (jb == q_blk) | (jb == q_blk - 1)
    score = jnp.where(forced, 1e4, jnp.where(jb <= q_blk, imp_t, -1e4))
    score = jnp.where(jb < n_sb, score, -3e38)

    w0 = pl.multiple_of(s0, Q_BLOCK)
    n_w = WINDOW + Q_BLOCK
    sw_ref[...] = _nt(q, kw_ref[pl.ds(w0, n_w), :])
    qi_w = lax.broadcasted_iota(jnp.int32, (rb_n, n_w), 0)
    kk_w = lax.broadcasted_iota(jnp.int32, (rb_n, n_w), 1)

    def window_block(hh, rb):
        ok_w = (kk_w > qi_w + rb * rb_n) & (s0 + kk_w >= WINDOW)
        rs = slice(hh * Q_BLOCK + rb * rb_n, hh * Q_BLOCK + (rb + 1) * rb_n)
        s_b = jnp.where(ok_w, sw_ref[rs, :], NEG)
        s_b = jnp.concatenate([s_b[:, :WINDOW - Q_BLOCK],
                               s_b[:, WINDOW - Q_BLOCK:] + tb_ref[hh, rb * rb_n:(rb + 1) * rb_n, :]], axis=-1)
        pw_ref[rs, :] = jnp.exp(s_b - jnp.max(s_b, axis=-1, keepdims=True)).astype(BF16)

    gt = jax.nn.sigmoid(gt_ref[...])
    glane = lax.broadcasted_iota(jnp.int32, gt.shape, 1)
    gates = [[None] * 3 for _ in range(NSA_HPG)]

    def gate_column(hh, br):
        col = 3 * (NSA_HPG * g + hh) + br
        gates[hh][br] = jnp.sum(jnp.where(glane == col, gt, 0.0), axis=-1, keepdims=True)

    fillers = []
    for hh in range(NSA_HPG):
        fillers += [functools.partial(window_block, hh, rb) for rb in range(Q_BLOCK // rb_n)]
        fillers += [functools.partial(gate_column, hh, br) for br in range(3)]
    sel_t = _topk_rows_mask(score, min(N_SELECT, n_sb), fillers)
    acc_w = jnp.dot(pw_ref[...], vw_ref[pl.ds(w0, n_w), :], preferred_element_type=F32)
    o_w = acc_w[:, :NSA_HEAD_DIM] / acc_w[:, NSA_HEAD_DIM:NSA_HEAD_DIM + 1]
    unsel = 1.0 - sel_t.T
    blk_lane = lax.broadcasted_iota(jnp.int32, (Q_BLOCK, LANES), 1)
    near_blk0 = 2 * qb - 2
    unsel_far = jnp.where(blk_lane >= near_blk0, 1.0, unsel)

    qaug_ref[:, LANES:] = q
    for hh in range(NSA_HPG):
        qaug_ref[hh * Q_BLOCK:(hh + 1) * Q_BLOCK, :LANES] = unsel.astype(BF16)
    a0 = pl.multiple_of(jnp.maximum(s0 - Q_BLOCK, 0), Q_BLOCK)
    b0 = pl.multiple_of(s0, Q_BLOCK)
    kn = jnp.concatenate([ks_ref[pl.ds(a0, Q_BLOCK), :], ks_ref[pl.ds(b0, Q_BLOCK), :]], axis=0)
    vn = jnp.concatenate([vs_ref[pl.ds(a0, Q_BLOCK), :], vs_ref[pl.ds(b0, Q_BLOCK), :]], axis=0)
    kcol = lax.broadcasted_iota(jnp.int32, (Q_BLOCK, 2 * Q_BLOCK), 1)
    no_prev = jnp.where((kcol < Q_BLOCK) & (qb == 0), NEG, 0.0)
    s_n = _nt(qaug_ref[...], kn).reshape(NSA_HPG, Q_BLOCK, 2 * Q_BLOCK) + (tb + no_prev[None])
    s_n = s_n.reshape(rows, 2 * Q_BLOCK)
    m_run = jnp.max(s_n, axis=-1, keepdims=True)
    acc = jnp.dot(jnp.exp(s_n - m_run).astype(BF16), vn, preferred_element_type=F32)

    for hh in range(NSA_HPG):
        qaug_ref[hh * Q_BLOCK:(hh + 1) * Q_BLOCK, :LANES] = unsel_far.astype(BF16)
    kc_far = 4 * Q_BLOCK
    n_far = (jnp.maximum(qb - 1, 0) + 3) // 4

    pfar_ref[1] = jnp.zeros((rows, kc_far), BF16)

    def far_logits(c):
        return _nt(qaug_ref[...], ks_ref[pl.ds(pl.multiple_of(c * kc_far, kc_far), kc_far), :])

    def far_pv(slot, c):
        k0 = pl.multiple_of(jnp.maximum(c, 0) * kc_far, kc_far)
        return jnp.dot(pfar_ref[slot], vs_ref[pl.ds(k0, kc_far), :], preferred_element_type=F32)

    def far_trip(t, carry):
        m_old, acc_old, alpha_prev = carry
        s_a = far_logits(2 * t)
        acc_1 = alpha_prev * acc_old + far_pv(1, 2 * t - 1)
        s_b = far_logits(2 * t + 1)
        m_a = jnp.maximum(m_old, jnp.max(s_a, axis=-1, keepdims=True))
        pfar_ref[0] = jnp.exp(s_a - m_a).astype(BF16)
        acc_2 = jnp.exp(m_old - m_a) * acc_1 + far_pv(0, 2 * t)
        m_b = jnp.maximum(m_a, jnp.max(s_b, axis=-1, keepdims=True))
        pfar_ref[1] = jnp.exp(s_b - m_b).astype(BF16)
        return m_b, acc_2, jnp.exp(m_a - m_b)

    n_trips = (n_far + 1) // 2
    m_run, acc, alpha_last = lax.fori_loop(0, n_trips, far_trip, (m_run, acc, jnp.ones((rows, 1), F32)))
    acc = alpha_last * acc + far_pv(1, 2 * n_trips - 1)
    o_s = acc[:, :NSA_HEAD_DIM] / acc[:, NSA_HEAD_DIM:NSA_HEAD_DIM + 1]

    for hh in range(NSA_HPG):
        rs = slice(hh * Q_BLOCK, (hh + 1) * Q_BLOCK)
        o = gates[hh][0] * o_c[rs, :NSA_HEAD_DIM] + gates[hh][1] * o_s[rs] + gates[hh][2] * o_w[rs]
        o_ref[hh] = o.astype(o_ref.dtype)


def _overlap_t(n_blk_pad, nseg):
    m = np.arange(nseg)[None, :]
    j = np.arange(n_blk_pad)[:, None]
    c_start = CMP_STRIDE * m - CMP_STRIDE
    c_end = CMP_STRIDE * m + CMP_STRIDE - 1
    ov = (c_start < j * SLC_BLOCK + SLC_BLOCK) & (c_end >= j * SLC_BLOCK) & (m >= 1)
    return ov.astype(np.float32)


def _nsa_prompt_attention(q128, p, gate_col_block, kc, vc, ks, vs, kw, vw, tb, lut):
    b, _, t, _ = q128.shape
    nseg = kc.shape[2]
    n_sb = t // SLC_BLOCK
    assert n_sb <= LANES and t % (4 * Q_BLOCK) == 0 and nseg % LANES == 0
    ovt = jnp.asarray(_overlap_t(LANES, nseg), dtype=BF16)
    per_bg = lambda rows, cols: pl.BlockSpec((None, None, rows, cols), lambda b, g, i: (b, g, 0, 0))
    return pl.pallas_call(
        functools.partial(_nsa_prompt_kernel, nseg=nseg, n_sb=n_sb),
        grid=(b, NSA_KV_GROUPS, t // Q_BLOCK),
        in_specs=[pl.BlockSpec((None, NSA_HPG, Q_BLOCK, LANES), lambda b, g, i: (b, g, i, 0)),
                  pl.BlockSpec((None, Q_BLOCK, LANES), lambda b, g, i: (b, i, gate_col_block)),
                  per_bg(nseg, LANES), per_bg(nseg, LANES),
                  per_bg(t, 2 * LANES), per_bg(t, LANES),
                  per_bg(t + WINDOW, LANES), per_bg(t + WINDOW, LANES),
                  pl.BlockSpec((NSA_HPG, Q_BLOCK, 2 * Q_BLOCK), lambda b, g, i: (g, 0, 0)),
                  pl.BlockSpec((NSA_HPG, 8, LANES), lambda b, g, i: (g, 0, 0)),
                  pl.BlockSpec((LANES, nseg), lambda b, g, i: (0, 0))],
        out_specs=pl.BlockSpec((None, NSA_HPG, Q_BLOCK, NSA_HEAD_DIM), lambda b, g, i: (b, g, i, 0)),
        out_shape=jax.ShapeDtypeStruct((b, NSA_HEADS, t, NSA_HEAD_DIM), BF16),
        scratch_shapes=[pltpu.VMEM((NSA_HPG * Q_BLOCK, 2 * LANES), BF16),
                        pltpu.VMEM((NSA_HPG * Q_BLOCK, nseg), F32), pltpu.VMEM((NSA_HPG * Q_BLOCK, nseg), BF16),
                        pltpu.VMEM((Q_BLOCK, nseg), F32),
                        pltpu.VMEM((NSA_HPG * Q_BLOCK, WINDOW + Q_BLOCK), F32),
                        pltpu.VMEM((NSA_HPG * Q_BLOCK, WINDOW + Q_BLOCK), BF16),
                        pltpu.VMEM((2, NSA_HPG * Q_BLOCK, 4 * Q_BLOCK), BF16)],
        compiler_params=pltpu.CompilerParams(
            dimension_semantics=("parallel", "parallel", "arbitrary"), vmem_limit_bytes=VMEM_LIMIT),
        name="nsa_prompt_attention",
    )(q128, p, kc, vc, ks, vs, kw, vw, tb, lut, ovt)


def _gdn_conv_kernel(x_ref, w_ref, o_ref, carry_ref, *, tm, tc):
    j = pl.program_id(1)

    @pl.when(pl.program_id(2) == 0)
    def _():
        carry_ref[...] = jnp.zeros(carry_ref.shape, F32)

    x = x_ref[...]
    w = w_ref[...]
    prev = carry_ref[...]
    row8 = lax.broadcasted_iota(jnp.int32, (8, tc), 0)
    conv = x * w[CONV_W - 1:CONV_W, :]
    for sft in range(1, CONV_W):
        xs = pltpu.roll(x, sft, axis=0)
        top = jnp.where(row8 < sft, pltpu.roll(prev, sft, axis=0), xs[0:8])
        xs = top if tm == 8 else jnp.concatenate([top, xs[8:]], axis=0)
        conv = conv + xs * w[CONV_W - 1 - sft:CONV_W - sft, :]
    carry_ref[...] = x[tm - 8:tm, :]
    act = conv * jax.nn.sigmoid(conv)
    for hd in range(tc // GDN_HEAD_DIM):
        sl = slice(hd * GDN_HEAD_DIM, (hd + 1) * GDN_HEAD_DIM)
        a = act[:, sl]
        col0 = j * tc + hd * GDN_HEAD_DIM
        nrm = a * lax.rsqrt(jnp.sum(a * a, axis=-1, keepdims=True) + 1e-6)
        nrm = nrm * jnp.where(col0 < 1024, GDN_HEAD_DIM ** -0.5, 1.0)
        o_ref[:, sl] = jnp.where(col0 < 2048, nrm, a)


def _gdn_conv(p, conv_w):
    b, t, _ = p.shape
    tm = _row_tile(t)
    tc = 512
    return pl.pallas_call(
        functools.partial(_gdn_conv_kernel, tm=tm, tc=tc),
        grid=(b, C_CONV // tc, t // tm),
        in_specs=[pl.BlockSpec((None, tm, tc), lambda b, j, i: (b, i, j)),
                  pl.BlockSpec((CONV_W, tc), lambda b, j, i: (0, j))],
        out_specs=pl.BlockSpec((None, tm, tc), lambda b, j, i: (b, i, j)),
        out_shape=jax.ShapeDtypeStruct((b, t, C_CONV), F32),
        scratch_shapes=[pltpu.VMEM((8, tc), F32)],
        compiler_params=pltpu.CompilerParams(
            dimension_semantics=("parallel", "parallel", "arbitrary"), vmem_limit_bytes=VMEM_LIMIT),
        name="gdn_conv",
    )(p, conv_w)


def _gdn_gate_kernel(ba_ref, alog_ref, dtb_ref, o_ref):
    x = ba_ref[...]
    y = x + dtb_ref[...]
    softplus = jnp.maximum(y, 0.0) + jnp.log1p(jnp.exp(-jnp.abs(y)))
    g = -jnp.exp(alog_ref[...]) * softplus
    lane = lax.broadcasted_iota(jnp.int32, x.shape, 1)
    o_ref[...] = jnp.where(lane < GDN_V_HEADS, jax.nn.sigmoid(x), g)


def _gdn_gates(p, ba_col_block, a_log, dt_bias):
    b, t, _ = p.shape
    tm = _row_tile(t)
    pad = lambda v: jnp.pad(v.reshape(1, GDN_V_HEADS), ((0, 0), (GDN_V_HEADS, LANES - 2 * GDN_V_HEADS)))
    return pl.pallas_call(
        _gdn_gate_kernel,
        grid=(b, t // tm),
        in_specs=[pl.BlockSpec((None, tm, LANES), lambda b, i: (b, i, ba_col_block)),
                  pl.BlockSpec((1, LANES), lambda b, i: (0, 0)),
                  pl.BlockSpec((1, LANES), lambda b, i: (0, 0))],
        out_specs=pl.BlockSpec((None, tm, LANES), lambda b, i: (b, i, 0)),
        out_shape=jax.ShapeDtypeStruct((b, t, LANES), F32),
        compiler_params=pltpu.CompilerParams(dimension_semantics=("parallel", "parallel")),
        name="gdn_gates",
    )(p, pad(a_log), pad(dt_bias))


def _bdot(a, b):
    return jnp.dot(a.astype(BF16), b.astype(BF16), preferred_element_type=F32)


GDN_PACK = 4
_PACK_ORDER = (0, 2, 1, 3)
_PACK_HEADS = tuple(GDN_PACK * p + o for p in range(GDN_V_HEADS // GDN_PACK) for o in _PACK_ORDER)


def _iota2(shape, axis):
    return lax.broadcasted_iota(jnp.int32, shape, axis)


def _packed_mm(a_cat, b_cat, bd_mask):
    b_bd = jnp.where(bd_mask, jnp.concatenate([b_cat] * GDN_PACK, axis=0), 0.0)
    return _bdot(a_cat, b_bd)


def _unit_lower_inverse_packed(ls, row, col, bd_mask):
    eye = (row == col).astype(F32)
    same16 = (row // 16) == (col // 16)
    same32 = (row // 32) == (col // 32)
    ms = [jnp.where(same16, -l, 0.0) for l in ls]
    ps = [eye + m for m in ms]
    for _ in range(3):
        ms = [_packed_mm(m, m, bd_mask) for m in ms]
        ps = [p + _packed_mm(p, m, bd_mask) for p, m in zip(ps, ms)]
    for level in (same32 & jnp.logical_not(same16), jnp.logical_not(same32)):
        ts = [_packed_mm(jnp.where(level, l, 0.0), p, bd_mask) for l, p in zip(ls, ps)]
        ps = [p - _packed_mm(p, t, bd_mask) for p, t in zip(ps, ts)]
    return ps


def _gdn_delta_kernel(act_ref, bg_ref, gt_ref, s0_ref, ltri_ref, lbd_ref, o_ref, s_ref, sbd_ref):
    c, hd = GDN_CHUNK, GDN_HEAD_DIM
    n_packs = GDN_V_HEADS // GDN_PACK
    zero_hd = jnp.zeros((hd, hd), F32)

    @pl.when(pl.program_id(1) == 0)
    def _():
        for pr in range(GDN_V_HEADS // 2):
            h0, h1 = _PACK_HEADS[2 * pr], _PACK_HEADS[2 * pr + 1]
            sbd_ref[pr] = jnp.concatenate([jnp.concatenate([s0_ref[h0], zero_hd], axis=-1),
                                           jnp.concatenate([zero_hd, s0_ref[h1]], axis=-1)], axis=0)

    bg = bg_ref[...]
    cum = sum(jnp.dot(ltri_ref[...], part, preferred_element_type=F32) for part in _split3(bg))
    gcr_all = sum(_nt(part, lbd_ref[...]) for part in _split3(gt_ref[...]))
    row = _iota2((c, GDN_PACK * c), 0)
    lane = _iota2((c, GDN_PACK * c), 1)
    col, slot = lane % c, lane // c
    incl, strict = row >= col, row > col
    bd_mask = (_iota2((4 * c, 4 * c), 0) // c) == (_iota2((4 * c, 4 * c), 1) // c)
    pair_mask = (_iota2((2 * hd, 2 * hd), 0) // hd) == (_iota2((2 * hd, 2 * hd), 1) // hd)
    k_mask = (_iota2((2 * hd, hd), 0) // hd) == (_iota2((2 * hd, hd), 1) // c)
    row_pair = _iota2((2 * hd, 1), 0)

    def slot_cat(cols):
        out = jnp.broadcast_to(cols[3], (c, GDN_PACK * c))
        for x in (2, 1, 0):
            out = jnp.where(slot == x, cols[x], out)
        return out

    def side_by_side(a, b):
        return jnp.concatenate([a, b], axis=-1)

    qs, ks, betas, gcs, lmats, a_ins = [], [], [], [], [], []
    for p in range(n_packs):
        heads = _PACK_HEADS[GDN_PACK * p:GDN_PACK * (p + 1)]
        qa, qb = (act_ref[:, (2 * p + i) * hd:(2 * p + i + 1) * hd] for i in (0, 1))
        ka, kb = (act_ref[:, 1024 + (2 * p + i) * hd:1024 + (2 * p + i + 1) * hd] for i in (0, 1))
        kt = jnp.concatenate([ka, kb], axis=0).T
        k_bd = jnp.where(k_mask, jnp.concatenate([kt, kt], axis=0), 0.0)
        kq = _bdot(jnp.concatenate([side_by_side(ka, kb), side_by_side(qa, qb)], axis=0), k_bd)
        kk = side_by_side(kq[:c], kq[:c])
        qk = side_by_side(kq[c:], kq[c:])
        beta = [bg[:, h:h + 1] for h in heads]
        gc = [cum[:, GDN_V_HEADS + h:GDN_V_HEADS + h + 1] for h in heads]
        decay = jnp.where(incl, jnp.exp(jnp.where(incl, slot_cat(gc) - gcr_all[p:p + 1, :], 0.0)), 0.0)
        lmats.append(jnp.where(strict, slot_cat(beta) * kk * decay, 0.0))
        a_ins.append(qk * decay)
        qs.append((qa, qb, qa, qb)); ks.append((ka, kb, ka, kb)); betas.append(beta); gcs.append(gc)

    tinvs = _unit_lower_inverse_packed(lmats, row, col, bd_mask)

    uws, egs = [], []
    for p in range(n_packs):
        bands = []
        eg = [jnp.exp(g) for g in gcs[p]]
        for x in range(GDN_PACK):
            h = _PACK_HEADS[GDN_PACK * p + x]
            vh = act_ref[:, 2048 + h * hd:2048 + (h + 1) * hd]
            rhs = betas[p][x] * side_by_side(vh, ks[p][x] * eg[x])
            pieces = [jnp.zeros((c, 2 * hd * x), F32)] * (x > 0) + [rhs] + [jnp.zeros((c, 2 * hd * (3 - x)), F32)] * (x < 3)
            bands.append(jnp.concatenate(pieces, axis=-1))
        uws.append(_bdot(tinvs[p], jnp.concatenate(bands, axis=0)))
        egs.append(eg)

    wss, s_olds = [], []
    for p in range(n_packs):
        for pr in range(2):
            x0, x1 = 2 * pr, 2 * pr + 1
            w0, w1 = (uws[p][:, 2 * hd * x + hd:2 * hd * (x + 1)] for x in (x0, x1))
            lhs = jnp.concatenate([side_by_side(w0, w1),
                                   side_by_side(qs[p][x0] * egs[p][x0], qs[p][x1] * egs[p][x1])], axis=0)
            s_old = sbd_ref[2 * p + pr]
            s_olds.append(s_old)
            wss.append(_bdot(lhs, s_old))

    v_news = []
    for p in range(n_packs):
        vn = []
        for x in range(GDN_PACK):
            ws = wss[2 * p + x // 2]
            vn.append(uws[p][:, 2 * hd * x:2 * hd * x + hd] - ws[:c, hd * (x % 2):hd * (x % 2 + 1)])
        v_news.append(vn)
    for p in range(n_packs):
        bands = []
        for x in range(GDN_PACK):
            pieces = [jnp.zeros((c, hd * x), F32)] * (x > 0) + [v_news[p][x]] + [jnp.zeros((c, hd * (3 - x)), F32)] * (x < 3)
            bands.append(jnp.concatenate(pieces, axis=-1))
        av = _bdot(a_ins[p], jnp.concatenate(bands, axis=0))
        for x in range(GDN_PACK):
            h = _PACK_HEADS[GDN_PACK * p + x]
            ws = wss[2 * p + x // 2]
            o_ref[:, h * hd:(h + 1) * hd] = ws[c:, hd * (x % 2):hd * (x % 2 + 1)] + av[:, hd * x:hd * (x + 1)]
    zrows = jnp.zeros((c, 2 * hd), F32)
    for p in range(n_packs):
        for pr in range(2):
            x0, x1 = 2 * pr, 2 * pr + 1
            gl0, gl1 = gcs[p][x0][c - 1:c, :], gcs[p][x1][c - 1:c, :]
            kd = jnp.concatenate([side_by_side(ks[p][x0] * jnp.exp(gl0 - gcs[p][x0]),
                                               ks[p][x1] * jnp.exp(gl1 - gcs[p][x1])), zrows], axis=0)
            kd_t = jnp.concatenate([kd[:, :hd].T, kd[:, hd:].T], axis=0)
            vn = jnp.concatenate([side_by_side(v_news[p][x0], v_news[p][x1]), zrows], axis=0)
            d_last = jnp.where(row_pair < hd, jnp.exp(gl0), jnp.exp(gl1))
            sbd_ref[2 * p + pr] = jnp.where(pair_mask, s_olds[2 * p + pr] * d_last + _bdot(kd_t, vn), 0.0)

    @pl.when(pl.program_id(1) == pl.num_programs(1) - 1)
    def _():
        for pr in range(GDN_V_HEADS // 2):
            s_pair = sbd_ref[pr]
            s_ref[_PACK_HEADS[2 * pr]] = s_pair[:hd, :hd]
            s_ref[_PACK_HEADS[2 * pr + 1]] = s_pair[hd:, hd:]


def _gdn_delta(act, bg, s0):
    b, t, _ = act.shape
    nc = t // GDN_CHUNK
    n_packs = GDN_V_HEADS // GDN_PACK
    wp = GDN_PACK * GDN_CHUNK
    g_rows = bg[:, :, GDN_V_HEADS:2 * GDN_V_HEADS][:, :, np.asarray(_PACK_HEADS)]
    g_rows = g_rows.reshape(b, nc, GDN_CHUNK, n_packs, GDN_PACK).transpose(0, 1, 3, 4, 2).reshape(b, nc, n_packs, wp)
    g_rows = jnp.pad(g_rows, ((0, 0), (0, 0), (0, 8 - n_packs), (0, 0)))
    tri = np.tril(np.ones((GDN_CHUNK, GDN_CHUNK), np.float32))
    ltri = jnp.asarray(tri, dtype=BF16)
    lbd = jnp.asarray(np.kron(np.eye(GDN_PACK, dtype=np.float32), tri), dtype=BF16)
    state_spec = pl.BlockSpec((None, GDN_V_HEADS, GDN_HEAD_DIM, GDN_HEAD_DIM), lambda b, n: (b, 0, 0, 0))
    return pl.pallas_call(
        _gdn_delta_kernel,
        grid=(b, nc),
        in_specs=[pl.BlockSpec((None, GDN_CHUNK, C_CONV), lambda b, n: (b, n, 0)),
                  pl.BlockSpec((None, GDN_CHUNK, LANES), lambda b, n: (b, n, 0)),
                  pl.BlockSpec((None, None, 8, wp), lambda b, n: (b, n, 0, 0)),
                  state_spec,
                  pl.BlockSpec((GDN_CHUNK, GDN_CHUNK), lambda b, n: (0, 0)),
                  pl.BlockSpec((wp, wp), lambda b, n: (0, 0))],
        out_specs=[pl.BlockSpec((None, GDN_CHUNK, GDN_V_W), lambda b, n: (b, n, 0)), state_spec],
        out_shape=[jax.ShapeDtypeStruct((b, t, GDN_V_W), F32),
                   jax.ShapeDtypeStruct(s0.shape, F32)],
        scratch_shapes=[pltpu.VMEM((GDN_V_HEADS // 2, 2 * GDN_HEAD_DIM, 2 * GDN_HEAD_DIM), F32)],
        compiler_params=pltpu.CompilerParams(
            dimension_semantics=("parallel", "arbitrary"), vmem_limit_bytes=VMEM_LIMIT),
        name="gdn_delta_rule",
    )(act, bg, g_rows, s0, ltri, lbd)


SAMPLE_ROWS = NSA_HEADS * 4


def _sample_cmp_kernel(q_ref, kc_ref, vc_ref, lut_ref, ov_ref, oc_ref, un_ref, *, nseg, past_len, n_sb, nq):
    rg = NSA_HPG * nq
    ri = lax.broadcasted_iota(jnp.int32, (rg, nseg), 0)
    mi = lax.broadcasted_iota(jnp.int32, (rg, nseg), 1)
    dist = past_len + ri % nq - CMP_STRIDE * mi - (CMP_STRIDE - 1)
    valid = (dist >= 0) & (mi >= 1)
    jl = lax.broadcasted_iota(jnp.int32, (8, un_ref.shape[-1]), 1)
    q_blk = (past_len + lax.broadcasted_iota(jnp.int32, jl.shape, 0) % nq) // SLC_BLOCK
    forced = (jl == 0) | (jl == q_blk) | (jl == q_blk - 1)
    jf = jl.astype(F32)
    for g in range(NSA_KV_GROUPS):
        sc = _nt(q_ref[g], kc_ref[g])
        tail = sc[:, nseg - LANES:] + _lut_gather(lut_ref[g], dist[:, nseg - LANES:])
        sc = jnp.where(valid, jnp.concatenate([sc[:, :nseg - LANES], tail], axis=-1), NEG)
        mx = jnp.max(sc, axis=-1, keepdims=True)
        e = jnp.where(valid, jnp.exp(sc - mx), 0.0)
        ssum = jnp.sum(e, axis=-1, keepdims=True)
        pc = e / jnp.where(ssum > 0, ssum, 1.0)
        oc_ref[g] = jnp.dot(pc.astype(BF16), vc_ref[g], preferred_element_type=F32)
        pcs = pc
        for hh in range(1, NSA_HPG):
            pcs = pcs + pltpu.roll(pc, hh * nq, axis=0)
        hi, lo = _split2(pcs[0:8])
        imp = jnp.dot(hi, ov_ref[...], preferred_element_type=F32) + jnp.dot(lo, ov_ref[...],
                                                                              preferred_element_type=F32)
        score = jnp.where(forced, 1e4, jnp.where(jl <= q_blk, imp, -1e4))
        score = jnp.where(jl < n_sb, score, -3e38)
        sel = jnp.zeros(score.shape, F32)
        for _ in range(min(N_SELECT, n_sb)):
            mxs = jnp.max(score, axis=-1, keepdims=True)
            idx = jnp.min(jnp.where(score == mxs, jf, 1e9), axis=-1, keepdims=True)
            hit = jf == idx
            sel = jnp.where(hit, 1.0, sel)
            score = jnp.where(hit, -jnp.inf, score)
        un_ref[g] = 1.0 - sel


def _sample_cmp(q16, kc, vc, lut16, past_len, nq):
    b = q16.shape[0]
    nseg = kc.shape[2]
    rg = NSA_HPG * nq
    n_sb = past_len // SLC_BLOCK + 1
    n_sb_pad = -(-n_sb // LANES) * LANES
    assert nq == 4 and nseg * CMP_STRIDE == past_len
    m = np.arange(nseg)[:, None]
    j = np.arange(n_sb_pad)[None, :]
    ov = ((CMP_STRIDE * m - CMP_STRIDE < j * SLC_BLOCK + SLC_BLOCK) & (CMP_STRIDE * m + CMP_STRIDE - 1 >= j * SLC_BLOCK)
          & (m >= 1) & (j < n_sb)).astype(np.float32)
    whole = lambda *shape: pl.BlockSpec((None,) + shape, lambda b: (b,) + (0,) * len(shape))
    return pl.pallas_call(
        functools.partial(_sample_cmp_kernel, nseg=nseg, past_len=past_len, n_sb=n_sb, nq=nq),
        grid=(b,),
        in_specs=[whole(NSA_KV_GROUPS, rg, LANES), whole(NSA_KV_GROUPS, nseg, LANES), whole(NSA_KV_GROUPS, nseg, LANES),
                  pl.BlockSpec((NSA_KV_GROUPS, rg, LANES), lambda b: (0, 0, 0)),
                  pl.BlockSpec((nseg, n_sb_pad), lambda b: (0, 0))],
        out_specs=[whole(NSA_KV_GROUPS, rg, LANES), whole(NSA_KV_GROUPS, 8, n_sb_pad)],
        out_shape=[jax.ShapeDtypeStruct((b, NSA_KV_GROUPS, rg, LANES), F32),
                   jax.ShapeDtypeStruct((b, NSA_KV_GROUPS, 8, n_sb_pad), F32)],
        compiler_params=pltpu.CompilerParams(dimension_semantics=("parallel",), vmem_limit_bytes=VMEM_LIMIT),
        name="nsa_sample_cmp_topk",
    )(q16, kc, vc, lut16, jnp.asarray(ov, dtype=BF16))


def _sample_sel_kernel(tab_ref, *refs, npg, past_len, nq):
    del tab_ref
    pages = refs[:npg]
    qbd_ref, un_ref, ee_ref, far_ref, lut_ref, m_ref, l_ref, acc_ref = refs[npg:]
    c = pl.program_id(1)
    kc = npg * PAGE_SIZE

    @pl.when(c == 0)
    def _():
        m_ref[...] = jnp.full(m_ref.shape, NEG, F32)
        l_ref[...] = jnp.zeros(l_ref.shape, F32)
        acc_ref[...] = jnp.zeros(acc_ref.shape, F32)

    kt = jnp.concatenate([pg[0] for pg in pages], axis=1).astype(BF16)
    vt = jnp.concatenate([pg[1] for pg in pages], axis=1).astype(BF16)
    s = (jnp.dot(qbd_ref[...], kt, preferred_element_type=F32) + far_ref[...][:, 0:1]
         + jnp.dot(un_ref[...], ee_ref[...], preferred_element_type=F32))
    ri = lax.broadcasted_iota(jnp.int32, (SAMPLE_ROWS, LANES), 0)
    li = lax.broadcasted_iota(jnp.int32, (SAMPLE_ROWS, LANES), 1)
    dist = past_len + ri % nq - (c * kc + kc - LANES + li)
    s = jnp.concatenate([s[:, :kc - LANES], s[:, kc - LANES:] + _lut_gather(lut_ref[...], dist)], axis=-1)
    m_old = m_ref[...][:, 0:1]
    m_new = jnp.maximum(m_old, jnp.max(s, axis=-1, keepdims=True))
    alpha = jnp.exp(m_old - m_new)
    p = jnp.exp(s - m_new)
    l_ref[...] = alpha * l_ref[...] + jnp.sum(p, axis=-1, keepdims=True)
    acc_ref[...] = alpha * acc_ref[...] + _nt(p.astype(BF16), vt)
    m_ref[...] = jnp.broadcast_to(m_new, m_ref.shape)


def _sample_sel(pages, table, qbd, unsel_c, farcol, lut64, past_len, nq):
    b, n_pages = table.shape
    npg = min(16, n_pages)
    kc = npg * PAGE_SIZE
    nch = n_pages // npg
    blk_per_chunk = kc // SLC_BLOCK
    ee = np.zeros((LANES, kc), np.float32)
    ee[np.arange(kc) // SLC_BLOCK, np.arange(kc)] = NEG
    assert blk_per_chunk <= LANES

    def page_spec(j):
        return pl.BlockSpec((None, 2, NSA_KV_W // 2, PAGE_SIZE), lambda b, c, tab: (tab[b, c * npg + j], 0, 0, 0))

    const = lambda *shape: pl.BlockSpec(shape, lambda b, c, tab: (0,) * len(shape))
    acc_spec = lambda cols: pl.BlockSpec((None, SAMPLE_ROWS, cols), lambda b, c, tab: (b, 0, 0))
    grid_spec = pltpu.PrefetchScalarGridSpec(
        num_scalar_prefetch=1,
        grid=(b, nch),
        in_specs=[page_spec(j) for j in range(npg)] + [
            pl.BlockSpec((None, SAMPLE_ROWS, 2 * LANES), lambda b, c, tab: (b, 0, 0)),
            pl.BlockSpec((None, None, SAMPLE_ROWS, LANES), lambda b, c, tab: (b, c, 0, 0)),
            const(LANES, kc), const(SAMPLE_ROWS, LANES), const(SAMPLE_ROWS, LANES)],
        out_specs=[acc_spec(LANES), acc_spec(LANES), acc_spec(2 * LANES)],
    )
    return pl.pallas_call(
        functools.partial(_sample_sel_kernel, npg=npg, past_len=past_len, nq=nq),
        grid_spec=grid_spec,
        out_shape=[jax.ShapeDtypeStruct((b, SAMPLE_ROWS, LANES), F32),
                   jax.ShapeDtypeStruct((b, SAMPLE_ROWS, LANES), F32),
                   jax.ShapeDtypeStruct((b, SAMPLE_ROWS, 2 * LANES), F32)],
        compiler_params=pltpu.CompilerParams(
            dimension_semantics=("parallel", "arbitrary"), vmem_limit_bytes=VMEM_LIMIT),
        name="nsa_sample_selected",
    )(table, *([pages] * npg), qbd, unsel_c, jnp.asarray(ee, dtype=BF16), farcol, lut64)


def _own_group_cols(x, grp):
    out = jnp.zeros((x.shape[0], NSA_HEAD_DIM), F32)
    for g in range(NSA_KV_GROUPS):
        out = jnp.where(grp == g, x[:, g * NSA_HEAD_DIM:(g + 1) * NSA_HEAD_DIM], out)
    return out


def _sample_final_kernel(qbd_ref, m_ref, l_ref, acc_ref, snew_ref, wc_ref, wnew_ref, oc_ref, gr_ref, far_ref, lut_ref,
                         o_ref, *, nq, w_buf):
    rows = SAMPLE_ROWS
    qbd = qbd_ref[...]
    far = far_ref[...][:, 0:1]
    lut = lut_ref[...]
    ri = lax.broadcasted_iota(jnp.int32, (rows, LANES), 0)
    li = lax.broadcasted_iota(jnp.int32, (rows, LANES), 1)
    tok = ri % nq
    grp = lax.broadcasted_iota(jnp.int32, (rows, NSA_HEAD_DIM), 0) // (NSA_HPG * nq)

    knew = snew_ref[...]
    s_new = _nt(qbd, knew[:, :256].astype(BF16)) + far
    d_new = tok - li
    s_new = jnp.where((d_new >= 0) & (li < nq), s_new + _lut_gather(lut, d_new), NEG)
    m_old = m_ref[...][:, 0:1]
    m_new = jnp.maximum(m_old, jnp.max(s_new, axis=-1, keepdims=True))
    alpha = jnp.exp(m_old - m_new)
    p_new = jnp.exp(s_new - m_new)
    l_s = alpha * l_ref[...][:, 0:1] + jnp.sum(p_new, axis=-1, keepdims=True)
    acc_s = alpha * acc_ref[...] + jnp.dot(p_new.astype(BF16), knew[:, 256:].astype(BF16), preferred_element_type=F32)
    o_s = _own_group_cols(acc_s, grp) / l_s

    kv_w = jnp.concatenate([wc_ref[...], wnew_ref[...]], axis=0)
    s_w = _nt(qbd, kv_w[:, :256].astype(BF16)) + far
    n_w = w_buf + LANES
    idx = lax.broadcasted_iota(jnp.int32, (rows, n_w), 1)
    d_w = w_buf + lax.broadcasted_iota(jnp.int32, (rows, n_w), 0) % nq - idx
    ok_w = (d_w >= 0) & (d_w < WINDOW) & (idx < w_buf + nq)
    corr = [jnp.zeros((rows, n_w - 2 * LANES), F32)]
    for cidx in range(2):
        lo = n_w - 2 * LANES + cidx * LANES
        corr.append(_lut_gather(lut, d_w[:, lo:lo + LANES]))
    s_w = jnp.where(ok_w, s_w + jnp.concatenate(corr, axis=-1), NEG)
    m_w = jnp.max(s_w, axis=-1, keepdims=True)
    p_w = jnp.exp(s_w - m_w)
    l_w = jnp.sum(p_w, axis=-1, keepdims=True)
    acc_w = jnp.dot(p_w.astype(BF16), kv_w[:, 256:].astype(BF16), preferred_element_type=F32)
    o_w = _own_group_cols(acc_w, grp) / l_w

    gt = jax.nn.sigmoid(gr_ref[...])
    o_ref[...] = gt[:, 0:1] * oc_ref[...][:, :NSA_HEAD_DIM] + gt[:, 1:2] * o_s + gt[:, 2:3] * o_w


def _sample_final(qbd, m, l, acc, snew, wcache, wnew, o_c, graw, farcol, lut64, nq):
    b = qbd.shape[0]
    w_buf = wcache.shape[1]
    assert w_buf == WINDOW
    whole = lambda *shape: pl.BlockSpec((None,) + shape, lambda b: (b,) + (0,) * len(shape))
    const = lambda *shape: pl.BlockSpec(shape, lambda b: (0,) * len(shape))
    return pl.pallas_call(
        functools.partial(_sample_final_kernel, nq=nq, w_buf=w_buf),
        grid=(b,),
        in_specs=[whole(SAMPLE_ROWS, 2 * LANES), whole(SAMPLE_ROWS, LANES), whole(SAMPLE_ROWS, LANES),
                  whole(SAMPLE_ROWS, 2 * LANES), whole(LANES, NSA_KV_W), whole(w_buf, NSA_KV_W), whole(LANES, NSA_KV_W),
                  whole(SAMPLE_ROWS, LANES), whole(SAMPLE_ROWS, LANES),
                  const(SAMPLE_ROWS, LANES), const(SAMPLE_ROWS, LANES)],
        out_specs=whole(SAMPLE_ROWS, NSA_HEAD_DIM),
        out_shape=jax.ShapeDtypeStruct((b, SAMPLE_ROWS, NSA_HEAD_DIM), F32),
        compiler_params=pltpu.CompilerParams(dimension_semantics=("parallel",), vmem_limit_bytes=VMEM_LIMIT),
        name="nsa_sample_final",
    )(qbd, m, l, acc, snew, wcache, wnew, o_c, graw, farcol, lut64)


def _ffn(x, mod, gains, w_in, w_out):
    hid = _norm_mod_swiglu(x, gains[2], mod[3], mod[4], w_in)
    return _matmul_rms_residual(hid, w_out, x, mod[5], gains[3])


def _heads_major(x, n_heads):
    b, t, _ = x.shape
    return x.reshape(b, t, n_heads, NSA_HEAD_DIM).transpose(0, 2, 1, 3)


def _const_cols(vals, lead_shape):
    cols = np.zeros((NSA_HEAD_DIM,), np.float32)
    cols[:len(vals)] = vals
    return jnp.broadcast_to(jnp.asarray(cols, dtype=BF16), tuple(lead_shape) + (NSA_HEAD_DIM,))


def _nsa_prompt(x, mod, gains, w_in, cmp_w, w_out, tb, lut, bias_cols):
    b, t, _ = x.shape
    p = _norm_mod_linear(x, gains[0], mod[0], mod[1], w_in)
    kvc, kvs, kvw = (p[..., 1024 + i * NSA_KV_W:1024 + (i + 1) * NSA_KV_W] for i in range(3))
    q128 = jnp.concatenate([_heads_major(p[..., :NSA_Q_W].astype(BF16), NSA_HEADS),
                            jnp.broadcast_to(bias_cols[None, :, None, :], (b, NSA_HEADS, t, NSA_HEAD_DIM))], axis=-1)
    lead = (b, NSA_KV_GROUPS, t)
    blk_onehot = np.zeros((t, LANES), np.float32)
    blk_onehot[np.arange(t), np.arange(t) // SLC_BLOCK] = NEG
    ks = jnp.concatenate([jnp.broadcast_to(jnp.asarray(blk_onehot, dtype=BF16), lead + (LANES,)),
                          _heads_major(kvs[..., :256].astype(BF16), NSA_KV_GROUPS),
                          _const_cols([1.0, 1.0], lead)], axis=-1)
    vs = jnp.concatenate([_heads_major(kvs[..., 256:].astype(BF16), NSA_KV_GROUPS), _const_cols([1.0], lead)], axis=-1)
    front = ((0, 0), (0, 0), (WINDOW, 0), (0, 0))
    kw = jnp.pad(jnp.concatenate([_heads_major(kvw[..., :256].astype(BF16), NSA_KV_GROUPS),
                                  _const_cols([1.0, 1.0], lead)], axis=-1), front)
    vw = jnp.pad(jnp.concatenate([_heads_major(kvw[..., 256:].astype(BF16), NSA_KV_GROUPS),
                                  _const_cols([1.0], lead)], axis=-1), front)
    n_pages = t // PAGE_SIZE
    table = jnp.arange(b * n_pages, dtype=jnp.int32).reshape(b, n_pages)
    kc, vc = _compress(kvc.reshape(b * n_pages, SEGS_PER_PAGE, SEG_W), table, *cmp_w)
    o = _nsa_prompt_attention(q128, p, (NSA_Q_W + 3 * NSA_KV_W) // LANES, kc, vc, ks, vs, kw, vw, tb, lut)
    o = o.transpose(0, 2, 1, 3).reshape(b, t, NSA_Q_W)
    x = _matmul_rms_residual(o, w_out, x, mod[2], gains[1])
    shape5 = (b, t, 2, NSA_KV_GROUPS, NSA_HEAD_DIM)
    return x, kvc.reshape(shape5), kvs.reshape(shape5), kvw.reshape(shape5)[:, -min(WINDOW, t):]


def _nsa_sample(x, mod, gains, w_in, cmp_w, w_out, lut, rel_bias, bias_cols, cache_cmp, cache_slc, cache_win,
                page_table, db, nq):
    n_pages = page_table.shape[1]
    past_len = n_pages * PAGE_SIZE
    rows = db * nq
    p = _norm_mod_linear(x, gains[0], mod[0], mod[1], w_in)[0]
    kvc, kvs, kvw = (p[:, 1024 + i * NSA_KV_W:1024 + (i + 1) * NSA_KV_W] for i in range(3))
    qh = p[:, :NSA_Q_W].astype(BF16).reshape(db, nq, NSA_KV_GROUPS, NSA_HPG, NSA_HEAD_DIM).transpose(0, 2, 3, 1, 4)
    q16 = jnp.concatenate([qh, jnp.broadcast_to(bias_cols.reshape(1, NSA_KV_GROUPS, NSA_HPG, 1, NSA_HEAD_DIM), qh.shape)],
                          axis=-1).reshape(db, NSA_KV_GROUPS, NSA_HPG * nq, LANES)
    eye_g = jnp.eye(NSA_KV_GROUPS, dtype=BF16)
    qbd = jnp.einsum('bghtd,gj->bghtjd', qh, eye_g).reshape(db, SAMPLE_ROWS, NSA_KV_GROUPS * NSA_HEAD_DIM)
    row_head = np.repeat(np.arange(NSA_HEADS), nq)
    lut64 = lut[:, 0, :][row_head]
    farcol = jnp.broadcast_to(rel_bias[N_BUCKETS - 1][row_head][:, None], (SAMPLE_ROWS, LANES))
    rows_minor = lambda cache: jnp.transpose(cache, (0, 2, 3, 4, 1))
    kc, vc = _compress(rows_minor(cache_cmp).reshape(-1, 2, 2, LANES, PAGE_SIZE), page_table, *cmp_w)
    o_c, unsel = _sample_cmp(q16, kc, vc, lut64.reshape(NSA_KV_GROUPS, NSA_HPG * nq, LANES), past_len, nq)
    npg = min(16, n_pages)
    nch = n_pages // npg
    bpc = npg * PAGE_SIZE // SLC_BLOCK
    un = unsel[:, :, :nq, :past_len // SLC_BLOCK].reshape(db, NSA_KV_GROUPS, 1, nq, nch, bpc)
    un = jnp.broadcast_to(un, (db, NSA_KV_GROUPS, NSA_HPG, nq, nch, bpc)).transpose(0, 4, 1, 2, 3, 5)
    un = jnp.pad(un.reshape(db, nch, SAMPLE_ROWS, bpc), ((0, 0), (0, 0), (0, 0), (0, LANES - bpc))).astype(BF16)
    m, l, acc = _sample_sel(rows_minor(cache_slc).reshape(-1, 2, NSA_KV_W // 2, PAGE_SIZE), page_table, qbd, un, farcol,
                            lut64, past_len, nq)
    pad_rows = lambda a: jnp.pad(a.reshape(db, nq, NSA_KV_W), ((0, 0), (0, LANES - nq), (0, 0)))
    wcache = cache_win.reshape(db, -1, NSA_KV_W)
    graw = p[:, NSA_Q_W + 3 * NSA_KV_W:NSA_Q_W + 3 * NSA_KV_W + 3 * NSA_HEADS]
    graw = graw.reshape(db, nq, NSA_HEADS, 3).transpose(0, 2, 1, 3).reshape(db, SAMPLE_ROWS, 3)
    graw = jnp.pad(graw, ((0, 0), (0, 0), (0, LANES - 3)))
    o = _sample_final(qbd, m, l, acc, pad_rows(kvs), wcache, pad_rows(kvw), o_c.reshape(db, SAMPLE_ROWS, LANES), graw,
                      farcol, lut64, nq)
    o = o.reshape(db, NSA_HEADS, nq, NSA_HEAD_DIM).transpose(0, 2, 1, 3).reshape(1, rows, NSA_Q_W)
    x = _matmul_rms_residual(o, w_out, x, mod[2], gains[1])
    shape5 = (db, nq, 2, NSA_KV_GROUPS, NSA_HEAD_DIM)
    kv_win = jnp.concatenate([cache_win, kvw.reshape(shape5)], axis=1)[:, -cache_win.shape[1]:]
    return x, kvc.reshape(shape5), kvs.reshape(shape5), kv_win


def _gdn_prompt(x, mod, gains, w_in, conv_w, a_log, dt_bias, norm_w, w_out):
    b, t, _ = x.shape
    p = _norm_mod_linear(x, gains[0], mod[0], mod[1], w_in)
    act = _gdn_conv(p, conv_w)
    bg = _gdn_gates(p, (C_CONV + GDN_V_W) // LANES, a_log, dt_bias)
    s0 = jnp.zeros((b, GDN_V_HEADS, GDN_HEAD_DIM, GDN_HEAD_DIM), F32)
    o, s_fin = _gdn_delta(act, bg, s0)
    x = _gdn_out(o, p, C_CONV // GDN_V_W, norm_w, w_out, x, mod[2], gains[1])
    return x, p[:, t - (CONV_W - 1):, :C_CONV], s_fin


def _gdn_sample(x, mod, gains, w_in, conv_w, a_log, dt_bias, norm_w, w_out, conv_buf, s0, db, nq):
    p = _norm_mod_linear(x, gains[0], mod[0], mod[1], w_in)
    qkv = p[0, :, :C_CONV].reshape(db, nq, C_CONV)
    xp = jnp.concatenate([conv_buf, qkv], axis=1)
    act = _gdn_conv(jnp.pad(xp, ((0, 0), (0, 8 - xp.shape[1]), (0, 0))), conv_w)[:, CONV_W - 1:CONV_W - 1 + nq]
    bg = _gdn_gates(p, (C_CONV + GDN_V_W) // LANES, a_log, dt_bias).reshape(db, nq, LANES)
    pad_t = ((0, 0), (0, GDN_CHUNK - nq), (0, 0))
    o, s_fin = _gdn_delta(jnp.pad(act, pad_t), jnp.pad(bg, pad_t), s0)
    o = o[:, :nq].reshape(1, db * nq, GDN_V_W)
    x = _gdn_out(o, p, C_CONV // GDN_V_W, norm_w, w_out, x, mod[2], gains[1])
    return x, xp[:, -(CONV_W - 1):], s_fin


def kernel(x_prompt, x_sample, c_prompt, c_sample, cache_kv_cmp, cache_kv_slc, cache_kv_win, state_conv, state_ssm,
           page_table, rel_bias, norm_gains, w_ada, b_ada, w_ffn_in, w_ffn_out, nsa_w_in, nsa_cmp_pe, nsa_cmp_w1,
           nsa_cmp_b1, nsa_cmp_w2, nsa_w_out, gdn_w_in, gdn_conv_w, gdn_a_log, gdn_dt_bias, gdn_norm_w, gdn_w_out):
    depth = w_ada.shape[0]
    bp, t, d = x_prompt.shape
    db, nq, _ = x_sample.shape
    assert nq + CONV_W - 1 <= 8 and nq <= GDN_CHUNK

    c_all = jnp.concatenate([c_prompt, c_sample], axis=0)
    rows_pad = -(-c_all.shape[0] // 8) * 8
    ada = _adaln(jnp.pad(c_all, ((0, rows_pad - c_all.shape[0]), (0, 0))), w_ada, b_ada)
    ada = ada.reshape(depth, rows_pad, 6, d)
    tb, lut = _bias_tables(rel_bias)
    far_hi, far_lo = _split2(rel_bias[N_BUCKETS - 1])
    bias_cols = jnp.zeros((NSA_HEADS, NSA_HEAD_DIM), BF16).at[:, 0].set(far_hi).at[:, 1].set(far_lo)

    xp = x_prompt
    xs = x_sample.reshape(1, db * nq, d)
    kvc_p, kvc_s, kvs_p, kvs_s, kvw_p, kvw_s, cv_p, cv_s, ss_p, ss_s = ([] for _ in range(10))
    for i in range(depth):
        mod_p = [ada[i, :bp, k][:, None, :] for k in range(6)]
        mod_s = [jnp.repeat(ada[i, bp:bp + db, k], nq, axis=0)[None] for k in range(6)]
        gains = norm_gains[i]
        l = i // 2
        if i % 2 == 0:
            w_in = jnp.concatenate([nsa_w_in[l][:, :NSA_Q_W] * (NSA_HEAD_DIM ** -0.5), nsa_w_in[l][:, NSA_Q_W:]], axis=1)
            w_in = jnp.pad(w_in, ((0, 0), (0, -w_in.shape[1] % LANES))).astype(BF16)
            cmp_w = _compress_weights(nsa_cmp_pe[l], nsa_cmp_w1[l], nsa_cmp_b1[l], nsa_cmp_w2[l])
            w_out = nsa_w_out[l].astype(BF16)
            xp, a, bq, cq = _nsa_prompt(xp, mod_p, gains, w_in, cmp_w, w_out, tb, lut, bias_cols)
            kvc_p.append(a); kvs_p.append(bq); kvw_p.append(cq)
            xs, a, bq, cq = _nsa_sample(xs, mod_s, gains, w_in, cmp_w, w_out, lut, rel_bias, bias_cols, cache_kv_cmp[l],
                                        cache_kv_slc[l], cache_kv_win[l], page_table, db, nq)
            kvc_s.append(a); kvs_s.append(bq); kvw_s.append(cq)
        else:
            w_in = jnp.pad(gdn_w_in[l], ((0, 0), (0, -gdn_w_in.shape[2] % LANES))).astype(BF16)
            gdn_w = (w_in, gdn_conv_w[l], gdn_a_log[l], gdn_dt_bias[l], gdn_norm_w[l], gdn_w_out[l].astype(BF16))
            xp, a, bq = _gdn_prompt(xp, mod_p, gains, *gdn_w)
            cv_p.append(a); ss_p.append(bq)
            xs, a, bq = _gdn_sample(xs, mod_s, gains, *gdn_w, state_conv[l], state_ssm[l], db, nq)
            cv_s.append(a); ss_s.append(bq)
        w_ffn = (w_ffn_in[i].astype(BF16), w_ffn_out[i].astype(BF16))
        xp = _ffn(xp, mod_p, gains, *w_ffn)
        xs = _ffn(xs, mod_s, gains, *w_ffn)
    return (xp, xs.reshape(db, nq, d), jnp.stack(kvc_p), jnp.stack(kvc_s), jnp.stack(kvs_p), jnp.stack(kvs_s),
            jnp.stack(kvw_p), jnp.stack(kvw_s), jnp.stack(cv_p), jnp.stack(cv_s), jnp.stack(ss_p), jnp.stack(ss_s))
```

```python
import functools
import math

import numpy as np
import jax
import jax.numpy as jnp
from jax import lax
from jax.experimental import pallas as pl
from jax.experimental.pallas import tpu as pltpu

F32 = jnp.float32
BF16 = jnp.bfloat16

D_MODEL = 1024
RMS_EPS = 1e-6
D_FF = 2816
NSA_HEADS = 16
NSA_HEAD_DIM = 64
NSA_KV_GROUPS = 4
NSA_HPG = 4
CMP_BLOCK = 32
CMP_STRIDE = 16
CMP_HID = 256
SLC_BLOCK = 64
N_SELECT = 16
WINDOW = 512
Q_BLOCK = 128
PAGE_SIZE = 128
N_BUCKETS = 32
GDN_QK_HEADS = 8
GDN_V_HEADS = 16
GDN_HEAD_DIM = 128
CONV_W = 4
GDN_CHUNK = 64
NSA_Q_W = 1024
NSA_KV_W = 512
C_CONV = 4096
GDN_V_W = 2048

LANES = 128
SEG_W = CMP_STRIDE * NSA_KV_W
SEGS_PER_PAGE = PAGE_SIZE // CMP_STRIDE
NEG = -1e30
VMEM_LIMIT = 48 * 1024 * 1024

_BUCKET_THR = (19, 21, 24, 27, 31, 35, 40, 46, 52, 59, 67, 77, 87, 99, 113)
FAR_DIST = 128


def _nt(a, b):
    return lax.dot_general(a, b, (((1,), (1,)), ((), ())), preferred_element_type=F32)


def _split2(x):
    hi = x.astype(BF16)
    lo = (x - hi.astype(F32)).astype(BF16)
    return hi, lo


def _split3(x):
    hi = x.astype(BF16)
    r = x - hi.astype(F32)
    mid = r.astype(BF16)
    lo = (r - mid.astype(F32)).astype(BF16)
    return hi, mid, lo


MXU_WIDTH = 256
MAX_TN = 2816


def _pick_tn(n):
    units = n // LANES
    cands = [d * LANES for d in range(1, units + 1) if units % d == 0 and d * LANES <= MAX_TN]
    full = [c for c in cands if c % MXU_WIDTH == 0]
    return max(full) if full and 2 * max(full) >= max(cands) else max(cands)


def _adaln_kernel(c_ref, w_ref, b_ref, o_ref):
    c = c_ref[...]
    a = (c * jax.nn.sigmoid(c)).astype(BF16)
    o_ref[...] = jnp.dot(a, w_ref[...].astype(BF16), preferred_element_type=F32) + b_ref[...]


def _adaln(c_all, w_ada, b_ada):
    depth, d, n = w_ada.shape
    rows = c_all.shape[0]
    tn = 768
    return pl.pallas_call(
        _adaln_kernel,
        grid=(depth, n // tn),
        in_specs=[pl.BlockSpec((rows, d), lambda l, j: (0, 0)),
                  pl.BlockSpec((None, d, tn), lambda l, j: (l, 0, j)),
                  pl.BlockSpec((None, 1, tn), lambda l, j: (l, 0, j))],
        out_specs=pl.BlockSpec((None, rows, tn), lambda l, j: (l, 0, j)),
        out_shape=jax.ShapeDtypeStruct((depth, rows, n), F32),
        compiler_params=pltpu.CompilerParams(dimension_semantics=("parallel", "parallel")),
        name="adaln",
    )(c_all, w_ada, b_ada.reshape(depth, 1, n))


def _mod_norm(x, gain, shift, scale):
    ms = jnp.mean(x * x, axis=-1, keepdims=True)
    y = x * lax.rsqrt(ms + RMS_EPS) * gain
    return y * (1.0 + scale) + shift


def _nml_kernel(x_ref, g_ref, sh_ref, sc_ref, w_ref, o_ref, h_ref):
    @pl.when(pl.program_id(2) == 0)
    def _():
        h_ref[...] = _mod_norm(x_ref[...], g_ref[...], sh_ref[...], sc_ref[...]).astype(BF16)

    o_ref[...] = jnp.dot(h_ref[...], w_ref[...], preferred_element_type=F32).astype(o_ref.dtype)


def _nml_swiglu_kernel(x_ref, g_ref, sh_ref, sc_ref, wg_ref, wu_ref, o_ref, h_ref):
    @pl.when(pl.program_id(2) == 0)
    def _():
        h_ref[...] = _mod_norm(x_ref[...], g_ref[...], sh_ref[...], sc_ref[...]).astype(BF16)

    h = h_ref[...]
    gate = jnp.dot(h, wg_ref[...], preferred_element_type=F32)
    up = jnp.dot(h, wu_ref[...], preferred_element_type=F32)
    o_ref[...] = (gate * jax.nn.sigmoid(gate) * up).astype(o_ref.dtype)


def _mod_spec(mod, tm):
    if mod.shape[1] == 1:
        return pl.BlockSpec((None, 1, mod.shape[2]), lambda b, i, *_: (b, 0, 0))
    return pl.BlockSpec((None, tm, mod.shape[2]), lambda b, i, *_: (b, i, 0))


def _row_tile(t):
    return 512 if t % 512 == 0 else t


def _norm_mod_linear(x, gain, shift, scale, w, out_dtype=F32):
    b, t, d = x.shape
    n = w.shape[1]
    tm, tn = _row_tile(t), _pick_tn(n)
    return pl.pallas_call(
        _nml_kernel,
        grid=(b, t // tm, n // tn),
        in_specs=[pl.BlockSpec((None, tm, d), lambda b, i, j: (b, i, 0)),
                  pl.BlockSpec((1, d), lambda b, i, j: (0, 0)),
                  _mod_spec(shift, tm), _mod_spec(scale, tm),
                  pl.BlockSpec((d, tn), lambda b, i, j: (0, j))],
        out_specs=pl.BlockSpec((None, tm, tn), lambda b, i, j: (b, i, j)),
        out_shape=jax.ShapeDtypeStruct((b, t, n), out_dtype),
        scratch_shapes=[pltpu.VMEM((tm, d), BF16)],
        compiler_params=pltpu.CompilerParams(
            dimension_semantics=("parallel", "parallel", "arbitrary"), vmem_limit_bytes=VMEM_LIMIT),
        name="norm_mod_linear",
    )(x, gain.reshape(1, d), shift, scale, w)


def _norm_mod_swiglu(x, gain, shift, scale, w_in):
    b, t, d = x.shape
    nf = w_in.shape[1] // 2
    tm, tn = _row_tile(t), _pick_tn(nf)
    nj = nf // tn
    return pl.pallas_call(
        _nml_swiglu_kernel,
        grid=(b, t // tm, nj),
        in_specs=[pl.BlockSpec((None, tm, d), lambda b, i, j: (b, i, 0)),
                  pl.BlockSpec((1, d), lambda b, i, j: (0, 0)),
                  _mod_spec(shift, tm), _mod_spec(scale, tm),
                  pl.BlockSpec((d, tn), lambda b, i, j: (0, j)),
                  pl.BlockSpec((d, tn), lambda b, i, j: (0, j + nj))],
        out_specs=pl.BlockSpec((None, tm, tn), lambda b, i, j: (b, i, j)),
        out_shape=jax.ShapeDtypeStruct((b, t, nf), BF16),
        scratch_shapes=[pltpu.VMEM((tm, d), BF16)],
        compiler_params=pltpu.CompilerParams(
            dimension_semantics=("parallel", "parallel", "arbitrary"), vmem_limit_bytes=VMEM_LIMIT),
        name="norm_mod_swiglu",
    )(x, gain.reshape(1, d), shift, scale, w_in, w_in)


def _rms_gated_residual(y, x, gate, gain):
    ms = jnp.mean(y * y, axis=-1, keepdims=True)
    return x + gate * (y * lax.rsqrt(ms + RMS_EPS) * gain)


def _mrr_kernel(a_ref, w_ref, x_ref, gate_ref, gain_ref, o_ref):
    y = jnp.dot(a_ref[...].astype(BF16), w_ref[...], preferred_element_type=F32)
    o_ref[...] = _rms_gated_residual(y, x_ref[...], gate_ref[...], gain_ref[...])


def _matmul_rms_residual(a, w, x, gate, gain):
    b, t, k = a.shape
    d = w.shape[1]
    tm = _row_tile(t)
    return pl.pallas_call(
        _mrr_kernel,
        grid=(b, t // tm),
        in_specs=[pl.BlockSpec((None, tm, k), lambda b, i: (b, i, 0)),
                  pl.BlockSpec((k, d), lambda b, i: (0, 0)),
                  pl.BlockSpec((None, tm, d), lambda b, i: (b, i, 0)),
                  _mod_spec(gate, tm),
                  pl.BlockSpec((1, d), lambda b, i: (0, 0))],
        out_specs=pl.BlockSpec((None, tm, d), lambda b, i: (b, i, 0)),
        out_shape=jax.ShapeDtypeStruct((b, t, d), F32),
        compiler_params=pltpu.CompilerParams(
            dimension_semantics=("parallel", "parallel"), vmem_limit_bytes=VMEM_LIMIT),
        name="matmul_rms_residual",
    )(a, w, x, gate, gain.reshape(1, d))


def _gdn_out_kernel(o_ref, z_ref, nw_ref, w_ref, x_ref, gate_ref, gain_ref, out_ref, a_ref):
    nw = nw_ref[...]
    for h in range(GDN_V_HEADS):
        sl = slice(h * GDN_HEAD_DIM, (h + 1) * GDN_HEAD_DIM)
        o = o_ref[:, sl]
        z = z_ref[:, sl]
        ms = jnp.mean(o * o, axis=-1, keepdims=True)
        a_ref[:, sl] = ((o * lax.rsqrt(ms + RMS_EPS) * nw) * (z * jax.nn.sigmoid(z))).astype(BF16)
    y = jnp.dot(a_ref[...], w_ref[...], preferred_element_type=F32)
    out_ref[...] = _rms_gated_residual(y, x_ref[...], gate_ref[...], gain_ref[...])


def _gdn_out(o, p, z_col_block, norm_w, w, x, gate, gain):
    b, t, k = o.shape
    d = w.shape[1]
    tm = _row_tile(t)
    return pl.pallas_call(
        _gdn_out_kernel,
        grid=(b, t // tm),
        in_specs=[pl.BlockSpec((None, tm, k), lambda b, i: (b, i, 0)),
                  pl.BlockSpec((None, tm, k), lambda b, i: (b, i, z_col_block)),
                  pl.BlockSpec((1, GDN_HEAD_DIM), lambda b, i: (0, 0)),
                  pl.BlockSpec((k, d), lambda b, i: (0, 0)),
                  pl.BlockSpec((None, tm, d), lambda b, i: (b, i, 0)),
                  _mod_spec(gate, tm),
                  pl.BlockSpec((1, d), lambda b, i: (0, 0))],
        out_specs=pl.BlockSpec((None, tm, d), lambda b, i: (b, i, 0)),
        out_shape=jax.ShapeDtypeStruct((b, t, d), F32),
        scratch_shapes=[pltpu.VMEM((tm, k), BF16)],
        compiler_params=pltpu.CompilerParams(
            dimension_semantics=("parallel", "parallel"), vmem_limit_bytes=VMEM_LIMIT),
        name="gdn_out",
    )(o, p, norm_w.reshape(1, GDN_HEAD_DIM), w, x, gate, gain.reshape(1, d))


def _bucket_of(n):
    big = jnp.full(n.shape, 16, jnp.int32)
    for thr in _BUCKET_THR:
        big = big + (n >= thr).astype(jnp.int32)
    return jnp.where(n < 16, n, big)


def _bias_tab_kernel(tbl_ref, tb_ref, lut_ref):
    h = pl.program_id(0)
    far = tbl_ref[N_BUCKETS - 1, h]

    def lookup(dist):
        bkt = _bucket_of(jnp.maximum(dist, 0))
        out = jnp.zeros(dist.shape, F32)
        for bb in range(N_BUCKETS):
            out = jnp.where(bkt == bb, tbl_ref[bb, h], out)
        return out - far

    qi = lax.broadcasted_iota(jnp.int32, (Q_BLOCK, 2 * Q_BLOCK), 0)
    kj = lax.broadcasted_iota(jnp.int32, (Q_BLOCK, 2 * Q_BLOCK), 1)
    dist = Q_BLOCK + qi - kj
    tb_ref[...] = jnp.where(dist >= 0, lookup(dist), NEG)
    lut_ref[...] = lookup(lax.broadcasted_iota(jnp.int32, (8, LANES), 1))


def _bias_tables(rel_bias):
    return pl.pallas_call(
        _bias_tab_kernel,
        grid=(NSA_HEADS,),
        in_specs=[pl.BlockSpec(memory_space=pltpu.SMEM)],
        out_specs=[pl.BlockSpec((None, Q_BLOCK, 2 * Q_BLOCK), lambda h: (h, 0, 0)),
                   pl.BlockSpec((None, 8, LANES), lambda h: (h, 0, 0))],
        out_shape=[jax.ShapeDtypeStruct((NSA_HEADS, Q_BLOCK, 2 * Q_BLOCK), F32),
                   jax.ShapeDtypeStruct((NSA_HEADS, 8, LANES), F32)],
        compiler_params=pltpu.CompilerParams(dimension_semantics=("parallel",)),
        name="bias_tables",
    )(rel_bias)


def _lut_gather(lut_rows, dist):
    idx = jnp.clip(dist, 0, LANES - 1)
    val = jnp.take_along_axis(lut_rows, idx, axis=1)
    return jnp.where((dist >= 0) & (dist < FAR_DIST), val, 0.0)


def _pe_term_kernel(pe_ref, wbd_ref, b1_ref, o_ref):
    y = jnp.dot(pe_ref[...], wbd_ref[...], preferred_element_type=F32)
    o_ref[...] = y[:, 0:CMP_HID] + y[:, 3 * CMP_HID:4 * CMP_HID] + b1_ref[...]


def _pe_term(pe_x, wbd, b1):
    return pl.pallas_call(
        _pe_term_kernel,
        grid=(2,),
        in_specs=[pl.BlockSpec((None, 8, 2048), lambda k: (k, 0, 0)),
                  pl.BlockSpec((None, 2048, 1024), lambda k: (k, 0, 0)),
                  pl.BlockSpec((None, 1, CMP_HID), lambda k: (k, 0, 0))],
        out_specs=pl.BlockSpec((None, 8, CMP_HID), lambda k: (k, 0, 0)),
        out_shape=jax.ShapeDtypeStruct((2, 8, CMP_HID), F32),
        compiler_params=pltpu.CompilerParams(dimension_semantics=("parallel",), vmem_limit_bytes=VMEM_LIMIT),
        name="cmp_pe_term",
    )(pe_x, wbd, b1.reshape(2, 1, CMP_HID))


def _compress_kernel(tab_ref, *refs, npg, rows_minor):
    del tab_ref
    pages = refs[:npg]
    wbd_ref, w2_ref, pe_ref, cc_ref, kc_ref, vc_ref, xs_ref, carry_ref = refs[npg:npg + 8]
    ts = SEGS_PER_PAGE * npg

    @pl.when(pl.program_id(1) == 0)
    def _():
        carry_ref[...] = jnp.zeros(carry_ref.shape, F32)

    row0 = lax.broadcasted_iota(jnp.int32, (ts, CMP_HID), 0) == 0
    low_half = lax.broadcasted_iota(jnp.int32, (ts, LANES), 1) < NSA_HEAD_DIM
    half = CMP_STRIDE * NSA_HEAD_DIM
    for k in range(2):
        out_ref = kc_ref if k == 0 else vc_ref
        for gp in range(2):
            if rows_minor:
                rt_ref = refs[npg + 8].at[gp]
                xb_ref = xs_ref.at[gp]
                for j, pg in enumerate(pages):
                    rt_ref[j * PAGE_SIZE:(j + 1) * PAGE_SIZE, :] = pg[k, gp].astype(BF16).T.astype(F32)
                for j in range(CMP_STRIDE // 2):
                    ra = rt_ref[pl.ds(2 * j, ts, stride=CMP_STRIDE), :]
                    rb = rt_ref[pl.ds(2 * j + 1, ts, stride=CMP_STRIDE), :]
                    lanes = slice(j * LANES, (j + 1) * LANES)
                    xb_ref[:, lanes] = jnp.where(low_half, ra, pltpu.roll(rb, NSA_HEAD_DIM, axis=1)).astype(BF16)
                    xb_ref[:, half + j * LANES:half + (j + 1) * LANES] = jnp.where(
                        low_half, pltpu.roll(ra, NSA_HEAD_DIM, axis=1), rb).astype(BF16)
                ys = [jnp.dot(xb_ref[:, g2 * half:(g2 + 1) * half], wbd_ref[k], preferred_element_type=F32)
                      for g2 in range(2)]
            else:
                xb_ref = xs_ref.at[gp]
                off = k * 256 + gp * LANES
                for s in range(CMP_STRIDE):
                    lo = s * NSA_KV_W + off
                    piece = jnp.concatenate([pg[:, lo:lo + LANES] for pg in pages], axis=0)
                    xb_ref[:, s * LANES:(s + 1) * LANES] = piece.astype(BF16)
                y = jnp.dot(xb_ref[...], wbd_ref[k], preferred_element_type=F32)
                ys = [y[:, :2 * CMP_HID], y[:, 2 * CMP_HID:]]
            hs = []
            for g2 in range(2):
                pa = ys[g2][:, :CMP_HID]
                pb = ys[g2][:, CMP_HID:]
                ci = (k * 2 + gp) * 2 + g2
                prev = carry_ref[ci]
                pa_prev = jnp.where(row0, prev[7:8, :], pltpu.roll(pa, 1, axis=0))
                carry_ref[ci] = pa[ts - 8:ts, :]
                hs.append(jax.nn.gelu(pa_prev + pb + pe_ref[k, 0:1, :]))
            hid = jnp.concatenate(hs, axis=-1).astype(BF16)
            o = jnp.dot(hid, w2_ref[k], preferred_element_type=F32) + cc_ref[k]
            out_ref[2 * gp] = o[:, :LANES].astype(BF16)
            out_ref[2 * gp + 1] = o[:, LANES:].astype(BF16)


def _compress(pages, table, wbd, w1cat, w2bd, pe_term, ccols):
    bc, n_pages = table.shape
    npg = min(32, n_pages)
    ts = SEGS_PER_PAGE * npg
    nseg = n_pages * SEGS_PER_PAGE
    rows_minor = pages.ndim == 5
    w_first = w1cat if rows_minor else wbd
    page_block = (None,) + pages.shape[1:]

    def page_spec(j):
        return pl.BlockSpec(page_block, lambda b, i, tab: (tab[b, i * npg + j],) + (0,) * (pages.ndim - 1))

    const = lambda *shape: pl.BlockSpec(shape, lambda b, i, tab: (0,) * len(shape), pipeline_mode=pl.Buffered(1))
    out_spec = pl.BlockSpec((None, NSA_KV_GROUPS, ts, LANES), lambda b, i, tab: (b, 0, i, 0))
    grid_spec = pltpu.PrefetchScalarGridSpec(
        num_scalar_prefetch=1,
        grid=(bc, n_pages // npg),
        in_specs=[page_spec(j) for j in range(npg)] + [
            const(*w_first.shape), const(2, 512, 256), const(2, 8, CMP_HID), const(2, 1, 256)],
        out_specs=[out_spec, out_spec],
        scratch_shapes=[pltpu.VMEM((2, ts, 2048), BF16), pltpu.VMEM((8, 8, CMP_HID), F32)] + (
            [pltpu.VMEM((2, npg * PAGE_SIZE, LANES), F32)] if rows_minor else []),
    )
    out_sds = jax.ShapeDtypeStruct((bc, NSA_KV_GROUPS, nseg, LANES), BF16)
    return pl.pallas_call(
        functools.partial(_compress_kernel, npg=npg, rows_minor=rows_minor),
        grid_spec=grid_spec,
        out_shape=[out_sds, out_sds],
        compiler_params=pltpu.CompilerParams(
            dimension_semantics=("parallel", "arbitrary"), vmem_limit_bytes=VMEM_LIMIT),
        name="kv_compress",
    )(table, *([pages] * npg), w_first, w2bd, pe_term, ccols)


def _compress_weights(pe, w1, b1, w2):
    eye2 = jnp.eye(2, dtype=F32)
    w = w1.reshape(2, 2, CMP_STRIDE, NSA_HEAD_DIM, CMP_HID)
    wbd = jnp.einsum('kasdh,gj->ksgdjah', w, eye2).reshape(2, 2048, 1024).astype(BF16)
    w1cat = w.transpose(0, 2, 3, 1, 4).reshape(2, CMP_STRIDE * NSA_HEAD_DIM, 2 * CMP_HID).astype(BF16)
    w2p = jnp.pad(w2, ((0, 0), (0, 0), (0, LANES - NSA_HEAD_DIM)))
    w2bd = jnp.einsum('khd,gj->kghjd', w2p, eye2).reshape(2, 512, 256).astype(BF16)
    pe_x = pe.reshape(2, 2, CMP_STRIDE, NSA_HEAD_DIM).transpose(0, 2, 1, 3).reshape(2, 1, 2048)
    pe_x = jnp.broadcast_to(pe_x, (2, 8, 2048)).astype(BF16)
    pe_term = _pe_term(pe_x, wbd, b1)
    cc = np.zeros((2, 1, 256), np.float32)
    for g2 in range(2):
        cc[0, 0, g2 * LANES + 64] = 1.0
        cc[0, 0, g2 * LANES + 65] = 1.0
        cc[1, 0, g2 * LANES + 64] = 1.0
    return wbd, w1cat, w2bd, pe_term, jnp.asarray(cc)


def _topk_rows_mask(score, k, fillers=(), preselected=None):
    blk = lax.broadcasted_iota(jnp.int32, score.shape, 0).astype(F32)
    sel = jnp.zeros(score.shape, F32) if preselected is None else preselected.astype(F32)
    fillers = list(fillers)
    for it in range(k):
        mx = jnp.max(score, axis=0, keepdims=True)
        idx = jnp.min(jnp.where(score == mx, blk, 1e9), axis=0, keepdims=True)
        hit = blk == idx
        sel = jnp.where(hit, 1.0, sel)
        score = jnp.where(hit, -jnp.inf, score)
        for f in fillers[it * len(fillers) // k:(it + 1) * len(fillers) // k]:
            f()
    return sel


def _nsa_prompt_kernel(q_ref, gt_ref, kc_ref, vc_ref, ks_ref, vs_ref, kw_ref, vw_ref, tb_ref, lut_ref, ovt_ref,
                       o_ref, qaug_ref, sc_ref, pcb_ref, pcs_ref, sw_ref, pw_ref, pfar_ref, *, nseg, n_sb):
    g = pl.program_id(1)
    qb = pl.program_id(2)
    s0 = qb * Q_BLOCK
    rows = NSA_HPG * Q_BLOCK
    q = q_ref[...].reshape(rows, LANES)
    tb = tb_ref[...]

    sc_ref[...] = _nt(q, kc_ref[...])
    qi128 = lax.broadcasted_iota(jnp.int32, (Q_BLOCK, LANES), 0)
    li128 = lax.broadcasted_iota(jnp.int32, (Q_BLOCK, LANES), 1)

    def add_near_bias(chunk):
        l0 = pl.multiple_of(chunk * LANES, LANES)
        dist = s0 + qi128 - CMP_STRIDE * (chunk * LANES + li128) - (CMP_STRIDE - 1)
        for hh in range(NSA_HPG):
            lut = jnp.broadcast_to(lut_ref[hh, 0:1, :], (Q_BLOCK, LANES))
            rs = slice(hh * Q_BLOCK, (hh + 1) * Q_BLOCK)
            sc_ref[rs, pl.ds(l0, LANES)] = sc_ref[rs, pl.ds(l0, LANES)] + _lut_gather(lut, dist)

    chunk_lo = jnp.maximum(8 * qb - 8, 0) // LANES
    chunk_hi = (8 * qb + 7) // LANES
    add_near_bias(chunk_lo)

    @pl.when(chunk_hi != chunk_lo)
    def _():
        add_near_bias(chunk_hi)

    rb_n = 32
    qi_b = lax.broadcasted_iota(jnp.int32, (rb_n, nseg), 0)
    mi_b = lax.broadcasted_iota(jnp.int32, (rb_n, nseg), 1)
    for rb in range(Q_BLOCK // rb_n):
        dist_b = s0 + rb * rb_n + qi_b - CMP_STRIDE * mi_b - (CMP_STRIDE - 1)
        valid_b = (dist_b >= 0) & (mi_b >= 1)
        pcs_b = jnp.zeros((rb_n, nseg), F32)
        for hh in range(NSA_HPG):
            rs = slice(hh * Q_BLOCK + rb * rb_n, hh * Q_BLOCK + (rb + 1) * rb_n)
            s_b = jnp.where(valid_b, sc_ref[rs, :], NEG)
            e = jnp.where(valid_b, jnp.exp(s_b - jnp.max(s_b, axis=-1, keepdims=True)), 0.0)
            ssum = jnp.sum(e, axis=-1, keepdims=True)
            pc = e * (1.0 / jnp.where(ssum > 0, ssum, 1.0))
            pcb_ref[rs, :] = pc.astype(BF16)
            pcs_b = pcs_b + pc
        pcs_ref[rb * rb_n:(rb + 1) * rb_n, :] = pcs_b
    o_c = jnp.dot(pcb_ref[...], vc_ref[...], preferred_element_type=F32)

    pcs = pcs_ref[...]
    hi, lo = _split2(pcs)
    ovt = ovt_ref[...]
    imp_t = _nt(ovt, hi) + _nt(ovt, lo)
    jb = lax.broadcasted_iota(jnp.int32, (LANES, Q_BLOCK), 0)
    q_blk = (s0 + lax.broadcasted_iota(jnp.int32, (LANES, Q_BLOCK), 1)) // SLC_BLOCK
    forced = (jb == 0) | (jb == q_blk) | (jb == q_blk - 1)
    score = jnp.where(forced, -jnp.inf, jnp.where(jb <= q_blk, imp_t, -1e4))
    score = jnp.where(jb < n_sb, score, -3e38)
    n_forced = 3

    w0 = pl.multiple_of(s0, Q_BLOCK)
    n_w = WINDOW + Q_BLOCK
    sw_ref[...] = _nt(q, kw_ref[pl.ds(w0, n_w), :])
    qi_w = lax.broadcasted_iota(jnp.int32, (rb_n, n_w), 0)
    kk_w = lax.broadcasted_iota(jnp.int32, (rb_n, n_w), 1)

    def window_block(hh, rb):
        ok_w = (kk_w > qi_w + rb * rb_n) & (s0 + kk_w >= WINDOW)
        rs = slice(hh * Q_BLOCK + rb * rb_n, hh * Q_BLOCK + (rb + 1) * rb_n)
        s_b = jnp.where(ok_w, sw_ref[rs, :], NEG)
        s_b = jnp.concatenate([s_b[:, :WINDOW - Q_BLOCK],
                               s_b[:, WINDOW - Q_BLOCK:] + tb_ref[hh, rb * rb_n:(rb + 1) * rb_n, :]], axis=-1)
        pw_ref[rs, :] = jnp.exp(s_b - jnp.max(s_b, axis=-1, keepdims=True)).astype(BF16)

    gt = jax.nn.sigmoid(gt_ref[...])
    glane = lax.broadcasted_iota(jnp.int32, gt.shape, 1)
    gates = [[None] * 3 for _ in range(NSA_HPG)]

    def gate_column(hh, br):
        col = 3 * (NSA_HPG * g + hh) + br
        gates[hh][br] = jnp.sum(jnp.where(glane == col, gt, 0.0), axis=-1, keepdims=True)

    fillers = []
    for hh in range(NSA_HPG):
        fillers += [functools.partial(window_block, hh, rb) for rb in range(Q_BLOCK // rb_n)]
        fillers += [functools.partial(gate_column, hh, br) for br in range(3)]
    sel_t = _topk_rows_mask(score, min(N_SELECT, n_sb) - n_forced, fillers, preselected=forced)
    acc_w = jnp.dot(pw_ref[...], vw_ref[pl.ds(w0, n_w), :], preferred_element_type=F32)
    o_w = acc_w[:, :NSA_HEAD_DIM] / acc_w[:, NSA_HEAD_DIM:NSA_HEAD_DIM + 1]
    unsel = 1.0 - sel_t.T
    blk_lane = lax.broadcasted_iota(jnp.int32, (Q_BLOCK, LANES), 1)
    near_blk0 = 2 * qb - 2
    unsel_far = jnp.where(blk_lane >= near_blk0, 1.0, unsel)

    qaug_ref[:, LANES:] = q
    for hh in range(NSA_HPG):
        qaug_ref[hh * Q_BLOCK:(hh + 1) * Q_BLOCK, :LANES] = unsel.astype(BF16)
    a0 = pl.multiple_of(jnp.maximum(s0 - Q_BLOCK, 0), Q_BLOCK)
    b0 = pl.multiple_of(s0, Q_BLOCK)
    kn = jnp.concatenate([ks_ref[pl.ds(a0, Q_BLOCK), :], ks_ref[pl.ds(b0, Q_BLOCK), :]], axis=0)
    vn = jnp.concatenate([vs_ref[pl.ds(a0, Q_BLOCK), :], vs_ref[pl.ds(b0, Q_BLOCK), :]], axis=0)
    kcol = lax.broadcasted_iota(jnp.int32, (Q_BLOCK, 2 * Q_BLOCK), 1)
    no_prev = jnp.where((kcol < Q_BLOCK) & (qb == 0), NEG, 0.0)
    s_n = _nt(qaug_ref[...], kn).reshape(NSA_HPG, Q_BLOCK, 2 * Q_BLOCK) + (tb + no_prev[None])
    s_n = s_n.reshape(rows, 2 * Q_BLOCK)
    m_run = jnp.max(s_n, axis=-1, keepdims=True)
    acc = jnp.dot(jnp.exp(s_n - m_run).astype(BF16), vn, preferred_element_type=F32)

    for hh in range(NSA_HPG):
        qaug_ref[hh * Q_BLOCK:(hh + 1) * Q_BLOCK, :LANES] = unsel_far.astype(BF16)
    kc_far = 4 * Q_BLOCK
    n_far = (jnp.maximum(qb - 1, 0) + 3) // 4

    pfar_ref[1] = jnp.zeros((rows, kc_far), BF16)

    def far_logits(c):
        return _nt(qaug_ref[...], ks_ref[pl.ds(pl.multiple_of(c * kc_far, kc_far), kc_far), :])

    def far_pv(slot, c):
        k0 = pl.multiple_of(jnp.maximum(c, 0) * kc_far, kc_far)
        return jnp.dot(pfar_ref[slot], vs_ref[pl.ds(k0, kc_far), :], preferred_element_type=F32)

    def far_trip(t, carry):
        m_old, acc_old, alpha_prev = carry
        s_a = far_logits(2 * t)
        acc_1 = alpha_prev * acc_old + far_pv(1, 2 * t - 1)
        s_b = far_logits(2 * t + 1)
        m_a = jnp.maximum(m_old, jnp.max(s_a, axis=-1, keepdims=True))
        pfar_ref[0] = jnp.exp((s_a - m_a).astype(BF16))
        acc_2 = jnp.exp(m_old - m_a) * acc_1 + far_pv(0, 2 * t)
        m_b = jnp.maximum(m_a, jnp.max(s_b, axis=-1, keepdims=True))
        pfar_ref[1] = jnp.exp((s_b - m_b).astype(BF16))
        return m_b, acc_2, jnp.exp(m_a - m_b)

    n_trips = (n_far + 1) // 2
    m_run, acc, alpha_last = lax.fori_loop(0, n_trips, far_trip, (m_run, acc, jnp.ones((rows, 1), F32)))
    acc = alpha_last * acc + far_pv(1, 2 * n_trips - 1)
    o_s = acc[:, :NSA_HEAD_DIM] / acc[:, NSA_HEAD_DIM:NSA_HEAD_DIM + 1]

    for hh in range(NSA_HPG):
        rs = slice(hh * Q_BLOCK, (hh + 1) * Q_BLOCK)
        o = gates[hh][0] * o_c[rs, :NSA_HEAD_DIM] + gates[hh][1] * o_s[rs] + gates[hh][2] * o_w[rs]
        o_ref[hh] = o.astype(o_ref.dtype)


def _overlap_t(n_blk_pad, nseg):
    m = np.arange(nseg)[None, :]
    j = np.arange(n_blk_pad)[:, None]
    c_start = CMP_STRIDE * m - CMP_STRIDE
    c_end = CMP_STRIDE * m + CMP_STRIDE - 1
    ov = (c_start < j * SLC_BLOCK + SLC_BLOCK) & (c_end >= j * SLC_BLOCK) & (m >= 1)
    return ov.astype(np.float32)


def _nsa_prompt_attention(q128, p, gate_col_block, kc, vc, ks, vs, kw, vw, tb, lut):
    b, _, t, _ = q128.shape
    nseg = kc.shape[2]
    n_sb = t // SLC_BLOCK
    assert n_sb <= LANES and t % (4 * Q_BLOCK) == 0 and nseg % LANES == 0
    ovt = jnp.asarray(_overlap_t(LANES, nseg), dtype=BF16)
    per_bg = lambda rows, cols: pl.BlockSpec((None, None, rows, cols), lambda b, g, i: (b, g, 0, 0))
    return pl.pallas_call(
        functools.partial(_nsa_prompt_kernel, nseg=nseg, n_sb=n_sb),
        grid=(b, NSA_KV_GROUPS, t // Q_BLOCK),
        in_specs=[pl.BlockSpec((None, NSA_HPG, Q_BLOCK, LANES), lambda b, g, i: (b, g, i, 0)),
                  pl.BlockSpec((None, Q_BLOCK, LANES), lambda b, g, i: (b, i, gate_col_block)),
                  per_bg(nseg, LANES), per_bg(nseg, LANES),
                  per_bg(t, 2 * LANES), per_bg(t, LANES),
                  per_bg(t + WINDOW, LANES), per_bg(t + WINDOW, LANES),
                  pl.BlockSpec((NSA_HPG, Q_BLOCK, 2 * Q_BLOCK), lambda b, g, i: (g, 0, 0)),
                  pl.BlockSpec((NSA_HPG, 8, LANES), lambda b, g, i: (g, 0, 0)),
                  pl.BlockSpec((LANES, nseg), lambda b, g, i: (0, 0))],
        out_specs=pl.BlockSpec((None, NSA_HPG, Q_BLOCK, NSA_HEAD_DIM), lambda b, g, i: (b, g, i, 0)),
        out_shape=jax.ShapeDtypeStruct((b, NSA_HEADS, t, NSA_HEAD_DIM), BF16),
        scratch_shapes=[pltpu.VMEM((NSA_HPG * Q_BLOCK, 2 * LANES), BF16),
                        pltpu.VMEM((NSA_HPG * Q_BLOCK, nseg), F32), pltpu.VMEM((NSA_HPG * Q_BLOCK, nseg), BF16),
                        pltpu.VMEM((Q_BLOCK, nseg), F32),
                        pltpu.VMEM((NSA_HPG * Q_BLOCK, WINDOW + Q_BLOCK), F32),
                        pltpu.VMEM((NSA_HPG * Q_BLOCK, WINDOW + Q_BLOCK), BF16),
                        pltpu.VMEM((2, NSA_HPG * Q_BLOCK, 4 * Q_BLOCK), BF16)],
        compiler_params=pltpu.CompilerParams(
            dimension_semantics=("parallel", "parallel", "arbitrary"), vmem_limit_bytes=VMEM_LIMIT),
        name="nsa_prompt_attention",
    )(q128, p, kc, vc, ks, vs, kw, vw, tb, lut, ovt)


def _gdn_conv_kernel(x_ref, w_ref, o_ref, carry_ref, *, tm, tc):
    j = pl.program_id(1)

    @pl.when(pl.program_id(2) == 0)
    def _():
        carry_ref[...] = jnp.zeros(carry_ref.shape, F32)

    x = x_ref[...]
    w = w_ref[...]
    prev = carry_ref[...]
    row8 = lax.broadcasted_iota(jnp.int32, (8, tc), 0)
    conv = x * w[CONV_W - 1:CONV_W, :]
    for sft in range(1, CONV_W):
        xs = pltpu.roll(x, sft, axis=0)
        top = jnp.where(row8 < sft, pltpu.roll(prev, sft, axis=0), xs[0:8])
        xs = top if tm == 8 else jnp.concatenate([top, xs[8:]], axis=0)
        conv = conv + xs * w[CONV_W - 1 - sft:CONV_W - sft, :]
    carry_ref[...] = x[tm - 8:tm, :]
    act = conv * jax.nn.sigmoid(conv)
    for hd in range(tc // GDN_HEAD_DIM):
        sl = slice(hd * GDN_HEAD_DIM, (hd + 1) * GDN_HEAD_DIM)
        a = act[:, sl]
        col0 = j * tc + hd * GDN_HEAD_DIM
        nrm = a * lax.rsqrt(jnp.sum(a * a, axis=-1, keepdims=True) + 1e-6)
        nrm = nrm * jnp.where(col0 < 1024, GDN_HEAD_DIM ** -0.5, 1.0)
        o_ref[:, sl] = jnp.where(col0 < 2048, nrm, a)


def _gdn_conv(p, conv_w):
    b, t, _ = p.shape
    tm = _row_tile(t)
    tc = 512
    return pl.pallas_call(
        functools.partial(_gdn_conv_kernel, tm=tm, tc=tc),
        grid=(b, C_CONV // tc, t // tm),
        in_specs=[pl.BlockSpec((None, tm, tc), lambda b, j, i: (b, i, j)),
                  pl.BlockSpec((CONV_W, tc), lambda b, j, i: (0, j))],
        out_specs=pl.BlockSpec((None, tm, tc), lambda b, j, i: (b, i, j)),
        out_shape=jax.ShapeDtypeStruct((b, t, C_CONV), F32),
        scratch_shapes=[pltpu.VMEM((8, tc), F32)],
        compiler_params=pltpu.CompilerParams(
            dimension_semantics=("parallel", "parallel", "arbitrary"), vmem_limit_bytes=VMEM_LIMIT),
        name="gdn_conv",
    )(p, conv_w)


def _gdn_gate_kernel(ba_ref, alog_ref, dtb_ref, o_ref):
    x = ba_ref[...]
    y = x + dtb_ref[...]
    softplus = jnp.maximum(y, 0.0) + jnp.log1p(jnp.exp(-jnp.abs(y)))
    g = -jnp.exp(alog_ref[...]) * softplus
    lane = lax.broadcasted_iota(jnp.int32, x.shape, 1)
    o_ref[...] = jnp.where(lane < GDN_V_HEADS, jax.nn.sigmoid(x), g)


def _gdn_gates(p, ba_col_block, a_log, dt_bias):
    b, t, _ = p.shape
    tm = _row_tile(t)
    pad = lambda v: jnp.pad(v.reshape(1, GDN_V_HEADS), ((0, 0), (GDN_V_HEADS, LANES - 2 * GDN_V_HEADS)))
    return pl.pallas_call(
        _gdn_gate_kernel,
        grid=(b, t // tm),
        in_specs=[pl.BlockSpec((None, tm, LANES), lambda b, i: (b, i, ba_col_block)),
                  pl.BlockSpec((1, LANES), lambda b, i: (0, 0)),
                  pl.BlockSpec((1, LANES), lambda b, i: (0, 0))],
        out_specs=pl.BlockSpec((None, tm, LANES), lambda b, i: (b, i, 0)),
        out_shape=jax.ShapeDtypeStruct((b, t, LANES), F32),
        compiler_params=pltpu.CompilerParams(dimension_semantics=("parallel", "parallel")),
        name="gdn_gates",
    )(p, pad(a_log), pad(dt_bias))


def _bdot(a, b):
    return jnp.dot(a.astype(BF16), b.astype(BF16), preferred_element_type=F32)


GDN_PACK = 4
_PACK_ORDER = (0, 2, 1, 3)
_PACK_HEADS = tuple(GDN_PACK * p + o for p in range(GDN_V_HEADS // GDN_PACK) for o in _PACK_ORDER)


def _iota2(shape, axis):
    return lax.broadcasted_iota(jnp.int32, shape, axis)


def _packed_mm(a_cat, b_cat, bd_mask):
    b_bd = jnp.where(bd_mask, jnp.concatenate([b_cat] * GDN_PACK, axis=0), 0.0)
    return _bdot(a_cat, b_bd)


def _unit_lower_inverse_packed(ls, row, col, bd_mask):
    eye = (row == col).astype(F32)
    same16 = (row // 16) == (col // 16)
    same32 = (row // 32) == (col // 32)
    ms = [jnp.where(same16, -l, 0.0) for l in ls]
    ps = [eye + m for m in ms]
    for _ in range(3):
        ms = [_packed_mm(m, m, bd_mask) for m in ms]
        ps = [p + _packed_mm(p, m, bd_mask) for p, m in zip(ps, ms)]
    for level in (same32 & jnp.logical_not(same16), jnp.logical_not(same32)):
        ts = [_packed_mm(jnp.where(level, l, 0.0), p, bd_mask) for l, p in zip(ls, ps)]
        ps = [p - _packed_mm(p, t, bd_mask) for p, t in zip(ps, ts)]
    return ps


def _gdn_delta_kernel(act_ref, bg_ref, gt_ref, s0_ref, ltri_ref, lbd_ref, o_ref, s_ref, sbd_ref, *, bb):
    c, hd = GDN_CHUNK, GDN_HEAD_DIM
    n_packs = GDN_V_HEADS // GDN_PACK
    n_units = bb * n_packs
    n_pairs = GDN_V_HEADS // 2
    zero_hd = jnp.zeros((hd, hd), F32)

    @pl.when(pl.program_id(1) == 0)
    def _():
        for bi in range(bb):
            for pr in range(n_pairs):
                h0, h1 = _PACK_HEADS[2 * pr], _PACK_HEADS[2 * pr + 1]
                sbd_ref[bi * n_pairs + pr] = jnp.concatenate(
                    [jnp.concatenate([s0_ref[bi, h0], zero_hd], axis=-1),
                     jnp.concatenate([zero_hd, s0_ref[bi, h1]], axis=-1)], axis=0)

    bgs = [bg_ref[bi] for bi in range(bb)]
    cums = [sum(jnp.dot(ltri_ref[...], part, preferred_element_type=F32) for part in _split3(bg)) for bg in bgs]
    gcr_alls = [sum(_nt(part, lbd_ref[...]) for part in _split3(gt_ref[bi])) for bi in range(bb)]
    row = _iota2((c, GDN_PACK * c), 0)
    lane = _iota2((c, GDN_PACK * c), 1)
    col, slot = lane % c, lane // c
    incl, strict = row >= col, row > col
    bd_mask = (_iota2((4 * c, 4 * c), 0) // c) == (_iota2((4 * c, 4 * c), 1) // c)
    pair_mask = (_iota2((2 * hd, 2 * hd), 0) // hd) == (_iota2((2 * hd, 2 * hd), 1) // hd)
    k_mask = (_iota2((2 * hd, hd), 0) // hd) == (_iota2((2 * hd, hd), 1) // c)
    row_pair = _iota2((2 * hd, 1), 0)

    def slot_cat(cols):
        out = jnp.broadcast_to(cols[3], (c, GDN_PACK * c))
        for x in (2, 1, 0):
            out = jnp.where(slot == x, cols[x], out)
        return out

    def side_by_side(a, b):
        return jnp.concatenate([a, b], axis=-1)

    qs, ks, betas, gcs, lmats, a_ins = [], [], [], [], [], []
    for u in range(n_units):
        bi, p = divmod(u, n_packs)
        bg, cum, gcr_all = bgs[bi], cums[bi], gcr_alls[bi]
        heads = _PACK_HEADS[GDN_PACK * p:GDN_PACK * (p + 1)]
        qa, qb = (act_ref[bi, :, (2 * p + i) * hd:(2 * p + i + 1) * hd] for i in (0, 1))
        ka, kb = (act_ref[bi, :, 1024 + (2 * p + i) * hd:1024 + (2 * p + i + 1) * hd] for i in (0, 1))
        kt = jnp.concatenate([ka, kb], axis=0).T
        k_bd = jnp.where(k_mask, jnp.concatenate([kt, kt], axis=0), 0.0)
        kq = _bdot(jnp.concatenate([side_by_side(ka, kb), side_by_side(qa, qb)], axis=0), k_bd)
        kk = side_by_side(kq[:c], kq[:c])
        qk = side_by_side(kq[c:], kq[c:])
        beta = [bg[:, h:h + 1] for h in heads]
        gc = [cum[:, GDN_V_HEADS + h:GDN_V_HEADS + h + 1] for h in heads]
        decay = jnp.where(incl, jnp.exp(jnp.where(incl, slot_cat(gc) - gcr_all[p:p + 1, :], 0.0)), 0.0)
        lmats.append(jnp.where(strict, slot_cat(beta) * kk * decay, 0.0))
        a_ins.append(qk * decay)
        qs.append((qa, qb, qa, qb)); ks.append((ka, kb, ka, kb)); betas.append(beta); gcs.append(gc)

    tinvs = _unit_lower_inverse_packed(lmats, row, col, bd_mask)

    uws, egs = [], []
    for u in range(n_units):
        bi, p = divmod(u, n_packs)
        bands = []
        eg = [jnp.exp(g) for g in gcs[u]]
        for x in range(GDN_PACK):
            h = _PACK_HEADS[GDN_PACK * p + x]
            vh = act_ref[bi, :, 2048 + h * hd:2048 + (h + 1) * hd]
            rhs = betas[u][x] * side_by_side(vh, ks[u][x] * eg[x])
            pieces = [jnp.zeros((c, 2 * hd * x), F32)] * (x > 0) + [rhs] + [jnp.zeros((c, 2 * hd * (3 - x)), F32)] * (x < 3)
            bands.append(jnp.concatenate(pieces, axis=-1))
        uws.append(_bdot(tinvs[u], jnp.concatenate(bands, axis=0)))
        egs.append(eg)

    wss, s_olds = [], []
    for u in range(n_units):
        for pr in range(2):
            x0, x1 = 2 * pr, 2 * pr + 1
            w0, w1 = (uws[u][:, 2 * hd * x + hd:2 * hd * (x + 1)] for x in (x0, x1))
            lhs = jnp.concatenate([side_by_side(w0, w1),
                                   side_by_side(qs[u][x0] * egs[u][x0], qs[u][x1] * egs[u][x1])], axis=0)
            s_old = sbd_ref[2 * u + pr]
            s_olds.append(s_old)
            wss.append(_bdot(lhs, s_old))

    v_news = []
    for u in range(n_units):
        vn = []
        for x in range(GDN_PACK):
            ws = wss[2 * u + x // 2]
            vn.append(uws[u][:, 2 * hd * x:2 * hd * x + hd] - ws[:c, hd * (x % 2):hd * (x % 2 + 1)])
        v_news.append(vn)
    for u in range(n_units):
        bi, p = divmod(u, n_packs)
        bands = []
        for x in range(GDN_PACK):
            pieces = [jnp.zeros((c, hd * x), F32)] * (x > 0) + [v_news[u][x]] + [jnp.zeros((c, hd * (3 - x)), F32)] * (x < 3)
            bands.append(jnp.concatenate(pieces, axis=-1))
        av = _bdot(a_ins[u], jnp.concatenate(bands, axis=0))
        for x in range(GDN_PACK):
            h = _PACK_HEADS[GDN_PACK * p + x]
            ws = wss[2 * u + x // 2]
            o_ref[bi, :, h * hd:(h + 1) * hd] = ws[c:, hd * (x % 2):hd * (x % 2 + 1)] + av[:, hd * x:hd * (x + 1)]
    zrows = jnp.zeros((c, 2 * hd), F32)
    for u in range(n_units):
        for pr in range(2):
            x0, x1 = 2 * pr, 2 * pr + 1
            gl0, gl1 = gcs[u][x0][c - 1:c, :], gcs[u][x1][c - 1:c, :]
            kd = jnp.concatenate([side_by_side(ks[u][x0] * jnp.exp(gl0 - gcs[u][x0]),
                                               ks[u][x1] * jnp.exp(gl1 - gcs[u][x1])), zrows], axis=0)
            kd_t = jnp.concatenate([kd[:, :hd].T, kd[:, hd:].T], axis=0)
            vn = jnp.concatenate([side_by_side(v_news[u][x0], v_news[u][x1]), zrows], axis=0)
            d_last = jnp.where(row_pair < hd, jnp.exp(gl0), jnp.exp(gl1))
            sbd_ref[2 * u + pr] = jnp.where(pair_mask, s_olds[2 * u + pr] * d_last + _bdot(kd_t, vn), 0.0)

    @pl.when(pl.program_id(1) == pl.num_programs(1) - 1)
    def _():
        for bi in range(bb):
            for pr in range(n_pairs):
                s_pair = sbd_ref[bi * n_pairs + pr]
                s_ref[bi, _PACK_HEADS[2 * pr]] = s_pair[:hd, :hd]
                s_ref[bi, _PACK_HEADS[2 * pr + 1]] = s_pair[hd:, hd:]


def _gdn_delta(act, bg, s0):
    b, t, _ = act.shape
    nc = t // GDN_CHUNK
    n_packs = GDN_V_HEADS // GDN_PACK
    wp = GDN_PACK * GDN_CHUNK
    g_rows = bg[:, :, GDN_V_HEADS:2 * GDN_V_HEADS][:, :, np.asarray(_PACK_HEADS)]
    g_rows = g_rows.reshape(b, nc, GDN_CHUNK, n_packs, GDN_PACK).transpose(0, 1, 3, 4, 2).reshape(b, nc, n_packs, wp)
    g_rows = jnp.pad(g_rows, ((0, 0), (0, 0), (0, 8 - n_packs), (0, 0)))
    tri = np.tril(np.ones((GDN_CHUNK, GDN_CHUNK), np.float32))
    ltri = jnp.asarray(tri, dtype=BF16)
    lbd = jnp.asarray(np.kron(np.eye(GDN_PACK, dtype=np.float32), tri), dtype=BF16)
    bb = 2 if b % 2 == 0 else 1
    state_spec = pl.BlockSpec((bb, GDN_V_HEADS, GDN_HEAD_DIM, GDN_HEAD_DIM), lambda b, n: (b, 0, 0, 0))
    return pl.pallas_call(
        functools.partial(_gdn_delta_kernel, bb=bb),
        grid=(b // bb, nc),
        in_specs=[pl.BlockSpec((bb, GDN_CHUNK, C_CONV), lambda b, n: (b, n, 0)),
                  pl.BlockSpec((bb, GDN_CHUNK, LANES), lambda b, n: (b, n, 0)),
                  pl.BlockSpec((bb, None, 8, wp), lambda b, n: (b, n, 0, 0)),
                  state_spec,
                  pl.BlockSpec((GDN_CHUNK, GDN_CHUNK), lambda b, n: (0, 0)),
                  pl.BlockSpec((wp, wp), lambda b, n: (0, 0))],
        out_specs=[pl.BlockSpec((bb, GDN_CHUNK, GDN_V_W), lambda b, n: (b, n, 0)), state_spec],
        out_shape=[jax.ShapeDtypeStruct((b, t, GDN_V_W), F32),
                   jax.ShapeDtypeStruct(s0.shape, F32)],
        scratch_shapes=[pltpu.VMEM((bb * GDN_V_HEADS // 2, 2 * GDN_HEAD_DIM, 2 * GDN_HEAD_DIM), F32)],
        compiler_params=pltpu.CompilerParams(
            dimension_semantics=("parallel", "arbitrary"), vmem_limit_bytes=VMEM_LIMIT),
        name="gdn_delta_rule",
    )(act, bg, g_rows, s0, ltri, lbd)


SAMPLE_ROWS = NSA_HEADS * 4


def _sample_cmp_kernel(q_ref, kc_ref, vc_ref, lut_ref, ov_ref, oc_ref, un_ref, *, nseg, past_len, n_sb, nq):
    rg = NSA_HPG * nq
    ri = lax.broadcasted_iota(jnp.int32, (rg, nseg), 0)
    mi = lax.broadcasted_iota(jnp.int32, (rg, nseg), 1)
    dist = past_len + ri % nq - CMP_STRIDE * mi - (CMP_STRIDE - 1)
    valid = (dist >= 0) & (mi >= 1)
    jl = lax.broadcasted_iota(jnp.int32, (8, un_ref.shape[-1]), 1)
    q_blk = (past_len + lax.broadcasted_iota(jnp.int32, jl.shape, 0) % nq) // SLC_BLOCK
    forced = (jl == 0) | (jl == q_blk) | (jl == q_blk - 1)
    jf = jl.astype(F32)
    for g in range(NSA_KV_GROUPS):
        sc = _nt(q_ref[g], kc_ref[g])
        tail = sc[:, nseg - LANES:] + _lut_gather(lut_ref[g], dist[:, nseg - LANES:])
        sc = jnp.where(valid, jnp.concatenate([sc[:, :nseg - LANES], tail], axis=-1), NEG)
        mx = jnp.max(sc, axis=-1, keepdims=True)
        e = jnp.where(valid, jnp.exp(sc - mx), 0.0)
        ssum = jnp.sum(e, axis=-1, keepdims=True)
        pc = e / jnp.where(ssum > 0, ssum, 1.0)
        oc_ref[g] = jnp.dot(pc.astype(BF16), vc_ref[g], preferred_element_type=F32)
        pcs = pc
        for hh in range(1, NSA_HPG):
            pcs = pcs + pltpu.roll(pc, hh * nq, axis=0)
        hi, lo = _split2(pcs[0:8])
        imp = jnp.dot(hi, ov_ref[...], preferred_element_type=F32) + jnp.dot(lo, ov_ref[...],
                                                                              preferred_element_type=F32)
        score = jnp.where(forced, 1e4, jnp.where(jl <= q_blk, imp, -1e4))
        score = jnp.where(jl < n_sb, score, -3e38)
        sel = jnp.zeros(score.shape, F32)
        for _ in range(min(N_SELECT, n_sb)):
            mxs = jnp.max(score, axis=-1, keepdims=True)
            idx = jnp.min(jnp.where(score == mxs, jf, 1e9), axis=-1, keepdims=True)
            hit = jf == idx
            sel = jnp.where(hit, 1.0, sel)
            score = jnp.where(hit, -jnp.inf, score)
        un_ref[g] = 1.0 - sel


def _sample_cmp(q16, kc, vc, lut16, past_len, nq):
    b = q16.shape[0]
    nseg = kc.shape[2]
    rg = NSA_HPG * nq
    n_sb = past_len // SLC_BLOCK + 1
    n_sb_pad = -(-n_sb // LANES) * LANES
    assert nq == 4 and nseg * CMP_STRIDE == past_len
    m = np.arange(nseg)[:, None]
    j = np.arange(n_sb_pad)[None, :]
    ov = ((CMP_STRIDE * m - CMP_STRIDE < j * SLC_BLOCK + SLC_BLOCK) & (CMP_STRIDE * m + CMP_STRIDE - 1 >= j * SLC_BLOCK)
          & (m >= 1) & (j < n_sb)).astype(np.float32)
    whole = lambda *shape: pl.BlockSpec((None,) + shape, lambda b: (b,) + (0,) * len(shape))
    return pl.pallas_call(
        functools.partial(_sample_cmp_kernel, nseg=nseg, past_len=past_len, n_sb=n_sb, nq=nq),
        grid=(b,),
        in_specs=[whole(NSA_KV_GROUPS, rg, LANES), whole(NSA_KV_GROUPS, nseg, LANES), whole(NSA_KV_GROUPS, nseg, LANES),
                  pl.BlockSpec((NSA_KV_GROUPS, rg, LANES), lambda b: (0, 0, 0)),
                  pl.BlockSpec((nseg, n_sb_pad), lambda b: (0, 0))],
        out_specs=[whole(NSA_KV_GROUPS, rg, LANES), whole(NSA_KV_GROUPS, 8, n_sb_pad)],
        out_shape=[jax.ShapeDtypeStruct((b, NSA_KV_GROUPS, rg, LANES), F32),
                   jax.ShapeDtypeStruct((b, NSA_KV_GROUPS, 8, n_sb_pad), F32)],
        compiler_params=pltpu.CompilerParams(dimension_semantics=("parallel",), vmem_limit_bytes=VMEM_LIMIT),
        name="nsa_sample_cmp_topk",
    )(q16, kc, vc, lut16, jnp.asarray(ov, dtype=BF16))


def _sample_sel_kernel(tab_ref, *refs, npg, past_len, nq):
    del tab_ref
    pages = refs[:npg]
    qbd_ref, un_ref, ee_ref, far_ref, lut_ref, m_ref, l_ref, acc_ref = refs[npg:]
    c = pl.program_id(1)
    kc = npg * PAGE_SIZE

    @pl.when(c == 0)
    def _():
        m_ref[...] = jnp.full(m_ref.shape, NEG, F32)
        l_ref[...] = jnp.zeros(l_ref.shape, F32)
        acc_ref[...] = jnp.zeros(acc_ref.shape, F32)

    kt = jnp.concatenate([pg[0] for pg in pages], axis=1).astype(BF16)
    vt = jnp.concatenate([pg[1] for pg in pages], axis=1).astype(BF16)
    s = (jnp.dot(qbd_ref[...], kt, preferred_element_type=F32) + far_ref[...][:, 0:1]
         + jnp.dot(un_ref[...], ee_ref[...], preferred_element_type=F32))
    ri = lax.broadcasted_iota(jnp.int32, (SAMPLE_ROWS, LANES), 0)
    li = lax.broadcasted_iota(jnp.int32, (SAMPLE_ROWS, LANES), 1)
    dist = past_len + ri % nq - (c * kc + kc - LANES + li)
    s = jnp.concatenate([s[:, :kc - LANES], s[:, kc - LANES:] + _lut_gather(lut_ref[...], dist)], axis=-1)
    m_old = m_ref[...][:, 0:1]
    m_new = jnp.maximum(m_old, jnp.max(s, axis=-1, keepdims=True))
    alpha = jnp.exp(m_old - m_new)
    p = jnp.exp(s - m_new)
    l_ref[...] = alpha * l_ref[...] + jnp.sum(p, axis=-1, keepdims=True)
    acc_ref[...] = alpha * acc_ref[...] + _nt(p.astype(BF16), vt)
    m_ref[...] = jnp.broadcast_to(m_new, m_ref.shape)


def _sample_sel(pages, table, qbd, unsel_c, farcol, lut64, past_len, nq):
    b, n_pages = table.shape
    npg = min(16, n_pages)
    kc = npg * PAGE_SIZE
    nch = n_pages // npg
    blk_per_chunk = kc // SLC_BLOCK
    ee = np.zeros((LANES, kc), np.float32)
    ee[np.arange(kc) // SLC_BLOCK, np.arange(kc)] = NEG
    assert blk_per_chunk <= LANES

    def page_spec(j):
        return pl.BlockSpec((None, 2, NSA_KV_W // 2, PAGE_SIZE), lambda b, c, tab: (tab[b, c * npg + j], 0, 0, 0))

    const = lambda *shape: pl.BlockSpec(shape, lambda b, c, tab: (0,) * len(shape))
    acc_spec = lambda cols: pl.BlockSpec((None, SAMPLE_ROWS, cols), lambda b, c, tab: (b, 0, 0))
    grid_spec = pltpu.PrefetchScalarGridSpec(
        num_scalar_prefetch=1,
        grid=(b, nch),
        in_specs=[page_spec(j) for j in range(npg)] + [
            pl.BlockSpec((None, SAMPLE_ROWS, 2 * LANES), lambda b, c, tab: (b, 0, 0)),
            pl.BlockSpec((None, None, SAMPLE_ROWS, LANES), lambda b, c, tab: (b, c, 0, 0)),
            const(LANES, kc), const(SAMPLE_ROWS, LANES), const(SAMPLE_ROWS, LANES)],
        out_specs=[acc_spec(LANES), acc_spec(LANES), acc_spec(2 * LANES)],
    )
    return pl.pallas_call(
        functools.partial(_sample_sel_kernel, npg=npg, past_len=past_len, nq=nq),
        grid_spec=grid_spec,
        out_shape=[jax.ShapeDtypeStruct((b, SAMPLE_ROWS, LANES), F32),
                   jax.ShapeDtypeStruct((b, SAMPLE_ROWS, LANES), F32),
                   jax.ShapeDtypeStruct((b, SAMPLE_ROWS, 2 * LANES), F32)],
        compiler_params=pltpu.CompilerParams(
            dimension_semantics=("parallel", "arbitrary"), vmem_limit_bytes=VMEM_LIMIT),
        name="nsa_sample_selected",
    )(table, *([pages] * npg), qbd, unsel_c, jnp.asarray(ee, dtype=BF16), farcol, lut64)


def _own_group_cols(x, grp):
    out = jnp.zeros((x.shape[0], NSA_HEAD_DIM), F32)
    for g in range(NSA_KV_GROUPS):
        out = jnp.where(grp == g, x[:, g * NSA_HEAD_DIM:(g + 1) * NSA_HEAD_DIM], out)
    return out


def _sample_final_kernel(qbd_ref, m_ref, l_ref, acc_ref, snew_ref, wc_ref, wnew_ref, oc_ref, gr_ref, far_ref, lut_ref,
                         o_ref, *, nq, w_buf):
    rows = SAMPLE_ROWS
    qbd = qbd_ref[...]
    far = far_ref[...][:, 0:1]
    lut = lut_ref[...]
    ri = lax.broadcasted_iota(jnp.int32, (rows, LANES), 0)
    li = lax.broadcasted_iota(jnp.int32, (rows, LANES), 1)
    tok = ri % nq
    grp = lax.broadcasted_iota(jnp.int32, (rows, NSA_HEAD_DIM), 0) // (NSA_HPG * nq)

    knew = snew_ref[...]
    s_new = _nt(qbd, knew[:, :256].astype(BF16)) + far
    d_new = tok - li
    s_new = jnp.where((d_new >= 0) & (li < nq), s_new + _lut_gather(lut, d_new), NEG)
    m_old = m_ref[...][:, 0:1]
    m_new = jnp.maximum(m_old, jnp.max(s_new, axis=-1, keepdims=True))
    alpha = jnp.exp(m_old - m_new)
    p_new = jnp.exp(s_new - m_new)
    l_s = alpha * l_ref[...][:, 0:1] + jnp.sum(p_new, axis=-1, keepdims=True)
    acc_s = alpha * acc_ref[...] + jnp.dot(p_new.astype(BF16), knew[:, 256:].astype(BF16), preferred_element_type=F32)
    o_s = _own_group_cols(acc_s, grp) / l_s

    kv_w = jnp.concatenate([wc_ref[...], wnew_ref[...]], axis=0)
    s_w = _nt(qbd, kv_w[:, :256].astype(BF16)) + far
    n_w = w_buf + LANES
    idx = lax.broadcasted_iota(jnp.int32, (rows, n_w), 1)
    d_w = w_buf + lax.broadcasted_iota(jnp.int32, (rows, n_w), 0) % nq - idx
    ok_w = (d_w >= 0) & (d_w < WINDOW) & (idx < w_buf + nq)
    corr = [jnp.zeros((rows, n_w - 2 * LANES), F32)]
    for cidx in range(2):
        lo = n_w - 2 * LANES + cidx * LANES
        corr.append(_lut_gather(lut, d_w[:, lo:lo + LANES]))
    s_w = jnp.where(ok_w, s_w + jnp.concatenate(corr, axis=-1), NEG)
    m_w = jnp.max(s_w, axis=-1, keepdims=True)
    p_w = jnp.exp(s_w - m_w)
    l_w = jnp.sum(p_w, axis=-1, keepdims=True)
    acc_w = jnp.dot(p_w.astype(BF16), kv_w[:, 256:].astype(BF16), preferred_element_type=F32)
    o_w = _own_group_cols(acc_w, grp) / l_w

    gt = jax.nn.sigmoid(gr_ref[...])
    o_ref[...] = gt[:, 0:1] * oc_ref[...][:, :NSA_HEAD_DIM] + gt[:, 1:2] * o_s + gt[:, 2:3] * o_w


def _sample_final(qbd, m, l, acc, snew, wcache, wnew, o_c, graw, farcol, lut64, nq):
    b = qbd.shape[0]
    w_buf = wcache.shape[1]
    assert w_buf == WINDOW
    whole = lambda *shape: pl.BlockSpec((None,) + shape, lambda b: (b,) + (0,) * len(shape))
    const = lambda *shape: pl.BlockSpec(shape, lambda b: (0,) * len(shape))
    return pl.pallas_call(
        functools.partial(_sample_final_kernel, nq=nq, w_buf=w_buf),
        grid=(b,),
        in_specs=[whole(SAMPLE_ROWS, 2 * LANES), whole(SAMPLE_ROWS, LANES), whole(SAMPLE_ROWS, LANES),
                  whole(SAMPLE_ROWS, 2 * LANES), whole(LANES, NSA_KV_W), whole(w_buf, NSA_KV_W), whole(LANES, NSA_KV_W),
                  whole(SAMPLE_ROWS, LANES), whole(SAMPLE_ROWS, LANES),
                  const(SAMPLE_ROWS, LANES), const(SAMPLE_ROWS, LANES)],
        out_specs=whole(SAMPLE_ROWS, NSA_HEAD_DIM),
        out_shape=jax.ShapeDtypeStruct((b, SAMPLE_ROWS, NSA_HEAD_DIM), F32),
        compiler_params=pltpu.CompilerParams(dimension_semantics=("parallel",), vmem_limit_bytes=VMEM_LIMIT),
        name="nsa_sample_final",
    )(qbd, m, l, acc, snew, wcache, wnew, o_c, graw, farcol, lut64)


def _ffn(x, mod, gains, w_in, w_out):
    hid = _norm_mod_swiglu(x, gains[2], mod[3], mod[4], w_in)
    return _matmul_rms_residual(hid, w_out, x, mod[5], gains[3])


def _heads_major(x, n_heads):
    b, t, _ = x.shape
    return x.reshape(b, t, n_heads, NSA_HEAD_DIM).transpose(0, 2, 1, 3)


def _const_cols(vals, lead_shape):
    cols = np.zeros((NSA_HEAD_DIM,), np.float32)
    cols[:len(vals)] = vals
    return jnp.broadcast_to(jnp.asarray(cols, dtype=BF16), tuple(lead_shape) + (NSA_HEAD_DIM,))


def _nsa_prompt(x, mod, gains, w_in, cmp_w, w_out, tb, lut, bias_cols):
    b, t, _ = x.shape
    p = _norm_mod_linear(x, gains[0], mod[0], mod[1], w_in)
    kvc, kvs, kvw = (p[..., 1024 + i * NSA_KV_W:1024 + (i + 1) * NSA_KV_W] for i in range(3))
    q128 = jnp.concatenate([_heads_major(p[..., :NSA_Q_W].astype(BF16), NSA_HEADS),
                            jnp.broadcast_to(bias_cols[None, :, None, :], (b, NSA_HEADS, t, NSA_HEAD_DIM))], axis=-1)
    lead = (b, NSA_KV_GROUPS, t)
    blk_onehot = np.zeros((t, LANES), np.float32)
    blk_onehot[np.arange(t), np.arange(t) // SLC_BLOCK] = NEG
    ks = jnp.concatenate([jnp.broadcast_to(jnp.asarray(blk_onehot, dtype=BF16), lead + (LANES,)),
                          _heads_major(kvs[..., :256].astype(BF16), NSA_KV_GROUPS),
                          _const_cols([1.0, 1.0], lead)], axis=-1)
    vs = jnp.concatenate([_heads_major(kvs[..., 256:].astype(BF16), NSA_KV_GROUPS), _const_cols([1.0], lead)], axis=-1)
    front = ((0, 0), (0, 0), (WINDOW, 0), (0, 0))
    kw = jnp.pad(jnp.concatenate([_heads_major(kvw[..., :256].astype(BF16), NSA_KV_GROUPS),
                                  _const_cols([1.0, 1.0], lead)], axis=-1), front)
    vw = jnp.pad(jnp.concatenate([_heads_major(kvw[..., 256:].astype(BF16), NSA_KV_GROUPS),
                                  _const_cols([1.0], lead)], axis=-1), front)
    n_pages = t // PAGE_SIZE
    table = jnp.arange(b * n_pages, dtype=jnp.int32).reshape(b, n_pages)
    kc, vc = _compress(kvc.reshape(b * n_pages, SEGS_PER_PAGE, SEG_W), table, *cmp_w)
    o = _nsa_prompt_attention(q128, p, (NSA_Q_W + 3 * NSA_KV_W) // LANES, kc, vc, ks, vs, kw, vw, tb, lut)
    o = o.transpose(0, 2, 1, 3).reshape(b, t, NSA_Q_W)
    x = _matmul_rms_residual(o, w_out, x, mod[2], gains[1])
    shape5 = (b, t, 2, NSA_KV_GROUPS, NSA_HEAD_DIM)
    return x, kvc.reshape(shape5), kvs.reshape(shape5), kvw.reshape(shape5)[:, -min(WINDOW, t):]


def _nsa_sample(x, mod, gains, w_in, cmp_w, w_out, lut, rel_bias, bias_cols, cache_cmp, cache_slc, cache_win,
                page_table, db, nq):
    n_pages = page_table.shape[1]
    past_len = n_pages * PAGE_SIZE
    rows = db * nq
    p = _norm_mod_linear(x, gains[0], mod[0], mod[1], w_in)[0]
    kvc, kvs, kvw = (p[:, 1024 + i * NSA_KV_W:1024 + (i + 1) * NSA_KV_W] for i in range(3))
    qh = p[:, :NSA_Q_W].astype(BF16).reshape(db, nq, NSA_KV_GROUPS, NSA_HPG, NSA_HEAD_DIM).transpose(0, 2, 3, 1, 4)
    q16 = jnp.concatenate([qh, jnp.broadcast_to(bias_cols.reshape(1, NSA_KV_GROUPS, NSA_HPG, 1, NSA_HEAD_DIM), qh.shape)],
                          axis=-1).reshape(db, NSA_KV_GROUPS, NSA_HPG * nq, LANES)
    eye_g = jnp.eye(NSA_KV_GROUPS, dtype=BF16)
    qbd = jnp.einsum('bghtd,gj->bghtjd', qh, eye_g).reshape(db, SAMPLE_ROWS, NSA_KV_GROUPS * NSA_HEAD_DIM)
    row_head = np.repeat(np.arange(NSA_HEADS), nq)
    lut64 = lut[:, 0, :][row_head]
    farcol = jnp.broadcast_to(rel_bias[N_BUCKETS - 1][row_head][:, None], (SAMPLE_ROWS, LANES))
    rows_minor = lambda cache: jnp.transpose(cache, (0, 2, 3, 4, 1))
    kc, vc = _compress(rows_minor(cache_cmp).reshape(-1, 2, 2, LANES, PAGE_SIZE), page_table, *cmp_w)
    o_c, unsel = _sample_cmp(q16, kc, vc, lut64.reshape(NSA_KV_GROUPS, NSA_HPG * nq, LANES), past_len, nq)
    npg = min(16, n_pages)
    nch = n_pages // npg
    bpc = npg * PAGE_SIZE // SLC_BLOCK
    un = unsel[:, :, :nq, :past_len // SLC_BLOCK].reshape(db, NSA_KV_GROUPS, 1, nq, nch, bpc)
    un = jnp.broadcast_to(un, (db, NSA_KV_GROUPS, NSA_HPG, nq, nch, bpc)).transpose(0, 4, 1, 2, 3, 5)
    un = jnp.pad(un.reshape(db, nch, SAMPLE_ROWS, bpc), ((0, 0), (0, 0), (0, 0), (0, LANES - bpc))).astype(BF16)
    m, l, acc = _sample_sel(rows_minor(cache_slc).reshape(-1, 2, NSA_KV_W // 2, PAGE_SIZE), page_table, qbd, un, farcol,
                            lut64, past_len, nq)
    pad_rows = lambda a: jnp.pad(a.reshape(db, nq, NSA_KV_W), ((0, 0), (0, LANES - nq), (0, 0)))
    wcache = cache_win.reshape(db, -1, NSA_KV_W)
    graw = p[:, NSA_Q_W + 3 * NSA_KV_W:NSA_Q_W + 3 * NSA_KV_W + 3 * NSA_HEADS]
    graw = graw.reshape(db, nq, NSA_HEADS, 3).transpose(0, 2, 1, 3).reshape(db, SAMPLE_ROWS, 3)
    graw = jnp.pad(graw, ((0, 0), (0, 0), (0, LANES - 3)))
    o = _sample_final(qbd, m, l, acc, pad_rows(kvs), wcache, pad_rows(kvw), o_c.reshape(db, SAMPLE_ROWS, LANES), graw,
                      farcol, lut64, nq)
    o = o.reshape(db, NSA_HEADS, nq, NSA_HEAD_DIM).transpose(0, 2, 1, 3).reshape(1, rows, NSA_Q_W)
    x = _matmul_rms_residual(o, w_out, x, mod[2], gains[1])
    shape5 = (db, nq, 2, NSA_KV_GROUPS, NSA_HEAD_DIM)
    kv_win = jnp.concatenate([cache_win, kvw.reshape(shape5)], axis=1)[:, -cache_win.shape[1]:]
    return x, kvc.reshape(shape5), kvs.reshape(shape5), kv_win


def _gdn_prompt(x, mod, gains, w_in, conv_w, a_log, dt_bias, norm_w, w_out):
    b, t, _ = x.shape
    p = _norm_mod_linear(x, gains[0], mod[0], mod[1], w_in)
    act = _gdn_conv(p, conv_w)
    bg = _gdn_gates(p, (C_CONV + GDN_V_W) // LANES, a_log, dt_bias)
    s0 = jnp.zeros((b, GDN_V_HEADS, GDN_HEAD_DIM, GDN_HEAD_DIM), F32)
    o, s_fin = _gdn_delta(act, bg, s0)
    x = _gdn_out(o, p, C_CONV // GDN_V_W, norm_w, w_out, x, mod[2], gains[1])
    return x, p[:, t - (CONV_W - 1):, :C_CONV], s_fin


def _gdn_sample(x, mod, gains, w_in, conv_w, a_log, dt_bias, norm_w, w_out, conv_buf, s0, db, nq):
    p = _norm_mod_linear(x, gains[0], mod[0], mod[1], w_in)
    qkv = p[0, :, :C_CONV].reshape(db, nq, C_CONV)
    xp = jnp.concatenate([conv_buf, qkv], axis=1)
    act = _gdn_conv(jnp.pad(xp, ((0, 0), (0, 8 - xp.shape[1]), (0, 0))), conv_w)[:, CONV_W - 1:CONV_W - 1 + nq]
    bg = _gdn_gates(p, (C_CONV + GDN_V_W) // LANES, a_log, dt_bias).reshape(db, nq, LANES)
    pad_t = ((0, 0), (0, GDN_CHUNK - nq), (0, 0))
    o, s_fin = _gdn_delta(jnp.pad(act, pad_t), jnp.pad(bg, pad_t), s0)
    o = o[:, :nq].reshape(1, db * nq, GDN_V_W)
    x = _gdn_out(o, p, C_CONV // GDN_V_W, norm_w, w_out, x, mod[2], gains[1])
    return x, xp[:, -(CONV_W - 1):], s_fin


def kernel(x_prompt, x_sample, c_prompt, c_sample, cache_kv_cmp, cache_kv_slc, cache_kv_win, state_conv, state_ssm,
           page_table, rel_bias, norm_gains, w_ada, b_ada, w_ffn_in, w_ffn_out, nsa_w_in, nsa_cmp_pe, nsa_cmp_w1,
           nsa_cmp_b1, nsa_cmp_w2, nsa_w_out, gdn_w_in, gdn_conv_w, gdn_a_log, gdn_dt_bias, gdn_norm_w, gdn_w_out):
    depth = w_ada.shape[0]
    bp, t, d = x_prompt.shape
    db, nq, _ = x_sample.shape
    assert nq + CONV_W - 1 <= 8 and nq <= GDN_CHUNK

    c_all = jnp.concatenate([c_prompt, c_sample], axis=0)
    rows_pad = -(-c_all.shape[0] // 8) * 8
    ada = _adaln(jnp.pad(c_all, ((0, rows_pad - c_all.shape[0]), (0, 0))), w_ada, b_ada)
    ada = ada.reshape(depth, rows_pad, 6, d)
    tb, lut = _bias_tables(rel_bias)
    far_hi, far_lo = _split2(rel_bias[N_BUCKETS - 1])
    bias_cols = jnp.zeros((NSA_HEADS, NSA_HEAD_DIM), BF16).at[:, 0].set(far_hi).at[:, 1].set(far_lo)

    xp = x_prompt
    xs = x_sample.reshape(1, db * nq, d)
    kvc_p, kvc_s, kvs_p, kvs_s, kvw_p, kvw_s, cv_p, cv_s, ss_p, ss_s = ([] for _ in range(10))
    for i in range(depth):
        mod_p = [ada[i, :bp, k][:, None, :] for k in range(6)]
        mod_s = [jnp.repeat(ada[i, bp:bp + db, k], nq, axis=0)[None] for k in range(6)]
        gains = norm_gains[i]
        l = i // 2
        if i % 2 == 0:
            w_in = jnp.concatenate([nsa_w_in[l][:, :NSA_Q_W] * (NSA_HEAD_DIM ** -0.5), nsa_w_in[l][:, NSA_Q_W:]], axis=1)
            w_in = jnp.pad(w_in, ((0, 0), (0, -w_in.shape[1] % LANES))).astype(BF16)
            cmp_w = _compress_weights(nsa_cmp_pe[l], nsa_cmp_w1[l], nsa_cmp_b1[l], nsa_cmp_w2[l])
            w_out = nsa_w_out[l].astype(BF16)
            xp, a, bq, cq = _nsa_prompt(xp, mod_p, gains, w_in, cmp_w, w_out, tb, lut, bias_cols)
            kvc_p.append(a); kvs_p.append(bq); kvw_p.append(cq)
            xs, a, bq, cq = _nsa_sample(xs, mod_s, gains, w_in, cmp_w, w_out, lut, rel_bias, bias_cols, cache_kv_cmp[l],
                                        cache_kv_slc[l], cache_kv_win[l], page_table, db, nq)
            kvc_s.append(a); kvs_s.append(bq); kvw_s.append(cq)
        else:
            w_in = jnp.pad(gdn_w_in[l], ((0, 0), (0, -gdn_w_in.shape[2] % (5 * MXU_WIDTH)))).astype(BF16)
            gdn_w = (w_in, gdn_conv_w[l], gdn_a_log[l], gdn_dt_bias[l], gdn_norm_w[l], gdn_w_out[l].astype(BF16))
            xp, a, bq = _gdn_prompt(xp, mod_p, gains, *gdn_w)
            cv_p.append(a); ss_p.append(bq)
            xs, a, bq = _gdn_sample(xs, mod_s, gains, *gdn_w, state_conv[l], state_ssm[l], db, nq)
            cv_s.append(a); ss_s.append(bq)
        w_ffn = (w_ffn_in[i].astype(BF16), w_ffn_out[i].astype(BF16))
        xp = _ffn(xp, mod_p, gains, *w_ffn)
        xs = _ffn(xs, mod_s, gains, *w_ffn)
    return (xp, xs.reshape(db, nq, d), jnp.stack(kvc_p), jnp.stack(kvc_s), jnp.stack(kvs_p), jnp.stack(kvs_s),
            jnp.stack(kvw_p), jnp.stack(kvw_s), jnp.stack(cv_p), jnp.stack(cv_s), jnp.stack(ss_p), jnp.stack(ss_s))
```

```python
import functools
import math

import numpy as np
import jax
import jax.numpy as jnp
from jax import lax
from jax.experimental import pallas as pl
from jax.experimental.pallas import tpu as pltpu

F32 = jnp.float32
BF16 = jnp.bfloat16

D_MODEL = 1024
RMS_EPS = 1e-6
D_FF = 2816
NSA_HEADS = 16
NSA_HEAD_DIM = 64
NSA_KV_GROUPS = 4
NSA_HPG = 4
CMP_BLOCK = 32
CMP_STRIDE = 16
CMP_HID = 256
SLC_BLOCK = 64
N_SELECT = 16
WINDOW = 512
Q_BLOCK = 128
PAGE_SIZE = 128
N_BUCKETS = 32
GDN_QK_HEADS = 8
GDN_V_HEADS = 16
GDN_HEAD_DIM = 128
CONV_W = 4
GDN_CHUNK = 64
NSA_Q_W = 1024
NSA_KV_W = 512
C_CONV = 4096
GDN_V_W = 2048

LANES = 128
SEG_W = CMP_STRIDE * NSA_KV_W
SEGS_PER_PAGE = PAGE_SIZE // CMP_STRIDE
NEG = -1e30
VMEM_LIMIT = 48 * 1024 * 1024

_BUCKET_THR = (19, 21, 24, 27, 31, 35, 40, 46, 52, 59, 67, 77, 87, 99, 113)
FAR_DIST = 128


def _nt(a, b):
    return lax.dot_general(a, b, (((1,), (1,)), ((), ())), preferred_element_type=F32)


def _split2(x):
    hi = x.astype(BF16)
    lo = (x - hi.astype(F32)).astype(BF16)
    return hi, lo


def _split3(x):
    hi = x.astype(BF16)
    r = x - hi.astype(F32)
    mid = r.astype(BF16)
    lo = (r - mid.astype(F32)).astype(BF16)
    return hi, mid, lo


MXU_WIDTH = 256
MAX_TN = 2816


def _pick_tn(n):
    units = n // LANES
    cands = [d * LANES for d in range(1, units + 1) if units % d == 0 and d * LANES <= MAX_TN]
    full = [c for c in cands if c % MXU_WIDTH == 0]
    return max(full) if full and 2 * max(full) >= max(cands) else max(cands)


def _adaln_kernel(c_ref, w_ref, b_ref, o_ref):
    c = c_ref[...]
    a = (c * jax.nn.sigmoid(c)).astype(BF16)
    o_ref[...] = jnp.dot(a, w_ref[...].astype(BF16), preferred_element_type=F32) + b_ref[...]


def _adaln(c_all, w_ada, b_ada):
    depth, d, n = w_ada.shape
    rows = c_all.shape[0]
    tn = 768
    return pl.pallas_call(
        _adaln_kernel,
        grid=(depth, n // tn),
        in_specs=[pl.BlockSpec((rows, d), lambda l, j: (0, 0)),
                  pl.BlockSpec((None, d, tn), lambda l, j: (l, 0, j)),
                  pl.BlockSpec((None, 1, tn), lambda l, j: (l, 0, j))],
        out_specs=pl.BlockSpec((None, rows, tn), lambda l, j: (l, 0, j)),
        out_shape=jax.ShapeDtypeStruct((depth, rows, n), F32),
        compiler_params=pltpu.CompilerParams(dimension_semantics=("parallel", "parallel")),
        name="adaln",
    )(c_all, w_ada, b_ada.reshape(depth, 1, n))


def _mod_norm(x, gain, shift, scale):
    ms = jnp.mean(x * x, axis=-1, keepdims=True)
    y = x * lax.rsqrt(ms + RMS_EPS) * gain
    return y * (1.0 + scale) + shift


def _nml_kernel(x_ref, g_ref, sh_ref, sc_ref, w_ref, o_ref, h_ref):
    @pl.when(pl.program_id(2) == 0)
    def _():
        h_ref[...] = _mod_norm(x_ref[...], g_ref[...], sh_ref[...], sc_ref[...]).astype(BF16)

    o_ref[...] = jnp.dot(h_ref[...], w_ref[...], preferred_element_type=F32).astype(o_ref.dtype)


def _nml_swiglu_kernel(x_ref, g_ref, sh_ref, sc_ref, wg_ref, wu_ref, o_ref, h_ref):
    @pl.when(pl.program_id(2) == 0)
    def _():
        h_ref[...] = _mod_norm(x_ref[...], g_ref[...], sh_ref[...], sc_ref[...]).astype(BF16)

    h = h_ref[...]
    gate = jnp.dot(h, wg_ref[...], preferred_element_type=F32)
    up = jnp.dot(h, wu_ref[...], preferred_element_type=F32)
    o_ref[...] = (gate * jax.nn.sigmoid(gate) * up).astype(o_ref.dtype)


def _mod_spec(mod, tm):
    if mod.shape[1] == 1:
        return pl.BlockSpec((None, 1, mod.shape[2]), lambda b, i, *_: (b, 0, 0))
    return pl.BlockSpec((None, tm, mod.shape[2]), lambda b, i, *_: (b, i, 0))


def _row_tile(t):
    return 512 if t % 512 == 0 else t


def _norm_mod_linear(x, gain, shift, scale, w, out_dtype=F32):
    b, t, d = x.shape
    n = w.shape[1]
    tm, tn = _row_tile(t), _pick_tn(n)
    return pl.pallas_call(
        _nml_kernel,
        grid=(b, t // tm, n // tn),
        in_specs=[pl.BlockSpec((None, tm, d), lambda b, i, j: (b, i, 0)),
                  pl.BlockSpec((1, d), lambda b, i, j: (0, 0)),
                  _mod_spec(shift, tm), _mod_spec(scale, tm),
                  pl.BlockSpec((d, tn), lambda b, i, j: (0, j))],
        out_specs=pl.BlockSpec((None, tm, tn), lambda b, i, j: (b, i, j)),
        out_shape=jax.ShapeDtypeStruct((b, t, n), out_dtype),
        scratch_shapes=[pltpu.VMEM((tm, d), BF16)],
        compiler_params=pltpu.CompilerParams(
            dimension_semantics=("parallel", "parallel", "arbitrary"), vmem_limit_bytes=VMEM_LIMIT),
        name="norm_mod_linear",
    )(x, gain.reshape(1, d), shift, scale, w)


def _norm_mod_swiglu(x, gain, shift, scale, w_in):
    b, t, d = x.shape
    nf = w_in.shape[1] // 2
    tm, tn = _row_tile(t), _pick_tn(nf)
    nj = nf // tn
    return pl.pallas_call(
        _nml_swiglu_kernel,
        grid=(b, t // tm, nj),
        in_specs=[pl.BlockSpec((None, tm, d), lambda b, i, j: (b, i, 0)),
                  pl.BlockSpec((1, d), lambda b, i, j: (0, 0)),
                  _mod_spec(shift, tm), _mod_spec(scale, tm),
                  pl.BlockSpec((d, tn), lambda b, i, j: (0, j)),
                  pl.BlockSpec((d, tn), lambda b, i, j: (0, j + nj))],
        out_specs=pl.BlockSpec((None, tm, tn), lambda b, i, j: (b, i, j)),
        out_shape=jax.ShapeDtypeStruct((b, t, nf), BF16),
        scratch_shapes=[pltpu.VMEM((tm, d), BF16)],
        compiler_params=pltpu.CompilerParams(
            dimension_semantics=("parallel", "parallel", "arbitrary"), vmem_limit_bytes=VMEM_LIMIT),
        name="norm_mod_swiglu",
    )(x, gain.reshape(1, d), shift, scale, w_in, w_in)


def _rms_gated_residual(y, x, gate, gain):
    ms = jnp.mean(y * y, axis=-1, keepdims=True)
    return x + gate * (y * lax.rsqrt(ms + RMS_EPS) * gain)


def _mrr_kernel(a_ref, w_ref, x_ref, gate_ref, gain_ref, o_ref):
    y = jnp.dot(a_ref[...].astype(BF16), w_ref[...], preferred_element_type=F32)
    o_ref[...] = _rms_gated_residual(y, x_ref[...], gate_ref[...], gain_ref[...])


def _matmul_rms_residual(a, w, x, gate, gain):
    b, t, k = a.shape
    d = w.shape[1]
    tm = _row_tile(t)
    return pl.pallas_call(
        _mrr_kernel,
        grid=(b, t // tm),
        in_specs=[pl.BlockSpec((None, tm, k), lambda b, i: (b, i, 0)),
                  pl.BlockSpec((k, d), lambda b, i: (0, 0)),
                  pl.BlockSpec((None, tm, d), lambda b, i: (b, i, 0)),
                  _mod_spec(gate, tm),
                  pl.BlockSpec((1, d), lambda b, i: (0, 0))],
        out_specs=pl.BlockSpec((None, tm, d), lambda b, i: (b, i, 0)),
        out_shape=jax.ShapeDtypeStruct((b, t, d), F32),
        compiler_params=pltpu.CompilerParams(
            dimension_semantics=("parallel", "parallel"), vmem_limit_bytes=VMEM_LIMIT),
        name="matmul_rms_residual",
    )(a, w, x, gate, gain.reshape(1, d))


def _gdn_out_kernel(o_ref, z_ref, nw_ref, w_ref, x_ref, gate_ref, gain_ref, out_ref, a_ref):
    nw = nw_ref[...]
    for h in range(GDN_V_HEADS):
        sl = slice(h * GDN_HEAD_DIM, (h + 1) * GDN_HEAD_DIM)
        o = o_ref[:, sl]
        z = z_ref[:, sl]
        ms = jnp.mean(o * o, axis=-1, keepdims=True)
        a_ref[:, sl] = ((o * lax.rsqrt(ms + RMS_EPS) * nw) * (z * jax.nn.sigmoid(z))).astype(BF16)
    y = jnp.dot(a_ref[...], w_ref[...], preferred_element_type=F32)
    out_ref[...] = _rms_gated_residual(y, x_ref[...], gate_ref[...], gain_ref[...])


def _gdn_out(o, p, z_col_block, norm_w, w, x, gate, gain):
    b, t, k = o.shape
    d = w.shape[1]
    tm = _row_tile(t)
    return pl.pallas_call(
        _gdn_out_kernel,
        grid=(b, t // tm),
        in_specs=[pl.BlockSpec((None, tm, k), lambda b, i: (b, i, 0)),
                  pl.BlockSpec((None, tm, k), lambda b, i: (b, i, z_col_block)),
                  pl.BlockSpec((1, GDN_HEAD_DIM), lambda b, i: (0, 0)),
                  pl.BlockSpec((k, d), lambda b, i: (0, 0)),
                  pl.BlockSpec((None, tm, d), lambda b, i: (b, i, 0)),
                  _mod_spec(gate, tm),
                  pl.BlockSpec((1, d), lambda b, i: (0, 0))],
        out_specs=pl.BlockSpec((None, tm, d), lambda b, i: (b, i, 0)),
        out_shape=jax.ShapeDtypeStruct((b, t, d), F32),
        scratch_shapes=[pltpu.VMEM((tm, k), BF16)],
        compiler_params=pltpu.CompilerParams(
            dimension_semantics=("parallel", "parallel"), vmem_limit_bytes=VMEM_LIMIT),
        name="gdn_out",
    )(o, p, norm_w.reshape(1, GDN_HEAD_DIM), w, x, gate, gain.reshape(1, d))


def _bucket_of(n):
    big = jnp.full(n.shape, 16, jnp.int32)
    for thr in _BUCKET_THR:
        big = big + (n >= thr).astype(jnp.int32)
    return jnp.where(n < 16, n, big)


def _bias_tab_kernel(tbl_ref, tb_ref, lut_ref):
    h = pl.program_id(0)
    far = tbl_ref[N_BUCKETS - 1, h]

    def lookup(dist):
        bkt = _bucket_of(jnp.maximum(dist, 0))
        out = jnp.zeros(dist.shape, F32)
        for bb in range(N_BUCKETS):
            out = jnp.where(bkt == bb, tbl_ref[bb, h], out)
        return out - far

    qi = lax.broadcasted_iota(jnp.int32, (Q_BLOCK, 2 * Q_BLOCK), 0)
    kj = lax.broadcasted_iota(jnp.int32, (Q_BLOCK, 2 * Q_BLOCK), 1)
    dist = Q_BLOCK + qi - kj
    tb_ref[...] = jnp.where(dist >= 0, lookup(dist), NEG)
    lut_ref[...] = lookup(lax.broadcasted_iota(jnp.int32, (8, LANES), 1))


def _bias_tables(rel_bias):
    return pl.pallas_call(
        _bias_tab_kernel,
        grid=(NSA_HEADS,),
        in_specs=[pl.BlockSpec(memory_space=pltpu.SMEM)],
        out_specs=[pl.BlockSpec((None, Q_BLOCK, 2 * Q_BLOCK), lambda h: (h, 0, 0)),
                   pl.BlockSpec((None, 8, LANES), lambda h: (h, 0, 0))],
        out_shape=[jax.ShapeDtypeStruct((NSA_HEADS, Q_BLOCK, 2 * Q_BLOCK), F32),
                   jax.ShapeDtypeStruct((NSA_HEADS, 8, LANES), F32)],
        compiler_params=pltpu.CompilerParams(dimension_semantics=("parallel",)),
        name="bias_tables",
    )(rel_bias)


def _lut_gather(lut_rows, dist):
    idx = jnp.clip(dist, 0, LANES - 1)
    val = jnp.take_along_axis(lut_rows, idx, axis=1)
    return jnp.where((dist >= 0) & (dist < FAR_DIST), val, 0.0)


def _pe_term_kernel(pe_ref, wbd_ref, b1_ref, o_ref):
    y = jnp.dot(pe_ref[...], wbd_ref[...], preferred_element_type=F32)
    o_ref[...] = y[:, 0:CMP_HID] + y[:, 3 * CMP_HID:4 * CMP_HID] + b1_ref[...]


def _pe_term(pe_x, wbd, b1):
    return pl.pallas_call(
        _pe_term_kernel,
        grid=(2,),
        in_specs=[pl.BlockSpec((None, 8, 2048), lambda k: (k, 0, 0)),
                  pl.BlockSpec((None, 2048, 1024), lambda k: (k, 0, 0)),
                  pl.BlockSpec((None, 1, CMP_HID), lambda k: (k, 0, 0))],
        out_specs=pl.BlockSpec((None, 8, CMP_HID), lambda k: (k, 0, 0)),
        out_shape=jax.ShapeDtypeStruct((2, 8, CMP_HID), F32),
        compiler_params=pltpu.CompilerParams(dimension_semantics=("parallel",), vmem_limit_bytes=VMEM_LIMIT),
        name="cmp_pe_term",
    )(pe_x, wbd, b1.reshape(2, 1, CMP_HID))


def _compress_kernel(tab_ref, *refs, npg, rows_minor):
    del tab_ref
    pages = refs[:npg]
    wbd_ref, w2_ref, pe_ref, cc_ref, kc_ref, vc_ref, xs_ref, carry_ref = refs[npg:npg + 8]
    ts = SEGS_PER_PAGE * npg

    @pl.when(pl.program_id(1) == 0)
    def _():
        carry_ref[...] = jnp.zeros(carry_ref.shape, F32)

    row0 = lax.broadcasted_iota(jnp.int32, (ts, CMP_HID), 0) == 0
    low_half = lax.broadcasted_iota(jnp.int32, (ts, LANES), 1) < NSA_HEAD_DIM
    half = CMP_STRIDE * NSA_HEAD_DIM
    for k in range(2):
        out_ref = kc_ref if k == 0 else vc_ref
        for gp in range(2):
            if rows_minor:
                rt_ref = refs[npg + 8].at[gp]
                xb_ref = xs_ref.at[gp]
                for j, pg in enumerate(pages):
                    rt_ref[j * PAGE_SIZE:(j + 1) * PAGE_SIZE, :] = pg[k, gp].astype(BF16).T.astype(F32)
                for j in range(CMP_STRIDE // 2):
                    ra = rt_ref[pl.ds(2 * j, ts, stride=CMP_STRIDE), :]
                    rb = rt_ref[pl.ds(2 * j + 1, ts, stride=CMP_STRIDE), :]
                    lanes = slice(j * LANES, (j + 1) * LANES)
                    xb_ref[:, lanes] = jnp.where(low_half, ra, pltpu.roll(rb, NSA_HEAD_DIM, axis=1)).astype(BF16)
                    xb_ref[:, half + j * LANES:half + (j + 1) * LANES] = jnp.where(
                        low_half, pltpu.roll(ra, NSA_HEAD_DIM, axis=1), rb).astype(BF16)
                ys = [jnp.dot(xb_ref[:, g2 * half:(g2 + 1) * half], wbd_ref[k], preferred_element_type=F32)
                      for g2 in range(2)]
            else:
                xb_ref = xs_ref.at[gp]
                off = k * 256 + gp * LANES
                for s in range(CMP_STRIDE):
                    lo = s * NSA_KV_W + off
                    piece = jnp.concatenate([pg[:, lo:lo + LANES] for pg in pages], axis=0)
                    xb_ref[:, s * LANES:(s + 1) * LANES] = piece.astype(BF16)
                y = jnp.dot(xb_ref[...], wbd_ref[k], preferred_element_type=F32)
                ys = [y[:, :2 * CMP_HID], y[:, 2 * CMP_HID:]]
            hs = []
            for g2 in range(2):
                pa = ys[g2][:, :CMP_HID]
                pb = ys[g2][:, CMP_HID:]
                ci = (k * 2 + gp) * 2 + g2
                prev = carry_ref[ci]
                pa_prev = jnp.where(row0, prev[7:8, :], pltpu.roll(pa, 1, axis=0))
                carry_ref[ci] = pa[ts - 8:ts, :]
                hs.append(jax.nn.gelu(pa_prev + pb + pe_ref[k, 0:1, :]))
            hid = jnp.concatenate(hs, axis=-1).astype(BF16)
            o = jnp.dot(hid, w2_ref[k], preferred_element_type=F32) + cc_ref[k]
            out_ref[2 * gp] = o[:, :LANES].astype(BF16)
            out_ref[2 * gp + 1] = o[:, LANES:].astype(BF16)


def _compress(pages, table, wbd, w1cat, w2bd, pe_term, ccols):
    bc, n_pages = table.shape
    npg = min(32, n_pages)
    ts = SEGS_PER_PAGE * npg
    nseg = n_pages * SEGS_PER_PAGE
    rows_minor = pages.ndim == 5
    w_first = w1cat if rows_minor else wbd
    page_block = (None,) + pages.shape[1:]

    def page_spec(j):
        return pl.BlockSpec(page_block, lambda b, i, tab: (tab[b, i * npg + j],) + (0,) * (pages.ndim - 1))

    const = lambda *shape: pl.BlockSpec(shape, lambda b, i, tab: (0,) * len(shape), pipeline_mode=pl.Buffered(1))
    out_spec = pl.BlockSpec((None, NSA_KV_GROUPS, ts, LANES), lambda b, i, tab: (b, 0, i, 0))
    grid_spec = pltpu.PrefetchScalarGridSpec(
        num_scalar_prefetch=1,
        grid=(bc, n_pages // npg),
        in_specs=[page_spec(j) for j in range(npg)] + [
            const(*w_first.shape), const(2, 512, 256), const(2, 8, CMP_HID), const(2, 1, 256)],
        out_specs=[out_spec, out_spec],
        scratch_shapes=[pltpu.VMEM((2, ts, 2048), BF16), pltpu.VMEM((8, 8, CMP_HID), F32)] + (
            [pltpu.VMEM((2, npg * PAGE_SIZE, LANES), F32)] if rows_minor else []),
    )
    out_sds = jax.ShapeDtypeStruct((bc, NSA_KV_GROUPS, nseg, LANES), BF16)
    return pl.pallas_call(
        functools.partial(_compress_kernel, npg=npg, rows_minor=rows_minor),
        grid_spec=grid_spec,
        out_shape=[out_sds, out_sds],
        compiler_params=pltpu.CompilerParams(
            dimension_semantics=("parallel", "arbitrary"), vmem_limit_bytes=VMEM_LIMIT),
        name="kv_compress",
    )(table, *([pages] * npg), w_first, w2bd, pe_term, ccols)


def _compress_weights(pe, w1, b1, w2):
    eye2 = jnp.eye(2, dtype=F32)
    w = w1.reshape(2, 2, CMP_STRIDE, NSA_HEAD_DIM, CMP_HID)
    wbd = jnp.einsum('kasdh,gj->ksgdjah', w, eye2).reshape(2, 2048, 1024).astype(BF16)
    w1cat = w.transpose(0, 2, 3, 1, 4).reshape(2, CMP_STRIDE * NSA_HEAD_DIM, 2 * CMP_HID).astype(BF16)
    w2p = jnp.pad(w2, ((0, 0), (0, 0), (0, LANES - NSA_HEAD_DIM)))
    w2bd = jnp.einsum('khd,gj->kghjd', w2p, eye2).reshape(2, 512, 256).astype(BF16)
    pe_x = pe.reshape(2, 2, CMP_STRIDE, NSA_HEAD_DIM).transpose(0, 2, 1, 3).reshape(2, 1, 2048)
    pe_x = jnp.broadcast_to(pe_x, (2, 8, 2048)).astype(BF16)
    pe_term = _pe_term(pe_x, wbd, b1)
    cc = np.zeros((2, 1, 256), np.float32)
    for g2 in range(2):
        cc[0, 0, g2 * LANES + 64] = 1.0
        cc[0, 0, g2 * LANES + 65] = 1.0
        cc[1, 0, g2 * LANES + 64] = 1.0
    return wbd, w1cat, w2bd, pe_term, jnp.asarray(cc)


def _topk_rows_mask(score, k, fillers=(), preselected=None):
    blk = lax.broadcasted_iota(jnp.int32, score.shape, 0).astype(F32)
    sel = jnp.zeros(score.shape, F32) if preselected is None else preselected.astype(F32)
    fillers = list(fillers)
    for it in range(k):
        mx = jnp.max(score, axis=0, keepdims=True)
        idx = jnp.min(jnp.where(score == mx, blk, 1e9), axis=0, keepdims=True)
        hit = blk == idx
        sel = jnp.where(hit, 1.0, sel)
        score = jnp.where(hit, -jnp.inf, score)
        for f in fillers[it * len(fillers) // k:(it + 1) * len(fillers) // k]:
            f()
    return sel


def _nsa_prompt_kernel(q_ref, gt_ref, kc_ref, vc_ref, ks_ref, vs_ref, kw_ref, vw_ref, tb_ref, lut_ref, ovt_ref,
                       o_ref, qaug_ref, sc_ref, pcb_ref, pcs_ref, sw_ref, pw_ref, pfar_ref, *, nseg, n_sb):
    g = pl.program_id(1)
    qb = pl.program_id(2)
    s0 = qb * Q_BLOCK
    rows = NSA_HPG * Q_BLOCK
    q = q_ref[...].reshape(rows, LANES)
    tb = tb_ref[...]

    sc_ref[...] = _nt(q, kc_ref[...])
    qi128 = lax.broadcasted_iota(jnp.int32, (Q_BLOCK, LANES), 0)
    li128 = lax.broadcasted_iota(jnp.int32, (Q_BLOCK, LANES), 1)

    def add_near_bias(chunk):
        l0 = pl.multiple_of(chunk * LANES, LANES)
        dist = s0 + qi128 - CMP_STRIDE * (chunk * LANES + li128) - (CMP_STRIDE - 1)
        for hh in range(NSA_HPG):
            lut = jnp.broadcast_to(lut_ref[hh, 0:1, :], (Q_BLOCK, LANES))
            rs = slice(hh * Q_BLOCK, (hh + 1) * Q_BLOCK)
            sc_ref[rs, pl.ds(l0, LANES)] = sc_ref[rs, pl.ds(l0, LANES)] + _lut_gather(lut, dist)

    chunk_lo = jnp.maximum(8 * qb - 8, 0) // LANES
    chunk_hi = (8 * qb + 7) // LANES
    add_near_bias(chunk_lo)

    @pl.when(chunk_hi != chunk_lo)
    def _():
        add_near_bias(chunk_hi)

    rb_n = 32
    qi_b = lax.broadcasted_iota(jnp.int32, (rb_n, nseg), 0)
    mi_b = lax.broadcasted_iota(jnp.int32, (rb_n, nseg), 1)
    for rb in range(Q_BLOCK // rb_n):
        dist_b = s0 + rb * rb_n + qi_b - CMP_STRIDE * mi_b - (CMP_STRIDE - 1)
        valid_b = (dist_b >= 0) & (mi_b >= 1)
        pcs_b = jnp.zeros((rb_n, nseg), F32)
        for hh in range(NSA_HPG):
            rs = slice(hh * Q_BLOCK + rb * rb_n, hh * Q_BLOCK + (rb + 1) * rb_n)
            s_b = jnp.where(valid_b, sc_ref[rs, :], NEG)
            e = jnp.where(valid_b, jnp.exp(s_b - jnp.max(s_b, axis=-1, keepdims=True)), 0.0)
            ssum = jnp.sum(e, axis=-1, keepdims=True)
            pc = e * (1.0 / jnp.where(ssum > 0, ssum, 1.0))
            pcb_ref[rs, :] = pc.astype(BF16)
            pcs_b = pcs_b + pc
        pcs_ref[rb * rb_n:(rb + 1) * rb_n, :] = pcs_b
    o_c = jnp.dot(pcb_ref[...], vc_ref[...], preferred_element_type=F32)

    pcs = pcs_ref[...]
    hi, lo = _split2(pcs)
    ovt = ovt_ref[...]
    imp_t = _nt(ovt, hi) + _nt(ovt, lo)
    jb = lax.broadcasted_iota(jnp.int32, (LANES, Q_BLOCK), 0)
    q_blk = (s0 + lax.broadcasted_iota(jnp.int32, (LANES, Q_BLOCK), 1)) // SLC_BLOCK
    forced = (jb == 0) | (jb == q_blk) | (jb == q_blk - 1)
    score = jnp.where(forced, -jnp.inf, jnp.where(jb <= q_blk, imp_t, -1e4))
    score = jnp.where(jb < n_sb, score, -3e38)
    n_forced = 3

    w0 = pl.multiple_of(s0, Q_BLOCK)
    n_w = WINDOW + Q_BLOCK
    sw_ref[...] = _nt(q, kw_ref[pl.ds(w0, n_w), :])
    qi_w = lax.broadcasted_iota(jnp.int32, (rb_n, n_w), 0)
    kk_w = lax.broadcasted_iota(jnp.int32, (rb_n, n_w), 1)

    def window_block(hh, rb):
        ok_w = (kk_w > qi_w + rb * rb_n) & (s0 + kk_w >= WINDOW)
        rs = slice(hh * Q_BLOCK + rb * rb_n, hh * Q_BLOCK + (rb + 1) * rb_n)
        s_b = jnp.where(ok_w, sw_ref[rs, :], NEG)
        s_b = jnp.concatenate([s_b[:, :WINDOW - Q_BLOCK],
                               s_b[:, WINDOW - Q_BLOCK:] + tb_ref[hh, rb * rb_n:(rb + 1) * rb_n, :]], axis=-1)
        pw_ref[rs, :] = jnp.exp(s_b - jnp.max(s_b, axis=-1, keepdims=True)).astype(BF16)

    gt = jax.nn.sigmoid(gt_ref[...])
    glane = lax.broadcasted_iota(jnp.int32, gt.shape, 1)
    gates = [[None] * 3 for _ in range(NSA_HPG)]

    def gate_column(hh, br):
        col = 3 * (NSA_HPG * g + hh) + br
        gates[hh][br] = jnp.sum(jnp.where(glane == col, gt, 0.0), axis=-1, keepdims=True)

    fillers = []
    for hh in range(NSA_HPG):
        fillers += [functools.partial(window_block, hh, rb) for rb in range(Q_BLOCK // rb_n)]
        fillers += [functools.partial(gate_column, hh, br) for br in range(3)]
    sel_t = _topk_rows_mask(score, min(N_SELECT, n_sb) - n_forced, fillers, preselected=forced)
    acc_w = jnp.dot(pw_ref[...], vw_ref[pl.ds(w0, n_w), :], preferred_element_type=F32)
    o_w = acc_w * (1.0 / acc_w[:, NSA_HEAD_DIM:NSA_HEAD_DIM + 1])
    unsel = 1.0 - sel_t.T
    blk_lane = lax.broadcasted_iota(jnp.int32, (Q_BLOCK, LANES), 1)
    near_blk0 = 2 * qb - 2
    unsel_far = jnp.where(blk_lane >= near_blk0, 1.0, unsel)

    qaug_ref[:, LANES:] = q
    for hh in range(NSA_HPG):
        qaug_ref[hh * Q_BLOCK:(hh + 1) * Q_BLOCK, :LANES] = unsel.astype(BF16)
    a0 = pl.multiple_of(jnp.maximum(s0 - Q_BLOCK, 0), Q_BLOCK)
    b0 = pl.multiple_of(s0, Q_BLOCK)
    kn = jnp.concatenate([ks_ref[pl.ds(a0, Q_BLOCK), :], ks_ref[pl.ds(b0, Q_BLOCK), :]], axis=0)
    vn = jnp.concatenate([vs_ref[pl.ds(a0, Q_BLOCK), :], vs_ref[pl.ds(b0, Q_BLOCK), :]], axis=0)
    kcol = lax.broadcasted_iota(jnp.int32, (Q_BLOCK, 2 * Q_BLOCK), 1)
    no_prev = jnp.where((kcol < Q_BLOCK) & (qb == 0), NEG, 0.0)
    s_n = _nt(qaug_ref[...], kn).reshape(NSA_HPG, Q_BLOCK, 2 * Q_BLOCK) + (tb + no_prev[None])
    s_n = s_n.reshape(rows, 2 * Q_BLOCK)
    m_run = jnp.max(s_n, axis=-1, keepdims=True)
    acc = jnp.dot(jnp.exp(s_n - m_run).astype(BF16), vn, preferred_element_type=F32)

    for hh in range(NSA_HPG):
        qaug_ref[hh * Q_BLOCK:(hh + 1) * Q_BLOCK, :LANES] = unsel_far.astype(BF16)
    kc_far = 4 * Q_BLOCK
    n_far = (jnp.maximum(qb - 1, 0) + 3) // 4

    pfar_ref[1] = jnp.zeros((rows, kc_far), BF16)

    def far_logits(c):
        return _nt(qaug_ref[...], ks_ref[pl.ds(pl.multiple_of(c * kc_far, kc_far), kc_far), :])

    def far_pv(slot, c):
        k0 = pl.multiple_of(jnp.maximum(c, 0) * kc_far, kc_far)
        return jnp.dot(pfar_ref[slot], vs_ref[pl.ds(k0, kc_far), :], preferred_element_type=F32)

    def far_trip(t, carry):
        m_old, acc_old, alpha_prev = carry
        s_a = far_logits(2 * t)
        acc_1 = alpha_prev * acc_old + far_pv(1, 2 * t - 1)
        s_b = far_logits(2 * t + 1)
        m_a = jnp.maximum(m_old, jnp.max(s_a, axis=-1, keepdims=True))
        pfar_ref[0] = jnp.exp((s_a - m_a).astype(BF16))
        acc_2 = jnp.exp(m_old - m_a) * acc_1 + far_pv(0, 2 * t)
        m_b = jnp.maximum(m_a, jnp.max(s_b, axis=-1, keepdims=True))
        pfar_ref[1] = jnp.exp((s_b - m_b).astype(BF16))
        return m_b, acc_2, jnp.exp(m_a - m_b)

    n_trips = (n_far + 1) // 2
    m_run, acc, alpha_last = lax.fori_loop(0, n_trips, far_trip, (m_run, acc, jnp.ones((rows, 1), F32)))
    acc = alpha_last * acc + far_pv(1, 2 * n_trips - 1)
    o_s = acc * (1.0 / acc[:, NSA_HEAD_DIM:NSA_HEAD_DIM + 1])

    low = lax.broadcasted_iota(jnp.int32, (Q_BLOCK, LANES), 1) < NSA_HEAD_DIM
    o_heads = []
    for hh in range(NSA_HPG):
        rs = slice(hh * Q_BLOCK, (hh + 1) * Q_BLOCK)
        o_heads.append(gates[hh][0] * o_c[rs] + gates[hh][1] * o_s[rs] + gates[hh][2] * o_w[rs])
    for pair in range(NSA_HPG // 2):
        o_ref[:, pair * LANES:(pair + 1) * LANES] = jnp.where(
            low, o_heads[2 * pair], pltpu.roll(o_heads[2 * pair + 1], NSA_HEAD_DIM, axis=1)).astype(o_ref.dtype)


def _overlap_t(n_blk_pad, nseg):
    m = np.arange(nseg)[None, :]
    j = np.arange(n_blk_pad)[:, None]
    c_start = CMP_STRIDE * m - CMP_STRIDE
    c_end = CMP_STRIDE * m + CMP_STRIDE - 1
    ov = (c_start < j * SLC_BLOCK + SLC_BLOCK) & (c_end >= j * SLC_BLOCK) & (m >= 1)
    return ov.astype(np.float32)


def _nsa_prompt_attention(q128, p, gate_col_block, kc, vc, ks, vs, kw, vw, tb, lut):
    b, _, t, _ = q128.shape
    nseg = kc.shape[2]
    n_sb = t // SLC_BLOCK
    assert n_sb <= LANES and t % (4 * Q_BLOCK) == 0 and nseg % LANES == 0
    ovt = jnp.asarray(_overlap_t(LANES, nseg), dtype=BF16)
    per_bg = lambda rows, cols: pl.BlockSpec((None, None, rows, cols), lambda b, g, i: (b, g, 0, 0))
    return pl.pallas_call(
        functools.partial(_nsa_prompt_kernel, nseg=nseg, n_sb=n_sb),
        grid=(b, NSA_KV_GROUPS, t // Q_BLOCK),
        in_specs=[pl.BlockSpec((None, NSA_HPG, Q_BLOCK, LANES), lambda b, g, i: (b, g, i, 0)),
                  pl.BlockSpec((None, Q_BLOCK, LANES), lambda b, g, i: (b, i, gate_col_block)),
                  per_bg(nseg, LANES), per_bg(nseg, LANES),
                  per_bg(t, 2 * LANES), per_bg(t, LANES),
                  per_bg(t + WINDOW, LANES), per_bg(t + WINDOW, LANES),
                  pl.BlockSpec((NSA_HPG, Q_BLOCK, 2 * Q_BLOCK), lambda b, g, i: (g, 0, 0)),
                  pl.BlockSpec((NSA_HPG, 8, LANES), lambda b, g, i: (g, 0, 0)),
                  pl.BlockSpec((LANES, nseg), lambda b, g, i: (0, 0))],
        out_specs=pl.BlockSpec((None, Q_BLOCK, NSA_HPG * NSA_HEAD_DIM), lambda b, g, i: (b, i, g)),
        out_shape=jax.ShapeDtypeStruct((b, t, NSA_Q_W), BF16),
        scratch_shapes=[pltpu.VMEM((NSA_HPG * Q_BLOCK, 2 * LANES), BF16),
                        pltpu.VMEM((NSA_HPG * Q_BLOCK, nseg), F32), pltpu.VMEM((NSA_HPG * Q_BLOCK, nseg), BF16),
                        pltpu.VMEM((Q_BLOCK, nseg), F32),
                        pltpu.VMEM((NSA_HPG * Q_BLOCK, WINDOW + Q_BLOCK), F32),
                        pltpu.VMEM((NSA_HPG * Q_BLOCK, WINDOW + Q_BLOCK), BF16),
                        pltpu.VMEM((2, NSA_HPG * Q_BLOCK, 4 * Q_BLOCK), BF16)],
        compiler_params=pltpu.CompilerParams(
            dimension_semantics=("parallel", "parallel", "arbitrary"), vmem_limit_bytes=VMEM_LIMIT),
        name="nsa_prompt_attention",
    )(q128, p, kc, vc, ks, vs, kw, vw, tb, lut, ovt)


def _gdn_conv_kernel(x_ref, w_ref, o_ref, carry_ref, *, tm, tc):
    j = pl.program_id(1)

    @pl.when(pl.program_id(2) == 0)
    def _():
        carry_ref[...] = jnp.zeros(carry_ref.shape, F32)

    x = x_ref[...]
    w = w_ref[...]
    prev = carry_ref[...]
    row8 = lax.broadcasted_iota(jnp.int32, (8, tc), 0)
    conv = x * w[CONV_W - 1:CONV_W, :]
    for sft in range(1, CONV_W):
        xs = pltpu.roll(x, sft, axis=0)
        top = jnp.where(row8 < sft, pltpu.roll(prev, sft, axis=0), xs[0:8])
        xs = top if tm == 8 else jnp.concatenate([top, xs[8:]], axis=0)
        conv = conv + xs * w[CONV_W - 1 - sft:CONV_W - sft, :]
    carry_ref[...] = x[tm - 8:tm, :]
    act = conv * jax.nn.sigmoid(conv)
    for hd in range(tc // GDN_HEAD_DIM):
        sl = slice(hd * GDN_HEAD_DIM, (hd + 1) * GDN_HEAD_DIM)
        a = act[:, sl]
        col0 = j * tc + hd * GDN_HEAD_DIM
        nrm = a * lax.rsqrt(jnp.sum(a * a, axis=-1, keepdims=True) + 1e-6)
        nrm = nrm * jnp.where(col0 < 1024, GDN_HEAD_DIM ** -0.5, 1.0)
        o_ref[:, sl] = jnp.where(col0 < 2048, nrm, a)


def _gdn_conv(p, conv_w):
    b, t, _ = p.shape
    tm = _row_tile(t)
    tc = 1024
    return pl.pallas_call(
        functools.partial(_gdn_conv_kernel, tm=tm, tc=tc),
        grid=(b, C_CONV // tc, t // tm),
        in_specs=[pl.BlockSpec((None, tm, tc), lambda b, j, i: (b, i, j)),
                  pl.BlockSpec((CONV_W, tc), lambda b, j, i: (0, j))],
        out_specs=pl.BlockSpec((None, tm, tc), lambda b, j, i: (b, i, j)),
        out_shape=jax.ShapeDtypeStruct((b, t, C_CONV), F32),
        scratch_shapes=[pltpu.VMEM((8, tc), F32)],
        compiler_params=pltpu.CompilerParams(
            dimension_semantics=("parallel", "parallel", "arbitrary"), vmem_limit_bytes=VMEM_LIMIT),
        name="gdn_conv",
    )(p, conv_w)


def _gdn_gate_kernel(ba_ref, alog_ref, dtb_ref, o_ref):
    x = ba_ref[...]
    y = x + dtb_ref[...]
    softplus = jnp.maximum(y, 0.0) + jnp.log1p(jnp.exp(-jnp.abs(y)))
    g = -jnp.exp(alog_ref[...]) * softplus
    lane = lax.broadcasted_iota(jnp.int32, x.shape, 1)
    o_ref[...] = jnp.where(lane < GDN_V_HEADS, jax.nn.sigmoid(x), g)


def _gdn_gates(p, ba_col_block, a_log, dt_bias):
    b, t, _ = p.shape
    tm = _row_tile(t)
    pad = lambda v: jnp.pad(v.reshape(1, GDN_V_HEADS), ((0, 0), (GDN_V_HEADS, LANES - 2 * GDN_V_HEADS)))
    return pl.pallas_call(
        _gdn_gate_kernel,
        grid=(b, t // tm),
        in_specs=[pl.BlockSpec((None, tm, LANES), lambda b, i: (b, i, ba_col_block)),
                  pl.BlockSpec((1, LANES), lambda b, i: (0, 0)),
                  pl.BlockSpec((1, LANES), lambda b, i: (0, 0))],
        out_specs=pl.BlockSpec((None, tm, LANES), lambda b, i: (b, i, 0)),
        out_shape=jax.ShapeDtypeStruct((b, t, LANES), F32),
        compiler_params=pltpu.CompilerParams(dimension_semantics=("parallel", "parallel")),
        name="gdn_gates",
    )(p, pad(a_log), pad(dt_bias))


def _bdot(a, b):
    return jnp.dot(a.astype(BF16), b.astype(BF16), preferred_element_type=F32)


GDN_PACK = 4
_PACK_ORDER = (0, 2, 1, 3)
_PACK_HEADS = tuple(GDN_PACK * p + o for p in range(GDN_V_HEADS // GDN_PACK) for o in _PACK_ORDER)


def _iota2(shape, axis):
    return lax.broadcasted_iota(jnp.int32, shape, axis)


def _packed_mm(a_cat, b_cat, bd_mask):
    b_bd = jnp.where(bd_mask, jnp.concatenate([b_cat] * GDN_PACK, axis=0), 0.0)
    return _bdot(a_cat, b_bd)


def _unit_lower_inverse_packed(ls, row, col, bd_mask):
    eye = (row == col).astype(F32)
    same16 = (row // 16) == (col // 16)
    same32 = (row // 32) == (col // 32)
    ms = [jnp.where(same16, -l, 0.0) for l in ls]
    ps = [eye + m for m in ms]
    for _ in range(3):
        ms = [_packed_mm(m, m, bd_mask) for m in ms]
        ps = [p + _packed_mm(p, m, bd_mask) for p, m in zip(ps, ms)]
    for level in (same32 & jnp.logical_not(same16), jnp.logical_not(same32)):
        ts = [_packed_mm(jnp.where(level, l, 0.0), p, bd_mask) for l, p in zip(ls, ps)]
        ps = [p - _packed_mm(p, t, bd_mask) for p, t in zip(ps, ts)]
    return ps


def _gdn_delta_kernel(act_ref, bg_ref, gt_ref, s0_ref, ltri_ref, lbd_ref, o_ref, s_ref, sbd_ref, *, bb):
    c, hd = GDN_CHUNK, GDN_HEAD_DIM
    n_packs = GDN_V_HEADS // GDN_PACK
    n_units = bb * n_packs
    n_pairs = GDN_V_HEADS // 2
    zero_hd = jnp.zeros((hd, hd), F32)

    @pl.when(pl.program_id(1) == 0)
    def _():
        for bi in range(bb):
            for pr in range(n_pairs):
                h0, h1 = _PACK_HEADS[2 * pr], _PACK_HEADS[2 * pr + 1]
                sbd_ref[bi * n_pairs + pr] = jnp.concatenate(
                    [jnp.concatenate([s0_ref[bi, h0], zero_hd], axis=-1),
                     jnp.concatenate([zero_hd, s0_ref[bi, h1]], axis=-1)], axis=0)

    bgs = [bg_ref[bi] for bi in range(bb)]
    cums = [sum(jnp.dot(ltri_ref[...], part, preferred_element_type=F32) for part in _split3(bg)) for bg in bgs]
    gcr_alls = [sum(_nt(part, lbd_ref[...]) for part in _split3(gt_ref[bi])) for bi in range(bb)]
    row = _iota2((c, GDN_PACK * c), 0)
    lane = _iota2((c, GDN_PACK * c), 1)
    col, slot = lane % c, lane // c
    incl, strict = row >= col, row > col
    bd_mask = (_iota2((4 * c, 4 * c), 0) // c) == (_iota2((4 * c, 4 * c), 1) // c)
    pair_mask = (_iota2((2 * hd, 2 * hd), 0) // hd) == (_iota2((2 * hd, 2 * hd), 1) // hd)
    k_mask = (_iota2((2 * hd, hd), 0) // hd) == (_iota2((2 * hd, hd), 1) // c)
    row_pair = _iota2((2 * hd, 1), 0)

    def slot_cat(cols):
        out = jnp.broadcast_to(cols[3], (c, GDN_PACK * c))
        for x in (2, 1, 0):
            out = jnp.where(slot == x, cols[x], out)
        return out

    def side_by_side(a, b):
        return jnp.concatenate([a, b], axis=-1)

    qs, ks, betas, gcs, lmats, a_ins = [], [], [], [], [], []
    for u in range(n_units):
        bi, p = divmod(u, n_packs)
        bg, cum, gcr_all = bgs[bi], cums[bi], gcr_alls[bi]
        heads = _PACK_HEADS[GDN_PACK * p:GDN_PACK * (p + 1)]
        qa, qb = (act_ref[bi, :, (2 * p + i) * hd:(2 * p + i + 1) * hd] for i in (0, 1))
        ka, kb = (act_ref[bi, :, 1024 + (2 * p + i) * hd:1024 + (2 * p + i + 1) * hd] for i in (0, 1))
        kt = jnp.concatenate([ka, kb], axis=0).T
        k_bd = jnp.where(k_mask, jnp.concatenate([kt, kt], axis=0), 0.0)
        kq = _bdot(jnp.concatenate([side_by_side(ka, kb), side_by_side(qa, qb)], axis=0), k_bd)
        kk = side_by_side(kq[:c], kq[:c])
        qk = side_by_side(kq[c:], kq[c:])
        beta = [bg[:, h:h + 1] for h in heads]
        gc = [cum[:, GDN_V_HEADS + h:GDN_V_HEADS + h + 1] for h in heads]
        decay = jnp.where(incl, jnp.exp(jnp.where(incl, slot_cat(gc) - gcr_all[p:p + 1, :], 0.0)), 0.0)
        lmats.append(jnp.where(strict, slot_cat(beta) * kk * decay, 0.0))
        a_ins.append(qk * decay)
        qs.append((qa, qb, qa, qb)); ks.append((ka, kb, ka, kb)); betas.append(beta); gcs.append(gc)

    tinvs = _unit_lower_inverse_packed(lmats, row, col, bd_mask)

    uws, egs = [], []
    for u in range(n_units):
        bi, p = divmod(u, n_packs)
        bands = []
        eg = [jnp.exp(g) for g in gcs[u]]
        for x in range(GDN_PACK):
            h = _PACK_HEADS[GDN_PACK * p + x]
            vh = act_ref[bi, :, 2048 + h * hd:2048 + (h + 1) * hd]
            rhs = betas[u][x] * side_by_side(vh, ks[u][x] * eg[x])
            pieces = [jnp.zeros((c, 2 * hd * x), F32)] * (x > 0) + [rhs] + [jnp.zeros((c, 2 * hd * (3 - x)), F32)] * (x < 3)
            bands.append(jnp.concatenate(pieces, axis=-1))
        uws.append(_bdot(tinvs[u], jnp.concatenate(bands, axis=0)))
        egs.append(eg)

    wss, s_olds = [], []
    for u in range(n_units):
        for pr in range(2):
            x0, x1 = 2 * pr, 2 * pr + 1
            w0, w1 = (uws[u][:, 2 * hd * x + hd:2 * hd * (x + 1)] for x in (x0, x1))
            lhs = jnp.concatenate([side_by_side(w0, w1),
                                   side_by_side(qs[u][x0] * egs[u][x0], qs[u][x1] * egs[u][x1])], axis=0)
            s_old = sbd_ref[2 * u + pr]
            s_olds.append(s_old)
            wss.append(_bdot(lhs, s_old))

    v_news = []
    for u in range(n_units):
        vn = []
        for x in range(GDN_PACK):
            ws = wss[2 * u + x // 2]
            vn.append(uws[u][:, 2 * hd * x:2 * hd * x + hd] - ws[:c, hd * (x % 2):hd * (x % 2 + 1)])
        v_news.append(vn)
    for u in range(n_units):
        bi, p = divmod(u, n_packs)
        bands = []
        for x in range(GDN_PACK):
            pieces = [jnp.zeros((c, hd * x), F32)] * (x > 0) + [v_news[u][x]] + [jnp.zeros((c, hd * (3 - x)), F32)] * (x < 3)
            bands.append(jnp.concatenate(pieces, axis=-1))
        av = _bdot(a_ins[u], jnp.concatenate(bands, axis=0))
        for x in range(GDN_PACK):
            h = _PACK_HEADS[GDN_PACK * p + x]
            ws = wss[2 * u + x // 2]
            o_ref[bi, :, h * hd:(h + 1) * hd] = ws[c:, hd * (x % 2):hd * (x % 2 + 1)] + av[:, hd * x:hd * (x + 1)]
    zrows = jnp.zeros((c, 2 * hd), F32)
    for u in range(n_units):
        for pr in range(2):
            x0, x1 = 2 * pr, 2 * pr + 1
            gl0, gl1 = gcs[u][x0][c - 1:c, :], gcs[u][x1][c - 1:c, :]
            kd = jnp.concatenate([side_by_side(ks[u][x0] * jnp.exp(gl0 - gcs[u][x0]),
                                               ks[u][x1] * jnp.exp(gl1 - gcs[u][x1])), zrows], axis=0)
            kd_t = jnp.concatenate([kd[:, :hd].T, kd[:, hd:].T], axis=0)
            vn = jnp.concatenate([side_by_side(v_news[u][x0], v_news[u][x1]), zrows], axis=0)
            d_last = jnp.where(row_pair < hd, jnp.exp(gl0), jnp.exp(gl1))
            sbd_ref[2 * u + pr] = jnp.where(pair_mask, s_olds[2 * u + pr] * d_last + _bdot(kd_t, vn), 0.0)

    @pl.when(pl.program_id(1) == pl.num_programs(1) - 1)
    def _():
        for bi in range(bb):
            for pr in range(n_pairs):
                s_pair = sbd_ref[bi * n_pairs + pr]
                s_ref[bi, _PACK_HEADS[2 * pr]] = s_pair[:hd, :hd]
                s_ref[bi, _PACK_HEADS[2 * pr + 1]] = s_pair[hd:, hd:]


def _gdn_delta(act, bg, s0):
    b, t, _ = act.shape
    nc = t // GDN_CHUNK
    n_packs = GDN_V_HEADS // GDN_PACK
    wp = GDN_PACK * GDN_CHUNK
    g_rows = bg[:, :, GDN_V_HEADS:2 * GDN_V_HEADS][:, :, np.asarray(_PACK_HEADS)]
    g_rows = g_rows.reshape(b, nc, GDN_CHUNK, n_packs, GDN_PACK).transpose(0, 1, 3, 4, 2).reshape(b, nc, n_packs, wp)
    g_rows = jnp.pad(g_rows, ((0, 0), (0, 0), (0, 8 - n_packs), (0, 0)))
    tri = np.tril(np.ones((GDN_CHUNK, GDN_CHUNK), np.float32))
    ltri = jnp.asarray(tri, dtype=BF16)
    lbd = jnp.asarray(np.kron(np.eye(GDN_PACK, dtype=np.float32), tri), dtype=BF16)
    bb = 2 if b % 2 == 0 else 1
    state_spec = pl.BlockSpec((bb, GDN_V_HEADS, GDN_HEAD_DIM, GDN_HEAD_DIM), lambda b, n: (b, 0, 0, 0))
    return pl.pallas_call(
        functools.partial(_gdn_delta_kernel, bb=bb),
        grid=(b // bb, nc),
        in_specs=[pl.BlockSpec((bb, GDN_CHUNK, C_CONV), lambda b, n: (b, n, 0)),
                  pl.BlockSpec((bb, GDN_CHUNK, LANES), lambda b, n: (b, n, 0)),
                  pl.BlockSpec((bb, None, 8, wp), lambda b, n: (b, n, 0, 0)),
                  state_spec,
                  pl.BlockSpec((GDN_CHUNK, GDN_CHUNK), lambda b, n: (0, 0)),
                  pl.BlockSpec((wp, wp), lambda b, n: (0, 0))],
        out_specs=[pl.BlockSpec((bb, GDN_CHUNK, GDN_V_W), lambda b, n: (b, n, 0)), state_spec],
        out_shape=[jax.ShapeDtypeStruct((b, t, GDN_V_W), F32),
                   jax.ShapeDtypeStruct(s0.shape, F32)],
        scratch_shapes=[pltpu.VMEM((bb * GDN_V_HEADS // 2, 2 * GDN_HEAD_DIM, 2 * GDN_HEAD_DIM), F32)],
        compiler_params=pltpu.CompilerParams(
            dimension_semantics=("parallel", "arbitrary"), vmem_limit_bytes=VMEM_LIMIT),
        name="gdn_delta_rule",
    )(act, bg, g_rows, s0, ltri, lbd)


SAMPLE_ROWS = NSA_HEADS * 4


def _sample_cmp_kernel(q_ref, kc_ref, vc_ref, lut_ref, ov_ref, oc_ref, un_ref, *, nseg, past_len, n_sb, nq):
    rg = NSA_HPG * nq
    ri = lax.broadcasted_iota(jnp.int32, (rg, nseg), 0)
    mi = lax.broadcasted_iota(jnp.int32, (rg, nseg), 1)
    dist = past_len + ri % nq - CMP_STRIDE * mi - (CMP_STRIDE - 1)
    valid = (dist >= 0) & (mi >= 1)
    jl = lax.broadcasted_iota(jnp.int32, (8, un_ref.shape[-1]), 1)
    q_blk = (past_len + lax.broadcasted_iota(jnp.int32, jl.shape, 0) % nq) // SLC_BLOCK
    forced = (jl == 0) | (jl == q_blk) | (jl == q_blk - 1)
    jf = jl.astype(F32)
    for g in range(NSA_KV_GROUPS):
        sc = _nt(q_ref[g], kc_ref[g])
        tail = sc[:, nseg - LANES:] + _lut_gather(lut_ref[g], dist[:, nseg - LANES:])
        sc = jnp.where(valid, jnp.concatenate([sc[:, :nseg - LANES], tail], axis=-1), NEG)
        mx = jnp.max(sc, axis=-1, keepdims=True)
        e = jnp.where(valid, jnp.exp(sc - mx), 0.0)
        ssum = jnp.sum(e, axis=-1, keepdims=True)
        pc = e / jnp.where(ssum > 0, ssum, 1.0)
        oc_ref[g] = jnp.dot(pc.astype(BF16), vc_ref[g], preferred_element_type=F32)
        pcs = pc
        for hh in range(1, NSA_HPG):
            pcs = pcs + pltpu.roll(pc, hh * nq, axis=0)
        hi, lo = _split2(pcs[0:8])
        imp = jnp.dot(hi, ov_ref[...], preferred_element_type=F32) + jnp.dot(lo, ov_ref[...],
                                                                              preferred_element_type=F32)
        score = jnp.where(forced, 1e4, jnp.where(jl <= q_blk, imp, -1e4))
        score = jnp.where(jl < n_sb, score, -3e38)
        sel = jnp.zeros(score.shape, F32)
        for _ in range(min(N_SELECT, n_sb)):
            mxs = jnp.max(score, axis=-1, keepdims=True)
            idx = jnp.min(jnp.where(score == mxs, jf, 1e9), axis=-1, keepdims=True)
            hit = jf == idx
            sel = jnp.where(hit, 1.0, sel)
            score = jnp.where(hit, -jnp.inf, score)
        un_ref[g] = 1.0 - sel


def _sample_cmp(q16, kc, vc, lut16, past_len, nq):
    b = q16.shape[0]
    nseg = kc.shape[2]
    rg = NSA_HPG * nq
    n_sb = past_len // SLC_BLOCK + 1
    n_sb_pad = -(-n_sb // LANES) * LANES
    assert nq == 4 and nseg * CMP_STRIDE == past_len
    m = np.arange(nseg)[:, None]
    j = np.arange(n_sb_pad)[None, :]
    ov = ((CMP_STRIDE * m - CMP_STRIDE < j * SLC_BLOCK + SLC_BLOCK) & (CMP_STRIDE * m + CMP_STRIDE - 1 >= j * SLC_BLOCK)
          & (m >= 1) & (j < n_sb)).astype(np.float32)
    whole = lambda *shape: pl.BlockSpec((None,) + shape, lambda b: (b,) + (0,) * len(shape))
    return pl.pallas_call(
        functools.partial(_sample_cmp_kernel, nseg=nseg, past_len=past_len, n_sb=n_sb, nq=nq),
        grid=(b,),
        in_specs=[whole(NSA_KV_GROUPS, rg, LANES), whole(NSA_KV_GROUPS, nseg, LANES), whole(NSA_KV_GROUPS, nseg, LANES),
                  pl.BlockSpec((NSA_KV_GROUPS, rg, LANES), lambda b: (0, 0, 0)),
                  pl.BlockSpec((nseg, n_sb_pad), lambda b: (0, 0))],
        out_specs=[whole(NSA_KV_GROUPS, rg, LANES), whole(NSA_KV_GROUPS, 8, n_sb_pad)],
        out_shape=[jax.ShapeDtypeStruct((b, NSA_KV_GROUPS, rg, LANES), F32),
                   jax.ShapeDtypeStruct((b, NSA_KV_GROUPS, 8, n_sb_pad), F32)],
        compiler_params=pltpu.CompilerParams(dimension_semantics=("parallel",), vmem_limit_bytes=VMEM_LIMIT),
        name="nsa_sample_cmp_topk",
    )(q16, kc, vc, lut16, jnp.asarray(ov, dtype=BF16))


def _sample_sel_kernel(tab_ref, *refs, npg, past_len, nq):
    del tab_ref
    pages = refs[:npg]
    qbd_ref, un_ref, ee_ref, far_ref, lut_ref, m_ref, l_ref, acc_ref = refs[npg:]
    c = pl.program_id(1)
    kc = npg * PAGE_SIZE

    @pl.when(c == 0)
    def _():
        m_ref[...] = jnp.full(m_ref.shape, NEG, F32)
        l_ref[...] = jnp.zeros(l_ref.shape, F32)
        acc_ref[...] = jnp.zeros(acc_ref.shape, F32)

    kt = jnp.concatenate([pg[0] for pg in pages], axis=1).astype(BF16)
    vt = jnp.concatenate([pg[1] for pg in pages], axis=1).astype(BF16)
    s = (jnp.dot(qbd_ref[...], kt, preferred_element_type=F32) + far_ref[...][:, 0:1]
         + jnp.dot(un_ref[...], ee_ref[...], preferred_element_type=F32))
    ri = lax.broadcasted_iota(jnp.int32, (SAMPLE_ROWS, LANES), 0)
    li = lax.broadcasted_iota(jnp.int32, (SAMPLE_ROWS, LANES), 1)
    dist = past_len + ri % nq - (c * kc + kc - LANES + li)
    s = jnp.concatenate([s[:, :kc - LANES], s[:, kc - LANES:] + _lut_gather(lut_ref[...], dist)], axis=-1)
    m_old = m_ref[...][:, 0:1]
    m_new = jnp.maximum(m_old, jnp.max(s, axis=-1, keepdims=True))
    alpha = jnp.exp(m_old - m_new)
    p = jnp.exp(s - m_new)
    l_ref[...] = alpha * l_ref[...] + jnp.sum(p, axis=-1, keepdims=True)
    acc_ref[...] = alpha * acc_ref[...] + _nt(p.astype(BF16), vt)
    m_ref[...] = jnp.broadcast_to(m_new, m_ref.shape)


def _sample_sel(pages, table, qbd, unsel_c, farcol, lut64, past_len, nq):
    b, n_pages = table.shape
    npg = min(16, n_pages)
    kc = npg * PAGE_SIZE
    nch = n_pages // npg
    blk_per_chunk = kc // SLC_BLOCK
    ee = np.zeros((LANES, kc), np.float32)
    ee[np.arange(kc) // SLC_BLOCK, np.arange(kc)] = NEG
    assert blk_per_chunk <= LANES

    def page_spec(j):
        return pl.BlockSpec((None, 2, NSA_KV_W // 2, PAGE_SIZE), lambda b, c, tab: (tab[b, c * npg + j], 0, 0, 0))

    const = lambda *shape: pl.BlockSpec(shape, lambda b, c, tab: (0,) * len(shape))
    acc_spec = lambda cols: pl.BlockSpec((None, SAMPLE_ROWS, cols), lambda b, c, tab: (b, 0, 0))
    grid_spec = pltpu.PrefetchScalarGridSpec(
        num_scalar_prefetch=1,
        grid=(b, nch),
        in_specs=[page_spec(j) for j in range(npg)] + [
            pl.BlockSpec((None, SAMPLE_ROWS, 2 * LANES), lambda b, c, tab: (b, 0, 0)),
            pl.BlockSpec((None, None, SAMPLE_ROWS, LANES), lambda b, c, tab: (b, c, 0, 0)),
            const(LANES, kc), const(SAMPLE_ROWS, LANES), const(SAMPLE_ROWS, LANES)],
        out_specs=[acc_spec(LANES), acc_spec(LANES), acc_spec(2 * LANES)],
    )
    return pl.pallas_call(
        functools.partial(_sample_sel_kernel, npg=npg, past_len=past_len, nq=nq),
        grid_spec=grid_spec,
        out_shape=[jax.ShapeDtypeStruct((b, SAMPLE_ROWS, LANES), F32),
                   jax.ShapeDtypeStruct((b, SAMPLE_ROWS, LANES), F32),
                   jax.ShapeDtypeStruct((b, SAMPLE_ROWS, 2 * LANES), F32)],
        compiler_params=pltpu.CompilerParams(
            dimension_semantics=("parallel", "arbitrary"), vmem_limit_bytes=VMEM_LIMIT),
        name="nsa_sample_selected",
    )(table, *([pages] * npg), qbd, unsel_c, jnp.asarray(ee, dtype=BF16), farcol, lut64)


def _own_group_cols(x, grp):
    out = jnp.zeros((x.shape[0], NSA_HEAD_DIM), F32)
    for g in range(NSA_KV_GROUPS):
        out = jnp.where(grp == g, x[:, g * NSA_HEAD_DIM:(g + 1) * NSA_HEAD_DIM], out)
    return out


def _sample_final_kernel(qbd_ref, m_ref, l_ref, acc_ref, snew_ref, wc_ref, wnew_ref, oc_ref, gr_ref, far_ref, lut_ref,
                         o_ref, *, nq, w_buf):
    rows = SAMPLE_ROWS
    qbd = qbd_ref[...]
    far = far_ref[...][:, 0:1]
    lut = lut_ref[...]
    ri = lax.broadcasted_iota(jnp.int32, (rows, LANES), 0)
    li = lax.broadcasted_iota(jnp.int32, (rows, LANES), 1)
    tok = ri % nq
    grp = lax.broadcasted_iota(jnp.int32, (rows, NSA_HEAD_DIM), 0) // (NSA_HPG * nq)

    knew = snew_ref[...]
    s_new = _nt(qbd, knew[:, :256].astype(BF16)) + far
    d_new = tok - li
    s_new = jnp.where((d_new >= 0) & (li < nq), s_new + _lut_gather(lut, d_new), NEG)
    m_old = m_ref[...][:, 0:1]
    m_new = jnp.maximum(m_old, jnp.max(s_new, axis=-1, keepdims=True))
    alpha = jnp.exp(m_old - m_new)
    p_new = jnp.exp(s_new - m_new)
    l_s = alpha * l_ref[...][:, 0:1] + jnp.sum(p_new, axis=-1, keepdims=True)
    acc_s = alpha * acc_ref[...] + jnp.dot(p_new.astype(BF16), knew[:, 256:].astype(BF16), preferred_element_type=F32)
    o_s = _own_group_cols(acc_s, grp) / l_s

    kv_w = jnp.concatenate([wc_ref[...], wnew_ref[...]], axis=0)
    s_w = _nt(qbd, kv_w[:, :256].astype(BF16)) + far
    n_w = w_buf + LANES
    idx = lax.broadcasted_iota(jnp.int32, (rows, n_w), 1)
    d_w = w_buf + lax.broadcasted_iota(jnp.int32, (rows, n_w), 0) % nq - idx
    ok_w = (d_w >= 0) & (d_w < WINDOW) & (idx < w_buf + nq)
    corr = [jnp.zeros((rows, n_w - 2 * LANES), F32)]
    for cidx in range(2):
        lo = n_w - 2 * LANES + cidx * LANES
        corr.append(_lut_gather(lut, d_w[:, lo:lo + LANES]))
    s_w = jnp.where(ok_w, s_w + jnp.concatenate(corr, axis=-1), NEG)
    m_w = jnp.max(s_w, axis=-1, keepdims=True)
    p_w = jnp.exp(s_w - m_w)
    l_w = jnp.sum(p_w, axis=-1, keepdims=True)
    acc_w = jnp.dot(p_w.astype(BF16), kv_w[:, 256:].astype(BF16), preferred_element_type=F32)
    o_w = _own_group_cols(acc_w, grp) / l_w

    gt = jax.nn.sigmoid(gr_ref[...])
    o_ref[...] = gt[:, 0:1] * oc_ref[...][:, :NSA_HEAD_DIM] + gt[:, 1:2] * o_s + gt[:, 2:3] * o_w


def _sample_final(qbd, m, l, acc, snew, wcache, wnew, o_c, graw, farcol, lut64, nq):
    b = qbd.shape[0]
    w_buf = wcache.shape[1]
    assert w_buf == WINDOW
    whole = lambda *shape: pl.BlockSpec((None,) + shape, lambda b: (b,) + (0,) * len(shape))
    const = lambda *shape: pl.BlockSpec(shape, lambda b: (0,) * len(shape))
    return pl.pallas_call(
        functools.partial(_sample_final_kernel, nq=nq, w_buf=w_buf),
        grid=(b,),
        in_specs=[whole(SAMPLE_ROWS, 2 * LANES), whole(SAMPLE_ROWS, LANES), whole(SAMPLE_ROWS, LANES),
                  whole(SAMPLE_ROWS, 2 * LANES), whole(LANES, NSA_KV_W), whole(w_buf, NSA_KV_W), whole(LANES, NSA_KV_W),
                  whole(SAMPLE_ROWS, LANES), whole(SAMPLE_ROWS, LANES),
                  const(SAMPLE_ROWS, LANES), const(SAMPLE_ROWS, LANES)],
        out_specs=whole(SAMPLE_ROWS, NSA_HEAD_DIM),
        out_shape=jax.ShapeDtypeStruct((b, SAMPLE_ROWS, NSA_HEAD_DIM), F32),
        compiler_params=pltpu.CompilerParams(dimension_semantics=("parallel",), vmem_limit_bytes=VMEM_LIMIT),
        name="nsa_sample_final",
    )(qbd, m, l, acc, snew, wcache, wnew, o_c, graw, farcol, lut64)


def _ffn(x, mod, gains, w_in, w_out):
    hid = _norm_mod_swiglu(x, gains[2], mod[3], mod[4], w_in)
    return _matmul_rms_residual(hid, w_out, x, mod[5], gains[3])


def _nsa_layout_kernel(pq_ref, ps_ref, pw_ref, bias_ref, q_ref, ks_ref, vs_ref, kw_ref, vw_ref, *, tm):
    i = pl.program_id(1)
    lane = lax.broadcasted_iota(jnp.int32, (tm, LANES), 1)
    low = lane < NSA_HEAD_DIM

    def head_tile(ref, h):
        tile = ref[:, (h // 2) * LANES:(h // 2 + 1) * LANES]
        return pltpu.roll(tile, NSA_HEAD_DIM, axis=1) if h % 2 else tile

    for h in range(NSA_HEADS):
        q_ref[h] = jnp.where(low, head_tile(pq_ref, h), bias_ref[h]).astype(BF16)
    key = jnp.maximum(i - 1, 0) * tm + lax.broadcasted_iota(jnp.int32, (tm, LANES), 0)
    onehot = jnp.where(key // SLC_BLOCK == lane, NEG, 0.0).astype(BF16)
    k_ones = jnp.where((lane == NSA_HEAD_DIM) | (lane == NSA_HEAD_DIM + 1), 1.0, 0.0)
    v_ones = jnp.where(lane == NSA_HEAD_DIM, 1.0, 0.0)
    live = i > 0
    for g in range(NSA_KV_GROUPS):
        ks_ref[g, :, :LANES] = onehot
        ks_ref[g, :, LANES:] = jnp.where(low, head_tile(ps_ref, g), k_ones).astype(BF16)
        vs_ref[g] = jnp.where(low, head_tile(ps_ref, NSA_KV_GROUPS + g), v_ones).astype(BF16)
        kw_ref[g] = jnp.where(live, jnp.where(low, head_tile(pw_ref, g), k_ones), 0.0).astype(BF16)
        vw_ref[g] = jnp.where(live, jnp.where(low, head_tile(pw_ref, NSA_KV_GROUPS + g), v_ones), 0.0).astype(BF16)


def _nsa_layouts(p, bias_cols):
    b, t, _ = p.shape
    tm = WINDOW
    assert t % tm == 0
    src = lambda width, col_block: pl.BlockSpec((None, tm, width), lambda b, i: (b, jnp.maximum(i - 1, 0), col_block))
    same = lambda heads, width: pl.BlockSpec((None, heads, tm, width), lambda b, i: (b, 0, jnp.maximum(i - 1, 0), 0))
    late = pl.BlockSpec((None, NSA_KV_GROUPS, tm, LANES), lambda b, i: (b, 0, i, 0))
    bias = jnp.pad(bias_cols.astype(F32), ((0, 0), (NSA_HEAD_DIM, 0))).reshape(NSA_HEADS, 1, LANES)
    sds = lambda heads, rows, width: jax.ShapeDtypeStruct((b, heads, rows, width), BF16)
    return pl.pallas_call(
        functools.partial(_nsa_layout_kernel, tm=tm),
        grid=(b, t // tm + 1),
        in_specs=[src(NSA_Q_W, 0), src(NSA_KV_W, (NSA_Q_W + NSA_KV_W) // NSA_KV_W),
                  src(NSA_KV_W, (NSA_Q_W + 2 * NSA_KV_W) // NSA_KV_W),
                  pl.BlockSpec((NSA_HEADS, 1, LANES), lambda b, i: (0, 0, 0))],
        out_specs=[same(NSA_HEADS, LANES), same(NSA_KV_GROUPS, 2 * LANES), same(NSA_KV_GROUPS, LANES), late, late],
        out_shape=[sds(NSA_HEADS, t, LANES), sds(NSA_KV_GROUPS, t, 2 * LANES), sds(NSA_KV_GROUPS, t, LANES),
                   sds(NSA_KV_GROUPS, t + WINDOW, LANES), sds(NSA_KV_GROUPS, t + WINDOW, LANES)],
        compiler_params=pltpu.CompilerParams(
            dimension_semantics=("parallel", "arbitrary"), vmem_limit_bytes=VMEM_LIMIT),
        name="nsa_layouts",
    )(p, p, p, bias)


def _nsa_prompt(x, mod, gains, w_in, cmp_w, w_out, tb, lut, bias_cols):
    b, t, _ = x.shape
    p = _norm_mod_linear(x, gains[0], mod[0], mod[1], w_in)
    kvc, kvs, kvw = (p[..., 1024 + i * NSA_KV_W:1024 + (i + 1) * NSA_KV_W] for i in range(3))
    q128, ks, vs, kw, vw = _nsa_layouts(p, bias_cols)
    n_pages = t // PAGE_SIZE
    table = jnp.arange(b * n_pages, dtype=jnp.int32).reshape(b, n_pages)
    kc, vc = _compress(kvc.reshape(b * n_pages, SEGS_PER_PAGE, SEG_W), table, *cmp_w)
    o = _nsa_prompt_attention(q128, p, (NSA_Q_W + 3 * NSA_KV_W) // LANES, kc, vc, ks, vs, kw, vw, tb, lut)
    x = _matmul_rms_residual(o, w_out, x, mod[2], gains[1])
    shape5 = (b, t, 2, NSA_KV_GROUPS, NSA_HEAD_DIM)
    return x, kvc.reshape(shape5), kvs.reshape(shape5), kvw.reshape(shape5)[:, -min(WINDOW, t):]


def _nsa_sample(x, mod, gains, w_in, cmp_w, w_out, lut, rel_bias, bias_cols, cache_cmp, cache_slc, cache_win,
                page_table, db, nq):
    n_pages = page_table.shape[1]
    past_len = n_pages * PAGE_SIZE
    rows = db * nq
    p = _norm_mod_linear(x, gains[0], mod[0], mod[1], w_in)[0]
    kvc, kvs, kvw = (p[:, 1024 + i * NSA_KV_W:1024 + (i + 1) * NSA_KV_W] for i in range(3))
    qh = p[:, :NSA_Q_W].astype(BF16).reshape(db, nq, NSA_KV_GROUPS, NSA_HPG, NSA_HEAD_DIM).transpose(0, 2, 3, 1, 4)
    q16 = jnp.concatenate([qh, jnp.broadcast_to(bias_cols.reshape(1, NSA_KV_GROUPS, NSA_HPG, 1, NSA_HEAD_DIM), qh.shape)],
                          axis=-1).reshape(db, NSA_KV_GROUPS, NSA_HPG * nq, LANES)
    eye_g = jnp.eye(NSA_KV_GROUPS, dtype=BF16)
    qbd = jnp.einsum('bghtd,gj->bghtjd', qh, eye_g).reshape(db, SAMPLE_ROWS, NSA_KV_GROUPS * NSA_HEAD_DIM)
    row_head = np.repeat(np.arange(NSA_HEADS), nq)
    lut64 = lut[:, 0, :][row_head]
    farcol = jnp.broadcast_to(rel_bias[N_BUCKETS - 1][row_head][:, None], (SAMPLE_ROWS, LANES))
    rows_minor = lambda cache: jnp.transpose(cache, (0, 2, 3, 4, 1))
    kc, vc = _compress(rows_minor(cache_cmp).reshape(-1, 2, 2, LANES, PAGE_SIZE), page_table, *cmp_w)
    o_c, unsel = _sample_cmp(q16, kc, vc, lut64.reshape(NSA_KV_GROUPS, NSA_HPG * nq, LANES), past_len, nq)
    npg = min(16, n_pages)
    nch = n_pages // npg
    bpc = npg * PAGE_SIZE // SLC_BLOCK
    un = unsel[:, :, :nq, :past_len // SLC_BLOCK].reshape(db, NSA_KV_GROUPS, 1, nq, nch, bpc)
    un = jnp.broadcast_to(un, (db, NSA_KV_GROUPS, NSA_HPG, nq, nch, bpc)).transpose(0, 4, 1, 2, 3, 5)
    un = jnp.pad(un.reshape(db, nch, SAMPLE_ROWS, bpc), ((0, 0), (0, 0), (0, 0), (0, LANES - bpc))).astype(BF16)
    m, l, acc = _sample_sel(rows_minor(cache_slc).reshape(-1, 2, NSA_KV_W // 2, PAGE_SIZE), page_table, qbd, un, farcol,
                            lut64, past_len, nq)
    pad_rows = lambda a: jnp.pad(a.reshape(db, nq, NSA_KV_W), ((0, 0), (0, LANES - nq), (0, 0)))
    wcache = cache_win.reshape(db, -1, NSA_KV_W)
    graw = p[:, NSA_Q_W + 3 * NSA_KV_W:NSA_Q_W + 3 * NSA_KV_W + 3 * NSA_HEADS]
    graw = graw.reshape(db, nq, NSA_HEADS, 3).transpose(0, 2, 1, 3).reshape(db, SAMPLE_ROWS, 3)
    graw = jnp.pad(graw, ((0, 0), (0, 0), (0, LANES - 3)))
    o = _sample_final(qbd, m, l, acc, pad_rows(kvs), wcache, pad_rows(kvw), o_c.reshape(db, SAMPLE_ROWS, LANES), graw,
                      farcol, lut64, nq)
    o = o.reshape(db, NSA_HEADS, nq, NSA_HEAD_DIM).transpose(0, 2, 1, 3).reshape(1, rows, NSA_Q_W)
    x = _matmul_rms_residual(o, w_out, x, mod[2], gains[1])
    shape5 = (db, nq, 2, NSA_KV_GROUPS, NSA_HEAD_DIM)
    kv_win = jnp.concatenate([cache_win, kvw.reshape(shape5)], axis=1)[:, -cache_win.shape[1]:]
    return x, kvc.reshape(shape5), kvs.reshape(shape5), kv_win


def _gdn_prompt(x, mod, gains, w_in, conv_w, a_log, dt_bias, norm_w, w_out):
    b, t, _ = x.shape
    p = _norm_mod_linear(x, gains[0], mod[0], mod[1], w_in)
    act = _gdn_conv(p, conv_w)
    bg = _gdn_gates(p, (C_CONV + GDN_V_W) // LANES, a_log, dt_bias)
    s0 = jnp.zeros((b, GDN_V_HEADS, GDN_HEAD_DIM, GDN_HEAD_DIM), F32)
    o, s_fin = _gdn_delta(act, bg, s0)
    x = _gdn_out(o, p, C_CONV // GDN_V_W, norm_w, w_out, x, mod[2], gains[1])
    return x, p[:, t - (CONV_W - 1):, :C_CONV], s_fin


def _gdn_sample(x, mod, gains, w_in, conv_w, a_log, dt_bias, norm_w, w_out, conv_buf, s0, db, nq):
    p = _norm_mod_linear(x, gains[0], mod[0], mod[1], w_in)
    qkv = p[0, :, :C_CONV].reshape(db, nq, C_CONV)
    xp = jnp.concatenate([conv_buf, qkv], axis=1)
    act = _gdn_conv(jnp.pad(xp, ((0, 0), (0, 8 - xp.shape[1]), (0, 0))), conv_w)[:, CONV_W - 1:CONV_W - 1 + nq]
    bg = _gdn_gates(p, (C_CONV + GDN_V_W) // LANES, a_log, dt_bias).reshape(db, nq, LANES)
    pad_t = ((0, 0), (0, GDN_CHUNK - nq), (0, 0))
    o, s_fin = _gdn_delta(jnp.pad(act, pad_t), jnp.pad(bg, pad_t), s0)
    o = o[:, :nq].reshape(1, db * nq, GDN_V_W)
    x = _gdn_out(o, p, C_CONV // GDN_V_W, norm_w, w_out, x, mod[2], gains[1])
    return x, xp[:, -(CONV_W - 1):], s_fin


def kernel(x_prompt, x_sample, c_prompt, c_sample, cache_kv_cmp, cache_kv_slc, cache_kv_win, state_conv, state_ssm,
           page_table, rel_bias, norm_gains, w_ada, b_ada, w_ffn_in, w_ffn_out, nsa_w_in, nsa_cmp_pe, nsa_cmp_w1,
           nsa_cmp_b1, nsa_cmp_w2, nsa_w_out, gdn_w_in, gdn_conv_w, gdn_a_log, gdn_dt_bias, gdn_norm_w, gdn_w_out):
    depth = w_ada.shape[0]
    bp, t, d = x_prompt.shape
    db, nq, _ = x_sample.shape
    assert nq + CONV_W - 1 <= 8 and nq <= GDN_CHUNK

    c_all = jnp.concatenate([c_prompt, c_sample], axis=0)
    rows_pad = -(-c_all.shape[0] // 8) * 8
    ada = _adaln(jnp.pad(c_all, ((0, rows_pad - c_all.shape[0]), (0, 0))), w_ada, b_ada)
    ada = ada.reshape(depth, rows_pad, 6, d)
    tb, lut = _bias_tables(rel_bias)
    far_hi, far_lo = _split2(rel_bias[N_BUCKETS - 1])
    bias_cols = jnp.zeros((NSA_HEADS, NSA_HEAD_DIM), BF16).at[:, 0].set(far_hi).at[:, 1].set(far_lo)

    xp = x_prompt
    xs = x_sample.reshape(1, db * nq, d)
    kvc_p, kvc_s, kvs_p, kvs_s, kvw_p, kvw_s, cv_p, cv_s, ss_p, ss_s = ([] for _ in range(10))
    for i in range(depth):
        mod_p = [ada[i, :bp, k][:, None, :] for k in range(6)]
        mod_s = [jnp.repeat(ada[i, bp:bp + db, k], nq, axis=0)[None] for k in range(6)]
        gains = norm_gains[i]
        l = i // 2
        if i % 2 == 0:
            w_in = jnp.concatenate([nsa_w_in[l][:, :NSA_Q_W] * (NSA_HEAD_DIM ** -0.5), nsa_w_in[l][:, NSA_Q_W:]], axis=1)
            w_in = jnp.pad(w_in, ((0, 0), (0, -w_in.shape[1] % LANES))).astype(BF16)
            cmp_w = _compress_weights(nsa_cmp_pe[l], nsa_cmp_w1[l], nsa_cmp_b1[l], nsa_cmp_w2[l])
            w_out = nsa_w_out[l].astype(BF16)
            xp, a, bq, cq = _nsa_prompt(xp, mod_p, gains, w_in, cmp_w, w_out, tb, lut, bias_cols)
            kvc_p.append(a); kvs_p.append(bq); kvw_p.append(cq)
            xs, a, bq, cq = _nsa_sample(xs, mod_s, gains, w_in, cmp_w, w_out, lut, rel_bias, bias_cols, cache_kv_cmp[l],
                                        cache_kv_slc[l], cache_kv_win[l], page_table, db, nq)
            kvc_s.append(a); kvs_s.append(bq); kvw_s.append(cq)
        else:
            w_in = jnp.pad(gdn_w_in[l], ((0, 0), (0, -gdn_w_in.shape[2] % (5 * MXU_WIDTH)))).astype(BF16)
            gdn_w = (w_in, gdn_conv_w[l], gdn_a_log[l], gdn_dt_bias[l], gdn_norm_w[l], gdn_w_out[l].astype(BF16))
            xp, a, bq = _gdn_prompt(xp, mod_p, gains, *gdn_w)
            cv_p.append(a); ss_p.append(bq)
            xs, a, bq = _gdn_sample(xs, mod_s, gains, *gdn_w, state_conv[l], state_ssm[l], db, nq)
            cv_s.append(a); ss_s.append(bq)
        w_ffn = (w_ffn_in[i].astype(BF16), w_ffn_out[i].astype(BF16))
        xp = _ffn(xp, mod_p, gains, *w_ffn)
        xs = _ffn(xs, mod_s, gains, *w_ffn)
    return (xp, xs.reshape(db, nq, d), jnp.stack(kvc_p), jnp.stack(kvc_s), jnp.stack(kvs_p), jnp.stack(kvs_s),
            jnp.stack(kvw_p), jnp.stack(kvw_s), jnp.stack(cv_p), jnp.stack(cv_s), jnp.stack(ss_p), jnp.stack(ss_s))
```

```python
import functools
import math

import numpy as np
import jax
import jax.numpy as jnp
from jax import lax
from jax.experimental import pallas as pl
from jax.experimental.pallas import tpu as pltpu

F32 = jnp.float32
BF16 = jnp.bfloat16

D_MODEL = 1024
RMS_EPS = 1e-6
D_FF = 2816
NSA_HEADS = 16
NSA_HEAD_DIM = 64
NSA_KV_GROUPS = 4
NSA_HPG = 4
CMP_BLOCK = 32
CMP_STRIDE = 16
CMP_HID = 256
SLC_BLOCK = 64
N_SELECT = 16
WINDOW = 512
Q_BLOCK = 128
PAGE_SIZE = 128
N_BUCKETS = 32
GDN_QK_HEADS = 8
GDN_V_HEADS = 16
GDN_HEAD_DIM = 128
CONV_W = 4
GDN_CHUNK = 64
NSA_Q_W = 1024
NSA_KV_W = 512
C_CONV = 4096
GDN_V_W = 2048

LANES = 128
SEG_W = CMP_STRIDE * NSA_KV_W
SEGS_PER_PAGE = PAGE_SIZE // CMP_STRIDE
NEG = -1e30
VMEM_LIMIT = 48 * 1024 * 1024

_BUCKET_THR = (19, 21, 24, 27, 31, 35, 40, 46, 52, 59, 67, 77, 87, 99, 113)
FAR_DIST = 128


def _nt(a, b):
    return lax.dot_general(a, b, (((1,), (1,)), ((), ())), preferred_element_type=F32)


def _split2(x):
    hi = x.astype(BF16)
    lo = (x - hi.astype(F32)).astype(BF16)
    return hi, lo


def _split3(x):
    hi = x.astype(BF16)
    r = x - hi.astype(F32)
    mid = r.astype(BF16)
    lo = (r - mid.astype(F32)).astype(BF16)
    return hi, mid, lo


MXU_WIDTH = 256
MAX_TN = 2816


def _pick_tn(n):
    units = n // LANES
    cands = [d * LANES for d in range(1, units + 1) if units % d == 0 and d * LANES <= MAX_TN]
    full = [c for c in cands if c % MXU_WIDTH == 0]
    return max(full) if full and 2 * max(full) >= max(cands) else max(cands)


def _adaln_kernel(c_ref, w_ref, b_ref, o_ref):
    c = c_ref[...]
    a = (c * jax.nn.sigmoid(c)).astype(BF16)
    o_ref[...] = jnp.dot(a, w_ref[...].astype(BF16), preferred_element_type=F32) + b_ref[...]


def _adaln(c_all, w_ada, b_ada):
    depth, d, n = w_ada.shape
    rows = c_all.shape[0]
    tn = 768
    return pl.pallas_call(
        _adaln_kernel,
        grid=(depth, n // tn),
        in_specs=[pl.BlockSpec((rows, d), lambda l, j: (0, 0)),
                  pl.BlockSpec((None, d, tn), lambda l, j: (l, 0, j)),
                  pl.BlockSpec((None, 1, tn), lambda l, j: (l, 0, j))],
        out_specs=pl.BlockSpec((None, rows, tn), lambda l, j: (l, 0, j)),
        out_shape=jax.ShapeDtypeStruct((depth, rows, n), F32),
        compiler_params=pltpu.CompilerParams(dimension_semantics=("parallel", "parallel")),
        name="adaln",
    )(c_all, w_ada, b_ada.reshape(depth, 1, n))


def _mod_norm(x, gain, shift, scale):
    ms = jnp.mean(x * x, axis=-1, keepdims=True)
    y = x * lax.rsqrt(ms + RMS_EPS) * gain
    return y * (1.0 + scale) + shift


def _nml_kernel(x_ref, g_ref, sh_ref, sc_ref, w_ref, o_ref):
    h = _mod_norm(x_ref[...], g_ref[...], sh_ref[...], sc_ref[...]).astype(BF16)
    o_ref[...] = jnp.dot(h, w_ref[...], preferred_element_type=F32).astype(o_ref.dtype)


def _nml_swiglu_kernel(x_ref, g_ref, sh_ref, sc_ref, wg_ref, wu_ref, o_ref, h_ref):
    @pl.when(pl.program_id(2) == 0)
    def _():
        h_ref[...] = _mod_norm(x_ref[...], g_ref[...], sh_ref[...], sc_ref[...]).astype(BF16)

    h = h_ref[...]
    gate = jnp.dot(h, wg_ref[...], preferred_element_type=F32)
    up = jnp.dot(h, wu_ref[...], preferred_element_type=F32)
    o_ref[...] = (gate * jax.nn.sigmoid(gate) * up).astype(o_ref.dtype)


def _mod_spec(mod, tm):
    if mod.shape[1] == 1:
        return pl.BlockSpec((None, 1, mod.shape[2]), lambda b, i, *_: (b, 0, 0))
    return pl.BlockSpec((None, tm, mod.shape[2]), lambda b, i, *_: (b, i, 0))


def _row_tile(t):
    return 512 if t % 512 == 0 else t


def _norm_mod_linear(x, gain, shift, scale, w, out_dtype=F32):
    b, t, d = x.shape
    n = w.shape[1]
    tm, tn = _row_tile(t), _pick_tn(n)

    def mod_spec(mod):
        if mod.shape[1] == 1:
            return pl.BlockSpec((None, 1, d), lambda j, b, i: (b, 0, 0))
        return pl.BlockSpec((None, tm, d), lambda j, b, i: (b, i, 0))

    return pl.pallas_call(
        _nml_kernel,
        grid=(n // tn, b, t // tm),
        in_specs=[pl.BlockSpec((None, tm, d), lambda j, b, i: (b, i, 0)),
                  pl.BlockSpec((1, d), lambda j, b, i: (0, 0)),
                  mod_spec(shift), mod_spec(scale),
                  pl.BlockSpec((d, tn), lambda j, b, i: (0, j))],
        out_specs=pl.BlockSpec((None, tm, tn), lambda j, b, i: (b, i, j)),
        out_shape=jax.ShapeDtypeStruct((b, t, n), out_dtype),
        compiler_params=pltpu.CompilerParams(
            dimension_semantics=("parallel", "parallel", "parallel"), vmem_limit_bytes=VMEM_LIMIT),
        name="norm_mod_linear",
    )(x, gain.reshape(1, d), shift, scale, w)


def _norm_mod_swiglu(x, gain, shift, scale, w_in):
    b, t, d = x.shape
    nf = w_in.shape[1] // 2
    tm, tn = _row_tile(t), _pick_tn(nf)
    nj = nf // tn
    return pl.pallas_call(
        _nml_swiglu_kernel,
        grid=(b, t // tm, nj),
        in_specs=[pl.BlockSpec((None, tm, d), lambda b, i, j: (b, i, 0)),
                  pl.BlockSpec((1, d), lambda b, i, j: (0, 0)),
                  _mod_spec(shift, tm), _mod_spec(scale, tm),
                  pl.BlockSpec((d, tn), lambda b, i, j: (0, j)),
                  pl.BlockSpec((d, tn), lambda b, i, j: (0, j + nj))],
        out_specs=pl.BlockSpec((None, tm, tn), lambda b, i, j: (b, i, j)),
        out_shape=jax.ShapeDtypeStruct((b, t, nf), BF16),
        scratch_shapes=[pltpu.VMEM((tm, d), BF16)],
        compiler_params=pltpu.CompilerParams(
            dimension_semantics=("parallel", "parallel", "arbitrary"), vmem_limit_bytes=VMEM_LIMIT),
        name="norm_mod_swiglu",
    )(x, gain.reshape(1, d), shift, scale, w_in, w_in)


def _rms_gated_residual(y, x, gate, gain):
    ms = jnp.mean(y * y, axis=-1, keepdims=True)
    return x + gate * (y * lax.rsqrt(ms + RMS_EPS) * gain)


def _mrr_kernel(a_ref, w_ref, x_ref, gate_ref, gain_ref, o_ref):
    y = jnp.dot(a_ref[...].astype(BF16), w_ref[...], preferred_element_type=F32)
    o_ref[...] = _rms_gated_residual(y, x_ref[...], gate_ref[...], gain_ref[...])


def _matmul_rms_residual(a, w, x, gate, gain):
    b, t, k = a.shape
    d = w.shape[1]
    tm = _row_tile(t)
    return pl.pallas_call(
        _mrr_kernel,
        grid=(b, t // tm),
        in_specs=[pl.BlockSpec((None, tm, k), lambda b, i: (b, i, 0)),
                  pl.BlockSpec((k, d), lambda b, i: (0, 0)),
                  pl.BlockSpec((None, tm, d), lambda b, i: (b, i, 0)),
                  _mod_spec(gate, tm),
                  pl.BlockSpec((1, d), lambda b, i: (0, 0))],
        out_specs=pl.BlockSpec((None, tm, d), lambda b, i: (b, i, 0)),
        out_shape=jax.ShapeDtypeStruct((b, t, d), F32),
        compiler_params=pltpu.CompilerParams(
            dimension_semantics=("parallel", "parallel"), vmem_limit_bytes=VMEM_LIMIT),
        name="matmul_rms_residual",
    )(a, w, x, gate, gain.reshape(1, d))


def _gdn_out_kernel(o_ref, z_ref, nw_ref, w_ref, x_ref, gate_ref, gain_ref, out_ref, a_ref):
    nw = nw_ref[...]
    for h in range(GDN_V_HEADS):
        sl = slice(h * GDN_HEAD_DIM, (h + 1) * GDN_HEAD_DIM)
        o = o_ref[:, sl]
        z = z_ref[:, sl]
        ms = jnp.mean(o * o, axis=-1, keepdims=True)
        a_ref[:, sl] = ((o * lax.rsqrt(ms + RMS_EPS) * nw) * (z * jax.nn.sigmoid(z))).astype(BF16)
    y = jnp.dot(a_ref[...], w_ref[...], preferred_element_type=F32)
    out_ref[...] = _rms_gated_residual(y, x_ref[...], gate_ref[...], gain_ref[...])


def _gdn_out(o, p, z_col_block, norm_w, w, x, gate, gain):
    b, t, k = o.shape
    d = w.shape[1]
    tm = _row_tile(t)
    return pl.pallas_call(
        _gdn_out_kernel,
        grid=(b, t // tm),
        in_specs=[pl.BlockSpec((None, tm, k), lambda b, i: (b, i, 0)),
                  pl.BlockSpec((None, tm, k), lambda b, i: (b, i, z_col_block)),
                  pl.BlockSpec((1, GDN_HEAD_DIM), lambda b, i: (0, 0)),
                  pl.BlockSpec((k, d), lambda b, i: (0, 0)),
                  pl.BlockSpec((None, tm, d), lambda b, i: (b, i, 0)),
                  _mod_spec(gate, tm),
                  pl.BlockSpec((1, d), lambda b, i: (0, 0))],
        out_specs=pl.BlockSpec((None, tm, d), lambda b, i: (b, i, 0)),
        out_shape=jax.ShapeDtypeStruct((b, t, d), F32),
        scratch_shapes=[pltpu.VMEM((tm, k), BF16)],
        compiler_params=pltpu.CompilerParams(
            dimension_semantics=("parallel", "parallel"), vmem_limit_bytes=VMEM_LIMIT),
        name="gdn_out",
    )(o, p, norm_w.reshape(1, GDN_HEAD_DIM), w, x, gate, gain.reshape(1, d))


def _bucket_of(n):
    big = jnp.full(n.shape, 16, jnp.int32)
    for thr in _BUCKET_THR:
        big = big + (n >= thr).astype(jnp.int32)
    return jnp.where(n < 16, n, big)


def _bias_tab_kernel(tbl_ref, tb_ref, lut_ref):
    h = pl.program_id(0)
    far = tbl_ref[N_BUCKETS - 1, h]

    def lookup(dist):
        bkt = _bucket_of(jnp.maximum(dist, 0))
        out = jnp.zeros(dist.shape, F32)
        for bb in range(N_BUCKETS):
            out = jnp.where(bkt == bb, tbl_ref[bb, h], out)
        return out - far

    qi = lax.broadcasted_iota(jnp.int32, (Q_BLOCK, 2 * Q_BLOCK), 0)
    kj = lax.broadcasted_iota(jnp.int32, (Q_BLOCK, 2 * Q_BLOCK), 1)
    dist = Q_BLOCK + qi - kj
    tb_ref[...] = jnp.where(dist >= 0, lookup(dist), NEG)
    lut_ref[...] = lookup(lax.broadcasted_iota(jnp.int32, (8, LANES), 1))


def _bias_tables(rel_bias):
    return pl.pallas_call(
        _bias_tab_kernel,
        grid=(NSA_HEADS,),
        in_specs=[pl.BlockSpec(memory_space=pltpu.SMEM)],
        out_specs=[pl.BlockSpec((None, Q_BLOCK, 2 * Q_BLOCK), lambda h: (h, 0, 0)),
                   pl.BlockSpec((None, 8, LANES), lambda h: (h, 0, 0))],
        out_shape=[jax.ShapeDtypeStruct((NSA_HEADS, Q_BLOCK, 2 * Q_BLOCK), F32),
                   jax.ShapeDtypeStruct((NSA_HEADS, 8, LANES), F32)],
        compiler_params=pltpu.CompilerParams(dimension_semantics=("parallel",)),
        name="bias_tables",
    )(rel_bias)


def _lut_gather(lut_rows, dist):
    idx = jnp.clip(dist, 0, LANES - 1)
    val = jnp.take_along_axis(lut_rows, idx, axis=1)
    return jnp.where((dist >= 0) & (dist < FAR_DIST), val, 0.0)


def _pe_term_kernel(pe_ref, wbd_ref, b1_ref, o_ref):
    y = jnp.dot(pe_ref[...], wbd_ref[...], preferred_element_type=F32)
    o_ref[...] = y[:, 0:CMP_HID] + y[:, 3 * CMP_HID:4 * CMP_HID] + b1_ref[...]


def _pe_term(pe_x, wbd, b1):
    return pl.pallas_call(
        _pe_term_kernel,
        grid=(2,),
        in_specs=[pl.BlockSpec((None, 8, 2048), lambda k: (k, 0, 0)),
                  pl.BlockSpec((None, 2048, 1024), lambda k: (k, 0, 0)),
                  pl.BlockSpec((None, 1, CMP_HID), lambda k: (k, 0, 0))],
        out_specs=pl.BlockSpec((None, 8, CMP_HID), lambda k: (k, 0, 0)),
        out_shape=jax.ShapeDtypeStruct((2, 8, CMP_HID), F32),
        compiler_params=pltpu.CompilerParams(dimension_semantics=("parallel",), vmem_limit_bytes=VMEM_LIMIT),
        name="cmp_pe_term",
    )(pe_x, wbd, b1.reshape(2, 1, CMP_HID))


def _compress_kernel(tab_ref, *refs, npg, rows_minor):
    del tab_ref
    pages = refs[:npg]
    wbd_ref, w2_ref, pe_ref, cc_ref, kc_ref, vc_ref, xs_ref, carry_ref = refs[npg:npg + 8]
    ts = SEGS_PER_PAGE * npg

    @pl.when(pl.program_id(1) == 0)
    def _():
        carry_ref[...] = jnp.zeros(carry_ref.shape, F32)

    row0 = lax.broadcasted_iota(jnp.int32, (ts, CMP_HID), 0) == 0
    low_half = lax.broadcasted_iota(jnp.int32, (ts, LANES), 1) < NSA_HEAD_DIM
    half = CMP_STRIDE * NSA_HEAD_DIM
    passes = [(k, gp) for k in range(2) for gp in range(2)]

    def transpose_pages(n):
        k, gp = passes[n]
        rt_ref = refs[npg + 8].at[n % 2]
        for j, pg in enumerate(pages):
            rt_ref[j * PAGE_SIZE:(j + 1) * PAGE_SIZE, :] = pg[k, gp].astype(BF16).T.astype(F32)

    def build_features(n):
        k, gp = passes[n]
        xb_ref = xs_ref.at[n % 2]
        if rows_minor:
            rt_ref = refs[npg + 8].at[n % 2]
            for j in range(CMP_STRIDE // 2):
                ra = rt_ref[pl.ds(2 * j, ts, stride=CMP_STRIDE), :]
                rb = rt_ref[pl.ds(2 * j + 1, ts, stride=CMP_STRIDE), :]
                xb_ref[:, j * LANES:(j + 1) * LANES] = jnp.where(
                    low_half, ra, pltpu.roll(rb, NSA_HEAD_DIM, axis=1)).astype(BF16)
                xb_ref[:, half + j * LANES:half + (j + 1) * LANES] = jnp.where(
                    low_half, pltpu.roll(ra, NSA_HEAD_DIM, axis=1), rb).astype(BF16)
        else:
            off = k * 256 + gp * LANES
            for s in range(CMP_STRIDE):
                lo = s * NSA_KV_W + off
                piece = jnp.concatenate([pg[:, lo:lo + LANES] for pg in pages], axis=0)
                xb_ref[:, s * LANES:(s + 1) * LANES] = piece.astype(BF16)

    def first_layer(n):
        k, _ = passes[n]
        xb_ref = xs_ref.at[n % 2]
        if rows_minor:
            return [jnp.dot(xb_ref[:, g2 * half:(g2 + 1) * half], wbd_ref[k], preferred_element_type=F32)
                    for g2 in range(2)]
        y = jnp.dot(xb_ref[...], wbd_ref[k], preferred_element_type=F32)
        return [y[:, :2 * CMP_HID], y[:, 2 * CMP_HID:]]

    def second_layer(n, ys):
        k, gp = passes[n]
        out_ref = kc_ref if k == 0 else vc_ref
        hs = []
        for g2 in range(2):
            pa = ys[g2][:, :CMP_HID]
            pb = ys[g2][:, CMP_HID:]
            ci = (k * 2 + gp) * 2 + g2
            prev = carry_ref[ci]
            pa_prev = jnp.where(row0, prev[7:8, :], pltpu.roll(pa, 1, axis=0))
            carry_ref[ci] = pa[ts - 8:ts, :]
            hs.append(jax.nn.gelu(pa_prev + pb + pe_ref[k, 0:1, :]))
        hid = jnp.concatenate(hs, axis=-1).astype(BF16)
        o = jnp.dot(hid, w2_ref[k], preferred_element_type=F32) + cc_ref[k]
        out_ref[2 * gp] = o[:, :LANES].astype(BF16)
        out_ref[2 * gp + 1] = o[:, LANES:].astype(BF16)

    if rows_minor:
        transpose_pages(0)
    build_features(0)
    if rows_minor:
        transpose_pages(1)
    for n in range(len(passes)):
        ys = first_layer(n)
        if n + 1 < len(passes):
            build_features(n + 1)
        if rows_minor and n + 2 < len(passes):
            transpose_pages(n + 2)
        second_layer(n, ys)


def _compress(pages, table, wbd, w1cat, w2bd, pe_term, ccols):
    bc, n_pages = table.shape
    npg = min(32, n_pages)
    ts = SEGS_PER_PAGE * npg
    nseg = n_pages * SEGS_PER_PAGE
    rows_minor = pages.ndim == 5
    w_first = w1cat if rows_minor else wbd
    page_block = (None,) + pages.shape[1:]

    def page_spec(j):
        return pl.BlockSpec(page_block, lambda b, i, tab: (tab[b, i * npg + j],) + (0,) * (pages.ndim - 1))

    const = lambda *shape: pl.BlockSpec(shape, lambda b, i, tab: (0,) * len(shape), pipeline_mode=pl.Buffered(1))
    out_spec = pl.BlockSpec((None, NSA_KV_GROUPS, ts, LANES), lambda b, i, tab: (b, 0, i, 0))
    grid_spec = pltpu.PrefetchScalarGridSpec(
        num_scalar_prefetch=1,
        grid=(bc, n_pages // npg),
        in_specs=[page_spec(j) for j in range(npg)] + [
            const(*w_first.shape), const(2, 512, 256), const(2, 8, CMP_HID), const(2, 1, 256)],
        out_specs=[out_spec, out_spec],
        scratch_shapes=[pltpu.VMEM((2, ts, 2048), BF16), pltpu.VMEM((8, 8, CMP_HID), F32)] + (
            [pltpu.VMEM((2, npg * PAGE_SIZE, LANES), F32)] if rows_minor else []),
    )
    out_sds = jax.ShapeDtypeStruct((bc, NSA_KV_GROUPS, nseg, LANES), BF16)
    return pl.pallas_call(
        functools.partial(_compress_kernel, npg=npg, rows_minor=rows_minor),
        grid_spec=grid_spec,
        out_shape=[out_sds, out_sds],
        compiler_params=pltpu.CompilerParams(
            dimension_semantics=("parallel", "arbitrary"), vmem_limit_bytes=VMEM_LIMIT),
        name="kv_compress",
    )(table, *([pages] * npg), w_first, w2bd, pe_term, ccols)


def _compress_weights(pe, w1, b1, w2):
    eye2 = jnp.eye(2, dtype=F32)
    w = w1.reshape(2, 2, CMP_STRIDE, NSA_HEAD_DIM, CMP_HID)
    wbd = jnp.einsum('kasdh,gj->ksgdjah', w, eye2).reshape(2, 2048, 1024).astype(BF16)
    w1cat = w.transpose(0, 2, 3, 1, 4).reshape(2, CMP_STRIDE * NSA_HEAD_DIM, 2 * CMP_HID).astype(BF16)
    w2p = jnp.pad(w2, ((0, 0), (0, 0), (0, LANES - NSA_HEAD_DIM)))
    w2bd = jnp.einsum('khd,gj->kghjd', w2p, eye2).reshape(2, 512, 256).astype(BF16)
    pe_x = pe.reshape(2, 2, CMP_STRIDE, NSA_HEAD_DIM).transpose(0, 2, 1, 3).reshape(2, 1, 2048)
    pe_x = jnp.broadcast_to(pe_x, (2, 8, 2048)).astype(BF16)
    pe_term = _pe_term(pe_x, wbd, b1)
    cc = np.zeros((2, 1, 256), np.float32)
    for g2 in range(2):
        cc[0, 0, g2 * LANES + 64] = 1.0
        cc[0, 0, g2 * LANES + 65] = 1.0
        cc[1, 0, g2 * LANES + 64] = 1.0
    return wbd, w1cat, w2bd, pe_term, jnp.asarray(cc)


def _topk_rows_mask(score, k, fillers=(), preselected=None):
    blk = lax.broadcasted_iota(jnp.int32, score.shape, 0).astype(F32)
    sel = jnp.zeros(score.shape, F32) if preselected is None else preselected.astype(F32)
    fillers = list(fillers)
    for it in range(k):
        mx = jnp.max(score, axis=0, keepdims=True)
        idx = jnp.min(jnp.where(score == mx, blk, 1e9), axis=0, keepdims=True)
        hit = blk == idx
        sel = jnp.where(hit, 1.0, sel)
        score = jnp.where(hit, -jnp.inf, score)
        for f in fillers[it * len(fillers) // k:(it + 1) * len(fillers) // k]:
            f()
    return sel


def _nsa_prompt_kernel(q_ref, gt_ref, kc_ref, vc_ref, ks_ref, vs_ref, kw_ref, vw_ref, tb_ref, lut_ref, ovt_ref,
                       o_ref, qaug_ref, sc_ref, pcb_ref, pcs_ref, sw_ref, pw_ref, pfar_ref, *, nseg, n_sb):
    g = pl.program_id(1)
    qb = pl.program_id(2)
    s0 = qb * Q_BLOCK
    rows = NSA_HPG * Q_BLOCK
    q = q_ref[...].reshape(rows, LANES)
    tb = tb_ref[...]

    sc_ref[...] = _nt(q, kc_ref[...])
    qi128 = lax.broadcasted_iota(jnp.int32, (Q_BLOCK, LANES), 0)
    li128 = lax.broadcasted_iota(jnp.int32, (Q_BLOCK, LANES), 1)

    def add_near_bias(chunk):
        l0 = pl.multiple_of(chunk * LANES, LANES)
        dist = s0 + qi128 - CMP_STRIDE * (chunk * LANES + li128) - (CMP_STRIDE - 1)
        for hh in range(NSA_HPG):
            lut = jnp.broadcast_to(lut_ref[hh, 0:1, :], (Q_BLOCK, LANES))
            rs = slice(hh * Q_BLOCK, (hh + 1) * Q_BLOCK)
            sc_ref[rs, pl.ds(l0, LANES)] = sc_ref[rs, pl.ds(l0, LANES)] + _lut_gather(lut, dist)

    chunk_lo = jnp.maximum(8 * qb - 8, 0) // LANES
    chunk_hi = (8 * qb + 7) // LANES
    add_near_bias(chunk_lo)

    @pl.when(chunk_hi != chunk_lo)
    def _():
        add_near_bias(chunk_hi)

    rb_n = 32
    qi_b = lax.broadcasted_iota(jnp.int32, (rb_n, nseg), 0)
    mi_b = lax.broadcasted_iota(jnp.int32, (rb_n, nseg), 1)
    for rb in range(Q_BLOCK // rb_n):
        dist_b = s0 + rb * rb_n + qi_b - CMP_STRIDE * mi_b - (CMP_STRIDE - 1)
        valid_b = (dist_b >= 0) & (mi_b >= 1)
        pcs_b = jnp.zeros((rb_n, nseg), F32)
        for hh in range(NSA_HPG):
            rs = slice(hh * Q_BLOCK + rb * rb_n, hh * Q_BLOCK + (rb + 1) * rb_n)
            s_b = jnp.where(valid_b, sc_ref[rs, :], NEG)
            e = jnp.where(valid_b, jnp.exp(s_b - jnp.max(s_b, axis=-1, keepdims=True)), 0.0)
            ssum = jnp.sum(e, axis=-1, keepdims=True)
            pc = e * (1.0 / jnp.where(ssum > 0, ssum, 1.0))
            pcb_ref[rs, :] = pc.astype(BF16)
            pcs_b = pcs_b + pc
        pcs_ref[rb * rb_n:(rb + 1) * rb_n, :] = pcs_b
    o_c = jnp.dot(pcb_ref[...], vc_ref[...], preferred_element_type=F32)

    pcs = pcs_ref[...]
    hi, lo = _split2(pcs)
    ovt = ovt_ref[...]
    imp_t = _nt(ovt, hi) + _nt(ovt, lo)
    jb = lax.broadcasted_iota(jnp.int32, (LANES, Q_BLOCK), 0)
    q_blk = (s0 + lax.broadcasted_iota(jnp.int32, (LANES, Q_BLOCK), 1)) // SLC_BLOCK
    forced = (jb == 0) | (jb == q_blk) | (jb == q_blk - 1)
    score = jnp.where(forced, -jnp.inf, jnp.where(jb <= q_blk, imp_t, -1e4))
    score = jnp.where(jb < n_sb, score, -3e38)
    n_forced = 3

    w0 = pl.multiple_of(s0, Q_BLOCK)
    n_w = WINDOW + Q_BLOCK
    sw_ref[...] = _nt(q, kw_ref[pl.ds(w0, n_w), :])
    qi_w = lax.broadcasted_iota(jnp.int32, (rb_n, n_w), 0)
    kk_w = lax.broadcasted_iota(jnp.int32, (rb_n, n_w), 1)

    def window_block(hh, rb):
        ok_w = (kk_w > qi_w + rb * rb_n) & (s0 + kk_w >= WINDOW)
        rs = slice(hh * Q_BLOCK + rb * rb_n, hh * Q_BLOCK + (rb + 1) * rb_n)
        s_b = jnp.where(ok_w, sw_ref[rs, :], NEG)
        s_b = jnp.concatenate([s_b[:, :WINDOW - Q_BLOCK],
                               s_b[:, WINDOW - Q_BLOCK:] + tb_ref[hh, rb * rb_n:(rb + 1) * rb_n, :]], axis=-1)
        pw_ref[rs, :] = jnp.exp(s_b - jnp.max(s_b, axis=-1, keepdims=True)).astype(BF16)

    gt = jax.nn.sigmoid(gt_ref[...])
    glane = lax.broadcasted_iota(jnp.int32, gt.shape, 1)
    gates = [[None] * 3 for _ in range(NSA_HPG)]

    def gate_column(hh, br):
        col = 3 * (NSA_HPG * g + hh) + br
        gates[hh][br] = jnp.sum(jnp.where(glane == col, gt, 0.0), axis=-1, keepdims=True)

    fillers = []
    for hh in range(NSA_HPG):
        fillers += [functools.partial(window_block, hh, rb) for rb in range(Q_BLOCK // rb_n)]
        fillers += [functools.partial(gate_column, hh, br) for br in range(3)]
    sel_t = _topk_rows_mask(score, min(N_SELECT, n_sb) - n_forced, fillers, preselected=forced)
    acc_w = jnp.dot(pw_ref[...], vw_ref[pl.ds(w0, n_w), :], preferred_element_type=F32)
    o_w = acc_w * (1.0 / acc_w[:, NSA_HEAD_DIM:NSA_HEAD_DIM + 1])
    unsel = 1.0 - sel_t.T
    blk_lane = lax.broadcasted_iota(jnp.int32, (Q_BLOCK, LANES), 1)
    near_blk0 = 2 * qb - 2
    unsel_far = jnp.where(blk_lane >= near_blk0, 1.0, unsel)

    qaug_ref[:, LANES:] = q
    for hh in range(NSA_HPG):
        qaug_ref[hh * Q_BLOCK:(hh + 1) * Q_BLOCK, :LANES] = unsel.astype(BF16)
    a0 = pl.multiple_of(jnp.maximum(s0 - Q_BLOCK, 0), Q_BLOCK)
    b0 = pl.multiple_of(s0, Q_BLOCK)
    kn = jnp.concatenate([ks_ref[pl.ds(a0, Q_BLOCK), :], ks_ref[pl.ds(b0, Q_BLOCK), :]], axis=0)
    vn = jnp.concatenate([vs_ref[pl.ds(a0, Q_BLOCK), :], vs_ref[pl.ds(b0, Q_BLOCK), :]], axis=0)
    kcol = lax.broadcasted_iota(jnp.int32, (Q_BLOCK, 2 * Q_BLOCK), 1)
    no_prev = jnp.where((kcol < Q_BLOCK) & (qb == 0), NEG, 0.0)
    s_n = _nt(qaug_ref[...], kn).reshape(NSA_HPG, Q_BLOCK, 2 * Q_BLOCK) + (tb + no_prev[None])
    s_n = s_n.reshape(rows, 2 * Q_BLOCK)
    m_run = jnp.max(s_n, axis=-1, keepdims=True)
    acc = jnp.dot(jnp.exp(s_n - m_run).astype(BF16), vn, preferred_element_type=F32)

    for hh in range(NSA_HPG):
        qaug_ref[hh * Q_BLOCK:(hh + 1) * Q_BLOCK, :LANES] = unsel_far.astype(BF16)
    kc_far = 4 * Q_BLOCK
    n_far = (jnp.maximum(qb - 1, 0) + 3) // 4

    pfar_ref[1] = jnp.zeros((rows, kc_far), BF16)

    def far_logits(c):
        return _nt(qaug_ref[...], ks_ref[pl.ds(pl.multiple_of(c * kc_far, kc_far), kc_far), :])

    def far_pv(slot, c):
        k0 = pl.multiple_of(jnp.maximum(c, 0) * kc_far, kc_far)
        return jnp.dot(pfar_ref[slot], vs_ref[pl.ds(k0, kc_far), :], preferred_element_type=F32)

    def far_trip(t, carry):
        m_old, acc_old, alpha_prev = carry
        s_a = far_logits(2 * t)
        acc_1 = alpha_prev * acc_old + far_pv(1, 2 * t - 1)
        s_b = far_logits(2 * t + 1)
        m_a = jnp.maximum(m_old, jnp.max(s_a, axis=-1, keepdims=True))
        pfar_ref[0] = jnp.exp((s_a - m_a).astype(BF16))
        acc_2 = jnp.exp(m_old - m_a) * acc_1 + far_pv(0, 2 * t)
        m_b = jnp.maximum(m_a, jnp.max(s_b, axis=-1, keepdims=True))
        pfar_ref[1] = jnp.exp((s_b - m_b).astype(BF16))
        return m_b, acc_2, jnp.exp(m_a - m_b)

    n_trips = (n_far + 1) // 2
    m_run, acc, alpha_last = lax.fori_loop(0, n_trips, far_trip, (m_run, acc, jnp.ones((rows, 1), F32)))
    acc = alpha_last * acc + far_pv(1, 2 * n_trips - 1)
    o_s = acc * (1.0 / acc[:, NSA_HEAD_DIM:NSA_HEAD_DIM + 1])

    low = lax.broadcasted_iota(jnp.int32, (Q_BLOCK, LANES), 1) < NSA_HEAD_DIM
    o_heads = []
    for hh in range(NSA_HPG):
        rs = slice(hh * Q_BLOCK, (hh + 1) * Q_BLOCK)
        o_heads.append(gates[hh][0] * o_c[rs] + gates[hh][1] * o_s[rs] + gates[hh][2] * o_w[rs])
    for pair in range(NSA_HPG // 2):
        o_ref[:, pair * LANES:(pair + 1) * LANES] = jnp.where(
            low, o_heads[2 * pair], pltpu.roll(o_heads[2 * pair + 1], NSA_HEAD_DIM, axis=1)).astype(o_ref.dtype)


def _overlap_t(n_blk_pad, nseg):
    m = np.arange(nseg)[None, :]
    j = np.arange(n_blk_pad)[:, None]
    c_start = CMP_STRIDE * m - CMP_STRIDE
    c_end = CMP_STRIDE * m + CMP_STRIDE - 1
    ov = (c_start < j * SLC_BLOCK + SLC_BLOCK) & (c_end >= j * SLC_BLOCK) & (m >= 1)
    return ov.astype(np.float32)


def _nsa_prompt_attention(q128, p, gate_col_block, kc, vc, ks, vs, kw, vw, tb, lut):
    b, _, t, _ = q128.shape
    nseg = kc.shape[2]
    n_sb = t // SLC_BLOCK
    assert n_sb <= LANES and t % (4 * Q_BLOCK) == 0 and nseg % LANES == 0
    ovt = jnp.asarray(_overlap_t(LANES, nseg), dtype=BF16)
    per_bg = lambda rows, cols: pl.BlockSpec((None, None, rows, cols), lambda b, g, i: (b, g, 0, 0))
    return pl.pallas_call(
        functools.partial(_nsa_prompt_kernel, nseg=nseg, n_sb=n_sb),
        grid=(b, NSA_KV_GROUPS, t // Q_BLOCK),
        in_specs=[pl.BlockSpec((None, NSA_HPG, Q_BLOCK, LANES), lambda b, g, i: (b, g, i, 0)),
                  pl.BlockSpec((None, Q_BLOCK, LANES), lambda b, g, i: (b, i, gate_col_block)),
                  per_bg(nseg, LANES), per_bg(nseg, LANES),
                  per_bg(t, 2 * LANES), per_bg(t, LANES),
                  per_bg(t + WINDOW, LANES), per_bg(t + WINDOW, LANES),
                  pl.BlockSpec((NSA_HPG, Q_BLOCK, 2 * Q_BLOCK), lambda b, g, i: (g, 0, 0)),
                  pl.BlockSpec((NSA_HPG, 8, LANES), lambda b, g, i: (g, 0, 0)),
                  pl.BlockSpec((LANES, nseg), lambda b, g, i: (0, 0))],
        out_specs=pl.BlockSpec((None, Q_BLOCK, NSA_HPG * NSA_HEAD_DIM), lambda b, g, i: (b, i, g)),
        out_shape=jax.ShapeDtypeStruct((b, t, NSA_Q_W), BF16),
        scratch_shapes=[pltpu.VMEM((NSA_HPG * Q_BLOCK, 2 * LANES), BF16),
                        pltpu.VMEM((NSA_HPG * Q_BLOCK, nseg), F32), pltpu.VMEM((NSA_HPG * Q_BLOCK, nseg), BF16),
                        pltpu.VMEM((Q_BLOCK, nseg), F32),
                        pltpu.VMEM((NSA_HPG * Q_BLOCK, WINDOW + Q_BLOCK), F32),
                        pltpu.VMEM((NSA_HPG * Q_BLOCK, WINDOW + Q_BLOCK), BF16),
                        pltpu.VMEM((2, NSA_HPG * Q_BLOCK, 4 * Q_BLOCK), BF16)],
        compiler_params=pltpu.CompilerParams(
            dimension_semantics=("parallel", "parallel", "arbitrary"), vmem_limit_bytes=VMEM_LIMIT),
        name="nsa_prompt_attention",
    )(q128, p, kc, vc, ks, vs, kw, vw, tb, lut, ovt)


def _gdn_conv_kernel(x_ref, w_ref, o_ref, carry_ref, *, tm, tc):
    j = pl.program_id(1)

    @pl.when(pl.program_id(2) == 0)
    def _():
        carry_ref[...] = jnp.zeros(carry_ref.shape, F32)

    x = x_ref[...]
    w = w_ref[...]
    prev = carry_ref[...]
    row8 = lax.broadcasted_iota(jnp.int32, (8, tc), 0)
    conv = x * w[CONV_W - 1:CONV_W, :]
    for sft in range(1, CONV_W):
        xs = pltpu.roll(x, sft, axis=0)
        top = jnp.where(row8 < sft, pltpu.roll(prev, sft, axis=0), xs[0:8])
        xs = top if tm == 8 else jnp.concatenate([top, xs[8:]], axis=0)
        conv = conv + xs * w[CONV_W - 1 - sft:CONV_W - sft, :]
    carry_ref[...] = x[tm - 8:tm, :]
    act = conv * jax.nn.sigmoid(conv)
    for hd in range(tc // GDN_HEAD_DIM):
        sl = slice(hd * GDN_HEAD_DIM, (hd + 1) * GDN_HEAD_DIM)
        a = act[:, sl]
        col0 = j * tc + hd * GDN_HEAD_DIM
        nrm = a * lax.rsqrt(jnp.sum(a * a, axis=-1, keepdims=True) + 1e-6)
        nrm = nrm * jnp.where(col0 < 1024, GDN_HEAD_DIM ** -0.5, 1.0)
        o_ref[:, sl] = jnp.where(col0 < 2048, nrm, a)


def _gdn_conv(p, conv_w):
    b, t, _ = p.shape
    tm = _row_tile(t)
    tc = 1024
    return pl.pallas_call(
        functools.partial(_gdn_conv_kernel, tm=tm, tc=tc),
        grid=(b, C_CONV // tc, t // tm),
        in_specs=[pl.BlockSpec((None, tm, tc), lambda b, j, i: (b, i, j)),
                  pl.BlockSpec((CONV_W, tc), lambda b, j, i: (0, j))],
        out_specs=pl.BlockSpec((None, tm, tc), lambda b, j, i: (b, i, j)),
        out_shape=jax.ShapeDtypeStruct((b, t, C_CONV), F32),
        scratch_shapes=[pltpu.VMEM((8, tc), F32)],
        compiler_params=pltpu.CompilerParams(
            dimension_semantics=("parallel", "parallel", "arbitrary"), vmem_limit_bytes=VMEM_LIMIT),
        name="gdn_conv",
    )(p, conv_w)


def _gdn_gate_kernel(ba_ref, alog_ref, dtb_ref, o_ref):
    x = ba_ref[...]
    y = x + dtb_ref[...]
    softplus = jnp.maximum(y, 0.0) + jnp.log1p(jnp.exp(-jnp.abs(y)))
    g = -jnp.exp(alog_ref[...]) * softplus
    lane = lax.broadcasted_iota(jnp.int32, x.shape, 1)
    o_ref[...] = jnp.where(lane < GDN_V_HEADS, jax.nn.sigmoid(x), g)


def _gdn_gates(p, ba_col_block, a_log, dt_bias):
    b, t, _ = p.shape
    tm = _row_tile(t)
    pad = lambda v: jnp.pad(v.reshape(1, GDN_V_HEADS), ((0, 0), (GDN_V_HEADS, LANES - 2 * GDN_V_HEADS)))
    return pl.pallas_call(
        _gdn_gate_kernel,
        grid=(b, t // tm),
        in_specs=[pl.BlockSpec((None, tm, LANES), lambda b, i: (b, i, ba_col_block)),
                  pl.BlockSpec((1, LANES), lambda b, i: (0, 0)),
                  pl.BlockSpec((1, LANES), lambda b, i: (0, 0))],
        out_specs=pl.BlockSpec((None, tm, LANES), lambda b, i: (b, i, 0)),
        out_shape=jax.ShapeDtypeStruct((b, t, LANES), F32),
        compiler_params=pltpu.CompilerParams(dimension_semantics=("parallel", "parallel")),
        name="gdn_gates",
    )(p, pad(a_log), pad(dt_bias))


def _bdot(a, b):
    return jnp.dot(a.astype(BF16), b.astype(BF16), preferred_element_type=F32)


GDN_PACK = 4
_PACK_ORDER = (0, 2, 1, 3)
_PACK_HEADS = tuple(GDN_PACK * p + o for p in range(GDN_V_HEADS // GDN_PACK) for o in _PACK_ORDER)


def _iota2(shape, axis):
    return lax.broadcasted_iota(jnp.int32, shape, axis)


def _packed_mm(a_cat, b_cat, bd_mask):
    b_bd = jnp.where(bd_mask, jnp.concatenate([b_cat] * GDN_PACK, axis=0), 0.0)
    return _bdot(a_cat, b_bd)


def _unit_lower_inverse_packed(ls, row, col, bd_mask):
    eye = (row == col).astype(F32)
    same16 = (row // 16) == (col // 16)
    same32 = (row // 32) == (col // 32)
    ms = [jnp.where(same16, -l, 0.0) for l in ls]
    ps = [eye + m for m in ms]
    for _ in range(3):
        ms = [_packed_mm(m, m, bd_mask) for m in ms]
        ps = [p + _packed_mm(p, m, bd_mask) for p, m in zip(ps, ms)]
    for level in (same32 & jnp.logical_not(same16), jnp.logical_not(same32)):
        ts = [_packed_mm(jnp.where(level, l, 0.0), p, bd_mask) for l, p in zip(ls, ps)]
        ps = [p - _packed_mm(p, t, bd_mask) for p, t in zip(ps, ts)]
    return ps


def _gdn_delta_kernel(act_ref, bg_ref, gt_ref, s0_ref, ltri_ref, lbd_ref, o_ref, s_ref, sbd_ref, *, bb):
    c, hd = GDN_CHUNK, GDN_HEAD_DIM
    n_packs = GDN_V_HEADS // GDN_PACK
    n_units = bb * n_packs
    n_pairs = GDN_V_HEADS // 2
    zero_hd = jnp.zeros((hd, hd), F32)

    @pl.when(pl.program_id(1) == 0)
    def _():
        for bi in range(bb):
            for pr in range(n_pairs):
                h0, h1 = _PACK_HEADS[2 * pr], _PACK_HEADS[2 * pr + 1]
                sbd_ref[bi * n_pairs + pr] = jnp.concatenate(
                    [jnp.concatenate([s0_ref[bi, h0], zero_hd], axis=-1),
                     jnp.concatenate([zero_hd, s0_ref[bi, h1]], axis=-1)], axis=0)

    bgs = [bg_ref[bi] for bi in range(bb)]
    cums = [sum(jnp.dot(ltri_ref[...], part, preferred_element_type=F32) for part in _split3(bg)) for bg in bgs]
    gcr_alls = [sum(_nt(part, lbd_ref[...]) for part in _split3(gt_ref[bi])) for bi in range(bb)]
    row = _iota2((c, GDN_PACK * c), 0)
    lane = _iota2((c, GDN_PACK * c), 1)
    col, slot = lane % c, lane // c
    incl, strict = row >= col, row > col
    bd_mask = (_iota2((4 * c, 4 * c), 0) // c) == (_iota2((4 * c, 4 * c), 1) // c)
    pair_mask = (_iota2((2 * hd, 2 * hd), 0) // hd) == (_iota2((2 * hd, 2 * hd), 1) // hd)
    k_mask = (_iota2((2 * hd, hd), 0) // hd) == (_iota2((2 * hd, hd), 1) // c)
    row_pair = _iota2((2 * hd, 1), 0)

    def slot_cat(cols):
        out = jnp.broadcast_to(cols[3], (c, GDN_PACK * c))
        for x in (2, 1, 0):
            out = jnp.where(slot == x, cols[x], out)
        return out

    def side_by_side(a, b):
        return jnp.concatenate([a, b], axis=-1)

    qs, ks, betas, gcs, lmats, a_ins = [], [], [], [], [], []
    for u in range(n_units):
        bi, p = divmod(u, n_packs)
        bg, cum, gcr_all = bgs[bi], cums[bi], gcr_alls[bi]
        heads = _PACK_HEADS[GDN_PACK * p:GDN_PACK * (p + 1)]
        qa, qb = (act_ref[bi, :, (2 * p + i) * hd:(2 * p + i + 1) * hd] for i in (0, 1))
        ka, kb = (act_ref[bi, :, 1024 + (2 * p + i) * hd:1024 + (2 * p + i + 1) * hd] for i in (0, 1))
        kt = jnp.concatenate([ka, kb], axis=0).T
        k_bd = jnp.where(k_mask, jnp.concatenate([kt, kt], axis=0), 0.0)
        kq = _bdot(jnp.concatenate([side_by_side(ka, kb), side_by_side(qa, qb)], axis=0), k_bd)
        kk = side_by_side(kq[:c], kq[:c])
        qk = side_by_side(kq[c:], kq[c:])
        beta = [bg[:, h:h + 1] for h in heads]
        gc = [cum[:, GDN_V_HEADS + h:GDN_V_HEADS + h + 1] for h in heads]
        decay = jnp.where(incl, jnp.exp(jnp.where(incl, slot_cat(gc) - gcr_all[p:p + 1, :], 0.0)), 0.0)
        lmats.append(jnp.where(strict, slot_cat(beta) * kk * decay, 0.0))
        a_ins.append(qk * decay)
        qs.append((qa, qb, qa, qb)); ks.append((ka, kb, ka, kb)); betas.append(beta); gcs.append(gc)

    tinvs = _unit_lower_inverse_packed(lmats, row, col, bd_mask)

    uws, egs = [], []
    for u in range(n_units):
        bi, p = divmod(u, n_packs)
        bands = []
        eg = [jnp.exp(g) for g in gcs[u]]
        for x in range(GDN_PACK):
            h = _PACK_HEADS[GDN_PACK * p + x]
            vh = act_ref[bi, :, 2048 + h * hd:2048 + (h + 1) * hd]
            rhs = betas[u][x] * side_by_side(vh, ks[u][x] * eg[x])
            pieces = [jnp.zeros((c, 2 * hd * x), F32)] * (x > 0) + [rhs] + [jnp.zeros((c, 2 * hd * (3 - x)), F32)] * (x < 3)
            bands.append(jnp.concatenate(pieces, axis=-1))
        uws.append(_bdot(tinvs[u], jnp.concatenate(bands, axis=0)))
        egs.append(eg)

    wss, s_olds = [], []
    for u in range(n_units):
        for pr in range(2):
            x0, x1 = 2 * pr, 2 * pr + 1
            w0, w1 = (uws[u][:, 2 * hd * x + hd:2 * hd * (x + 1)] for x in (x0, x1))
            lhs = jnp.concatenate([side_by_side(w0, w1),
                                   side_by_side(qs[u][x0] * egs[u][x0], qs[u][x1] * egs[u][x1])], axis=0)
            s_old = sbd_ref[2 * u + pr]
            s_olds.append(s_old)
            wss.append(_bdot(lhs, s_old))

    v_news = []
    for u in range(n_units):
        vn = []
        for x in range(GDN_PACK):
            ws = wss[2 * u + x // 2]
            vn.append(uws[u][:, 2 * hd * x:2 * hd * x + hd] - ws[:c, hd * (x % 2):hd * (x % 2 + 1)])
        v_news.append(vn)
    for u in range(n_units):
        bi, p = divmod(u, n_packs)
        bands = []
        for x in range(GDN_PACK):
            pieces = [jnp.zeros((c, hd * x), F32)] * (x > 0) + [v_news[u][x]] + [jnp.zeros((c, hd * (3 - x)), F32)] * (x < 3)
            bands.append(jnp.concatenate(pieces, axis=-1))
        av = _bdot(a_ins[u], jnp.concatenate(bands, axis=0))
        for x in range(GDN_PACK):
            h = _PACK_HEADS[GDN_PACK * p + x]
            ws = wss[2 * u + x // 2]
            o_ref[bi, :, h * hd:(h + 1) * hd] = ws[c:, hd * (x % 2):hd * (x % 2 + 1)] + av[:, hd * x:hd * (x + 1)]
    zrows = jnp.zeros((c, 2 * hd), F32)
    for u in range(n_units):
        for pr in range(2):
            x0, x1 = 2 * pr, 2 * pr + 1
            gl0, gl1 = gcs[u][x0][c - 1:c, :], gcs[u][x1][c - 1:c, :]
            kd = jnp.concatenate([side_by_side(ks[u][x0] * jnp.exp(gl0 - gcs[u][x0]),
                                               ks[u][x1] * jnp.exp(gl1 - gcs[u][x1])), zrows], axis=0)
            kd_t = jnp.concatenate([kd[:, :hd].T, kd[:, hd:].T], axis=0)
            vn = jnp.concatenate([side_by_side(v_news[u][x0], v_news[u][x1]), zrows], axis=0)
            d_last = jnp.where(row_pair < hd, jnp.exp(gl0), jnp.exp(gl1))
            sbd_ref[2 * u + pr] = jnp.where(pair_mask, s_olds[2 * u + pr] * d_last + _bdot(kd_t, vn), 0.0)

    @pl.when(pl.program_id(1) == pl.num_programs(1) - 1)
    def _():
        for bi in range(bb):
            for pr in range(n_pairs):
                s_pair = sbd_ref[bi * n_pairs + pr]
                s_ref[bi, _PACK_HEADS[2 * pr]] = s_pair[:hd, :hd]
                s_ref[bi, _PACK_HEADS[2 * pr + 1]] = s_pair[hd:, hd:]


def _gdn_delta(act, bg, s0):
    b, t, _ = act.shape
    nc = t // GDN_CHUNK
    n_packs = GDN_V_HEADS // GDN_PACK
    wp = GDN_PACK * GDN_CHUNK
    g_rows = bg[:, :, GDN_V_HEADS:2 * GDN_V_HEADS][:, :, np.asarray(_PACK_HEADS)]
    g_rows = g_rows.reshape(b, nc, GDN_CHUNK, n_packs, GDN_PACK).transpose(0, 1, 3, 4, 2).reshape(b, nc, n_packs, wp)
    g_rows = jnp.pad(g_rows, ((0, 0), (0, 0), (0, 8 - n_packs), (0, 0)))
    tri = np.tril(np.ones((GDN_CHUNK, GDN_CHUNK), np.float32))
    ltri = jnp.asarray(tri, dtype=BF16)
    lbd = jnp.asarray(np.kron(np.eye(GDN_PACK, dtype=np.float32), tri), dtype=BF16)
    bb = 2 if b % 2 == 0 else 1
    state_spec = pl.BlockSpec((bb, GDN_V_HEADS, GDN_HEAD_DIM, GDN_HEAD_DIM), lambda b, n: (b, 0, 0, 0))
    return pl.pallas_call(
        functools.partial(_gdn_delta_kernel, bb=bb),
        grid=(b // bb, nc),
        in_specs=[pl.BlockSpec((bb, GDN_CHUNK, C_CONV), lambda b, n: (b, n, 0)),
                  pl.BlockSpec((bb, GDN_CHUNK, LANES), lambda b, n: (b, n, 0)),
                  pl.BlockSpec((bb, None, 8, wp), lambda b, n: (b, n, 0, 0)),
                  state_spec,
                  pl.BlockSpec((GDN_CHUNK, GDN_CHUNK), lambda b, n: (0, 0)),
                  pl.BlockSpec((wp, wp), lambda b, n: (0, 0))],
        out_specs=[pl.BlockSpec((bb, GDN_CHUNK, GDN_V_W), lambda b, n: (b, n, 0)), state_spec],
        out_shape=[jax.ShapeDtypeStruct((b, t, GDN_V_W), F32),
                   jax.ShapeDtypeStruct(s0.shape, F32)],
        scratch_shapes=[pltpu.VMEM((bb * GDN_V_HEADS // 2, 2 * GDN_HEAD_DIM, 2 * GDN_HEAD_DIM), F32)],
        compiler_params=pltpu.CompilerParams(
            dimension_semantics=("parallel", "arbitrary"), vmem_limit_bytes=VMEM_LIMIT),
        name="gdn_delta_rule",
    )(act, bg, g_rows, s0, ltri, lbd)


SAMPLE_ROWS = NSA_HEADS * 4
SEL_PAGES_PER_STEP = 32


def _sample_cmp_kernel(q_ref, kc_ref, vc_ref, lut_ref, ov_ref, oc_ref, un_ref, *, nseg, past_len, n_sb, nq):
    rg = NSA_HPG * nq
    ri = lax.broadcasted_iota(jnp.int32, (rg, nseg), 0)
    mi = lax.broadcasted_iota(jnp.int32, (rg, nseg), 1)
    dist = past_len + ri % nq - CMP_STRIDE * mi - (CMP_STRIDE - 1)
    valid = (dist >= 0) & (mi >= 1)
    jl = lax.broadcasted_iota(jnp.int32, (8, un_ref.shape[-1]), 1)
    q_blk = (past_len + lax.broadcasted_iota(jnp.int32, jl.shape, 0) % nq) // SLC_BLOCK
    forced = (jl == 0) | (jl == q_blk) | (jl == q_blk - 1)
    jf = jl.astype(F32)
    pcs_parts = []
    for g in range(NSA_KV_GROUPS):
        sc = _nt(q_ref[g], kc_ref[g])
        tail = sc[:, nseg - LANES:] + _lut_gather(lut_ref[g], dist[:, nseg - LANES:])
        sc = jnp.where(valid, jnp.concatenate([sc[:, :nseg - LANES], tail], axis=-1), NEG)
        mx = jnp.max(sc, axis=-1, keepdims=True)
        e = jnp.where(valid, jnp.exp(sc - mx), 0.0)
        ssum = jnp.sum(e, axis=-1, keepdims=True)
        pc = e / jnp.where(ssum > 0, ssum, 1.0)
        oc_ref[g] = jnp.dot(pc.astype(BF16), vc_ref[g], preferred_element_type=F32)
        pcs = pc
        for hh in range(1, NSA_HPG):
            pcs = pcs + pltpu.roll(pc, hh * nq, axis=0)
        hi = pcs[0:8].astype(BF16).astype(F32)
        pcs_parts += [hi, pcs[0:8] - hi]
    imp_all = jnp.dot(jnp.concatenate(pcs_parts, axis=0).astype(BF16), ov_ref[...], preferred_element_type=F32)
    scores = []
    for g in range(NSA_KV_GROUPS):
        imp = imp_all[16 * g:16 * g + 8] + imp_all[16 * g + 8:16 * g + 16]
        score = jnp.where(forced, -jnp.inf, jnp.where(jl <= q_blk, imp, -1e4))
        scores.append(jnp.where(jl < n_sb, score, -3e38))
    sels = [forced.astype(F32)] * NSA_KV_GROUPS
    for _ in range(min(N_SELECT, n_sb) - 3):
        for g in range(NSA_KV_GROUPS):
            mxs = jnp.max(scores[g], axis=-1, keepdims=True)
            idx = jnp.min(jnp.where(scores[g] == mxs, jf, 1e9), axis=-1, keepdims=True)
            hit = jf == idx
            sels[g] = jnp.where(hit, 1.0, sels[g])
            scores[g] = jnp.where(hit, -jnp.inf, scores[g])
    for g in range(NSA_KV_GROUPS):
        un_ref[g] = 1.0 - sels[g]


def _sample_cmp(q16, kc, vc, lut16, past_len, nq):
    b = q16.shape[0]
    nseg = kc.shape[2]
    rg = NSA_HPG * nq
    n_sb = past_len // SLC_BLOCK + 1
    n_sb_pad = -(-n_sb // LANES) * LANES
    assert nq == 4 and nseg * CMP_STRIDE == past_len
    m = np.arange(nseg)[:, None]
    j = np.arange(n_sb_pad)[None, :]
    ov = ((CMP_STRIDE * m - CMP_STRIDE < j * SLC_BLOCK + SLC_BLOCK) & (CMP_STRIDE * m + CMP_STRIDE - 1 >= j * SLC_BLOCK)
          & (m >= 1) & (j < n_sb)).astype(np.float32)
    whole = lambda *shape: pl.BlockSpec((None,) + shape, lambda b: (b,) + (0,) * len(shape))
    return pl.pallas_call(
        functools.partial(_sample_cmp_kernel, nseg=nseg, past_len=past_len, n_sb=n_sb, nq=nq),
        grid=(b,),
        in_specs=[whole(NSA_KV_GROUPS, rg, LANES), whole(NSA_KV_GROUPS, nseg, LANES), whole(NSA_KV_GROUPS, nseg, LANES),
                  pl.BlockSpec((NSA_KV_GROUPS, rg, LANES), lambda b: (0, 0, 0)),
                  pl.BlockSpec((nseg, n_sb_pad), lambda b: (0, 0))],
        out_specs=[whole(NSA_KV_GROUPS, rg, LANES), whole(NSA_KV_GROUPS, 8, n_sb_pad)],
        out_shape=[jax.ShapeDtypeStruct((b, NSA_KV_GROUPS, rg, LANES), F32),
                   jax.ShapeDtypeStruct((b, NSA_KV_GROUPS, 8, n_sb_pad), F32)],
        compiler_params=pltpu.CompilerParams(dimension_semantics=("parallel",), vmem_limit_bytes=VMEM_LIMIT),
        name="nsa_sample_cmp_topk",
    )(q16, kc, vc, lut16, jnp.asarray(ov, dtype=BF16))


def _sample_sel_kernel(tab_ref, *refs, npg, past_len, nq):
    del tab_ref
    pages = refs[:npg]
    qbd_ref, un_ref, ee_ref, far_ref, lut_ref, m_ref, l_ref, acc_ref = refs[npg:]
    c = pl.program_id(1)
    kc = npg * PAGE_SIZE

    @pl.when(c == 0)
    def _():
        m_ref[...] = jnp.full(m_ref.shape, NEG, F32)
        l_ref[...] = jnp.zeros(l_ref.shape, F32)
        acc_ref[...] = jnp.zeros(acc_ref.shape, F32)

    kt = jnp.concatenate([pg[0] for pg in pages], axis=1).astype(BF16)
    vt = jnp.concatenate([pg[1] for pg in pages], axis=1).astype(BF16)
    s = (jnp.dot(qbd_ref[...], kt, preferred_element_type=F32) + far_ref[...][:, 0:1]
         + jnp.dot(un_ref[...], ee_ref[...], preferred_element_type=F32))
    ri = lax.broadcasted_iota(jnp.int32, (SAMPLE_ROWS, LANES), 0)
    li = lax.broadcasted_iota(jnp.int32, (SAMPLE_ROWS, LANES), 1)
    dist = past_len + ri % nq - (c * kc + kc - LANES + li)
    s = jnp.concatenate([s[:, :kc - LANES], s[:, kc - LANES:] + _lut_gather(lut_ref[...], dist)], axis=-1)
    m_old = m_ref[...][:, 0:1]
    m_new = jnp.maximum(m_old, jnp.max(s, axis=-1, keepdims=True))
    alpha = jnp.exp(m_old - m_new)
    p = jnp.exp(s - m_new)
    l_ref[...] = alpha * l_ref[...] + jnp.sum(p, axis=-1, keepdims=True)
    acc_ref[...] = alpha * acc_ref[...] + _nt(p.astype(BF16), vt)
    m_ref[...] = jnp.broadcast_to(m_new, m_ref.shape)


def _sample_sel(pages, table, qbd, unsel_c, farcol, lut64, past_len, nq):
    b, n_pages = table.shape
    npg = min(SEL_PAGES_PER_STEP, n_pages)
    kc = npg * PAGE_SIZE
    nch = n_pages // npg
    blk_per_chunk = kc // SLC_BLOCK
    ee = np.zeros((LANES, kc), np.float32)
    ee[np.arange(kc) // SLC_BLOCK, np.arange(kc)] = NEG
    assert blk_per_chunk <= LANES

    def page_spec(j):
        return pl.BlockSpec((None, 2, NSA_KV_W // 2, PAGE_SIZE), lambda b, c, tab: (tab[b, c * npg + j], 0, 0, 0))

    const = lambda *shape: pl.BlockSpec(shape, lambda b, c, tab: (0,) * len(shape))
    acc_spec = lambda cols: pl.BlockSpec((None, SAMPLE_ROWS, cols), lambda b, c, tab: (b, 0, 0))
    grid_spec = pltpu.PrefetchScalarGridSpec(
        num_scalar_prefetch=1,
        grid=(b, nch),
        in_specs=[page_spec(j) for j in range(npg)] + [
            pl.BlockSpec((None, SAMPLE_ROWS, 2 * LANES), lambda b, c, tab: (b, 0, 0)),
            pl.BlockSpec((None, None, SAMPLE_ROWS, LANES), lambda b, c, tab: (b, c, 0, 0)),
            const(LANES, kc), const(SAMPLE_ROWS, LANES), const(SAMPLE_ROWS, LANES)],
        out_specs=[acc_spec(LANES), acc_spec(LANES), acc_spec(2 * LANES)],
    )
    return pl.pallas_call(
        functools.partial(_sample_sel_kernel, npg=npg, past_len=past_len, nq=nq),
        grid_spec=grid_spec,
        out_shape=[jax.ShapeDtypeStruct((b, SAMPLE_ROWS, LANES), F32),
                   jax.ShapeDtypeStruct((b, SAMPLE_ROWS, LANES), F32),
                   jax.ShapeDtypeStruct((b, SAMPLE_ROWS, 2 * LANES), F32)],
        compiler_params=pltpu.CompilerParams(
            dimension_semantics=("parallel", "arbitrary"), vmem_limit_bytes=VMEM_LIMIT),
        name="nsa_sample_selected",
    )(table, *([pages] * npg), qbd, unsel_c, jnp.asarray(ee, dtype=BF16), farcol, lut64)


def _own_group_cols(x, grp):
    out = jnp.zeros((x.shape[0], NSA_HEAD_DIM), F32)
    for g in range(NSA_KV_GROUPS):
        out = jnp.where(grp == g, x[:, g * NSA_HEAD_DIM:(g + 1) * NSA_HEAD_DIM], out)
    return out


def _sample_final_kernel(qbd_ref, m_ref, l_ref, acc_ref, snew_ref, wc_ref, wnew_ref, oc_ref, gr_ref, far_ref, lut_ref,
                         o_ref, *, nq, w_buf):
    rows = SAMPLE_ROWS
    qbd = qbd_ref[...]
    far = far_ref[...][:, 0:1]
    lut = lut_ref[...]
    ri = lax.broadcasted_iota(jnp.int32, (rows, LANES), 0)
    li = lax.broadcasted_iota(jnp.int32, (rows, LANES), 1)
    tok = ri % nq
    grp = lax.broadcasted_iota(jnp.int32, (rows, NSA_HEAD_DIM), 0) // (NSA_HPG * nq)

    knew = snew_ref[...]
    s_new = _nt(qbd, knew[:, :256].astype(BF16)) + far
    d_new = tok - li
    s_new = jnp.where((d_new >= 0) & (li < nq), s_new + _lut_gather(lut, d_new), NEG)
    m_old = m_ref[...][:, 0:1]
    m_new = jnp.maximum(m_old, jnp.max(s_new, axis=-1, keepdims=True))
    alpha = jnp.exp(m_old - m_new)
    p_new = jnp.exp(s_new - m_new)
    l_s = alpha * l_ref[...][:, 0:1] + jnp.sum(p_new, axis=-1, keepdims=True)
    acc_s = alpha * acc_ref[...] + jnp.dot(p_new.astype(BF16), knew[:, 256:].astype(BF16), preferred_element_type=F32)
    o_s = _own_group_cols(acc_s, grp) / l_s

    kv_w = jnp.concatenate([wc_ref[...], wnew_ref[...]], axis=0)
    s_w = _nt(qbd, kv_w[:, :256].astype(BF16)) + far
    n_w = w_buf + LANES
    idx = lax.broadcasted_iota(jnp.int32, (rows, n_w), 1)
    d_w = w_buf + lax.broadcasted_iota(jnp.int32, (rows, n_w), 0) % nq - idx
    ok_w = (d_w >= 0) & (d_w < WINDOW) & (idx < w_buf + nq)
    corr = [jnp.zeros((rows, n_w - 2 * LANES), F32)]
    for cidx in range(2):
        lo = n_w - 2 * LANES + cidx * LANES
        corr.append(_lut_gather(lut, d_w[:, lo:lo + LANES]))
    s_w = jnp.where(ok_w, s_w + jnp.concatenate(corr, axis=-1), NEG)
    m_w = jnp.max(s_w, axis=-1, keepdims=True)
    p_w = jnp.exp(s_w - m_w)
    l_w = jnp.sum(p_w, axis=-1, keepdims=True)
    acc_w = jnp.dot(p_w.astype(BF16), kv_w[:, 256:].astype(BF16), preferred_element_type=F32)
    o_w = _own_group_cols(acc_w, grp) / l_w

    gt = jax.nn.sigmoid(gr_ref[...])
    o_ref[...] = gt[:, 0:1] * oc_ref[...][:, :NSA_HEAD_DIM] + gt[:, 1:2] * o_s + gt[:, 2:3] * o_w


def _sample_final(qbd, m, l, acc, snew, wcache, wnew, o_c, graw, farcol, lut64, nq):
    b = qbd.shape[0]
    w_buf = wcache.shape[1]
    assert w_buf == WINDOW
    whole = lambda *shape: pl.BlockSpec((None,) + shape, lambda b: (b,) + (0,) * len(shape))
    const = lambda *shape: pl.BlockSpec(shape, lambda b: (0,) * len(shape))
    return pl.pallas_call(
        functools.partial(_sample_final_kernel, nq=nq, w_buf=w_buf),
        grid=(b,),
        in_specs=[whole(SAMPLE_ROWS, 2 * LANES), whole(SAMPLE_ROWS, LANES), whole(SAMPLE_ROWS, LANES),
                  whole(SAMPLE_ROWS, 2 * LANES), whole(LANES, NSA_KV_W), whole(w_buf, NSA_KV_W), whole(LANES, NSA_KV_W),
                  whole(SAMPLE_ROWS, LANES), whole(SAMPLE_ROWS, LANES),
                  const(SAMPLE_ROWS, LANES), const(SAMPLE_ROWS, LANES)],
        out_specs=whole(SAMPLE_ROWS, NSA_HEAD_DIM),
        out_shape=jax.ShapeDtypeStruct((b, SAMPLE_ROWS, NSA_HEAD_DIM), F32),
        compiler_params=pltpu.CompilerParams(dimension_semantics=("parallel",), vmem_limit_bytes=VMEM_LIMIT),
        name="nsa_sample_final",
    )(qbd, m, l, acc, snew, wcache, wnew, o_c, graw, farcol, lut64)


def _ffn(x, mod, gains, w_in, w_out):
    hid = _norm_mod_swiglu(x, gains[2], mod[3], mod[4], w_in)
    return _matmul_rms_residual(hid, w_out, x, mod[5], gains[3])


def _nsa_layout_kernel(pq_ref, ps_ref, pw_ref, bias_ref, q_ref, ks_ref, vs_ref, kw_ref, vw_ref, *, tm):
    i = pl.program_id(1)
    lane = lax.broadcasted_iota(jnp.int32, (tm, LANES), 1)
    low = lane < NSA_HEAD_DIM

    def head_tile(ref, h):
        tile = ref[:, (h // 2) * LANES:(h // 2 + 1) * LANES]
        return pltpu.roll(tile, NSA_HEAD_DIM, axis=1) if h % 2 else tile

    for h in range(NSA_HEADS):
        q_ref[h] = jnp.where(low, head_tile(pq_ref, h), bias_ref[h]).astype(BF16)
    key = jnp.maximum(i - 1, 0) * tm + lax.broadcasted_iota(jnp.int32, (tm, LANES), 0)
    onehot = jnp.where(key // SLC_BLOCK == lane, NEG, 0.0).astype(BF16)
    k_ones = jnp.where((lane == NSA_HEAD_DIM) | (lane == NSA_HEAD_DIM + 1), 1.0, 0.0)
    v_ones = jnp.where(lane == NSA_HEAD_DIM, 1.0, 0.0)
    live = i > 0
    for g in range(NSA_KV_GROUPS):
        ks_ref[g, :, :LANES] = onehot
        ks_ref[g, :, LANES:] = jnp.where(low, head_tile(ps_ref, g), k_ones).astype(BF16)
        vs_ref[g] = jnp.where(low, head_tile(ps_ref, NSA_KV_GROUPS + g), v_ones).astype(BF16)
        kw_ref[g] = jnp.where(live, jnp.where(low, head_tile(pw_ref, g), k_ones), 0.0).astype(BF16)
        vw_ref[g] = jnp.where(live, jnp.where(low, head_tile(pw_ref, NSA_KV_GROUPS + g), v_ones), 0.0).astype(BF16)


def _nsa_layouts(p, bias_cols):
    b, t, _ = p.shape
    tm = WINDOW
    assert t % tm == 0
    src = lambda width, col_block: pl.BlockSpec((None, tm, width), lambda b, i: (b, jnp.maximum(i - 1, 0), col_block))
    same = lambda heads, width: pl.BlockSpec((None, heads, tm, width), lambda b, i: (b, 0, jnp.maximum(i - 1, 0), 0))
    late = pl.BlockSpec((None, NSA_KV_GROUPS, tm, LANES), lambda b, i: (b, 0, i, 0))
    bias = jnp.pad(bias_cols.astype(F32), ((0, 0), (NSA_HEAD_DIM, 0))).reshape(NSA_HEADS, 1, LANES)
    sds = lambda heads, rows, width: jax.ShapeDtypeStruct((b, heads, rows, width), BF16)
    return pl.pallas_call(
        functools.partial(_nsa_layout_kernel, tm=tm),
        grid=(b, t // tm + 1),
        in_specs=[src(NSA_Q_W, 0), src(NSA_KV_W, (NSA_Q_W + NSA_KV_W) // NSA_KV_W),
                  src(NSA_KV_W, (NSA_Q_W + 2 * NSA_KV_W) // NSA_KV_W),
                  pl.BlockSpec((NSA_HEADS, 1, LANES), lambda b, i: (0, 0, 0))],
        out_specs=[same(NSA_HEADS, LANES), same(NSA_KV_GROUPS, 2 * LANES), same(NSA_KV_GROUPS, LANES), late, late],
        out_shape=[sds(NSA_HEADS, t, LANES), sds(NSA_KV_GROUPS, t, 2 * LANES), sds(NSA_KV_GROUPS, t, LANES),
                   sds(NSA_KV_GROUPS, t + WINDOW, LANES), sds(NSA_KV_GROUPS, t + WINDOW, LANES)],
        compiler_params=pltpu.CompilerParams(
            dimension_semantics=("parallel", "arbitrary"), vmem_limit_bytes=VMEM_LIMIT),
        name="nsa_layouts",
    )(p, p, p, bias)


def _nsa_prompt(x, mod, gains, w_in, cmp_w, w_out, tb, lut, bias_cols):
    b, t, _ = x.shape
    p = _norm_mod_linear(x, gains[0], mod[0], mod[1], w_in)
    kvc, kvs, kvw = (p[..., 1024 + i * NSA_KV_W:1024 + (i + 1) * NSA_KV_W] for i in range(3))
    q128, ks, vs, kw, vw = _nsa_layouts(p, bias_cols)
    n_pages = t // PAGE_SIZE
    table = jnp.arange(b * n_pages, dtype=jnp.int32).reshape(b, n_pages)
    kc, vc = _compress(kvc.reshape(b * n_pages, SEGS_PER_PAGE, SEG_W), table, *cmp_w)
    o = _nsa_prompt_attention(q128, p, (NSA_Q_W + 3 * NSA_KV_W) // LANES, kc, vc, ks, vs, kw, vw, tb, lut)
    x = _matmul_rms_residual(o, w_out, x, mod[2], gains[1])
    shape5 = (b, t, 2, NSA_KV_GROUPS, NSA_HEAD_DIM)
    return x, kvc.reshape(shape5), kvs.reshape(shape5), kvw.reshape(shape5)[:, -min(WINDOW, t):]


def _nsa_sample(x, mod, gains, w_in, cmp_w, w_out, lut, rel_bias, bias_cols, cache_cmp, cache_slc, cache_win,
                page_table, db, nq):
    n_pages = page_table.shape[1]
    past_len = n_pages * PAGE_SIZE
    rows = db * nq
    p = _norm_mod_linear(x, gains[0], mod[0], mod[1], w_in)[0]
    kvc, kvs, kvw = (p[:, 1024 + i * NSA_KV_W:1024 + (i + 1) * NSA_KV_W] for i in range(3))
    qh = p[:, :NSA_Q_W].astype(BF16).reshape(db, nq, NSA_KV_GROUPS, NSA_HPG, NSA_HEAD_DIM).transpose(0, 2, 3, 1, 4)
    q16 = jnp.concatenate([qh, jnp.broadcast_to(bias_cols.reshape(1, NSA_KV_GROUPS, NSA_HPG, 1, NSA_HEAD_DIM), qh.shape)],
                          axis=-1).reshape(db, NSA_KV_GROUPS, NSA_HPG * nq, LANES)
    eye_g = jnp.eye(NSA_KV_GROUPS, dtype=BF16)
    qbd = jnp.einsum('bghtd,gj->bghtjd', qh, eye_g).reshape(db, SAMPLE_ROWS, NSA_KV_GROUPS * NSA_HEAD_DIM)
    row_head = np.repeat(np.arange(NSA_HEADS), nq)
    lut64 = lut[:, 0, :][row_head]
    farcol = jnp.broadcast_to(rel_bias[N_BUCKETS - 1][row_head][:, None], (SAMPLE_ROWS, LANES))
    rows_minor = lambda cache: jnp.transpose(cache, (0, 2, 3, 4, 1))
    kc, vc = _compress(rows_minor(cache_cmp).reshape(-1, 2, 2, LANES, PAGE_SIZE), page_table, *cmp_w)
    o_c, unsel = _sample_cmp(q16, kc, vc, lut64.reshape(NSA_KV_GROUPS, NSA_HPG * nq, LANES), past_len, nq)
    npg = min(SEL_PAGES_PER_STEP, n_pages)
    nch = n_pages // npg
    bpc = npg * PAGE_SIZE // SLC_BLOCK
    un = unsel[:, :, :nq, :past_len // SLC_BLOCK].reshape(db, NSA_KV_GROUPS, 1, nq, nch, bpc)
    un = jnp.broadcast_to(un, (db, NSA_KV_GROUPS, NSA_HPG, nq, nch, bpc)).transpose(0, 4, 1, 2, 3, 5)
    un = jnp.pad(un.reshape(db, nch, SAMPLE_ROWS, bpc), ((0, 0), (0, 0), (0, 0), (0, LANES - bpc))).astype(BF16)
    m, l, acc = _sample_sel(rows_minor(cache_slc).reshape(-1, 2, NSA_KV_W // 2, PAGE_SIZE), page_table, qbd, un, farcol,
                            lut64, past_len, nq)
    pad_rows = lambda a: jnp.pad(a.reshape(db, nq, NSA_KV_W), ((0, 0), (0, LANES - nq), (0, 0)))
    wcache = cache_win.reshape(db, -1, NSA_KV_W)
    graw = p[:, NSA_Q_W + 3 * NSA_KV_W:NSA_Q_W + 3 * NSA_KV_W + 3 * NSA_HEADS]
    graw = graw.reshape(db, nq, NSA_HEADS, 3).transpose(0, 2, 1, 3).reshape(db, SAMPLE_ROWS, 3)
    graw = jnp.pad(graw, ((0, 0), (0, 0), (0, LANES - 3)))
    o = _sample_final(qbd, m, l, acc, pad_rows(kvs), wcache, pad_rows(kvw), o_c.reshape(db, SAMPLE_ROWS, LANES), graw,
                      farcol, lut64, nq)
    o = o.reshape(db, NSA_HEADS, nq, NSA_HEAD_DIM).transpose(0, 2, 1, 3).reshape(1, rows, NSA_Q_W)
    x = _matmul_rms_residual(o, w_out, x, mod[2], gains[1])
    shape5 = (db, nq, 2, NSA_KV_GROUPS, NSA_HEAD_DIM)
    kv_win = jnp.concatenate([cache_win, kvw.reshape(shape5)], axis=1)[:, -cache_win.shape[1]:]
    return x, kvc.reshape(shape5), kvs.reshape(shape5), kv_win


def _gdn_prompt(x, mod, gains, w_in, conv_w, a_log, dt_bias, norm_w, w_out):
    b, t, _ = x.shape
    p = _norm_mod_linear(x, gains[0], mod[0], mod[1], w_in)
    act = _gdn_conv(p, conv_w)
    bg = _gdn_gates(p, (C_CONV + GDN_V_W) // LANES, a_log, dt_bias)
    s0 = jnp.zeros((b, GDN_V_HEADS, GDN_HEAD_DIM, GDN_HEAD_DIM), F32)
    o, s_fin = _gdn_delta(act, bg, s0)
    x = _gdn_out(o, p, C_CONV // GDN_V_W, norm_w, w_out, x, mod[2], gains[1])
    return x, p[:, t - (CONV_W - 1):, :C_CONV], s_fin


def _gdn_sample(x, mod, gains, w_in, conv_w, a_log, dt_bias, norm_w, w_out, conv_buf, s0, db, nq):
    p = _norm_mod_linear(x, gains[0], mod[0], mod[1], w_in)
    qkv = p[0, :, :C_CONV].reshape(db, nq, C_CONV)
    xp = jnp.concatenate([conv_buf, qkv], axis=1)
    act = _gdn_conv(jnp.pad(xp, ((0, 0), (0, 8 - xp.shape[1]), (0, 0))), conv_w)[:, CONV_W - 1:CONV_W - 1 + nq]
    bg = _gdn_gates(p, (C_CONV + GDN_V_W) // LANES, a_log, dt_bias).reshape(db, nq, LANES)
    pad_t = ((0, 0), (0, GDN_CHUNK - nq), (0, 0))
    o, s_fin = _gdn_delta(jnp.pad(act, pad_t), jnp.pad(bg, pad_t), s0)
    o = o[:, :nq].reshape(1, db * nq, GDN_V_W)
    x = _gdn_out(o, p, C_CONV // GDN_V_W, norm_w, w_out, x, mod[2], gains[1])
    return x, xp[:, -(CONV_W - 1):], s_fin


def kernel(x_prompt, x_sample, c_prompt, c_sample, cache_kv_cmp, cache_kv_slc, cache_kv_win, state_conv, state_ssm,
           page_table, rel_bias, norm_gains, w_ada, b_ada, w_ffn_in, w_ffn_out, nsa_w_in, nsa_cmp_pe, nsa_cmp_w1,
           nsa_cmp_b1, nsa_cmp_w2, nsa_w_out, gdn_w_in, gdn_conv_w, gdn_a_log, gdn_dt_bias, gdn_norm_w, gdn_w_out):
    depth = w_ada.shape[0]
    bp, t, d = x_prompt.shape
    db, nq, _ = x_sample.shape
    assert nq + CONV_W - 1 <= 8 and nq <= GDN_CHUNK

    c_all = jnp.concatenate([c_prompt, c_sample], axis=0)
    rows_pad = -(-c_all.shape[0] // 8) * 8
    ada = _adaln(jnp.pad(c_all, ((0, rows_pad - c_all.shape[0]), (0, 0))), w_ada, b_ada)
    ada = ada.reshape(depth, rows_pad, 6, d)
    tb, lut = _bias_tables(rel_bias)
    far_hi, far_lo = _split2(rel_bias[N_BUCKETS - 1])
    bias_cols = jnp.zeros((NSA_HEADS, NSA_HEAD_DIM), BF16).at[:, 0].set(far_hi).at[:, 1].set(far_lo)

    xp = x_prompt
    xs = x_sample.reshape(1, db * nq, d)
    kvc_p, kvc_s, kvs_p, kvs_s, kvw_p, kvw_s, cv_p, cv_s, ss_p, ss_s = ([] for _ in range(10))
    for i in range(depth):
        mod_p = [ada[i, :bp, k][:, None, :] for k in range(6)]
        mod_s = [jnp.repeat(ada[i, bp:bp + db, k], nq, axis=0)[None] for k in range(6)]
        gains = norm_gains[i]
        l = i // 2
        if i % 2 == 0:
            w_in = jnp.concatenate([nsa_w_in[l][:, :NSA_Q_W] * (NSA_HEAD_DIM ** -0.5), nsa_w_in[l][:, NSA_Q_W:]], axis=1)
            w_in = jnp.pad(w_in, ((0, 0), (0, -w_in.shape[1] % LANES))).astype(BF16)
            cmp_w = _compress_weights(nsa_cmp_pe[l], nsa_cmp_w1[l], nsa_cmp_b1[l], nsa_cmp_w2[l])
            w_out = nsa_w_out[l].astype(BF16)
            xp, a, bq, cq = _nsa_prompt(xp, mod_p, gains, w_in, cmp_w, w_out, tb, lut, bias_cols)
            kvc_p.append(a); kvs_p.append(bq); kvw_p.append(cq)
            xs, a, bq, cq = _nsa_sample(xs, mod_s, gains, w_in, cmp_w, w_out, lut, rel_bias, bias_cols, cache_kv_cmp[l],
                                        cache_kv_slc[l], cache_kv_win[l], page_table, db, nq)
            kvc_s.append(a); kvs_s.append(bq); kvw_s.append(cq)
        else:
            w_in = jnp.pad(gdn_w_in[l], ((0, 0), (0, -gdn_w_in.shape[2] % (5 * MXU_WIDTH)))).astype(BF16)
            gdn_w = (w_in, gdn_conv_w[l], gdn_a_log[l], gdn_dt_bias[l], gdn_norm_w[l], gdn_w_out[l].astype(BF16))
            xp, a, bq = _gdn_prompt(xp, mod_p, gains, *gdn_w)
            cv_p.append(a); ss_p.append(bq)
            xs, a, bq = _gdn_sample(xs, mod_s, gains, *gdn_w, state_conv[l], state_ssm[l], db, nq)
            cv_s.append(a); ss_s.append(bq)
        w_ffn = (w_ffn_in[i].astype(BF16), w_ffn_out[i].astype(BF16))
        xp = _ffn(xp, mod_p, gains, *w_ffn)
        xs = _ffn(xs, mod_s, gains, *w_ffn)
    return (xp, xs.reshape(db, nq, d), jnp.stack(kvc_p), jnp.stack(kvc_s), jnp.stack(kvs_p), jnp.stack(kvs_s),
            jnp.stack(kvw_p), jnp.stack(kvw_s), jnp.stack(cv_p), jnp.stack(cv_s), jnp.stack(ss_p), jnp.stack(ss_s))
```

```python
import functools
import math

import numpy as np
import jax
import jax.numpy as jnp
from jax import lax
from jax.experimental import pallas as pl
from jax.experimental.pallas import tpu as pltpu

F32 = jnp.float32
BF16 = jnp.bfloat16

D_MODEL = 1024
RMS_EPS = 1e-6
D_FF = 2816
NSA_HEADS = 16
NSA_HEAD_DIM = 64
NSA_KV_GROUPS = 4
NSA_HPG = 4
CMP_BLOCK = 32
CMP_STRIDE = 16
CMP_HID = 256
SLC_BLOCK = 64
N_SELECT = 16
WINDOW = 512
Q_BLOCK = 128
PAGE_SIZE = 128
N_BUCKETS = 32
GDN_QK_HEADS = 8
GDN_V_HEADS = 16
GDN_HEAD_DIM = 128
CONV_W = 4
GDN_CHUNK = 64
NSA_Q_W = 1024
NSA_KV_W = 512
C_CONV = 4096
GDN_V_W = 2048

LANES = 128
SEG_W = CMP_STRIDE * NSA_KV_W
SEGS_PER_PAGE = PAGE_SIZE // CMP_STRIDE
NEG = -1e30
VMEM_LIMIT = 48 * 1024 * 1024

_BUCKET_THR = (19, 21, 24, 27, 31, 35, 40, 46, 52, 59, 67, 77, 87, 99, 113)
FAR_DIST = 128


def _nt(a, b):
    return lax.dot_general(a, b, (((1,), (1,)), ((), ())), preferred_element_type=F32)


def _split2(x):
    hi = x.astype(BF16)
    lo = (x - hi.astype(F32)).astype(BF16)
    return hi, lo


def _split3(x):
    hi = x.astype(BF16)
    r = x - hi.astype(F32)
    mid = r.astype(BF16)
    lo = (r - mid.astype(F32)).astype(BF16)
    return hi, mid, lo


MXU_WIDTH = 256
MAX_TN = 2816


def _pick_tn(n):
    units = n // LANES
    cands = [d * LANES for d in range(1, units + 1) if units % d == 0 and d * LANES <= MAX_TN]
    full = [c for c in cands if c % MXU_WIDTH == 0]
    return max(full) if full and 2 * max(full) >= max(cands) else max(cands)


def _adaln_kernel(c_ref, w_ref, b_ref, o_ref):
    c = c_ref[...]
    a = (c * jax.nn.sigmoid(c)).astype(BF16)
    o_ref[...] = jnp.dot(a, w_ref[...].astype(BF16), preferred_element_type=F32) + b_ref[...]


def _adaln(c_all, w_ada, b_ada):
    depth, d, n = w_ada.shape
    rows = c_all.shape[0]
    tn = 768
    return pl.pallas_call(
        _adaln_kernel,
        grid=(depth, n // tn),
        in_specs=[pl.BlockSpec((rows, d), lambda l, j: (0, 0)),
                  pl.BlockSpec((None, d, tn), lambda l, j: (l, 0, j)),
                  pl.BlockSpec((None, 1, tn), lambda l, j: (l, 0, j))],
        out_specs=pl.BlockSpec((None, rows, tn), lambda l, j: (l, 0, j)),
        out_shape=jax.ShapeDtypeStruct((depth, rows, n), F32),
        compiler_params=pltpu.CompilerParams(dimension_semantics=("parallel", "parallel")),
        name="adaln",
    )(c_all, w_ada, b_ada.reshape(depth, 1, n))


def _mod_norm(x, gain, shift, scale):
    ms = jnp.mean(x * x, axis=-1, keepdims=True)
    y = x * lax.rsqrt(ms + RMS_EPS) * gain
    return y * (1.0 + scale) + shift


def _nml_kernel(x_ref, g_ref, sh_ref, sc_ref, w_ref, o_ref):
    h = _mod_norm(x_ref[...], g_ref[...], sh_ref[...], sc_ref[...]).astype(BF16)
    o_ref[...] = jnp.dot(h, w_ref[...], preferred_element_type=F32).astype(o_ref.dtype)


def _nml_swiglu_kernel(x_ref, g_ref, sh_ref, sc_ref, wg_ref, wu_ref, o_ref, h_ref):
    @pl.when(pl.program_id(2) == 0)
    def _():
        h_ref[...] = _mod_norm(x_ref[...], g_ref[...], sh_ref[...], sc_ref[...]).astype(BF16)

    h = h_ref[...]
    gate = jnp.dot(h, wg_ref[...], preferred_element_type=F32)
    up = jnp.dot(h, wu_ref[...], preferred_element_type=F32)
    o_ref[...] = (gate * jax.nn.sigmoid(gate) * up).astype(o_ref.dtype)


def _mod_spec(mod, tm):
    if mod.shape[1] == 1:
        return pl.BlockSpec((None, 1, mod.shape[2]), lambda b, i, *_: (b, 0, 0))
    return pl.BlockSpec((None, tm, mod.shape[2]), lambda b, i, *_: (b, i, 0))


def _row_tile(t):
    return 512 if t % 512 == 0 else t


def _norm_mod_linear(x, gain, shift, scale, w, out_dtype=F32):
    b, t, d = x.shape
    n = w.shape[1]
    tm, tn = _row_tile(t), _pick_tn(n)

    def mod_spec(mod):
        if mod.shape[1] == 1:
            return pl.BlockSpec((None, 1, d), lambda j, b, i: (b, 0, 0))
        return pl.BlockSpec((None, tm, d), lambda j, b, i: (b, i, 0))

    return pl.pallas_call(
        _nml_kernel,
        grid=(n // tn, b, t // tm),
        in_specs=[pl.BlockSpec((None, tm, d), lambda j, b, i: (b, i, 0)),
                  pl.BlockSpec((1, d), lambda j, b, i: (0, 0)),
                  mod_spec(shift), mod_spec(scale),
                  pl.BlockSpec((d, tn), lambda j, b, i: (0, j))],
        out_specs=pl.BlockSpec((None, tm, tn), lambda j, b, i: (b, i, j)),
        out_shape=jax.ShapeDtypeStruct((b, t, n), out_dtype),
        compiler_params=pltpu.CompilerParams(
            dimension_semantics=("parallel", "parallel", "parallel"), vmem_limit_bytes=VMEM_LIMIT),
        name="norm_mod_linear",
    )(x, gain.reshape(1, d), shift, scale, w)


def _norm_mod_swiglu(x, gain, shift, scale, w_in):
    b, t, d = x.shape
    nf = w_in.shape[1] // 2
    tm, tn = _row_tile(t), _pick_tn(nf)
    nj = nf // tn
    return pl.pallas_call(
        _nml_swiglu_kernel,
        grid=(b, t // tm, nj),
        in_specs=[pl.BlockSpec((None, tm, d), lambda b, i, j: (b, i, 0)),
                  pl.BlockSpec((1, d), lambda b, i, j: (0, 0)),
                  _mod_spec(shift, tm), _mod_spec(scale, tm),
                  pl.BlockSpec((d, tn), lambda b, i, j: (0, j)),
                  pl.BlockSpec((d, tn), lambda b, i, j: (0, j + nj))],
        out_specs=pl.BlockSpec((None, tm, tn), lambda b, i, j: (b, i, j)),
        out_shape=jax.ShapeDtypeStruct((b, t, nf), BF16),
        scratch_shapes=[pltpu.VMEM((tm, d), BF16)],
        compiler_params=pltpu.CompilerParams(
            dimension_semantics=("parallel", "parallel", "arbitrary"), vmem_limit_bytes=VMEM_LIMIT),
        name="norm_mod_swiglu",
    )(x, gain.reshape(1, d), shift, scale, w_in, w_in)


def _rms_gated_residual(y, x, gate, gain):
    ms = jnp.mean(y * y, axis=-1, keepdims=True)
    return x + gate * (y * lax.rsqrt(ms + RMS_EPS) * gain)


def _mrr_kernel(a_ref, w_ref, x_ref, gate_ref, gain_ref, o_ref):
    y = jnp.dot(a_ref[...].astype(BF16), w_ref[...], preferred_element_type=F32)
    o_ref[...] = _rms_gated_residual(y, x_ref[...], gate_ref[...], gain_ref[...])


def _matmul_rms_residual(a, w, x, gate, gain):
    b, t, k = a.shape
    d = w.shape[1]
    tm = _row_tile(t)
    return pl.pallas_call(
        _mrr_kernel,
        grid=(b, t // tm),
        in_specs=[pl.BlockSpec((None, tm, k), lambda b, i: (b, i, 0)),
                  pl.BlockSpec((k, d), lambda b, i: (0, 0)),
                  pl.BlockSpec((None, tm, d), lambda b, i: (b, i, 0)),
                  _mod_spec(gate, tm),
                  pl.BlockSpec((1, d), lambda b, i: (0, 0))],
        out_specs=pl.BlockSpec((None, tm, d), lambda b, i: (b, i, 0)),
        out_shape=jax.ShapeDtypeStruct((b, t, d), F32),
        compiler_params=pltpu.CompilerParams(
            dimension_semantics=("parallel", "parallel"), vmem_limit_bytes=VMEM_LIMIT),
        name="matmul_rms_residual",
    )(a, w, x, gate, gain.reshape(1, d))


def _gdn_out_kernel(o_ref, z_ref, nw_ref, w_ref, x_ref, gate_ref, gain_ref, out_ref, a_ref):
    nw = nw_ref[...]
    for h in range(GDN_V_HEADS):
        sl = slice(h * GDN_HEAD_DIM, (h + 1) * GDN_HEAD_DIM)
        o = o_ref[:, sl]
        z = z_ref[:, sl]
        ms = jnp.mean(o * o, axis=-1, keepdims=True)
        a_ref[:, sl] = ((o * lax.rsqrt(ms + RMS_EPS) * nw) * (z * jax.nn.sigmoid(z))).astype(BF16)
    y = jnp.dot(a_ref[...], w_ref[...], preferred_element_type=F32)
    out_ref[...] = _rms_gated_residual(y, x_ref[...], gate_ref[...], gain_ref[...])


def _gdn_out(o, p, z_col_block, norm_w, w, x, gate, gain):
    b, t, k = o.shape
    d = w.shape[1]
    tm = _row_tile(t)
    return pl.pallas_call(
        _gdn_out_kernel,
        grid=(b, t // tm),
        in_specs=[pl.BlockSpec((None, tm, k), lambda b, i: (b, i, 0)),
                  pl.BlockSpec((None, tm, k), lambda b, i: (b, i, z_col_block)),
                  pl.BlockSpec((1, GDN_HEAD_DIM), lambda b, i: (0, 0)),
                  pl.BlockSpec((k, d), lambda b, i: (0, 0)),
                  pl.BlockSpec((None, tm, d), lambda b, i: (b, i, 0)),
                  _mod_spec(gate, tm),
                  pl.BlockSpec((1, d), lambda b, i: (0, 0))],
        out_specs=pl.BlockSpec((None, tm, d), lambda b, i: (b, i, 0)),
        out_shape=jax.ShapeDtypeStruct((b, t, d), F32),
        scratch_shapes=[pltpu.VMEM((tm, k), BF16)],
        compiler_params=pltpu.CompilerParams(
            dimension_semantics=("parallel", "parallel"), vmem_limit_bytes=VMEM_LIMIT),
        name="gdn_out",
    )(o, p, norm_w.reshape(1, GDN_HEAD_DIM), w, x, gate, gain.reshape(1, d))


def _bucket_of(n):
    big = jnp.full(n.shape, 16, jnp.int32)
    for thr in _BUCKET_THR:
        big = big + (n >= thr).astype(jnp.int32)
    return jnp.where(n < 16, n, big)


def _bias_tab_kernel(tbl_ref, tb_ref, lut_ref):
    h = pl.program_id(0)
    far = tbl_ref[N_BUCKETS - 1, h]

    def lookup(dist):
        bkt = _bucket_of(jnp.maximum(dist, 0))
        out = jnp.zeros(dist.shape, F32)
        for bb in range(N_BUCKETS):
            out = jnp.where(bkt == bb, tbl_ref[bb, h], out)
        return out - far

    qi = lax.broadcasted_iota(jnp.int32, (Q_BLOCK, 2 * Q_BLOCK), 0)
    kj = lax.broadcasted_iota(jnp.int32, (Q_BLOCK, 2 * Q_BLOCK), 1)
    dist = Q_BLOCK + qi - kj
    tb_ref[...] = jnp.where(dist >= 0, lookup(dist), NEG)
    lut_ref[...] = lookup(lax.broadcasted_iota(jnp.int32, (8, LANES), 1))


def _bias_tables(rel_bias):
    return pl.pallas_call(
        _bias_tab_kernel,
        grid=(NSA_HEADS,),
        in_specs=[pl.BlockSpec(memory_space=pltpu.SMEM)],
        out_specs=[pl.BlockSpec((None, Q_BLOCK, 2 * Q_BLOCK), lambda h: (h, 0, 0)),
                   pl.BlockSpec((None, 8, LANES), lambda h: (h, 0, 0))],
        out_shape=[jax.ShapeDtypeStruct((NSA_HEADS, Q_BLOCK, 2 * Q_BLOCK), F32),
                   jax.ShapeDtypeStruct((NSA_HEADS, 8, LANES), F32)],
        compiler_params=pltpu.CompilerParams(dimension_semantics=("parallel",)),
        name="bias_tables",
    )(rel_bias)


def _lut_gather(lut_rows, dist):
    idx = jnp.clip(dist, 0, LANES - 1)
    val = jnp.take_along_axis(lut_rows, idx, axis=1)
    return jnp.where((dist >= 0) & (dist < FAR_DIST), val, 0.0)


def _pe_term_kernel(pe_ref, wbd_ref, b1_ref, o_ref):
    y = jnp.dot(pe_ref[...], wbd_ref[...], preferred_element_type=F32)
    o_ref[...] = y[:, 0:CMP_HID] + y[:, 3 * CMP_HID:4 * CMP_HID] + b1_ref[...]


def _pe_term(pe_x, wbd, b1):
    return pl.pallas_call(
        _pe_term_kernel,
        grid=(2,),
        in_specs=[pl.BlockSpec((None, 8, 2048), lambda k: (k, 0, 0)),
                  pl.BlockSpec((None, 2048, 1024), lambda k: (k, 0, 0)),
                  pl.BlockSpec((None, 1, CMP_HID), lambda k: (k, 0, 0))],
        out_specs=pl.BlockSpec((None, 8, CMP_HID), lambda k: (k, 0, 0)),
        out_shape=jax.ShapeDtypeStruct((2, 8, CMP_HID), F32),
        compiler_params=pltpu.CompilerParams(dimension_semantics=("parallel",), vmem_limit_bytes=VMEM_LIMIT),
        name="cmp_pe_term",
    )(pe_x, wbd, b1.reshape(2, 1, CMP_HID))


def _compress_kernel(tab_ref, *refs, npg, rows_minor):
    del tab_ref
    pages = refs[:npg]
    wbd_ref, w2_ref, pe_ref, cc_ref, kc_ref, vc_ref, xs_ref, carry_ref = refs[npg:npg + 8]
    ts = SEGS_PER_PAGE * npg

    @pl.when(pl.program_id(1) == 0)
    def _():
        carry_ref[...] = jnp.zeros(carry_ref.shape, F32)

    row0 = lax.broadcasted_iota(jnp.int32, (ts, CMP_HID), 0) == 0
    low_half = lax.broadcasted_iota(jnp.int32, (ts, LANES), 1) < NSA_HEAD_DIM
    half = CMP_STRIDE * NSA_HEAD_DIM
    passes = [(k, gp) for k in range(2) for gp in range(2)]

    def transpose_pages(n):
        k, gp = passes[n]
        rt_ref = refs[npg + 8].at[n % 2]
        for j, pg in enumerate(pages):
            rt_ref[j * PAGE_SIZE:(j + 1) * PAGE_SIZE, :] = pg[k, gp].astype(BF16).T.astype(F32)

    def build_features(n):
        k, gp = passes[n]
        xb_ref = xs_ref.at[n % 2]
        if rows_minor:
            rt_ref = refs[npg + 8].at[n % 2]
            for j in range(CMP_STRIDE // 2):
                ra = rt_ref[pl.ds(2 * j, ts, stride=CMP_STRIDE), :]
                rb = rt_ref[pl.ds(2 * j + 1, ts, stride=CMP_STRIDE), :]
                xb_ref[:, j * LANES:(j + 1) * LANES] = jnp.where(
                    low_half, ra, pltpu.roll(rb, NSA_HEAD_DIM, axis=1)).astype(BF16)
                xb_ref[:, half + j * LANES:half + (j + 1) * LANES] = jnp.where(
                    low_half, pltpu.roll(ra, NSA_HEAD_DIM, axis=1), rb).astype(BF16)
        else:
            off = k * 256 + gp * LANES
            for s in range(CMP_STRIDE):
                lo = s * NSA_KV_W + off
                piece = jnp.concatenate([pg[:, lo:lo + LANES] for pg in pages], axis=0)
                xb_ref[:, s * LANES:(s + 1) * LANES] = piece.astype(BF16)

    def first_layer(n):
        k, _ = passes[n]
        xb_ref = xs_ref.at[n % 2]
        if rows_minor:
            return [jnp.dot(xb_ref[:, g2 * half:(g2 + 1) * half], wbd_ref[k], preferred_element_type=F32)
                    for g2 in range(2)]
        y = jnp.dot(xb_ref[...], wbd_ref[k], preferred_element_type=F32)
        return [y[:, :2 * CMP_HID], y[:, 2 * CMP_HID:]]

    def second_layer(n, ys):
        k, gp = passes[n]
        out_ref = kc_ref if k == 0 else vc_ref
        hs = []
        for g2 in range(2):
            pa = ys[g2][:, :CMP_HID]
            pb = ys[g2][:, CMP_HID:]
            ci = (k * 2 + gp) * 2 + g2
            prev = carry_ref[ci]
            pa_prev = jnp.where(row0, prev[7:8, :], pltpu.roll(pa, 1, axis=0))
            carry_ref[ci] = pa[ts - 8:ts, :]
            hs.append(jax.nn.gelu(pa_prev + pb + pe_ref[k, 0:1, :]))
        hid = jnp.concatenate(hs, axis=-1).astype(BF16)
        o = jnp.dot(hid, w2_ref[k], preferred_element_type=F32) + cc_ref[k]
        out_ref[2 * gp] = o[:, :LANES].astype(BF16)
        out_ref[2 * gp + 1] = o[:, LANES:].astype(BF16)

    if rows_minor:
        transpose_pages(0)
    build_features(0)
    if rows_minor:
        transpose_pages(1)
    for n in range(len(passes)):
        ys = first_layer(n)
        if n + 1 < len(passes):
            build_features(n + 1)
        if rows_minor and n + 2 < len(passes):
            transpose_pages(n + 2)
        second_layer(n, ys)


def _compress(pages, table, wbd, w1cat, w2bd, pe_term, ccols):
    bc, n_pages = table.shape
    npg = min(32, n_pages)
    ts = SEGS_PER_PAGE * npg
    nseg = n_pages * SEGS_PER_PAGE
    rows_minor = pages.ndim == 5
    w_first = w1cat if rows_minor else wbd
    page_block = (None,) + pages.shape[1:]

    def page_spec(j):
        return pl.BlockSpec(page_block, lambda b, i, tab: (tab[b, i * npg + j],) + (0,) * (pages.ndim - 1))

    const = lambda *shape: pl.BlockSpec(shape, lambda b, i, tab: (0,) * len(shape), pipeline_mode=pl.Buffered(1))
    out_spec = pl.BlockSpec((None, NSA_KV_GROUPS, ts, LANES), lambda b, i, tab: (b, 0, i, 0))
    grid_spec = pltpu.PrefetchScalarGridSpec(
        num_scalar_prefetch=1,
        grid=(bc, n_pages // npg),
        in_specs=[page_spec(j) for j in range(npg)] + [
            const(*w_first.shape), const(2, 512, 256), const(2, 8, CMP_HID), const(2, 1, 256)],
        out_specs=[out_spec, out_spec],
        scratch_shapes=[pltpu.VMEM((2, ts, 2048), BF16), pltpu.VMEM((8, 8, CMP_HID), F32)] + (
            [pltpu.VMEM((2, npg * PAGE_SIZE, LANES), F32)] if rows_minor else []),
    )
    out_sds = jax.ShapeDtypeStruct((bc, NSA_KV_GROUPS, nseg, LANES), BF16)
    return pl.pallas_call(
        functools.partial(_compress_kernel, npg=npg, rows_minor=rows_minor),
        grid_spec=grid_spec,
        out_shape=[out_sds, out_sds],
        compiler_params=pltpu.CompilerParams(
            dimension_semantics=("parallel", "arbitrary"), vmem_limit_bytes=VMEM_LIMIT),
        name="kv_compress",
    )(table, *([pages] * npg), w_first, w2bd, pe_term, ccols)


def _compress_weights(pe, w1, b1, w2):
    eye2 = jnp.eye(2, dtype=F32)
    w = w1.reshape(2, 2, CMP_STRIDE, NSA_HEAD_DIM, CMP_HID)
    wbd = jnp.einsum('kasdh,gj->ksgdjah', w, eye2).reshape(2, 2048, 1024).astype(BF16)
    w1cat = w.transpose(0, 2, 3, 1, 4).reshape(2, CMP_STRIDE * NSA_HEAD_DIM, 2 * CMP_HID).astype(BF16)
    w2p = jnp.pad(w2, ((0, 0), (0, 0), (0, LANES - NSA_HEAD_DIM)))
    w2bd = jnp.einsum('khd,gj->kghjd', w2p, eye2).reshape(2, 512, 256).astype(BF16)
    pe_x = pe.reshape(2, 2, CMP_STRIDE, NSA_HEAD_DIM).transpose(0, 2, 1, 3).reshape(2, 1, 2048)
    pe_x = jnp.broadcast_to(pe_x, (2, 8, 2048)).astype(BF16)
    pe_term = _pe_term(pe_x, wbd, b1)
    cc = np.zeros((2, 1, 256), np.float32)
    for g2 in range(2):
        cc[0, 0, g2 * LANES + 64] = 1.0
        cc[0, 0, g2 * LANES + 65] = 1.0
        cc[1, 0, g2 * LANES + 64] = 1.0
    return wbd, w1cat, w2bd, pe_term, jnp.asarray(cc)


def _topk_rows_mask(scores, k, fillers=(), preselected=None):
    scores = list(scores)
    shape = scores[0].shape
    blk = lax.broadcasted_iota(jnp.int32, shape, 0).astype(F32)
    sels = [jnp.zeros(shape, F32) if preselected is None else preselected.astype(F32) for _ in scores]
    fillers = list(fillers)
    for it in range(k):
        for n, score in enumerate(scores):
            mx = jnp.max(score, axis=0, keepdims=True)
            idx = jnp.min(jnp.where(score == mx, blk, 1e9), axis=0, keepdims=True)
            hit = blk == idx
            sels[n] = jnp.where(hit, 1.0, sels[n])
            scores[n] = jnp.where(hit, -jnp.inf, score)
        for f in fillers[it * len(fillers) // k:(it + 1) * len(fillers) // k]:
            f()
    return sels


def _nsa_prompt_kernel(q_ref, gt_ref, kc_ref, vc_ref, ks_ref, vs_ref, kw_ref, vw_ref, tb_ref, lut_ref, ovt_ref,
                       o_ref, qaug_ref, sc_ref, pcb_ref, pcs_ref, sw_ref, pw_ref, pfar_ref, *, nseg, n_sb, bb):
    g = pl.program_id(1)
    qb = pl.program_id(2)
    s0 = qb * Q_BLOCK
    rows = NSA_HPG * Q_BLOCK
    bis = range(bb)
    qs = [q_ref[bi].reshape(rows, LANES) for bi in bis]
    tb = tb_ref[...]

    for bi in bis:
        sc_ref[bi] = _nt(qs[bi], kc_ref[bi])
    qi128 = lax.broadcasted_iota(jnp.int32, (Q_BLOCK, LANES), 0)
    li128 = lax.broadcasted_iota(jnp.int32, (Q_BLOCK, LANES), 1)

    def add_near_bias(chunk):
        l0 = pl.multiple_of(chunk * LANES, LANES)
        dist = s0 + qi128 - CMP_STRIDE * (chunk * LANES + li128) - (CMP_STRIDE - 1)
        for hh in range(NSA_HPG):
            lut = jnp.broadcast_to(lut_ref[hh, 0:1, :], (Q_BLOCK, LANES))
            rs = slice(hh * Q_BLOCK, (hh + 1) * Q_BLOCK)
            corr = _lut_gather(lut, dist)
            for bi in bis:
                sc_ref[bi, rs, pl.ds(l0, LANES)] = sc_ref[bi, rs, pl.ds(l0, LANES)] + corr

    chunk_lo = jnp.maximum(8 * qb - 8, 0) // LANES
    chunk_hi = (8 * qb + 7) // LANES
    add_near_bias(chunk_lo)

    @pl.when(chunk_hi != chunk_lo)
    def _():
        add_near_bias(chunk_hi)

    rb_n = 32
    qi_b = lax.broadcasted_iota(jnp.int32, (rb_n, nseg), 0)
    mi_b = lax.broadcasted_iota(jnp.int32, (rb_n, nseg), 1)
    for rb in range(Q_BLOCK // rb_n):
        dist_b = s0 + rb * rb_n + qi_b - CMP_STRIDE * mi_b - (CMP_STRIDE - 1)
        valid_b = (dist_b >= 0) & (mi_b >= 1)
        for bi in bis:
            pcs_b = jnp.zeros((rb_n, nseg), F32)
            for hh in range(NSA_HPG):
                rs = slice(hh * Q_BLOCK + rb * rb_n, hh * Q_BLOCK + (rb + 1) * rb_n)
                s_b = jnp.where(valid_b, sc_ref[bi, rs, :], NEG)
                e = jnp.where(valid_b, jnp.exp(s_b - jnp.max(s_b, axis=-1, keepdims=True)), 0.0)
                ssum = jnp.sum(e, axis=-1, keepdims=True)
                pc = e * (1.0 / jnp.where(ssum > 0, ssum, 1.0))
                pcb_ref[bi, rs, :] = pc.astype(BF16)
                pcs_b = pcs_b + pc
            pcs_ref[bi, rb * rb_n:(rb + 1) * rb_n, :] = pcs_b
    o_cs = [jnp.dot(pcb_ref[bi], vc_ref[bi], preferred_element_type=F32) for bi in bis]

    ovt = ovt_ref[...]
    imp_ts = []
    for bi in bis:
        hi, lo = _split2(pcs_ref[bi])
        imp_ts.append(_nt(ovt, hi) + _nt(ovt, lo))
    jb = lax.broadcasted_iota(jnp.int32, (LANES, Q_BLOCK), 0)
    q_blk = (s0 + lax.broadcasted_iota(jnp.int32, (LANES, Q_BLOCK), 1)) // SLC_BLOCK
    forced = (jb == 0) | (jb == q_blk) | (jb == q_blk - 1)
    scores = []
    for bi in bis:
        score = jnp.where(forced, -jnp.inf, jnp.where(jb <= q_blk, imp_ts[bi], -1e4))
        scores.append(jnp.where(jb < n_sb, score, -3e38))
    n_forced = 3

    w0 = pl.multiple_of(s0, Q_BLOCK)
    n_w = WINDOW + Q_BLOCK
    for bi in bis:
        sw_ref[bi] = _nt(qs[bi], kw_ref[bi, pl.ds(w0, n_w), :])
    qi_w = lax.broadcasted_iota(jnp.int32, (rb_n, n_w), 0)
    kk_w = lax.broadcasted_iota(jnp.int32, (rb_n, n_w), 1)

    def window_block(bi, hh, rb):
        ok_w = (kk_w > qi_w + rb * rb_n) & (s0 + kk_w >= WINDOW)
        rs = slice(hh * Q_BLOCK + rb * rb_n, hh * Q_BLOCK + (rb + 1) * rb_n)
        s_b = jnp.where(ok_w, sw_ref[bi, rs, :], NEG)
        s_b = jnp.concatenate([s_b[:, :WINDOW - Q_BLOCK],
                               s_b[:, WINDOW - Q_BLOCK:] + tb_ref[hh, rb * rb_n:(rb + 1) * rb_n, :]], axis=-1)
        pw_ref[bi, rs, :] = jnp.exp(s_b - jnp.max(s_b, axis=-1, keepdims=True)).astype(BF16)

    gts = [jax.nn.sigmoid(gt_ref[bi]) for bi in bis]
    glane = lax.broadcasted_iota(jnp.int32, (Q_BLOCK, LANES), 1)
    gates = [[[None] * 3 for _ in range(NSA_HPG)] for _ in bis]

    def gate_column(bi, hh, br):
        col = 3 * (NSA_HPG * g + hh) + br
        gates[bi][hh][br] = jnp.sum(jnp.where(glane == col, gts[bi], 0.0), axis=-1, keepdims=True)

    fillers = []
    for hh in range(NSA_HPG):
        for bi in bis:
            fillers += [functools.partial(window_block, bi, hh, rb) for rb in range(Q_BLOCK // rb_n)]
            fillers += [functools.partial(gate_column, bi, hh, br) for br in range(3)]
    sel_ts = _topk_rows_mask(scores, min(N_SELECT, n_sb) - n_forced, fillers, preselected=forced)
    blk_lane = lax.broadcasted_iota(jnp.int32, (Q_BLOCK, LANES), 1)
    near_blk0 = 2 * qb - 2
    a0 = pl.multiple_of(jnp.maximum(s0 - Q_BLOCK, 0), Q_BLOCK)
    b0 = pl.multiple_of(s0, Q_BLOCK)
    kcol = lax.broadcasted_iota(jnp.int32, (Q_BLOCK, 2 * Q_BLOCK), 1)
    no_prev = jnp.where((kcol < Q_BLOCK) & (qb == 0), NEG, 0.0)
    kc_far = 4 * Q_BLOCK
    o_ws, unsel_fars, m_runs, accs = [], [], [], []
    for bi in bis:
        acc_w = jnp.dot(pw_ref[bi], vw_ref[bi, pl.ds(w0, n_w), :], preferred_element_type=F32)
        o_ws.append(acc_w * (1.0 / acc_w[:, NSA_HEAD_DIM:NSA_HEAD_DIM + 1]))
        unsel = 1.0 - sel_ts[bi].T
        unsel_fars.append(jnp.where(blk_lane >= near_blk0, 1.0, unsel))
        qaug_ref[bi, :, LANES:] = qs[bi]
        for hh in range(NSA_HPG):
            qaug_ref[bi, hh * Q_BLOCK:(hh + 1) * Q_BLOCK, :LANES] = unsel.astype(BF16)
    for bi in bis:
        kn = jnp.concatenate([ks_ref[bi, pl.ds(a0, Q_BLOCK), :], ks_ref[bi, pl.ds(b0, Q_BLOCK), :]], axis=0)
        vn = jnp.concatenate([vs_ref[bi, pl.ds(a0, Q_BLOCK), :], vs_ref[bi, pl.ds(b0, Q_BLOCK), :]], axis=0)
        s_n = _nt(qaug_ref[bi], kn).reshape(NSA_HPG, Q_BLOCK, 2 * Q_BLOCK) + (tb + no_prev[None])
        s_n = s_n.reshape(rows, 2 * Q_BLOCK)
        m_run = jnp.max(s_n, axis=-1, keepdims=True)
        m_runs.append(m_run)
        accs.append(jnp.dot(jnp.exp(s_n - m_run).astype(BF16), vn, preferred_element_type=F32))

    for bi in bis:
        for hh in range(NSA_HPG):
            qaug_ref[bi, hh * Q_BLOCK:(hh + 1) * Q_BLOCK, :LANES] = unsel_fars[bi].astype(BF16)
        pfar_ref[bi, 1] = jnp.zeros((rows, kc_far), BF16)
    n_far = (jnp.maximum(qb - 1, 0) + 3) // 4


    def far_logits(bi, c):
        return _nt(qaug_ref[bi], ks_ref[bi, pl.ds(pl.multiple_of(c * kc_far, kc_far), kc_far), :])

    def far_pv(bi, slot, c):
        k0 = pl.multiple_of(jnp.maximum(c, 0) * kc_far, kc_far)
        return jnp.dot(pfar_ref[bi, slot], vs_ref[bi, pl.ds(k0, kc_far), :], preferred_element_type=F32)

    def far_trip(t, carry):
        m_old, acc_old, alpha_prev = carry[0::3], carry[1::3], carry[2::3]
        s_a = [far_logits(bi, 2 * t) for bi in bis]
        acc_1 = [alpha_prev[bi] * acc_old[bi] + far_pv(bi, 1, 2 * t - 1) for bi in bis]
        s_b = [far_logits(bi, 2 * t + 1) for bi in bis]
        m_a = [jnp.maximum(m_old[bi], jnp.max(s_a[bi], axis=-1, keepdims=True)) for bi in bis]
        for bi in bis:
            pfar_ref[bi, 0] = jnp.exp((s_a[bi] - m_a[bi]).astype(BF16))
        acc_2 = [jnp.exp(m_old[bi] - m_a[bi]) * acc_1[bi] + far_pv(bi, 0, 2 * t) for bi in bis]
        m_b = [jnp.maximum(m_a[bi], jnp.max(s_b[bi], axis=-1, keepdims=True)) for bi in bis]
        for bi in bis:
            pfar_ref[bi, 1] = jnp.exp((s_b[bi] - m_b[bi]).astype(BF16))
        out = ()
        for bi in bis:
            out += (m_b[bi], acc_2[bi], jnp.exp(m_a[bi] - m_b[bi]))
        return out

    n_trips = (n_far + 1) // 2
    init = ()
    for bi in bis:
        init += (m_runs[bi], accs[bi], jnp.ones((rows, 1), F32))
    final = lax.fori_loop(0, n_trips, far_trip, init)

    low = lax.broadcasted_iota(jnp.int32, (Q_BLOCK, LANES), 1) < NSA_HEAD_DIM
    for bi in bis:
        acc = final[3 * bi + 2] * final[3 * bi + 1] + far_pv(bi, 1, 2 * n_trips - 1)
        o_s = acc * (1.0 / acc[:, NSA_HEAD_DIM:NSA_HEAD_DIM + 1])
        o_heads = []
        for hh in range(NSA_HPG):
            rs = slice(hh * Q_BLOCK, (hh + 1) * Q_BLOCK)
            o_heads.append(gates[bi][hh][0] * o_cs[bi][rs] + gates[bi][hh][1] * o_s[rs]
                           + gates[bi][hh][2] * o_ws[bi][rs])
        for pair in range(NSA_HPG // 2):
            o_ref[bi, :, pair * LANES:(pair + 1) * LANES] = jnp.where(
                low, o_heads[2 * pair], pltpu.roll(o_heads[2 * pair + 1], NSA_HEAD_DIM, axis=1)).astype(o_ref.dtype)


def _overlap_t(n_blk_pad, nseg):
    m = np.arange(nseg)[None, :]
    j = np.arange(n_blk_pad)[:, None]
    c_start = CMP_STRIDE * m - CMP_STRIDE
    c_end = CMP_STRIDE * m + CMP_STRIDE - 1
    ov = (c_start < j * SLC_BLOCK + SLC_BLOCK) & (c_end >= j * SLC_BLOCK) & (m >= 1)
    return ov.astype(np.float32)


def _nsa_prompt_attention(q128, p, gate_col_block, kc, vc, ks, vs, kw, vw, tb, lut):
    b, _, t, _ = q128.shape
    nseg = kc.shape[2]
    n_sb = t // SLC_BLOCK
    assert n_sb <= LANES and t % (4 * Q_BLOCK) == 0 and nseg % LANES == 0
    ovt = jnp.asarray(_overlap_t(LANES, nseg), dtype=BF16)
    bb = 2 if b % 2 == 0 else 1
    rows = NSA_HPG * Q_BLOCK
    per_bg = lambda nrow, cols: pl.BlockSpec((bb, None, nrow, cols), lambda b, g, i: (b, g, 0, 0),
                                             pipeline_mode=pl.Buffered(1))
    return pl.pallas_call(
        functools.partial(_nsa_prompt_kernel, nseg=nseg, n_sb=n_sb, bb=bb),
        grid=(b // bb, NSA_KV_GROUPS, t // Q_BLOCK),
        in_specs=[pl.BlockSpec((bb, NSA_HPG, Q_BLOCK, LANES), lambda b, g, i: (b, g, i, 0)),
                  pl.BlockSpec((bb, Q_BLOCK, LANES), lambda b, g, i: (b, i, gate_col_block)),
                  per_bg(nseg, LANES), per_bg(nseg, LANES),
                  per_bg(t, 2 * LANES), per_bg(t, LANES),
                  per_bg(t + WINDOW, LANES), per_bg(t + WINDOW, LANES),
                  pl.BlockSpec((NSA_HPG, Q_BLOCK, 2 * Q_BLOCK), lambda b, g, i: (g, 0, 0)),
                  pl.BlockSpec((NSA_HPG, 8, LANES), lambda b, g, i: (g, 0, 0)),
                  pl.BlockSpec((LANES, nseg), lambda b, g, i: (0, 0))],
        out_specs=pl.BlockSpec((bb, Q_BLOCK, NSA_HPG * NSA_HEAD_DIM), lambda b, g, i: (b, i, g)),
        out_shape=jax.ShapeDtypeStruct((b, t, NSA_Q_W), BF16),
        scratch_shapes=[pltpu.VMEM((bb, rows, 2 * LANES), BF16),
                        pltpu.VMEM((bb, rows, nseg), F32), pltpu.VMEM((bb, rows, nseg), BF16),
                        pltpu.VMEM((bb, Q_BLOCK, nseg), F32),
                        pltpu.VMEM((bb, rows, WINDOW + Q_BLOCK), F32),
                        pltpu.VMEM((bb, rows, WINDOW + Q_BLOCK), BF16),
                        pltpu.VMEM((bb, 2, rows, 4 * Q_BLOCK), BF16)],
        compiler_params=pltpu.CompilerParams(
            dimension_semantics=("parallel", "parallel", "arbitrary"), vmem_limit_bytes=VMEM_LIMIT),
        name="nsa_prompt_attention",
    )(q128, p, kc, vc, ks, vs, kw, vw, tb, lut, ovt)


def _gdn_conv_kernel(x_ref, w_ref, o_ref, carry_ref, *, tm, tc):
    j = pl.program_id(1)

    @pl.when(pl.program_id(2) == 0)
    def _():
        carry_ref[...] = jnp.zeros(carry_ref.shape, F32)

    x = x_ref[...]
    w = w_ref[...]
    prev = carry_ref[...]
    row8 = lax.broadcasted_iota(jnp.int32, (8, tc), 0)
    conv = x * w[CONV_W - 1:CONV_W, :]
    for sft in range(1, CONV_W):
        xs = pltpu.roll(x, sft, axis=0)
        top = jnp.where(row8 < sft, pltpu.roll(prev, sft, axis=0), xs[0:8])
        xs = top if tm == 8 else jnp.concatenate([top, xs[8:]], axis=0)
        conv = conv + xs * w[CONV_W - 1 - sft:CONV_W - sft, :]
    carry_ref[...] = x[tm - 8:tm, :]
    act = conv * jax.nn.sigmoid(conv)
    for hd in range(tc // GDN_HEAD_DIM):
        sl = slice(hd * GDN_HEAD_DIM, (hd + 1) * GDN_HEAD_DIM)
        a = act[:, sl]
        col0 = j * tc + hd * GDN_HEAD_DIM
        nrm = a * lax.rsqrt(jnp.sum(a * a, axis=-1, keepdims=True) + 1e-6)
        nrm = nrm * jnp.where(col0 < 1024, GDN_HEAD_DIM ** -0.5, 1.0)
        o_ref[:, sl] = jnp.where(col0 < 2048, nrm, a)


def _gdn_conv(p, conv_w):
    b, t, _ = p.shape
    tm = _row_tile(t)
    tc = 1024
    return pl.pallas_call(
        functools.partial(_gdn_conv_kernel, tm=tm, tc=tc),
        grid=(b, C_CONV // tc, t // tm),
        in_specs=[pl.BlockSpec((None, tm, tc), lambda b, j, i: (b, i, j)),
                  pl.BlockSpec((CONV_W, tc), lambda b, j, i: (0, j))],
        out_specs=pl.BlockSpec((None, tm, tc), lambda b, j, i: (b, i, j)),
        out_shape=jax.ShapeDtypeStruct((b, t, C_CONV), F32),
        scratch_shapes=[pltpu.VMEM((8, tc), F32)],
        compiler_params=pltpu.CompilerParams(
            dimension_semantics=("parallel", "parallel", "arbitrary"), vmem_limit_bytes=VMEM_LIMIT),
        name="gdn_conv",
    )(p, conv_w)


def _gdn_gate_kernel(ba_ref, alog_ref, dtb_ref, o_ref):
    x = ba_ref[...]
    y = x + dtb_ref[...]
    softplus = jnp.maximum(y, 0.0) + jnp.log1p(jnp.exp(-jnp.abs(y)))
    g = -jnp.exp(alog_ref[...]) * softplus
    lane = lax.broadcasted_iota(jnp.int32, x.shape, 1)
    o_ref[...] = jnp.where(lane < GDN_V_HEADS, jax.nn.sigmoid(x), g)


def _gdn_gates(p, ba_col_block, a_log, dt_bias):
    b, t, _ = p.shape
    tm = _row_tile(t)
    pad = lambda v: jnp.pad(v.reshape(1, GDN_V_HEADS), ((0, 0), (GDN_V_HEADS, LANES - 2 * GDN_V_HEADS)))
    return pl.pallas_call(
        _gdn_gate_kernel,
        grid=(b, t // tm),
        in_specs=[pl.BlockSpec((None, tm, LANES), lambda b, i: (b, i, ba_col_block)),
                  pl.BlockSpec((1, LANES), lambda b, i: (0, 0)),
                  pl.BlockSpec((1, LANES), lambda b, i: (0, 0))],
        out_specs=pl.BlockSpec((None, tm, LANES), lambda b, i: (b, i, 0)),
        out_shape=jax.ShapeDtypeStruct((b, t, LANES), F32),
        compiler_params=pltpu.CompilerParams(dimension_semantics=("parallel", "parallel")),
        name="gdn_gates",
    )(p, pad(a_log), pad(dt_bias))


def _bdot(a, b):
    return jnp.dot(a.astype(BF16), b.astype(BF16), preferred_element_type=F32)


GDN_PACK = 4
_PACK_ORDER = (0, 2, 1, 3)
_PACK_HEADS = tuple(GDN_PACK * p + o for p in range(GDN_V_HEADS // GDN_PACK) for o in _PACK_ORDER)


def _iota2(shape, axis):
    return lax.broadcasted_iota(jnp.int32, shape, axis)


def _packed_mm(a_cat, b_cat, bd_mask):
    b_bd = jnp.where(bd_mask, jnp.concatenate([b_cat] * GDN_PACK, axis=0), 0.0)
    return _bdot(a_cat, b_bd)


def _unit_lower_inverse_packed(ls, row, col, bd_mask):
    eye = (row == col).astype(F32)
    same16 = (row // 16) == (col // 16)
    same32 = (row // 32) == (col // 32)
    ms = [jnp.where(same16, -l, 0.0) for l in ls]
    ps = [eye + m for m in ms]
    for _ in range(3):
        ms = [_packed_mm(m, m, bd_mask) for m in ms]
        ps = [p + _packed_mm(p, m, bd_mask) for p, m in zip(ps, ms)]
    for level in (same32 & jnp.logical_not(same16), jnp.logical_not(same32)):
        ts = [_packed_mm(jnp.where(level, l, 0.0), p, bd_mask) for l, p in zip(ls, ps)]
        ps = [p - _packed_mm(p, t, bd_mask) for p, t in zip(ps, ts)]
    return ps


def _gdn_delta_kernel(act_ref, bg_ref, gt_ref, s0_ref, ltri_ref, lbd_ref, o_ref, s_ref, sbd_ref, *, bb):
    c, hd = GDN_CHUNK, GDN_HEAD_DIM
    n_packs = GDN_V_HEADS // GDN_PACK
    n_units = bb * n_packs
    n_pairs = GDN_V_HEADS // 2
    zero_hd = jnp.zeros((hd, hd), F32)

    @pl.when(pl.program_id(1) == 0)
    def _():
        for bi in range(bb):
            for pr in range(n_pairs):
                h0, h1 = _PACK_HEADS[2 * pr], _PACK_HEADS[2 * pr + 1]
                sbd_ref[bi * n_pairs + pr] = jnp.concatenate(
                    [jnp.concatenate([s0_ref[bi, h0], zero_hd], axis=-1),
                     jnp.concatenate([zero_hd, s0_ref[bi, h1]], axis=-1)], axis=0)

    bgs = [bg_ref[bi] for bi in range(bb)]
    cums = [sum(jnp.dot(ltri_ref[...], part, preferred_element_type=F32) for part in _split3(bg)) for bg in bgs]
    gcr_alls = [sum(_nt(part, lbd_ref[...]) for part in _split3(gt_ref[bi])) for bi in range(bb)]
    row = _iota2((c, GDN_PACK * c), 0)
    lane = _iota2((c, GDN_PACK * c), 1)
    col, slot = lane % c, lane // c
    incl, strict = row >= col, row > col
    bd_mask = (_iota2((4 * c, 4 * c), 0) // c) == (_iota2((4 * c, 4 * c), 1) // c)
    pair_mask = (_iota2((2 * hd, 2 * hd), 0) // hd) == (_iota2((2 * hd, 2 * hd), 1) // hd)
    k_mask = (_iota2((2 * hd, hd), 0) // hd) == (_iota2((2 * hd, hd), 1) // c)
    row_pair = _iota2((2 * hd, 1), 0)

    def slot_cat(cols):
        out = jnp.broadcast_to(cols[3], (c, GDN_PACK * c))
        for x in (2, 1, 0):
            out = jnp.where(slot == x, cols[x], out)
        return out

    def side_by_side(a, b):
        return jnp.concatenate([a, b], axis=-1)

    qs, ks, betas, gcs, lmats, a_ins = [], [], [], [], [], []
    for u in range(n_units):
        bi, p = divmod(u, n_packs)
        bg, cum, gcr_all = bgs[bi], cums[bi], gcr_alls[bi]
        heads = _PACK_HEADS[GDN_PACK * p:GDN_PACK * (p + 1)]
        qa, qb = (act_ref[bi, :, (2 * p + i) * hd:(2 * p + i + 1) * hd] for i in (0, 1))
        ka, kb = (act_ref[bi, :, 1024 + (2 * p + i) * hd:1024 + (2 * p + i + 1) * hd] for i in (0, 1))
        kt = jnp.concatenate([ka, kb], axis=0).T
        k_bd = jnp.where(k_mask, jnp.concatenate([kt, kt], axis=0), 0.0)
        kq = _bdot(jnp.concatenate([side_by_side(ka, kb), side_by_side(qa, qb)], axis=0), k_bd)
        kk = side_by_side(kq[:c], kq[:c])
        qk = side_by_side(kq[c:], kq[c:])
        beta = [bg[:, h:h + 1] for h in heads]
        gc = [cum[:, GDN_V_HEADS + h:GDN_V_HEADS + h + 1] for h in heads]
        decay = jnp.where(incl, jnp.exp(jnp.where(incl, slot_cat(gc) - gcr_all[p:p + 1, :], 0.0)), 0.0)
        lmats.append(jnp.where(strict, slot_cat(beta) * kk * decay, 0.0))
        a_ins.append(qk * decay)
        qs.append((qa, qb, qa, qb)); ks.append((ka, kb, ka, kb)); betas.append(beta); gcs.append(gc)

    tinvs = _unit_lower_inverse_packed(lmats, row, col, bd_mask)

    uws, egs = [], []
    for u in range(n_units):
        bi, p = divmod(u, n_packs)
        bands = []
        eg = [jnp.exp(g) for g in gcs[u]]
        for x in range(GDN_PACK):
            h = _PACK_HEADS[GDN_PACK * p + x]
            vh = act_ref[bi, :, 2048 + h * hd:2048 + (h + 1) * hd]
            rhs = betas[u][x] * side_by_side(vh, ks[u][x] * eg[x])
            pieces = [jnp.zeros((c, 2 * hd * x), F32)] * (x > 0) + [rhs] + [jnp.zeros((c, 2 * hd * (3 - x)), F32)] * (x < 3)
            bands.append(jnp.concatenate(pieces, axis=-1))
        uws.append(_bdot(tinvs[u], jnp.concatenate(bands, axis=0)))
        egs.append(eg)

    wss, s_olds = [], []
    for u in range(n_units):
        for pr in range(2):
            x0, x1 = 2 * pr, 2 * pr + 1
            w0, w1 = (uws[u][:, 2 * hd * x + hd:2 * hd * (x + 1)] for x in (x0, x1))
            lhs = jnp.concatenate([side_by_side(w0, w1),
                                   side_by_side(qs[u][x0] * egs[u][x0], qs[u][x1] * egs[u][x1])], axis=0)
            s_old = sbd_ref[2 * u + pr]
            s_olds.append(s_old)
            wss.append(_bdot(lhs, s_old))

    v_news = []
    for u in range(n_units):
        vn = []
        for x in range(GDN_PACK):
            ws = wss[2 * u + x // 2]
            vn.append(uws[u][:, 2 * hd * x:2 * hd * x + hd] - ws[:c, hd * (x % 2):hd * (x % 2 + 1)])
        v_news.append(vn)
    for u in range(n_units):
        bi, p = divmod(u, n_packs)
        bands = []
        for x in range(GDN_PACK):
            pieces = [jnp.zeros((c, hd * x), F32)] * (x > 0) + [v_news[u][x]] + [jnp.zeros((c, hd * (3 - x)), F32)] * (x < 3)
            bands.append(jnp.concatenate(pieces, axis=-1))
        av = _bdot(a_ins[u], jnp.concatenate(bands, axis=0))
        for x in range(GDN_PACK):
            h = _PACK_HEADS[GDN_PACK * p + x]
            ws = wss[2 * u + x // 2]
            o_ref[bi, :, h * hd:(h + 1) * hd] = ws[c:, hd * (x % 2):hd * (x % 2 + 1)] + av[:, hd * x:hd * (x + 1)]
    zrows = jnp.zeros((c, 2 * hd), F32)
    for u in range(n_units):
        for pr in range(2):
            x0, x1 = 2 * pr, 2 * pr + 1
            gl0, gl1 = gcs[u][x0][c - 1:c, :], gcs[u][x1][c - 1:c, :]
            kd = jnp.concatenate([side_by_side(ks[u][x0] * jnp.exp(gl0 - gcs[u][x0]),
                                               ks[u][x1] * jnp.exp(gl1 - gcs[u][x1])), zrows], axis=0)
            kd_t = jnp.concatenate([kd[:, :hd].T, kd[:, hd:].T], axis=0)
            vn = jnp.concatenate([side_by_side(v_news[u][x0], v_news[u][x1]), zrows], axis=0)
            d_last = jnp.where(row_pair < hd, jnp.exp(gl0), jnp.exp(gl1))
            sbd_ref[2 * u + pr] = jnp.where(pair_mask, s_olds[2 * u + pr] * d_last + _bdot(kd_t, vn), 0.0)

    @pl.when(pl.program_id(1) == pl.num_programs(1) - 1)
    def _():
        for bi in range(bb):
            for pr in range(n_pairs):
                s_pair = sbd_ref[bi * n_pairs + pr]
                s_ref[bi, _PACK_HEADS[2 * pr]] = s_pair[:hd, :hd]
                s_ref[bi, _PACK_HEADS[2 * pr + 1]] = s_pair[hd:, hd:]


def _gdn_delta(act, bg, s0):
    b, t, _ = act.shape
    nc = t // GDN_CHUNK
    n_packs = GDN_V_HEADS // GDN_PACK
    wp = GDN_PACK * GDN_CHUNK
    g_rows = bg[:, :, GDN_V_HEADS:2 * GDN_V_HEADS][:, :, np.asarray(_PACK_HEADS)]
    g_rows = g_rows.reshape(b, nc, GDN_CHUNK, n_packs, GDN_PACK).transpose(0, 1, 3, 4, 2).reshape(b, nc, n_packs, wp)
    g_rows = jnp.pad(g_rows, ((0, 0), (0, 0), (0, 8 - n_packs), (0, 0)))
    tri = np.tril(np.ones((GDN_CHUNK, GDN_CHUNK), np.float32))
    ltri = jnp.asarray(tri, dtype=BF16)
    lbd = jnp.asarray(np.kron(np.eye(GDN_PACK, dtype=np.float32), tri), dtype=BF16)
    bb = 2 if b % 2 == 0 else 1
    state_spec = pl.BlockSpec((bb, GDN_V_HEADS, GDN_HEAD_DIM, GDN_HEAD_DIM), lambda b, n: (b, 0, 0, 0))
    return pl.pallas_call(
        functools.partial(_gdn_delta_kernel, bb=bb),
        grid=(b // bb, nc),
        in_specs=[pl.BlockSpec((bb, GDN_CHUNK, C_CONV), lambda b, n: (b, n, 0)),
                  pl.BlockSpec((bb, GDN_CHUNK, LANES), lambda b, n: (b, n, 0)),
                  pl.BlockSpec((bb, None, 8, wp), lambda b, n: (b, n, 0, 0)),
                  state_spec,
                  pl.BlockSpec((GDN_CHUNK, GDN_CHUNK), lambda b, n: (0, 0)),
                  pl.BlockSpec((wp, wp), lambda b, n: (0, 0))],
        out_specs=[pl.BlockSpec((bb, GDN_CHUNK, GDN_V_W), lambda b, n: (b, n, 0)), state_spec],
        out_shape=[jax.ShapeDtypeStruct((b, t, GDN_V_W), F32),
                   jax.ShapeDtypeStruct(s0.shape, F32)],
        scratch_shapes=[pltpu.VMEM((bb * GDN_V_HEADS // 2, 2 * GDN_HEAD_DIM, 2 * GDN_HEAD_DIM), F32)],
        compiler_params=pltpu.CompilerParams(
            dimension_semantics=("parallel", "arbitrary"), vmem_limit_bytes=VMEM_LIMIT),
        name="gdn_delta_rule",
    )(act, bg, g_rows, s0, ltri, lbd)


SAMPLE_ROWS = NSA_HEADS * 4
SEL_PAGES_PER_STEP = 32


def _sample_cmp_kernel(q_ref, kc_ref, vc_ref, lut_ref, ov_ref, oc_ref, un_ref, *, nseg, past_len, n_sb, nq):
    rg = NSA_HPG * nq
    ri = lax.broadcasted_iota(jnp.int32, (rg, nseg), 0)
    mi = lax.broadcasted_iota(jnp.int32, (rg, nseg), 1)
    dist = past_len + ri % nq - CMP_STRIDE * mi - (CMP_STRIDE - 1)
    valid = (dist >= 0) & (mi >= 1)
    jl = lax.broadcasted_iota(jnp.int32, (8, un_ref.shape[-1]), 1)
    q_blk = (past_len + lax.broadcasted_iota(jnp.int32, jl.shape, 0) % nq) // SLC_BLOCK
    forced = (jl == 0) | (jl == q_blk) | (jl == q_blk - 1)
    jf = jl.astype(F32)
    pcs_parts = []
    for g in range(NSA_KV_GROUPS):
        sc = _nt(q_ref[g], kc_ref[g])
        tail = sc[:, nseg - LANES:] + _lut_gather(lut_ref[g], dist[:, nseg - LANES:])
        sc = jnp.where(valid, jnp.concatenate([sc[:, :nseg - LANES], tail], axis=-1), NEG)
        mx = jnp.max(sc, axis=-1, keepdims=True)
        e = jnp.where(valid, jnp.exp(sc - mx), 0.0)
        ssum = jnp.sum(e, axis=-1, keepdims=True)
        pc = e / jnp.where(ssum > 0, ssum, 1.0)
        oc_ref[g] = jnp.dot(pc.astype(BF16), vc_ref[g], preferred_element_type=F32)
        pcs = pc
        for hh in range(1, NSA_HPG):
            pcs = pcs + pltpu.roll(pc, hh * nq, axis=0)
        hi = pcs[0:8].astype(BF16).astype(F32)
        pcs_parts += [hi, pcs[0:8] - hi]
    imp_all = jnp.dot(jnp.concatenate(pcs_parts, axis=0).astype(BF16), ov_ref[...], preferred_element_type=F32)
    scores = []
    for g in range(NSA_KV_GROUPS):
        imp = imp_all[16 * g:16 * g + 8] + imp_all[16 * g + 8:16 * g + 16]
        score = jnp.where(forced, -jnp.inf, jnp.where(jl <= q_blk, imp, -1e4))
        scores.append(jnp.where(jl < n_sb, score, -3e38))
    sels = [forced.astype(F32)] * NSA_KV_GROUPS
    for _ in range(min(N_SELECT, n_sb) - 3):
        for g in range(NSA_KV_GROUPS):
            mxs = jnp.max(scores[g], axis=-1, keepdims=True)
            idx = jnp.min(jnp.where(scores[g] == mxs, jf, 1e9), axis=-1, keepdims=True)
            hit = jf == idx
            sels[g] = jnp.where(hit, 1.0, sels[g])
            scores[g] = jnp.where(hit, -jnp.inf, scores[g])
    for g in range(NSA_KV_GROUPS):
        un_ref[g] = 1.0 - sels[g]


def _sample_cmp(q16, kc, vc, lut16, past_len, nq):
    b = q16.shape[0]
    nseg = kc.shape[2]
    rg = NSA_HPG * nq
    n_sb = past_len // SLC_BLOCK + 1
    n_sb_pad = -(-n_sb // LANES) * LANES
    assert nq == 4 and nseg * CMP_STRIDE == past_len
    m = np.arange(nseg)[:, None]
    j = np.arange(n_sb_pad)[None, :]
    ov = ((CMP_STRIDE * m - CMP_STRIDE < j * SLC_BLOCK + SLC_BLOCK) & (CMP_STRIDE * m + CMP_STRIDE - 1 >= j * SLC_BLOCK)
          & (m >= 1) & (j < n_sb)).astype(np.float32)
    whole = lambda *shape: pl.BlockSpec((None,) + shape, lambda b: (b,) + (0,) * len(shape))
    return pl.pallas_call(
        functools.partial(_sample_cmp_kernel, nseg=nseg, past_len=past_len, n_sb=n_sb, nq=nq),
        grid=(b,),
        in_specs=[whole(NSA_KV_GROUPS, rg, LANES), whole(NSA_KV_GROUPS, nseg, LANES), whole(NSA_KV_GROUPS, nseg, LANES),
                  pl.BlockSpec((NSA_KV_GROUPS, rg, LANES), lambda b: (0, 0, 0)),
                  pl.BlockSpec((nseg, n_sb_pad), lambda b: (0, 0))],
        out_specs=[whole(NSA_KV_GROUPS, rg, LANES), whole(NSA_KV_GROUPS, 8, n_sb_pad)],
        out_shape=[jax.ShapeDtypeStruct((b, NSA_KV_GROUPS, rg, LANES), F32),
                   jax.ShapeDtypeStruct((b, NSA_KV_GROUPS, 8, n_sb_pad), F32)],
        compiler_params=pltpu.CompilerParams(dimension_semantics=("parallel",), vmem_limit_bytes=VMEM_LIMIT),
        name="nsa_sample_cmp_topk",
    )(q16, kc, vc, lut16, jnp.asarray(ov, dtype=BF16))


def _sample_sel_kernel(tab_ref, *refs, npg, past_len, nq):
    del tab_ref
    pages = refs[:npg]
    qbd_ref, un_ref, ee_ref, far_ref, lut_ref, m_ref, l_ref, acc_ref = refs[npg:]
    c = pl.program_id(1)
    kc = npg * PAGE_SIZE

    @pl.when(c == 0)
    def _():
        m_ref[...] = jnp.full(m_ref.shape, NEG, F32)
        l_ref[...] = jnp.zeros(l_ref.shape, F32)
        acc_ref[...] = jnp.zeros(acc_ref.shape, F32)

    kt = jnp.concatenate([pg[0] for pg in pages], axis=1).astype(BF16)
    vt = jnp.concatenate([pg[1] for pg in pages], axis=1).astype(BF16)
    s = (jnp.dot(qbd_ref[...], kt, preferred_element_type=F32) + far_ref[...][:, 0:1]
         + jnp.dot(un_ref[...], ee_ref[...], preferred_element_type=F32))
    ri = lax.broadcasted_iota(jnp.int32, (SAMPLE_ROWS, LANES), 0)
    li = lax.broadcasted_iota(jnp.int32, (SAMPLE_ROWS, LANES), 1)
    dist = past_len + ri % nq - (c * kc + kc - LANES + li)
    s = jnp.concatenate([s[:, :kc - LANES], s[:, kc - LANES:] + _lut_gather(lut_ref[...], dist)], axis=-1)
    m_old = m_ref[...][:, 0:1]
    m_new = jnp.maximum(m_old, jnp.max(s, axis=-1, keepdims=True))
    alpha = jnp.exp(m_old - m_new)
    p = jnp.exp(s - m_new)
    l_ref[...] = alpha * l_ref[...] + jnp.sum(p, axis=-1, keepdims=True)
    acc_ref[...] = alpha * acc_ref[...] + _nt(p.astype(BF16), vt)
    m_ref[...] = jnp.broadcast_to(m_new, m_ref.shape)


def _sample_sel(pages, table, qbd, unsel_c, farcol, lut64, past_len, nq):
    b, n_pages = table.shape
    npg = min(SEL_PAGES_PER_STEP, n_pages)
    kc = npg * PAGE_SIZE
    nch = n_pages // npg
    blk_per_chunk = kc // SLC_BLOCK
    ee = np.zeros((LANES, kc), np.float32)
    ee[np.arange(kc) // SLC_BLOCK, np.arange(kc)] = NEG
    assert blk_per_chunk <= LANES

    def page_spec(j):
        return pl.BlockSpec((None, 2, NSA_KV_W // 2, PAGE_SIZE), lambda b, c, tab: (tab[b, c * npg + j], 0, 0, 0))

    const = lambda *shape: pl.BlockSpec(shape, lambda b, c, tab: (0,) * len(shape))
    acc_spec = lambda cols: pl.BlockSpec((None, SAMPLE_ROWS, cols), lambda b, c, tab: (b, 0, 0))
    grid_spec = pltpu.PrefetchScalarGridSpec(
        num_scalar_prefetch=1,
        grid=(b, nch),
        in_specs=[page_spec(j) for j in range(npg)] + [
            pl.BlockSpec((None, SAMPLE_ROWS, 2 * LANES), lambda b, c, tab: (b, 0, 0)),
            pl.BlockSpec((None, None, SAMPLE_ROWS, LANES), lambda b, c, tab: (b, c, 0, 0)),
            const(LANES, kc), const(SAMPLE_ROWS, LANES), const(SAMPLE_ROWS, LANES)],
        out_specs=[acc_spec(LANES), acc_spec(LANES), acc_spec(2 * LANES)],
    )
    return pl.pallas_call(
        functools.partial(_sample_sel_kernel, npg=npg, past_len=past_len, nq=nq),
        grid_spec=grid_spec,
        out_shape=[jax.ShapeDtypeStruct((b, SAMPLE_ROWS, LANES), F32),
                   jax.ShapeDtypeStruct((b, SAMPLE_ROWS, LANES), F32),
                   jax.ShapeDtypeStruct((b, SAMPLE_ROWS, 2 * LANES), F32)],
        compiler_params=pltpu.CompilerParams(
            dimension_semantics=("parallel", "arbitrary"), vmem_limit_bytes=VMEM_LIMIT),
        name="nsa_sample_selected",
    )(table, *([pages] * npg), qbd, unsel_c, jnp.asarray(ee, dtype=BF16), farcol, lut64)


def _own_group_cols(x, grp):
    out = jnp.zeros((x.shape[0], NSA_HEAD_DIM), F32)
    for g in range(NSA_KV_GROUPS):
        out = jnp.where(grp == g, x[:, g * NSA_HEAD_DIM:(g + 1) * NSA_HEAD_DIM], out)
    return out


def _sample_final_kernel(qbd_ref, m_ref, l_ref, acc_ref, snew_ref, wc_ref, wnew_ref, oc_ref, gr_ref, far_ref, lut_ref,
                         o_ref, *, nq, w_buf):
    rows = SAMPLE_ROWS
    qbd = qbd_ref[...]
    far = far_ref[...][:, 0:1]
    lut = lut_ref[...]
    ri = lax.broadcasted_iota(jnp.int32, (rows, LANES), 0)
    li = lax.broadcasted_iota(jnp.int32, (rows, LANES), 1)
    tok = ri % nq
    grp = lax.broadcasted_iota(jnp.int32, (rows, NSA_HEAD_DIM), 0) // (NSA_HPG * nq)

    knew = snew_ref[...]
    s_new = _nt(qbd, knew[:, :256].astype(BF16)) + far
    d_new = tok - li
    s_new = jnp.where((d_new >= 0) & (li < nq), s_new + _lut_gather(lut, d_new), NEG)
    m_old = m_ref[...][:, 0:1]
    m_new = jnp.maximum(m_old, jnp.max(s_new, axis=-1, keepdims=True))
    alpha = jnp.exp(m_old - m_new)
    p_new = jnp.exp(s_new - m_new)
    l_s = alpha * l_ref[...][:, 0:1] + jnp.sum(p_new, axis=-1, keepdims=True)
    acc_s = alpha * acc_ref[...] + jnp.dot(p_new.astype(BF16), knew[:, 256:].astype(BF16), preferred_element_type=F32)
    o_s = _own_group_cols(acc_s, grp) / l_s

    kv_w = jnp.concatenate([wc_ref[...], wnew_ref[...]], axis=0)
    s_w = _nt(qbd, kv_w[:, :256].astype(BF16)) + far
    n_w = w_buf + LANES
    idx = lax.broadcasted_iota(jnp.int32, (rows, n_w), 1)
    d_w = w_buf + lax.broadcasted_iota(jnp.int32, (rows, n_w), 0) % nq - idx
    ok_w = (d_w >= 0) & (d_w < WINDOW) & (idx < w_buf + nq)
    corr = [jnp.zeros((rows, n_w - 2 * LANES), F32)]
    for cidx in range(2):
        lo = n_w - 2 * LANES + cidx * LANES
        corr.append(_lut_gather(lut, d_w[:, lo:lo + LANES]))
    s_w = jnp.where(ok_w, s_w + jnp.concatenate(corr, axis=-1), NEG)
    m_w = jnp.max(s_w, axis=-1, keepdims=True)
    p_w = jnp.exp(s_w - m_w)
    l_w = jnp.sum(p_w, axis=-1, keepdims=True)
    acc_w = jnp.dot(p_w.astype(BF16), kv_w[:, 256:].astype(BF16), preferred_element_type=F32)
    o_w = _own_group_cols(acc_w, grp) / l_w

    gt = jax.nn.sigmoid(gr_ref[...])
    o_ref[...] = gt[:, 0:1] * oc_ref[...][:, :NSA_HEAD_DIM] + gt[:, 1:2] * o_s + gt[:, 2:3] * o_w


def _sample_final(qbd, m, l, acc, snew, wcache, wnew, o_c, graw, farcol, lut64, nq):
    b = qbd.shape[0]
    w_buf = wcache.shape[1]
    assert w_buf == WINDOW
    whole = lambda *shape: pl.BlockSpec((None,) + shape, lambda b: (b,) + (0,) * len(shape))
    const = lambda *shape: pl.BlockSpec(shape, lambda b: (0,) * len(shape))
    return pl.pallas_call(
        functools.partial(_sample_final_kernel, nq=nq, w_buf=w_buf),
        grid=(b,),
        in_specs=[whole(SAMPLE_ROWS, 2 * LANES), whole(SAMPLE_ROWS, LANES), whole(SAMPLE_ROWS, LANES),
                  whole(SAMPLE_ROWS, 2 * LANES), whole(LANES, NSA_KV_W), whole(w_buf, NSA_KV_W), whole(LANES, NSA_KV_W),
                  whole(SAMPLE_ROWS, LANES), whole(SAMPLE_ROWS, LANES),
                  const(SAMPLE_ROWS, LANES), const(SAMPLE_ROWS, LANES)],
        out_specs=whole(SAMPLE_ROWS, NSA_HEAD_DIM),
        out_shape=jax.ShapeDtypeStruct((b, SAMPLE_ROWS, NSA_HEAD_DIM), F32),
        compiler_params=pltpu.CompilerParams(dimension_semantics=("parallel",), vmem_limit_bytes=VMEM_LIMIT),
        name="nsa_sample_final",
    )(qbd, m, l, acc, snew, wcache, wnew, o_c, graw, farcol, lut64)


def _ffn(x, mod, gains, w_in, w_out):
    hid = _norm_mod_swiglu(x, gains[2], mod[3], mod[4], w_in)
    return _matmul_rms_residual(hid, w_out, x, mod[5], gains[3])


def _nsa_layout_kernel(pq_ref, ps_ref, pw_ref, bias_ref, q_ref, ks_ref, vs_ref, kw_ref, vw_ref, *, tm):
    i = pl.program_id(1)
    lane = lax.broadcasted_iota(jnp.int32, (tm, LANES), 1)
    low = lane < NSA_HEAD_DIM

    def head_tile(ref, h):
        tile = ref[:, (h // 2) * LANES:(h // 2 + 1) * LANES]
        return pltpu.roll(tile, NSA_HEAD_DIM, axis=1) if h % 2 else tile

    for h in range(NSA_HEADS):
        q_ref[h] = jnp.where(low, head_tile(pq_ref, h), bias_ref[h]).astype(BF16)
    key = jnp.maximum(i - 1, 0) * tm + lax.broadcasted_iota(jnp.int32, (tm, LANES), 0)
    onehot = jnp.where(key // SLC_BLOCK == lane, NEG, 0.0).astype(BF16)
    k_ones = jnp.where((lane == NSA_HEAD_DIM) | (lane == NSA_HEAD_DIM + 1), 1.0, 0.0)
    v_ones = jnp.where(lane == NSA_HEAD_DIM, 1.0, 0.0)
    live = i > 0
    for g in range(NSA_KV_GROUPS):
        ks_ref[g, :, :LANES] = onehot
        ks_ref[g, :, LANES:] = jnp.where(low, head_tile(ps_ref, g), k_ones).astype(BF16)
        vs_ref[g] = jnp.where(low, head_tile(ps_ref, NSA_KV_GROUPS + g), v_ones).astype(BF16)
        kw_ref[g] = jnp.where(live, jnp.where(low, head_tile(pw_ref, g), k_ones), 0.0).astype(BF16)
        vw_ref[g] = jnp.where(live, jnp.where(low, head_tile(pw_ref, NSA_KV_GROUPS + g), v_ones), 0.0).astype(BF16)


def _nsa_layouts(p, bias_cols):
    b, t, _ = p.shape
    tm = WINDOW
    assert t % tm == 0
    src = lambda width, col_block: pl.BlockSpec((None, tm, width), lambda b, i: (b, jnp.maximum(i - 1, 0), col_block))
    same = lambda heads, width: pl.BlockSpec((None, heads, tm, width), lambda b, i: (b, 0, jnp.maximum(i - 1, 0), 0))
    late = pl.BlockSpec((None, NSA_KV_GROUPS, tm, LANES), lambda b, i: (b, 0, i, 0))
    bias = jnp.pad(bias_cols.astype(F32), ((0, 0), (NSA_HEAD_DIM, 0))).reshape(NSA_HEADS, 1, LANES)
    sds = lambda heads, rows, width: jax.ShapeDtypeStruct((b, heads, rows, width), BF16)
    return pl.pallas_call(
        functools.partial(_nsa_layout_kernel, tm=tm),
        grid=(b, t // tm + 1),
        in_specs=[src(NSA_Q_W, 0), src(NSA_KV_W, (NSA_Q_W + NSA_KV_W) // NSA_KV_W),
                  src(NSA_KV_W, (NSA_Q_W + 2 * NSA_KV_W) // NSA_KV_W),
                  pl.BlockSpec((NSA_HEADS, 1, LANES), lambda b, i: (0, 0, 0))],
        out_specs=[same(NSA_HEADS, LANES), same(NSA_KV_GROUPS, 2 * LANES), same(NSA_KV_GROUPS, LANES), late, late],
        out_shape=[sds(NSA_HEADS, t, LANES), sds(NSA_KV_GROUPS, t, 2 * LANES), sds(NSA_KV_GROUPS, t, LANES),
                   sds(NSA_KV_GROUPS, t + WINDOW, LANES), sds(NSA_KV_GROUPS, t + WINDOW, LANES)],
        compiler_params=pltpu.CompilerParams(
            dimension_semantics=("parallel", "arbitrary"), vmem_limit_bytes=VMEM_LIMIT),
        name="nsa_layouts",
    )(p, p, p, bias)


def _nsa_prompt(x, mod, gains, w_in, cmp_w, w_out, tb, lut, bias_cols):
    b, t, _ = x.shape
    p = _norm_mod_linear(x, gains[0], mod[0], mod[1], w_in)
    kvc, kvs, kvw = (p[..., 1024 + i * NSA_KV_W:1024 + (i + 1) * NSA_KV_W] for i in range(3))
    q128, ks, vs, kw, vw = _nsa_layouts(p, bias_cols)
    n_pages = t // PAGE_SIZE
    table = jnp.arange(b * n_pages, dtype=jnp.int32).reshape(b, n_pages)
    kc, vc = _compress(kvc.reshape(b * n_pages, SEGS_PER_PAGE, SEG_W), table, *cmp_w)
    o = _nsa_prompt_attention(q128, p, (NSA_Q_W + 3 * NSA_KV_W) // LANES, kc, vc, ks, vs, kw, vw, tb, lut)
    x = _matmul_rms_residual(o, w_out, x, mod[2], gains[1])
    shape5 = (b, t, 2, NSA_KV_GROUPS, NSA_HEAD_DIM)
    return x, kvc.reshape(shape5), kvs.reshape(shape5), kvw.reshape(shape5)[:, -min(WINDOW, t):]


def _nsa_sample(x, mod, gains, w_in, cmp_w, w_out, lut, rel_bias, bias_cols, cache_cmp, cache_slc, cache_win,
                page_table, db, nq):
    n_pages = page_table.shape[1]
    past_len = n_pages * PAGE_SIZE
    rows = db * nq
    p = _norm_mod_linear(x, gains[0], mod[0], mod[1], w_in)[0]
    kvc, kvs, kvw = (p[:, 1024 + i * NSA_KV_W:1024 + (i + 1) * NSA_KV_W] for i in range(3))
    qh = p[:, :NSA_Q_W].astype(BF16).reshape(db, nq, NSA_KV_GROUPS, NSA_HPG, NSA_HEAD_DIM).transpose(0, 2, 3, 1, 4)
    q16 = jnp.concatenate([qh, jnp.broadcast_to(bias_cols.reshape(1, NSA_KV_GROUPS, NSA_HPG, 1, NSA_HEAD_DIM), qh.shape)],
                          axis=-1).reshape(db, NSA_KV_GROUPS, NSA_HPG * nq, LANES)
    eye_g = jnp.eye(NSA_KV_GROUPS, dtype=BF16)
    qbd = jnp.einsum('bghtd,gj->bghtjd', qh, eye_g).reshape(db, SAMPLE_ROWS, NSA_KV_GROUPS * NSA_HEAD_DIM)
    row_head = np.repeat(np.arange(NSA_HEADS), nq)
    lut64 = lut[:, 0, :][row_head]
    farcol = jnp.broadcast_to(rel_bias[N_BUCKETS - 1][row_head][:, None], (SAMPLE_ROWS, LANES))
    rows_minor = lambda cache: jnp.transpose(cache, (0, 2, 3, 4, 1))
    kc, vc = _compress(rows_minor(cache_cmp).reshape(-1, 2, 2, LANES, PAGE_SIZE), page_table, *cmp_w)
    o_c, unsel = _sample_cmp(q16, kc, vc, lut64.reshape(NSA_KV_GROUPS, NSA_HPG * nq, LANES), past_len, nq)
    npg = min(SEL_PAGES_PER_STEP, n_pages)
    nch = n_pages // npg
    bpc = npg * PAGE_SIZE // SLC_BLOCK
    un = unsel[:, :, :nq, :past_len // SLC_BLOCK].reshape(db, NSA_KV_GROUPS, 1, nq, nch, bpc)
    un = jnp.broadcast_to(un, (db, NSA_KV_GROUPS, NSA_HPG, nq, nch, bpc)).transpose(0, 4, 1, 2, 3, 5)
    un = jnp.pad(un.reshape(db, nch, SAMPLE_ROWS, bpc), ((0, 0), (0, 0), (0, 0), (0, LANES - bpc))).astype(BF16)
    m, l, acc = _sample_sel(rows_minor(cache_slc).reshape(-1, 2, NSA_KV_W // 2, PAGE_SIZE), page_table, qbd, un, farcol,
                            lut64, past_len, nq)
    pad_rows = lambda a: jnp.pad(a.reshape(db, nq, NSA_KV_W), ((0, 0), (0, LANES - nq), (0, 0)))
    wcache = cache_win.reshape(db, -1, NSA_KV_W)
    graw = p[:, NSA_Q_W + 3 * NSA_KV_W:NSA_Q_W + 3 * NSA_KV_W + 3 * NSA_HEADS]
    graw = graw.reshape(db, nq, NSA_HEADS, 3).transpose(0, 2, 1, 3).reshape(db, SAMPLE_ROWS, 3)
    graw = jnp.pad(graw, ((0, 0), (0, 0), (0, LANES - 3)))
    o = _sample_final(qbd, m, l, acc, pad_rows(kvs), wcache, pad_rows(kvw), o_c.reshape(db, SAMPLE_ROWS, LANES), graw,
                      farcol, lut64, nq)
    o = o.reshape(db, NSA_HEADS, nq, NSA_HEAD_DIM).transpose(0, 2, 1, 3).reshape(1, rows, NSA_Q_W)
    x = _matmul_rms_residual(o, w_out, x, mod[2], gains[1])
    shape5 = (db, nq, 2, NSA_KV_GROUPS, NSA_HEAD_DIM)
    kv_win = jnp.concatenate([cache_win, kvw.reshape(shape5)], axis=1)[:, -cache_win.shape[1]:]
    return x, kvc.reshape(shape5), kvs.reshape(shape5), kv_win


def _gdn_prompt(x, mod, gains, w_in, conv_w, a_log, dt_bias, norm_w, w_out):
    b, t, _ = x.shape
    p = _norm_mod_linear(x, gains[0], mod[0], mod[1], w_in)
    act = _gdn_conv(p, conv_w)
    bg = _gdn_gates(p, (C_CONV + GDN_V_W) // LANES, a_log, dt_bias)
    s0 = jnp.zeros((b, GDN_V_HEADS, GDN_HEAD_DIM, GDN_HEAD_DIM), F32)
    o, s_fin = _gdn_delta(act, bg, s0)
    x = _gdn_out(o, p, C_CONV // GDN_V_W, norm_w, w_out, x, mod[2], gains[1])
    return x, p[:, t - (CONV_W - 1):, :C_CONV], s_fin


def _gdn_sample(x, mod, gains, w_in, conv_w, a_log, dt_bias, norm_w, w_out, conv_buf, s0, db, nq):
    p = _norm_mod_linear(x, gains[0], mod[0], mod[1], w_in)
    qkv = p[0, :, :C_CONV].reshape(db, nq, C_CONV)
    xp = jnp.concatenate([conv_buf, qkv], axis=1)
    act = _gdn_conv(jnp.pad(xp, ((0, 0), (0, 8 - xp.shape[1]), (0, 0))), conv_w)[:, CONV_W - 1:CONV_W - 1 + nq]
    bg = _gdn_gates(p, (C_CONV + GDN_V_W) // LANES, a_log, dt_bias).reshape(db, nq, LANES)
    pad_t = ((0, 0), (0, GDN_CHUNK - nq), (0, 0))
    o, s_fin = _gdn_delta(jnp.pad(act, pad_t), jnp.pad(bg, pad_t), s0)
    o = o[:, :nq].reshape(1, db * nq, GDN_V_W)
    x = _gdn_out(o, p, C_CONV // GDN_V_W, norm_w, w_out, x, mod[2], gains[1])
    return x, xp[:, -(CONV_W - 1):], s_fin


def kernel(x_prompt, x_sample, c_prompt, c_sample, cache_kv_cmp, cache_kv_slc, cache_kv_win, state_conv, state_ssm,
           page_table, rel_bias, norm_gains, w_ada, b_ada, w_ffn_in, w_ffn_out, nsa_w_in, nsa_cmp_pe, nsa_cmp_w1,
           nsa_cmp_b1, nsa_cmp_w2, nsa_w_out, gdn_w_in, gdn_conv_w, gdn_a_log, gdn_dt_bias, gdn_norm_w, gdn_w_out):
    depth = w_ada.shape[0]
    bp, t, d = x_prompt.shape
    db, nq, _ = x_sample.shape
    assert nq + CONV_W - 1 <= 8 and nq <= GDN_CHUNK

    c_all = jnp.concatenate([c_prompt, c_sample], axis=0)
    rows_pad = -(-c_all.shape[0] // 8) * 8
    ada = _adaln(jnp.pad(c_all, ((0, rows_pad - c_all.shape[0]), (0, 0))), w_ada, b_ada)
    ada = ada.reshape(depth, rows_pad, 6, d)
    tb, lut = _bias_tables(rel_bias)
    far_hi, far_lo = _split2(rel_bias[N_BUCKETS - 1])
    bias_cols = jnp.zeros((NSA_HEADS, NSA_HEAD_DIM), BF16).at[:, 0].set(far_hi).at[:, 1].set(far_lo)

    xp = x_prompt
    xs = x_sample.reshape(1, db * nq, d)
    kvc_p, kvc_s, kvs_p, kvs_s, kvw_p, kvw_s, cv_p, cv_s, ss_p, ss_s = ([] for _ in range(10))
    for i in range(depth):
        mod_p = [ada[i, :bp, k][:, None, :] for k in range(6)]
        mod_s = [jnp.repeat(ada[i, bp:bp + db, k], nq, axis=0)[None] for k in range(6)]
        gains = norm_gains[i]
        l = i // 2
        if i % 2 == 0:
            w_in = jnp.concatenate([nsa_w_in[l][:, :NSA_Q_W] * (NSA_HEAD_DIM ** -0.5), nsa_w_in[l][:, NSA_Q_W:]], axis=1)
            w_in = jnp.pad(w_in, ((0, 0), (0, -w_in.shape[1] % LANES))).astype(BF16)
            cmp_w = _compress_weights(nsa_cmp_pe[l], nsa_cmp_w1[l], nsa_cmp_b1[l], nsa_cmp_w2[l])
            w_out = nsa_w_out[l].astype(BF16)
            xp, a, bq, cq = _nsa_prompt(xp, mod_p, gains, w_in, cmp_w, w_out, tb, lut, bias_cols)
            kvc_p.append(a); kvs_p.append(bq); kvw_p.append(cq)
            xs, a, bq, cq = _nsa_sample(xs, mod_s, gains, w_in, cmp_w, w_out, lut, rel_bias, bias_cols, cache_kv_cmp[l],
                                        cache_kv_slc[l], cache_kv_win[l], page_table, db, nq)
            kvc_s.append(a); kvs_s.append(bq); kvw_s.append(cq)
        else:
            w_in = jnp.pad(gdn_w_in[l], ((0, 0), (0, -gdn_w_in.shape[2] % (5 * MXU_WIDTH)))).astype(BF16)
            gdn_w = (w_in, gdn_conv_w[l], gdn_a_log[l], gdn_dt_bias[l], gdn_norm_w[l], gdn_w_out[l].astype(BF16))
            xp, a, bq = _gdn_prompt(xp, mod_p, gains, *gdn_w)
            cv_p.append(a); ss_p.append(bq)
            xs, a, bq = _gdn_sample(xs, mod_s, gains, *gdn_w, state_conv[l], state_ssm[l], db, nq)
            cv_s.append(a); ss_s.append(bq)
        w_ffn = (w_ffn_in[i].astype(BF16), w_ffn_out[i].astype(BF16))
        xp = _ffn(xp, mod_p, gains, *w_ffn)
        xs = _ffn(xs, mod_s, gains, *w_ffn)
    return (xp, xs.reshape(db, nq, d), jnp.stack(kvc_p), jnp.stack(kvc_s), jnp.stack(kvs_p), jnp.stack(kvs_s),
            jnp.stack(kvw_p), jnp.stack(kvw_s), jnp.stack(cv_p), jnp.stack(cv_s), jnp.stack(ss_p), jnp.stack(ss_s))
```

```python
import functools
import math

import numpy as np
import jax
import jax.numpy as jnp
from jax import lax
from jax.experimental import pallas as pl
from jax.experimental.pallas import tpu as pltpu

F32 = jnp.float32
BF16 = jnp.bfloat16

D_MODEL = 1024
RMS_EPS = 1e-6
D_FF = 2816
NSA_HEADS = 16
NSA_HEAD_DIM = 64
NSA_KV_GROUPS = 4
NSA_HPG = 4
CMP_BLOCK = 32
CMP_STRIDE = 16
CMP_HID = 256
SLC_BLOCK = 64
N_SELECT = 16
WINDOW = 512
Q_BLOCK = 128
PAGE_SIZE = 128
N_BUCKETS = 32
GDN_QK_HEADS = 8
GDN_V_HEADS = 16
GDN_HEAD_DIM = 128
CONV_W = 4
GDN_CHUNK = 64
NSA_Q_W = 1024
NSA_KV_W = 512
C_CONV = 4096
GDN_V_W = 2048

LANES = 128
SEG_W = CMP_STRIDE * NSA_KV_W
SEGS_PER_PAGE = PAGE_SIZE // CMP_STRIDE
NEG = -1e30
VMEM_LIMIT = 48 * 1024 * 1024

_BUCKET_THR = (19, 21, 24, 27, 31, 35, 40, 46, 52, 59, 67, 77, 87, 99, 113)
FAR_DIST = 128


def _nt(a, b):
    return lax.dot_general(a, b, (((1,), (1,)), ((), ())), preferred_element_type=F32)


def _split2(x):
    hi = x.astype(BF16)
    lo = (x - hi.astype(F32)).astype(BF16)
    return hi, lo


def _split3(x):
    hi = x.astype(BF16)
    r = x - hi.astype(F32)
    mid = r.astype(BF16)
    lo = (r - mid.astype(F32)).astype(BF16)
    return hi, mid, lo


MXU_WIDTH = 256
MAX_TN = 2816


def _pick_tn(n):
    units = n // LANES
    cands = [d * LANES for d in range(1, units + 1) if units % d == 0 and d * LANES <= MAX_TN]
    full = [c for c in cands if c % MXU_WIDTH == 0]
    return max(full) if full and 2 * max(full) >= max(cands) else max(cands)


def _adaln_kernel(c_ref, w_ref, b_ref, o_ref):
    c = c_ref[...]
    a = (c * jax.nn.sigmoid(c)).astype(BF16)
    o_ref[...] = jnp.dot(a, w_ref[...].astype(BF16), preferred_element_type=F32) + b_ref[...]


def _adaln(c_all, w_ada, b_ada):
    depth, d, n = w_ada.shape
    rows = c_all.shape[0]
    tn = 768
    return pl.pallas_call(
        _adaln_kernel,
        grid=(depth, n // tn),
        in_specs=[pl.BlockSpec((rows, d), lambda l, j: (0, 0)),
                  pl.BlockSpec((None, d, tn), lambda l, j: (l, 0, j)),
                  pl.BlockSpec((None, 1, tn), lambda l, j: (l, 0, j))],
        out_specs=pl.BlockSpec((None, rows, tn), lambda l, j: (l, 0, j)),
        out_shape=jax.ShapeDtypeStruct((depth, rows, n), F32),
        compiler_params=pltpu.CompilerParams(dimension_semantics=("parallel", "parallel")),
        name="adaln",
    )(c_all, w_ada, b_ada.reshape(depth, 1, n))


def _mod_norm(x, gain, shift, scale):
    ms = jnp.mean(x * x, axis=-1, keepdims=True)
    y = x * lax.rsqrt(ms + RMS_EPS) * gain
    return y * (1.0 + scale) + shift


def _nml_kernel(x_ref, g_ref, sh_ref, sc_ref, w_ref, o_ref):
    h = _mod_norm(x_ref[...], g_ref[...], sh_ref[...], sc_ref[...]).astype(BF16)
    o_ref[...] = jnp.dot(h, w_ref[...], preferred_element_type=F32).astype(o_ref.dtype)


def _nml_swiglu_kernel(x_ref, g_ref, sh_ref, sc_ref, wg_ref, wu_ref, o_ref, h_ref):
    @pl.when(pl.program_id(2) == 0)
    def _():
        h_ref[...] = _mod_norm(x_ref[...], g_ref[...], sh_ref[...], sc_ref[...]).astype(BF16)

    h = h_ref[...]
    gate = jnp.dot(h, wg_ref[...], preferred_element_type=F32)
    up = jnp.dot(h, wu_ref[...], preferred_element_type=F32)
    o_ref[...] = (gate * jax.nn.sigmoid(gate) * up).astype(o_ref.dtype)


def _mod_spec(mod, tm):
    if mod.shape[1] == 1:
        return pl.BlockSpec((None, 1, mod.shape[2]), lambda b, i, *_: (b, 0, 0))
    return pl.BlockSpec((None, tm, mod.shape[2]), lambda b, i, *_: (b, i, 0))


def _row_tile(t):
    return 512 if t % 512 == 0 else t


def _norm_mod_linear(x, gain, shift, scale, w, out_dtype=F32):
    b, t, d = x.shape
    n = w.shape[1]
    tm, tn = _row_tile(t), _pick_tn(n)

    def mod_spec(mod):
        if mod.shape[1] == 1:
            return pl.BlockSpec((None, 1, d), lambda j, b, i: (b, 0, 0))
        return pl.BlockSpec((None, tm, d), lambda j, b, i: (b, i, 0))

    return pl.pallas_call(
        _nml_kernel,
        grid=(n // tn, b, t // tm),
        in_specs=[pl.BlockSpec((None, tm, d), lambda j, b, i: (b, i, 0)),
                  pl.BlockSpec((1, d), lambda j, b, i: (0, 0)),
                  mod_spec(shift), mod_spec(scale),
                  pl.BlockSpec((d, tn), lambda j, b, i: (0, j))],
        out_specs=pl.BlockSpec((None, tm, tn), lambda j, b, i: (b, i, j)),
        out_shape=jax.ShapeDtypeStruct((b, t, n), out_dtype),
        compiler_params=pltpu.CompilerParams(
            dimension_semantics=("parallel", "parallel", "parallel"), vmem_limit_bytes=VMEM_LIMIT),
        name="norm_mod_linear",
    )(x, gain.reshape(1, d), shift, scale, w)


def _norm_mod_swiglu(x, gain, shift, scale, w_in):
    b, t, d = x.shape
    nf = w_in.shape[1] // 2
    tm, tn = _row_tile(t), _pick_tn(nf)
    nj = nf // tn
    return pl.pallas_call(
        _nml_swiglu_kernel,
        grid=(b, t // tm, nj),
        in_specs=[pl.BlockSpec((None, tm, d), lambda b, i, j: (b, i, 0)),
                  pl.BlockSpec((1, d), lambda b, i, j: (0, 0)),
                  _mod_spec(shift, tm), _mod_spec(scale, tm),
                  pl.BlockSpec((d, tn), lambda b, i, j: (0, j)),
                  pl.BlockSpec((d, tn), lambda b, i, j: (0, j + nj))],
        out_specs=pl.BlockSpec((None, tm, tn), lambda b, i, j: (b, i, j)),
        out_shape=jax.ShapeDtypeStruct((b, t, nf), BF16),
        scratch_shapes=[pltpu.VMEM((tm, d), BF16)],
        compiler_params=pltpu.CompilerParams(
            dimension_semantics=("parallel", "parallel", "arbitrary"), vmem_limit_bytes=VMEM_LIMIT),
        name="norm_mod_swiglu",
    )(x, gain.reshape(1, d), shift, scale, w_in, w_in)


def _rms_gated_residual(y, x, gate, gain):
    ms = jnp.mean(y * y, axis=-1, keepdims=True)
    return x + gate * (y * lax.rsqrt(ms + RMS_EPS) * gain)


def _mrr_kernel(a_ref, w_ref, x_ref, gate_ref, gain_ref, o_ref):
    y = jnp.dot(a_ref[...].astype(BF16), w_ref[...], preferred_element_type=F32)
    o_ref[...] = _rms_gated_residual(y, x_ref[...], gate_ref[...], gain_ref[...])


def _matmul_rms_residual(a, w, x, gate, gain):
    b, t, k = a.shape
    d = w.shape[1]
    tm = _row_tile(t)
    return pl.pallas_call(
        _mrr_kernel,
        grid=(b, t // tm),
        in_specs=[pl.BlockSpec((None, tm, k), lambda b, i: (b, i, 0)),
                  pl.BlockSpec((k, d), lambda b, i: (0, 0)),
                  pl.BlockSpec((None, tm, d), lambda b, i: (b, i, 0)),
                  _mod_spec(gate, tm),
                  pl.BlockSpec((1, d), lambda b, i: (0, 0))],
        out_specs=pl.BlockSpec((None, tm, d), lambda b, i: (b, i, 0)),
        out_shape=jax.ShapeDtypeStruct((b, t, d), F32),
        compiler_params=pltpu.CompilerParams(
            dimension_semantics=("parallel", "parallel"), vmem_limit_bytes=VMEM_LIMIT),
        name="matmul_rms_residual",
    )(a, w, x, gate, gain.reshape(1, d))


def _gdn_out_kernel(o_ref, z_ref, nw_ref, w_ref, x_ref, gate_ref, gain_ref, out_ref, a_ref):
    nw = nw_ref[...]
    for h in range(GDN_V_HEADS):
        sl = slice(h * GDN_HEAD_DIM, (h + 1) * GDN_HEAD_DIM)
        o = o_ref[:, sl]
        z = z_ref[:, sl]
        ms = jnp.mean(o * o, axis=-1, keepdims=True)
        a_ref[:, sl] = ((o * lax.rsqrt(ms + RMS_EPS) * nw) * (z * jax.nn.sigmoid(z))).astype(BF16)
    y = jnp.dot(a_ref[...], w_ref[...], preferred_element_type=F32)
    out_ref[...] = _rms_gated_residual(y, x_ref[...], gate_ref[...], gain_ref[...])


def _gdn_out(o, p, z_col_block, norm_w, w, x, gate, gain):
    b, t, k = o.shape
    d = w.shape[1]
    tm = _row_tile(t)
    return pl.pallas_call(
        _gdn_out_kernel,
        grid=(b, t // tm),
        in_specs=[pl.BlockSpec((None, tm, k), lambda b, i: (b, i, 0)),
                  pl.BlockSpec((None, tm, k), lambda b, i: (b, i, z_col_block)),
                  pl.BlockSpec((1, GDN_HEAD_DIM), lambda b, i: (0, 0)),
                  pl.BlockSpec((k, d), lambda b, i: (0, 0)),
                  pl.BlockSpec((None, tm, d), lambda b, i: (b, i, 0)),
                  _mod_spec(gate, tm),
                  pl.BlockSpec((1, d), lambda b, i: (0, 0))],
        out_specs=pl.BlockSpec((None, tm, d), lambda b, i: (b, i, 0)),
        out_shape=jax.ShapeDtypeStruct((b, t, d), F32),
        scratch_shapes=[pltpu.VMEM((tm, k), BF16)],
        compiler_params=pltpu.CompilerParams(
            dimension_semantics=("parallel", "parallel"), vmem_limit_bytes=VMEM_LIMIT),
        name="gdn_out",
    )(o, p, norm_w.reshape(1, GDN_HEAD_DIM), w, x, gate, gain.reshape(1, d))


def _bucket_of(n):
    big = jnp.full(n.shape, 16, jnp.int32)
    for thr in _BUCKET_THR:
        big = big + (n >= thr).astype(jnp.int32)
    return jnp.where(n < 16, n, big)


def _bias_tab_kernel(tbl_ref, tb_ref, lut_ref):
    h = pl.program_id(0)
    far = tbl_ref[N_BUCKETS - 1, h]

    def lookup(dist):
        bkt = _bucket_of(jnp.maximum(dist, 0))
        out = jnp.zeros(dist.shape, F32)
        for bb in range(N_BUCKETS):
            out = jnp.where(bkt == bb, tbl_ref[bb, h], out)
        return out - far

    qi = lax.broadcasted_iota(jnp.int32, (Q_BLOCK, 2 * Q_BLOCK), 0)
    kj = lax.broadcasted_iota(jnp.int32, (Q_BLOCK, 2 * Q_BLOCK), 1)
    dist = Q_BLOCK + qi - kj
    tb_ref[...] = jnp.where(dist >= 0, lookup(dist), NEG)
    lut_ref[...] = lookup(lax.broadcasted_iota(jnp.int32, (8, LANES), 1))


def _bias_tables(rel_bias):
    return pl.pallas_call(
        _bias_tab_kernel,
        grid=(NSA_HEADS,),
        in_specs=[pl.BlockSpec(memory_space=pltpu.SMEM)],
        out_specs=[pl.BlockSpec((None, Q_BLOCK, 2 * Q_BLOCK), lambda h: (h, 0, 0)),
                   pl.BlockSpec((None, 8, LANES), lambda h: (h, 0, 0))],
        out_shape=[jax.ShapeDtypeStruct((NSA_HEADS, Q_BLOCK, 2 * Q_BLOCK), F32),
                   jax.ShapeDtypeStruct((NSA_HEADS, 8, LANES), F32)],
        compiler_params=pltpu.CompilerParams(dimension_semantics=("parallel",)),
        name="bias_tables",
    )(rel_bias)


def _lut_gather(lut_rows, dist):
    idx = jnp.clip(dist, 0, LANES - 1)
    val = jnp.take_along_axis(lut_rows, idx, axis=1)
    return jnp.where((dist >= 0) & (dist < FAR_DIST), val, 0.0)


def _pe_term_kernel(pe_ref, wbd_ref, b1_ref, o_ref):
    y = jnp.dot(pe_ref[...], wbd_ref[...], preferred_element_type=F32)
    o_ref[...] = y[:, 0:CMP_HID] + y[:, 3 * CMP_HID:4 * CMP_HID] + b1_ref[...]


def _pe_term(pe_x, wbd, b1):
    return pl.pallas_call(
        _pe_term_kernel,
        grid=(2,),
        in_specs=[pl.BlockSpec((None, 8, 2048), lambda k: (k, 0, 0)),
                  pl.BlockSpec((None, 2048, 1024), lambda k: (k, 0, 0)),
                  pl.BlockSpec((None, 1, CMP_HID), lambda k: (k, 0, 0))],
        out_specs=pl.BlockSpec((None, 8, CMP_HID), lambda k: (k, 0, 0)),
        out_shape=jax.ShapeDtypeStruct((2, 8, CMP_HID), F32),
        compiler_params=pltpu.CompilerParams(dimension_semantics=("parallel",), vmem_limit_bytes=VMEM_LIMIT),
        name="cmp_pe_term",
    )(pe_x, wbd, b1.reshape(2, 1, CMP_HID))


def _compress_kernel(tab_ref, *refs, npg, rows_minor):
    del tab_ref
    pages = refs[:npg]
    wbd_ref, w2_ref, pe_ref, cc_ref, kc_ref, vc_ref, xs_ref, carry_ref = refs[npg:npg + 8]
    ts = SEGS_PER_PAGE * npg

    @pl.when(pl.program_id(1) == 0)
    def _():
        carry_ref[...] = jnp.zeros(carry_ref.shape, F32)

    row0 = lax.broadcasted_iota(jnp.int32, (ts, CMP_HID), 0) == 0
    low_half = lax.broadcasted_iota(jnp.int32, (ts, LANES), 1) < NSA_HEAD_DIM
    half = CMP_STRIDE * NSA_HEAD_DIM
    passes = [(k, gp) for k in range(2) for gp in range(2)]

    def transpose_pages(n):
        k, gp = passes[n]
        rt_ref = refs[npg + 8].at[n % 2]
        for j, pg in enumerate(pages):
            rt_ref[j * PAGE_SIZE:(j + 1) * PAGE_SIZE, :] = pg[k, gp].astype(BF16).T.astype(F32)

    def build_features(n):
        k, gp = passes[n]
        xb_ref = xs_ref.at[n % 2]
        if rows_minor:
            rt_ref = refs[npg + 8].at[n % 2]
            for j in range(CMP_STRIDE // 2):
                ra = rt_ref[pl.ds(2 * j, ts, stride=CMP_STRIDE), :]
                rb = rt_ref[pl.ds(2 * j + 1, ts, stride=CMP_STRIDE), :]
                xb_ref[:, j * LANES:(j + 1) * LANES] = jnp.where(
                    low_half, ra, pltpu.roll(rb, NSA_HEAD_DIM, axis=1)).astype(BF16)
                xb_ref[:, half + j * LANES:half + (j + 1) * LANES] = jnp.where(
                    low_half, pltpu.roll(ra, NSA_HEAD_DIM, axis=1), rb).astype(BF16)
        else:
            off = k * 256 + gp * LANES
            for s in range(CMP_STRIDE):
                lo = s * NSA_KV_W + off
                piece = jnp.concatenate([pg[:, lo:lo + LANES] for pg in pages], axis=0)
                xb_ref[:, s * LANES:(s + 1) * LANES] = piece.astype(BF16)

    def first_layer(n):
        k, _ = passes[n]
        xb_ref = xs_ref.at[n % 2]
        if rows_minor:
            return [jnp.dot(xb_ref[:, g2 * half:(g2 + 1) * half], wbd_ref[k], preferred_element_type=F32)
                    for g2 in range(2)]
        y = jnp.dot(xb_ref[...], wbd_ref[k], preferred_element_type=F32)
        return [y[:, :2 * CMP_HID], y[:, 2 * CMP_HID:]]

    def second_layer(n, ys):
        k, gp = passes[n]
        out_ref = kc_ref if k == 0 else vc_ref
        hs = []
        for g2 in range(2):
            pa = ys[g2][:, :CMP_HID]
            pb = ys[g2][:, CMP_HID:]
            ci = (k * 2 + gp) * 2 + g2
            prev = carry_ref[ci]
            pa_prev = jnp.where(row0, prev[7:8, :], pltpu.roll(pa, 1, axis=0))
            carry_ref[ci] = pa[ts - 8:ts, :]
            hs.append(jax.nn.gelu(pa_prev + pb + pe_ref[k, 0:1, :]))
        hid = jnp.concatenate(hs, axis=-1).astype(BF16)
        o = jnp.dot(hid, w2_ref[k], preferred_element_type=F32) + cc_ref[k]
        out_ref[2 * gp] = o[:, :LANES].astype(BF16)
        out_ref[2 * gp + 1] = o[:, LANES:].astype(BF16)

    if rows_minor:
        transpose_pages(0)
    build_features(0)
    if rows_minor:
        transpose_pages(1)
    for n in range(len(passes)):
        ys = first_layer(n)
        if n + 1 < len(passes):
            build_features(n + 1)
        if rows_minor and n + 2 < len(passes):
            transpose_pages(n + 2)
        second_layer(n, ys)


def _compress(pages, table, wbd, w1cat, w2bd, pe_term, ccols):
    bc, n_pages = table.shape
    npg = min(32, n_pages)
    ts = SEGS_PER_PAGE * npg
    nseg = n_pages * SEGS_PER_PAGE
    rows_minor = pages.ndim == 5
    w_first = w1cat if rows_minor else wbd
    page_block = (None,) + pages.shape[1:]

    def page_spec(j):
        return pl.BlockSpec(page_block, lambda b, i, tab: (tab[b, i * npg + j],) + (0,) * (pages.ndim - 1))

    const = lambda *shape: pl.BlockSpec(shape, lambda b, i, tab: (0,) * len(shape), pipeline_mode=pl.Buffered(1))
    out_spec = pl.BlockSpec((None, NSA_KV_GROUPS, ts, LANES), lambda b, i, tab: (b, 0, i, 0))
    grid_spec = pltpu.PrefetchScalarGridSpec(
        num_scalar_prefetch=1,
        grid=(bc, n_pages // npg),
        in_specs=[page_spec(j) for j in range(npg)] + [
            const(*w_first.shape), const(2, 512, 256), const(2, 8, CMP_HID), const(2, 1, 256)],
        out_specs=[out_spec, out_spec],
        scratch_shapes=[pltpu.VMEM((2, ts, 2048), BF16), pltpu.VMEM((8, 8, CMP_HID), F32)] + (
            [pltpu.VMEM((2, npg * PAGE_SIZE, LANES), F32)] if rows_minor else []),
    )
    out_sds = jax.ShapeDtypeStruct((bc, NSA_KV_GROUPS, nseg, LANES), BF16)
    return pl.pallas_call(
        functools.partial(_compress_kernel, npg=npg, rows_minor=rows_minor),
        grid_spec=grid_spec,
        out_shape=[out_sds, out_sds],
        compiler_params=pltpu.CompilerParams(
            dimension_semantics=("parallel", "arbitrary"), vmem_limit_bytes=VMEM_LIMIT),
        name="kv_compress",
    )(table, *([pages] * npg), w_first, w2bd, pe_term, ccols)


def _compress_weights(pe, w1, b1, w2):
    eye2 = jnp.eye(2, dtype=F32)
    w = w1.reshape(2, 2, CMP_STRIDE, NSA_HEAD_DIM, CMP_HID)
    wbd = jnp.einsum('kasdh,gj->ksgdjah', w, eye2).reshape(2, 2048, 1024).astype(BF16)
    w1cat = w.transpose(0, 2, 3, 1, 4).reshape(2, CMP_STRIDE * NSA_HEAD_DIM, 2 * CMP_HID).astype(BF16)
    w2p = jnp.pad(w2, ((0, 0), (0, 0), (0, LANES - NSA_HEAD_DIM)))
    w2bd = jnp.einsum('khd,gj->kghjd', w2p, eye2).reshape(2, 512, 256).astype(BF16)
    pe_x = pe.reshape(2, 2, CMP_STRIDE, NSA_HEAD_DIM).transpose(0, 2, 1, 3).reshape(2, 1, 2048)
    pe_x = jnp.broadcast_to(pe_x, (2, 8, 2048)).astype(BF16)
    pe_term = _pe_term(pe_x, wbd, b1)
    cc = np.zeros((2, 1, 256), np.float32)
    for g2 in range(2):
        cc[0, 0, g2 * LANES + 64] = 1.0
        cc[0, 0, g2 * LANES + 65] = 1.0
        cc[1, 0, g2 * LANES + 64] = 1.0
    return wbd, w1cat, w2bd, pe_term, jnp.asarray(cc)


def _topk_rows_mask(scores, k, fillers=(), preselected=None):
    scores = list(scores)
    shape = scores[0].shape
    blk = lax.broadcasted_iota(jnp.int32, shape, 0).astype(F32)
    sels = [jnp.zeros(shape, F32) if preselected is None else preselected.astype(F32) for _ in scores]
    fillers = list(fillers)
    for it in range(k):
        for n, score in enumerate(scores):
            mx = jnp.max(score, axis=0, keepdims=True)
            idx = jnp.min(jnp.where(score == mx, blk, 1e9), axis=0, keepdims=True)
            hit = blk == idx
            sels[n] = jnp.where(hit, 1.0, sels[n])
            scores[n] = jnp.where(hit, -jnp.inf, score)
        for f in fillers[it * len(fillers) // k:(it + 1) * len(fillers) // k]:
            f()
    return sels


def _nsa_prompt_kernel(q_ref, gt_ref, kc_ref, vc_ref, ks_ref, vs_ref, kw_ref, vw_ref, tb_ref, lut_ref, ovt_ref,
                       o_ref, qaug_ref, sc_ref, pcb_ref, pcs_ref, sw_ref, pw_ref, pfar_ref, *, nseg, n_sb, bb):
    g = pl.program_id(1)
    qb = pl.program_id(2)
    s0 = qb * Q_BLOCK
    rows = NSA_HPG * Q_BLOCK
    bis = range(bb)
    qs = [q_ref[bi].reshape(rows, LANES) for bi in bis]
    tb = tb_ref[...]

    for bi in bis:
        sc_ref[bi] = _nt(qs[bi], kc_ref[bi])
    qi128 = lax.broadcasted_iota(jnp.int32, (Q_BLOCK, LANES), 0)
    li128 = lax.broadcasted_iota(jnp.int32, (Q_BLOCK, LANES), 1)

    def add_near_bias(chunk):
        l0 = pl.multiple_of(chunk * LANES, LANES)
        dist = s0 + qi128 - CMP_STRIDE * (chunk * LANES + li128) - (CMP_STRIDE - 1)
        for hh in range(NSA_HPG):
            lut = jnp.broadcast_to(lut_ref[hh, 0:1, :], (Q_BLOCK, LANES))
            rs = slice(hh * Q_BLOCK, (hh + 1) * Q_BLOCK)
            corr = _lut_gather(lut, dist)
            for bi in bis:
                sc_ref[bi, rs, pl.ds(l0, LANES)] = sc_ref[bi, rs, pl.ds(l0, LANES)] + corr

    chunk_lo = jnp.maximum(8 * qb - 8, 0) // LANES
    chunk_hi = (8 * qb + 7) // LANES
    add_near_bias(chunk_lo)

    @pl.when(chunk_hi != chunk_lo)
    def _():
        add_near_bias(chunk_hi)

    rb_n = 32
    qi_b = lax.broadcasted_iota(jnp.int32, (rb_n, nseg), 0)
    mi_b = lax.broadcasted_iota(jnp.int32, (rb_n, nseg), 1)
    for rb in range(Q_BLOCK // rb_n):
        dist_b = s0 + rb * rb_n + qi_b - CMP_STRIDE * mi_b - (CMP_STRIDE - 1)
        valid_b = (dist_b >= 0) & (mi_b >= 1)
        for bi in bis:
            pcs_b = jnp.zeros((rb_n, nseg), F32)
            for hh in range(NSA_HPG):
                rs = slice(hh * Q_BLOCK + rb * rb_n, hh * Q_BLOCK + (rb + 1) * rb_n)
                s_b = jnp.where(valid_b, sc_ref[bi, rs, :], NEG)
                e = jnp.where(valid_b, jnp.exp(s_b - jnp.max(s_b, axis=-1, keepdims=True)), 0.0)
                ssum = jnp.sum(e, axis=-1, keepdims=True)
                pc = e * (1.0 / jnp.where(ssum > 0, ssum, 1.0))
                pcb_ref[bi, rs, :] = pc.astype(BF16)
                pcs_b = pcs_b + pc
            pcs_ref[bi, rb * rb_n:(rb + 1) * rb_n, :] = pcs_b
    o_cs = [jnp.dot(pcb_ref[bi], vc_ref[bi], preferred_element_type=F32) for bi in bis]

    ovt = ovt_ref[...]
    imp_ts = []
    for bi in bis:
        hi, lo = _split2(pcs_ref[bi])
        imp_ts.append(_nt(ovt, hi) + _nt(ovt, lo))
    jb = lax.broadcasted_iota(jnp.int32, (LANES, Q_BLOCK), 0)
    q_blk = (s0 + lax.broadcasted_iota(jnp.int32, (LANES, Q_BLOCK), 1)) // SLC_BLOCK
    forced = (jb == 0) | (jb == q_blk) | (jb == q_blk - 1)
    scores = []
    for bi in bis:
        score = jnp.where(forced, -jnp.inf, jnp.where(jb <= q_blk, imp_ts[bi], -1e4))
        scores.append(jnp.where(jb < n_sb, score, -3e38))
    n_forced = 3

    w0 = pl.multiple_of(s0, Q_BLOCK)
    n_w = WINDOW + Q_BLOCK
    for bi in bis:
        sw_ref[bi] = _nt(qs[bi], kw_ref[bi, pl.ds(w0, n_w), :])
    qi_w = lax.broadcasted_iota(jnp.int32, (rb_n, n_w), 0)
    kk_w = lax.broadcasted_iota(jnp.int32, (rb_n, n_w), 1)

    def window_block(bi, hh, rb):
        ok_w = (kk_w > qi_w + rb * rb_n) & (s0 + kk_w >= WINDOW)
        rs = slice(hh * Q_BLOCK + rb * rb_n, hh * Q_BLOCK + (rb + 1) * rb_n)
        s_b = jnp.where(ok_w, sw_ref[bi, rs, :], NEG)
        s_b = jnp.concatenate([s_b[:, :WINDOW - Q_BLOCK],
                               s_b[:, WINDOW - Q_BLOCK:] + tb_ref[hh, rb * rb_n:(rb + 1) * rb_n, :]], axis=-1)
        pw_ref[bi, rs, :] = jnp.exp(s_b - jnp.max(s_b, axis=-1, keepdims=True)).astype(BF16)

    gts = [jax.nn.sigmoid(gt_ref[bi]) for bi in bis]
    glane = lax.broadcasted_iota(jnp.int32, (Q_BLOCK, LANES), 1)
    gates = [[[None] * 3 for _ in range(NSA_HPG)] for _ in bis]

    def gate_column(bi, hh, br):
        col = 3 * (NSA_HPG * g + hh) + br
        gates[bi][hh][br] = jnp.sum(jnp.where(glane == col, gts[bi], 0.0), axis=-1, keepdims=True)

    fillers = []
    for hh in range(NSA_HPG):
        for bi in bis:
            fillers += [functools.partial(window_block, bi, hh, rb) for rb in range(Q_BLOCK // rb_n)]
            fillers += [functools.partial(gate_column, bi, hh, br) for br in range(3)]
    sel_ts = _topk_rows_mask(scores, min(N_SELECT, n_sb) - n_forced, fillers, preselected=forced)
    blk_lane = lax.broadcasted_iota(jnp.int32, (Q_BLOCK, LANES), 1)
    near_blk0 = 2 * qb - 2
    a0 = pl.multiple_of(jnp.maximum(s0 - Q_BLOCK, 0), Q_BLOCK)
    b0 = pl.multiple_of(s0, Q_BLOCK)
    kcol = lax.broadcasted_iota(jnp.int32, (Q_BLOCK, 2 * Q_BLOCK), 1)
    no_prev = jnp.where((kcol < Q_BLOCK) & (qb == 0), NEG, 0.0)
    kc_far = 4 * Q_BLOCK
    o_ws, unsel_fars, m_runs, accs = [], [], [], []
    for bi in bis:
        acc_w = jnp.dot(pw_ref[bi], vw_ref[bi, pl.ds(w0, n_w), :], preferred_element_type=F32)
        o_ws.append(acc_w * (1.0 / acc_w[:, NSA_HEAD_DIM:NSA_HEAD_DIM + 1]))
        unsel = 1.0 - sel_ts[bi].T
        unsel_fars.append(jnp.where(blk_lane >= near_blk0, 1.0, unsel))
        qaug_ref[bi, :, LANES:] = qs[bi]
        for hh in range(NSA_HPG):
            qaug_ref[bi, hh * Q_BLOCK:(hh + 1) * Q_BLOCK, :LANES] = unsel.astype(BF16)
    for bi in bis:
        kn = jnp.concatenate([ks_ref[bi, pl.ds(a0, Q_BLOCK), :], ks_ref[bi, pl.ds(b0, Q_BLOCK), :]], axis=0)
        vn = jnp.concatenate([vs_ref[bi, pl.ds(a0, Q_BLOCK), :], vs_ref[bi, pl.ds(b0, Q_BLOCK), :]], axis=0)
        s_n = _nt(qaug_ref[bi], kn).reshape(NSA_HPG, Q_BLOCK, 2 * Q_BLOCK) + (tb + no_prev[None])
        s_n = s_n.reshape(rows, 2 * Q_BLOCK)
        m_run = jnp.max(s_n, axis=-1, keepdims=True)
        m_runs.append(m_run)
        accs.append(jnp.dot(jnp.exp(s_n - m_run).astype(BF16), vn, preferred_element_type=F32))

    for bi in bis:
        for hh in range(NSA_HPG):
            qaug_ref[bi, hh * Q_BLOCK:(hh + 1) * Q_BLOCK, :LANES] = unsel_fars[bi].astype(BF16)
        pfar_ref[bi, 1] = jnp.zeros((rows, kc_far), BF16)
    n_far = (jnp.maximum(qb - 1, 0) + 3) // 4


    def far_logits(bi, c):
        return _nt(qaug_ref[bi], ks_ref[bi, pl.ds(pl.multiple_of(c * kc_far, kc_far), kc_far), :])

    def far_pv(bi, slot, c):
        k0 = pl.multiple_of(jnp.maximum(c, 0) * kc_far, kc_far)
        return jnp.dot(pfar_ref[bi, slot], vs_ref[bi, pl.ds(k0, kc_far), :], preferred_element_type=F32)

    def far_trip(t, carry):
        m_old, acc_old, alpha_prev = carry[0::3], carry[1::3], carry[2::3]
        s_a = [far_logits(bi, 2 * t) for bi in bis]
        acc_1 = [alpha_prev[bi] * acc_old[bi] + far_pv(bi, 1, 2 * t - 1) for bi in bis]
        s_b = [far_logits(bi, 2 * t + 1) for bi in bis]
        m_a = [jnp.maximum(m_old[bi], jnp.max(s_a[bi], axis=-1, keepdims=True)) for bi in bis]
        for bi in bis:
            pfar_ref[bi, 0] = jnp.exp((s_a[bi] - m_a[bi]).astype(BF16))
        acc_2 = [jnp.exp(m_old[bi] - m_a[bi]) * acc_1[bi] + far_pv(bi, 0, 2 * t) for bi in bis]
        m_b = [jnp.maximum(m_a[bi], jnp.max(s_b[bi], axis=-1, keepdims=True)) for bi in bis]
        for bi in bis:
            pfar_ref[bi, 1] = jnp.exp((s_b[bi] - m_b[bi]).astype(BF16))
        out = ()
        for bi in bis:
            out += (m_b[bi], acc_2[bi], jnp.exp(m_a[bi] - m_b[bi]))
        return out

    n_trips = (n_far + 1) // 2
    init = ()
    for bi in bis:
        init += (m_runs[bi], accs[bi], jnp.ones((rows, 1), F32))
    final = lax.fori_loop(0, n_trips, far_trip, init)

    low = lax.broadcasted_iota(jnp.int32, (Q_BLOCK, LANES), 1) < NSA_HEAD_DIM
    for bi in bis:
        acc = final[3 * bi + 2] * final[3 * bi + 1] + far_pv(bi, 1, 2 * n_trips - 1)
        o_s = acc * (1.0 / acc[:, NSA_HEAD_DIM:NSA_HEAD_DIM + 1])
        o_heads = []
        for hh in range(NSA_HPG):
            rs = slice(hh * Q_BLOCK, (hh + 1) * Q_BLOCK)
            o_heads.append(gates[bi][hh][0] * o_cs[bi][rs] + gates[bi][hh][1] * o_s[rs]
                           + gates[bi][hh][2] * o_ws[bi][rs])
        for pair in range(NSA_HPG // 2):
            o_ref[bi, :, pair * LANES:(pair + 1) * LANES] = jnp.where(
                low, o_heads[2 * pair], pltpu.roll(o_heads[2 * pair + 1], NSA_HEAD_DIM, axis=1)).astype(o_ref.dtype)


def _overlap_t(n_blk_pad, nseg):
    m = np.arange(nseg)[None, :]
    j = np.arange(n_blk_pad)[:, None]
    c_start = CMP_STRIDE * m - CMP_STRIDE
    c_end = CMP_STRIDE * m + CMP_STRIDE - 1
    ov = (c_start < j * SLC_BLOCK + SLC_BLOCK) & (c_end >= j * SLC_BLOCK) & (m >= 1)
    return ov.astype(np.float32)


def _nsa_prompt_attention(q128, p, gate_col_block, kc, vc, ks, vs, kw, vw, tb, lut):
    b, _, t, _ = q128.shape
    nseg = kc.shape[2]
    n_sb = t // SLC_BLOCK
    assert n_sb <= LANES and t % (4 * Q_BLOCK) == 0 and nseg % LANES == 0
    ovt = jnp.asarray(_overlap_t(LANES, nseg), dtype=BF16)
    bb = 2 if b % 2 == 0 else 1
    rows = NSA_HPG * Q_BLOCK
    per_bg = lambda nrow, cols: pl.BlockSpec((bb, None, nrow, cols), lambda b, g, i: (b, g, 0, 0),
                                             pipeline_mode=pl.Buffered(1))
    return pl.pallas_call(
        functools.partial(_nsa_prompt_kernel, nseg=nseg, n_sb=n_sb, bb=bb),
        grid=(b // bb, NSA_KV_GROUPS, t // Q_BLOCK),
        in_specs=[pl.BlockSpec((bb, NSA_HPG, Q_BLOCK, LANES), lambda b, g, i: (b, g, i, 0)),
                  pl.BlockSpec((bb, Q_BLOCK, LANES), lambda b, g, i: (b, i, gate_col_block)),
                  per_bg(nseg, LANES), per_bg(nseg, LANES),
                  per_bg(t, 2 * LANES), per_bg(t, LANES),
                  per_bg(t + WINDOW, LANES), per_bg(t + WINDOW, LANES),
                  pl.BlockSpec((NSA_HPG, Q_BLOCK, 2 * Q_BLOCK), lambda b, g, i: (g, 0, 0)),
                  pl.BlockSpec((NSA_HPG, 8, LANES), lambda b, g, i: (g, 0, 0)),
                  pl.BlockSpec((LANES, nseg), lambda b, g, i: (0, 0))],
        out_specs=pl.BlockSpec((bb, Q_BLOCK, NSA_HPG * NSA_HEAD_DIM), lambda b, g, i: (b, i, g)),
        out_shape=jax.ShapeDtypeStruct((b, t, NSA_Q_W), BF16),
        scratch_shapes=[pltpu.VMEM((bb, rows, 2 * LANES), BF16),
                        pltpu.VMEM((bb, rows, nseg), F32), pltpu.VMEM((bb, rows, nseg), BF16),
                        pltpu.VMEM((bb, Q_BLOCK, nseg), F32),
                        pltpu.VMEM((bb, rows, WINDOW + Q_BLOCK), F32),
                        pltpu.VMEM((bb, rows, WINDOW + Q_BLOCK), BF16),
                        pltpu.VMEM((bb, 2, rows, 4 * Q_BLOCK), BF16)],
        compiler_params=pltpu.CompilerParams(
            dimension_semantics=("parallel", "parallel", "arbitrary"), vmem_limit_bytes=VMEM_LIMIT),
        name="nsa_prompt_attention",
    )(q128, p, kc, vc, ks, vs, kw, vw, tb, lut, ovt)


def _gdn_conv_kernel(x_ref, w_ref, o_ref, carry_ref, *, tm, tc):
    j = pl.program_id(1)

    @pl.when(pl.program_id(2) == 0)
    def _():
        carry_ref[...] = jnp.zeros(carry_ref.shape, F32)

    x = x_ref[...]
    w = w_ref[...]
    prev = carry_ref[...]
    row8 = lax.broadcasted_iota(jnp.int32, (8, tc), 0)
    conv = x * w[CONV_W - 1:CONV_W, :]
    for sft in range(1, CONV_W):
        xs = pltpu.roll(x, sft, axis=0)
        top = jnp.where(row8 < sft, pltpu.roll(prev, sft, axis=0), xs[0:8])
        xs = top if tm == 8 else jnp.concatenate([top, xs[8:]], axis=0)
        conv = conv + xs * w[CONV_W - 1 - sft:CONV_W - sft, :]
    carry_ref[...] = x[tm - 8:tm, :]
    act = conv * jax.nn.sigmoid(conv)
    for hd in range(tc // GDN_HEAD_DIM):
        sl = slice(hd * GDN_HEAD_DIM, (hd + 1) * GDN_HEAD_DIM)
        a = act[:, sl]
        col0 = j * tc + hd * GDN_HEAD_DIM
        nrm = a * lax.rsqrt(jnp.sum(a * a, axis=-1, keepdims=True) + 1e-6)
        nrm = nrm * jnp.where(col0 < 1024, GDN_HEAD_DIM ** -0.5, 1.0)
        o_ref[:, sl] = jnp.where(col0 < 2048, nrm, a)


def _gdn_conv(p, conv_w):
    b, t, _ = p.shape
    tm = _row_tile(t)
    tc = 1024
    return pl.pallas_call(
        functools.partial(_gdn_conv_kernel, tm=tm, tc=tc),
        grid=(b, C_CONV // tc, t // tm),
        in_specs=[pl.BlockSpec((None, tm, tc), lambda b, j, i: (b, i, j)),
                  pl.BlockSpec((CONV_W, tc), lambda b, j, i: (0, j))],
        out_specs=pl.BlockSpec((None, tm, tc), lambda b, j, i: (b, i, j)),
        out_shape=jax.ShapeDtypeStruct((b, t, C_CONV), F32),
        scratch_shapes=[pltpu.VMEM((8, tc), F32)],
        compiler_params=pltpu.CompilerParams(
            dimension_semantics=("parallel", "parallel", "arbitrary"), vmem_limit_bytes=VMEM_LIMIT),
        name="gdn_conv",
    )(p, conv_w)


def _gdn_gate_kernel(ba_ref, alog_ref, dtb_ref, o_ref):
    x = ba_ref[...]
    y = x + dtb_ref[...]
    softplus = jnp.maximum(y, 0.0) + jnp.log1p(jnp.exp(-jnp.abs(y)))
    g = -jnp.exp(alog_ref[...]) * softplus
    lane = lax.broadcasted_iota(jnp.int32, x.shape, 1)
    o_ref[...] = jnp.where(lane < GDN_V_HEADS, jax.nn.sigmoid(x), g)


def _gdn_gates(p, ba_col_block, a_log, dt_bias):
    b, t, _ = p.shape
    tm = _row_tile(t)
    pad = lambda v: jnp.pad(v.reshape(1, GDN_V_HEADS), ((0, 0), (GDN_V_HEADS, LANES - 2 * GDN_V_HEADS)))
    return pl.pallas_call(
        _gdn_gate_kernel,
        grid=(b, t // tm),
        in_specs=[pl.BlockSpec((None, tm, LANES), lambda b, i: (b, i, ba_col_block)),
                  pl.BlockSpec((1, LANES), lambda b, i: (0, 0)),
                  pl.BlockSpec((1, LANES), lambda b, i: (0, 0))],
        out_specs=pl.BlockSpec((None, tm, LANES), lambda b, i: (b, i, 0)),
        out_shape=jax.ShapeDtypeStruct((b, t, LANES), F32),
        compiler_params=pltpu.CompilerParams(dimension_semantics=("parallel", "parallel")),
        name="gdn_gates",
    )(p, pad(a_log), pad(dt_bias))


def _bdot(a, b):
    return jnp.dot(a.astype(BF16), b.astype(BF16), preferred_element_type=F32)


GDN_PACK = 4
_PACK_ORDER = (0, 2, 1, 3)
_PACK_HEADS = tuple(GDN_PACK * p + o for p in range(GDN_V_HEADS // GDN_PACK) for o in _PACK_ORDER)


def _iota2(shape, axis):
    return lax.broadcasted_iota(jnp.int32, shape, axis)


def _packed_mm(a_cat, b_cat, bd_mask):
    b_bd = jnp.where(bd_mask, jnp.concatenate([b_cat] * GDN_PACK, axis=0), 0.0)
    return _bdot(a_cat, b_bd)


def _unit_lower_inverse_packed(ls, row, col, bd_mask):
    eye = (row == col).astype(F32)
    same16 = (row // 16) == (col // 16)
    same32 = (row // 32) == (col // 32)
    ms = [jnp.where(same16, -l, 0.0) for l in ls]
    ps = [eye + m for m in ms]
    for _ in range(3):
        ms = [_packed_mm(m, m, bd_mask) for m in ms]
        ps = [p + _packed_mm(p, m, bd_mask) for p, m in zip(ps, ms)]
    for level in (same32 & jnp.logical_not(same16), jnp.logical_not(same32)):
        ts = [_packed_mm(jnp.where(level, l, 0.0), p, bd_mask) for l, p in zip(ls, ps)]
        ps = [p - _packed_mm(p, t, bd_mask) for p, t in zip(ps, ts)]
    return ps


def _gdn_delta_kernel(act_ref, bg_ref, gt_ref, s0_ref, ltri_ref, lbd_ref, o_ref, s_ref, sbd_ref, *, bb):
    c, hd = GDN_CHUNK, GDN_HEAD_DIM
    n_packs = GDN_V_HEADS // GDN_PACK
    n_units = bb * n_packs
    n_pairs = GDN_V_HEADS // 2
    zero_hd = jnp.zeros((hd, hd), F32)

    @pl.when(pl.program_id(1) == 0)
    def _():
        for bi in range(bb):
            for pr in range(n_pairs):
                h0, h1 = _PACK_HEADS[2 * pr], _PACK_HEADS[2 * pr + 1]
                sbd_ref[bi * n_pairs + pr] = jnp.concatenate(
                    [jnp.concatenate([s0_ref[bi, h0], zero_hd], axis=-1),
                     jnp.concatenate([zero_hd, s0_ref[bi, h1]], axis=-1)], axis=0)

    bgs = [bg_ref[bi] for bi in range(bb)]
    cums = [sum(jnp.dot(ltri_ref[...], part, preferred_element_type=F32) for part in _split3(bg)) for bg in bgs]
    gcr_alls = [sum(_nt(part, lbd_ref[...]) for part in _split3(gt_ref[bi])) for bi in range(bb)]
    row = _iota2((c, GDN_PACK * c), 0)
    lane = _iota2((c, GDN_PACK * c), 1)
    col, slot = lane % c, lane // c
    incl, strict = row >= col, row > col
    bd_mask = (_iota2((4 * c, 4 * c), 0) // c) == (_iota2((4 * c, 4 * c), 1) // c)
    pair_mask = (_iota2((2 * hd, 2 * hd), 0) // hd) == (_iota2((2 * hd, 2 * hd), 1) // hd)
    k_mask = (_iota2((2 * hd, hd), 0) // hd) == (_iota2((2 * hd, hd), 1) // c)
    row_pair = _iota2((2 * hd, 1), 0)

    def slot_cat(cols):
        out = jnp.broadcast_to(cols[3], (c, GDN_PACK * c))
        for x in (2, 1, 0):
            out = jnp.where(slot == x, cols[x], out)
        return out

    def side_by_side(a, b):
        return jnp.concatenate([a, b], axis=-1)

    qs, ks, betas, gcs, lmats, a_ins = [], [], [], [], [], []
    for u in range(n_units):
        bi, p = divmod(u, n_packs)
        bg, cum, gcr_all = bgs[bi], cums[bi], gcr_alls[bi]
        heads = _PACK_HEADS[GDN_PACK * p:GDN_PACK * (p + 1)]
        qa, qb = (act_ref[bi, :, (2 * p + i) * hd:(2 * p + i + 1) * hd] for i in (0, 1))
        ka, kb = (act_ref[bi, :, 1024 + (2 * p + i) * hd:1024 + (2 * p + i + 1) * hd] for i in (0, 1))
        kt = jnp.concatenate([ka, kb], axis=0).T
        k_bd = jnp.where(k_mask, jnp.concatenate([kt, kt], axis=0), 0.0)
        kq = _bdot(jnp.concatenate([side_by_side(ka, kb), side_by_side(qa, qb)], axis=0), k_bd)
        kk = side_by_side(kq[:c], kq[:c])
        qk = side_by_side(kq[c:], kq[c:])
        beta = [bg[:, h:h + 1] for h in heads]
        gc = [cum[:, GDN_V_HEADS + h:GDN_V_HEADS + h + 1] for h in heads]
        decay = jnp.where(incl, jnp.exp(jnp.where(incl, slot_cat(gc) - gcr_all[p:p + 1, :], 0.0)), 0.0)
        lmats.append(jnp.where(strict, slot_cat(beta) * kk * decay, 0.0))
        a_ins.append(qk * decay)
        qs.append((qa, qb, qa, qb)); ks.append((ka, kb, ka, kb)); betas.append(beta); gcs.append(gc)

    tinvs = _unit_lower_inverse_packed(lmats, row, col, bd_mask)

    uws, egs = [], []
    for u in range(n_units):
        bi, p = divmod(u, n_packs)
        bands = []
        eg = [jnp.exp(g) for g in gcs[u]]
        for x in range(GDN_PACK):
            h = _PACK_HEADS[GDN_PACK * p + x]
            vh = act_ref[bi, :, 2048 + h * hd:2048 + (h + 1) * hd]
            rhs = betas[u][x] * side_by_side(vh, ks[u][x] * eg[x])
            pieces = [jnp.zeros((c, 2 * hd * x), F32)] * (x > 0) + [rhs] + [jnp.zeros((c, 2 * hd * (3 - x)), F32)] * (x < 3)
            bands.append(jnp.concatenate(pieces, axis=-1))
        uws.append(_bdot(tinvs[u], jnp.concatenate(bands, axis=0)))
        egs.append(eg)

    wss, s_olds = [], []
    for u in range(n_units):
        for pr in range(2):
            x0, x1 = 2 * pr, 2 * pr + 1
            w0, w1 = (uws[u][:, 2 * hd * x + hd:2 * hd * (x + 1)] for x in (x0, x1))
            lhs = jnp.concatenate([side_by_side(w0, w1),
                                   side_by_side(qs[u][x0] * egs[u][x0], qs[u][x1] * egs[u][x1])], axis=0)
            s_old = sbd_ref[2 * u + pr]
            s_olds.append(s_old)
            wss.append(_bdot(lhs, s_old))

    v_news = []
    for u in range(n_units):
        vn = []
        for x in range(GDN_PACK):
            ws = wss[2 * u + x // 2]
            vn.append(uws[u][:, 2 * hd * x:2 * hd * x + hd] - ws[:c, hd * (x % 2):hd * (x % 2 + 1)])
        v_news.append(vn)
    for u in range(n_units):
        bi, p = divmod(u, n_packs)
        bands = []
        for x in range(GDN_PACK):
            pieces = [jnp.zeros((c, hd * x), F32)] * (x > 0) + [v_news[u][x]] + [jnp.zeros((c, hd * (3 - x)), F32)] * (x < 3)
            bands.append(jnp.concatenate(pieces, axis=-1))
        av = _bdot(a_ins[u], jnp.concatenate(bands, axis=0))
        for x in range(GDN_PACK):
            h = _PACK_HEADS[GDN_PACK * p + x]
            ws = wss[2 * u + x // 2]
            o_ref[bi, :, h * hd:(h + 1) * hd] = ws[c:, hd * (x % 2):hd * (x % 2 + 1)] + av[:, hd * x:hd * (x + 1)]
    zrows = jnp.zeros((c, 2 * hd), F32)
    for u in range(n_units):
        for pr in range(2):
            x0, x1 = 2 * pr, 2 * pr + 1
            gl0, gl1 = gcs[u][x0][c - 1:c, :], gcs[u][x1][c - 1:c, :]
            kd = jnp.concatenate([side_by_side(ks[u][x0] * jnp.exp(gl0 - gcs[u][x0]),
                                               ks[u][x1] * jnp.exp(gl1 - gcs[u][x1])), zrows], axis=0)
            kd_t = jnp.concatenate([kd[:, :hd].T, kd[:, hd:].T], axis=0)
            vn = jnp.concatenate([side_by_side(v_news[u][x0], v_news[u][x1]), zrows], axis=0)
            d_last = jnp.where(row_pair < hd, jnp.exp(gl0), jnp.exp(gl1))
            sbd_ref[2 * u + pr] = jnp.where(pair_mask, s_olds[2 * u + pr] * d_last + _bdot(kd_t, vn), 0.0)

    @pl.when(pl.program_id(1) == pl.num_programs(1) - 1)
    def _():
        for bi in range(bb):
            for pr in range(n_pairs):
                s_pair = sbd_ref[bi * n_pairs + pr]
                s_ref[bi, _PACK_HEADS[2 * pr]] = s_pair[:hd, :hd]
                s_ref[bi, _PACK_HEADS[2 * pr + 1]] = s_pair[hd:, hd:]


def _gdn_delta(act, bg, s0):
    b, t, _ = act.shape
    nc = t // GDN_CHUNK
    n_packs = GDN_V_HEADS // GDN_PACK
    wp = GDN_PACK * GDN_CHUNK
    g_rows = bg[:, :, GDN_V_HEADS:2 * GDN_V_HEADS][:, :, np.asarray(_PACK_HEADS)]
    g_rows = g_rows.reshape(b, nc, GDN_CHUNK, n_packs, GDN_PACK).transpose(0, 1, 3, 4, 2).reshape(b, nc, n_packs, wp)
    g_rows = jnp.pad(g_rows, ((0, 0), (0, 0), (0, 8 - n_packs), (0, 0)))
    tri = np.tril(np.ones((GDN_CHUNK, GDN_CHUNK), np.float32))
    ltri = jnp.asarray(tri, dtype=BF16)
    lbd = jnp.asarray(np.kron(np.eye(GDN_PACK, dtype=np.float32), tri), dtype=BF16)
    bb = 2 if b % 2 == 0 else 1
    state_spec = pl.BlockSpec((bb, GDN_V_HEADS, GDN_HEAD_DIM, GDN_HEAD_DIM), lambda b, n: (b, 0, 0, 0))
    return pl.pallas_call(
        functools.partial(_gdn_delta_kernel, bb=bb),
        grid=(b // bb, nc),
        in_specs=[pl.BlockSpec((bb, GDN_CHUNK, C_CONV), lambda b, n: (b, n, 0)),
                  pl.BlockSpec((bb, GDN_CHUNK, LANES), lambda b, n: (b, n, 0)),
                  pl.BlockSpec((bb, None, 8, wp), lambda b, n: (b, n, 0, 0)),
                  state_spec,
                  pl.BlockSpec((GDN_CHUNK, GDN_CHUNK), lambda b, n: (0, 0)),
                  pl.BlockSpec((wp, wp), lambda b, n: (0, 0))],
        out_specs=[pl.BlockSpec((bb, GDN_CHUNK, GDN_V_W), lambda b, n: (b, n, 0)), state_spec],
        out_shape=[jax.ShapeDtypeStruct((b, t, GDN_V_W), F32),
                   jax.ShapeDtypeStruct(s0.shape, F32)],
        scratch_shapes=[pltpu.VMEM((bb * GDN_V_HEADS // 2, 2 * GDN_HEAD_DIM, 2 * GDN_HEAD_DIM), F32)],
        compiler_params=pltpu.CompilerParams(
            dimension_semantics=("parallel", "arbitrary"), vmem_limit_bytes=VMEM_LIMIT),
        name="gdn_delta_rule",
    )(act, bg, g_rows, s0, ltri, lbd)


SAMPLE_ROWS = NSA_HEADS * 4
SEL_PAGES_PER_STEP = 32


def _sample_cmp_kernel(q_ref, kc_ref, vc_ref, lut_ref, ov_ref, oc_ref, un_ref, *, nseg, past_len, n_sb, nq):
    rg = NSA_HPG * nq
    ri = lax.broadcasted_iota(jnp.int32, (rg, nseg), 0)
    mi = lax.broadcasted_iota(jnp.int32, (rg, nseg), 1)
    dist = past_len + ri % nq - CMP_STRIDE * mi - (CMP_STRIDE - 1)
    valid = (dist >= 0) & (mi >= 1)
    jl = lax.broadcasted_iota(jnp.int32, (8, un_ref.shape[-1]), 1)
    q_blk = (past_len + lax.broadcasted_iota(jnp.int32, jl.shape, 0) % nq) // SLC_BLOCK
    forced = (jl == 0) | (jl == q_blk) | (jl == q_blk - 1)
    jf = jl.astype(F32)
    pcs_parts = []
    for g in range(NSA_KV_GROUPS):
        sc = _nt(q_ref[g], kc_ref[g])
        tail = sc[:, nseg - LANES:] + _lut_gather(lut_ref[g], dist[:, nseg - LANES:])
        sc = jnp.where(valid, jnp.concatenate([sc[:, :nseg - LANES], tail], axis=-1), NEG)
        mx = jnp.max(sc, axis=-1, keepdims=True)
        e = jnp.where(valid, jnp.exp(sc - mx), 0.0)
        ssum = jnp.sum(e, axis=-1, keepdims=True)
        pc = e / jnp.where(ssum > 0, ssum, 1.0)
        oc_ref[g] = jnp.dot(pc.astype(BF16), vc_ref[g], preferred_element_type=F32)
        pcs = pc
        for hh in range(1, NSA_HPG):
            pcs = pcs + pltpu.roll(pc, hh * nq, axis=0)
        hi = pcs[0:8].astype(BF16).astype(F32)
        pcs_parts += [hi, pcs[0:8] - hi]
    imp_all = jnp.dot(jnp.concatenate(pcs_parts, axis=0).astype(BF16), ov_ref[...], preferred_element_type=F32)
    scores = []
    for g in range(NSA_KV_GROUPS):
        imp = imp_all[16 * g:16 * g + 8] + imp_all[16 * g + 8:16 * g + 16]
        score = jnp.where(forced, -jnp.inf, jnp.where(jl <= q_blk, imp, -1e4))
        scores.append(jnp.where(jl < n_sb, score, -3e38))
    sels = [forced.astype(F32)] * NSA_KV_GROUPS
    for _ in range(min(N_SELECT, n_sb) - 3):
        for g in range(NSA_KV_GROUPS):
            mxs = jnp.max(scores[g], axis=-1, keepdims=True)
            idx = jnp.min(jnp.where(scores[g] == mxs, jf, 1e9), axis=-1, keepdims=True)
            hit = jf == idx
            sels[g] = jnp.where(hit, 1.0, sels[g])
            scores[g] = jnp.where(hit, -jnp.inf, scores[g])
    for g in range(NSA_KV_GROUPS):
        un_ref[g] = 1.0 - sels[g]


def _sample_cmp(q16, kc, vc, lut16, past_len, nq):
    b = q16.shape[0]
    nseg = kc.shape[2]
    rg = NSA_HPG * nq
    n_sb = past_len // SLC_BLOCK + 1
    n_sb_pad = -(-n_sb // LANES) * LANES
    assert nq == 4 and nseg * CMP_STRIDE == past_len
    m = np.arange(nseg)[:, None]
    j = np.arange(n_sb_pad)[None, :]
    ov = ((CMP_STRIDE * m - CMP_STRIDE < j * SLC_BLOCK + SLC_BLOCK) & (CMP_STRIDE * m + CMP_STRIDE - 1 >= j * SLC_BLOCK)
          & (m >= 1) & (j < n_sb)).astype(np.float32)
    whole = lambda *shape: pl.BlockSpec((None,) + shape, lambda b: (b,) + (0,) * len(shape))
    return pl.pallas_call(
        functools.partial(_sample_cmp_kernel, nseg=nseg, past_len=past_len, n_sb=n_sb, nq=nq),
        grid=(b,),
        in_specs=[whole(NSA_KV_GROUPS, rg, LANES), whole(NSA_KV_GROUPS, nseg, LANES), whole(NSA_KV_GROUPS, nseg, LANES),
                  pl.BlockSpec((NSA_KV_GROUPS, rg, LANES), lambda b: (0, 0, 0)),
                  pl.BlockSpec((nseg, n_sb_pad), lambda b: (0, 0))],
        out_specs=[whole(NSA_KV_GROUPS, rg, LANES), whole(NSA_KV_GROUPS, 8, n_sb_pad)],
        out_shape=[jax.ShapeDtypeStruct((b, NSA_KV_GROUPS, rg, LANES), F32),
                   jax.ShapeDtypeStruct((b, NSA_KV_GROUPS, 8, n_sb_pad), F32)],
        compiler_params=pltpu.CompilerParams(dimension_semantics=("parallel",), vmem_limit_bytes=VMEM_LIMIT),
        name="nsa_sample_cmp_topk",
    )(q16, kc, vc, lut16, jnp.asarray(ov, dtype=BF16))


def _sample_sel_kernel(tab_ref, *refs, npg, past_len, nq):
    del tab_ref
    pages = refs[:npg]
    qbd_ref, un_ref, ee_ref, far_ref, lut_ref, m_ref, l_ref, acc_ref = refs[npg:]
    c = pl.program_id(1)
    kc = npg * PAGE_SIZE

    @pl.when(c == 0)
    def _():
        m_ref[...] = jnp.full(m_ref.shape, NEG, F32)
        l_ref[...] = jnp.zeros(l_ref.shape, F32)
        acc_ref[...] = jnp.zeros(acc_ref.shape, F32)

    kt = jnp.concatenate([pg[0] for pg in pages], axis=1).astype(BF16)
    vt = jnp.concatenate([pg[1] for pg in pages], axis=1).astype(BF16)
    s = (jnp.dot(qbd_ref[...], kt, preferred_element_type=F32) + far_ref[...][:, 0:1]
         + jnp.dot(un_ref[...], ee_ref[...], preferred_element_type=F32))
    ri = lax.broadcasted_iota(jnp.int32, (SAMPLE_ROWS, LANES), 0)
    li = lax.broadcasted_iota(jnp.int32, (SAMPLE_ROWS, LANES), 1)
    dist = past_len + ri % nq - (c * kc + kc - LANES + li)
    s = jnp.concatenate([s[:, :kc - LANES], s[:, kc - LANES:] + _lut_gather(lut_ref[...], dist)], axis=-1)
    m_old = m_ref[...][:, 0:1]
    m_new = jnp.maximum(m_old, jnp.max(s, axis=-1, keepdims=True))
    alpha = jnp.exp(m_old - m_new)
    p = jnp.exp(s - m_new)
    l_ref[...] = alpha * l_ref[...] + jnp.sum(p, axis=-1, keepdims=True)
    acc_ref[...] = alpha * acc_ref[...] + _nt(p.astype(BF16), vt)
    m_ref[...] = jnp.broadcast_to(m_new, m_ref.shape)


def _sample_sel(pages, table, qbd, unsel_c, farcol, lut64, past_len, nq):
    b, n_pages = table.shape
    npg = min(SEL_PAGES_PER_STEP, n_pages)
    kc = npg * PAGE_SIZE
    nch = n_pages // npg
    blk_per_chunk = kc // SLC_BLOCK
    ee = np.zeros((LANES, kc), np.float32)
    ee[np.arange(kc) // SLC_BLOCK, np.arange(kc)] = NEG
    assert blk_per_chunk <= LANES

    def page_spec(j):
        return pl.BlockSpec((None, 2, NSA_KV_W // 2, PAGE_SIZE), lambda b, c, tab: (tab[b, c * npg + j], 0, 0, 0))

    const = lambda *shape: pl.BlockSpec(shape, lambda b, c, tab: (0,) * len(shape))
    acc_spec = lambda cols: pl.BlockSpec((None, SAMPLE_ROWS, cols), lambda b, c, tab: (b, 0, 0))
    grid_spec = pltpu.PrefetchScalarGridSpec(
        num_scalar_prefetch=1,
        grid=(b, nch),
        in_specs=[page_spec(j) for j in range(npg)] + [
            pl.BlockSpec((None, SAMPLE_ROWS, 2 * LANES), lambda b, c, tab: (b, 0, 0)),
            pl.BlockSpec((None, None, SAMPLE_ROWS, LANES), lambda b, c, tab: (b, c, 0, 0)),
            const(LANES, kc), const(SAMPLE_ROWS, LANES), const(SAMPLE_ROWS, LANES)],
        out_specs=[acc_spec(LANES), acc_spec(LANES), acc_spec(2 * LANES)],
    )
    return pl.pallas_call(
        functools.partial(_sample_sel_kernel, npg=npg, past_len=past_len, nq=nq),
        grid_spec=grid_spec,
        out_shape=[jax.ShapeDtypeStruct((b, SAMPLE_ROWS, LANES), F32),
                   jax.ShapeDtypeStruct((b, SAMPLE_ROWS, LANES), F32),
                   jax.ShapeDtypeStruct((b, SAMPLE_ROWS, 2 * LANES), F32)],
        compiler_params=pltpu.CompilerParams(
            dimension_semantics=("parallel", "arbitrary"), vmem_limit_bytes=VMEM_LIMIT),
        name="nsa_sample_selected",
    )(table, *([pages] * npg), qbd, unsel_c, jnp.asarray(ee, dtype=BF16), farcol, lut64)


def _own_group_cols(x, grp):
    out = jnp.zeros((x.shape[0], NSA_HEAD_DIM), F32)
    for g in range(NSA_KV_GROUPS):
        out = jnp.where(grp == g, x[:, g * NSA_HEAD_DIM:(g + 1) * NSA_HEAD_DIM], out)
    return out


def _sample_final_kernel(qbd_ref, m_ref, l_ref, acc_ref, snew_ref, wc_ref, wnew_ref, oc_ref, gr_ref, far_ref, lut_ref,
                         o_ref, *, nq, w_buf):
    rows = SAMPLE_ROWS
    qbd = qbd_ref[...]
    far = far_ref[...][:, 0:1]
    lut = lut_ref[...]
    ri = lax.broadcasted_iota(jnp.int32, (rows, LANES), 0)
    li = lax.broadcasted_iota(jnp.int32, (rows, LANES), 1)
    tok = ri % nq
    grp = lax.broadcasted_iota(jnp.int32, (rows, NSA_HEAD_DIM), 0) // (NSA_HPG * nq)

    knew = snew_ref[...]
    s_new = _nt(qbd, knew[:, :256].astype(BF16)) + far
    d_new = tok - li
    s_new = jnp.where((d_new >= 0) & (li < nq), s_new + _lut_gather(lut, d_new), NEG)
    m_old = m_ref[...][:, 0:1]
    m_new = jnp.maximum(m_old, jnp.max(s_new, axis=-1, keepdims=True))
    alpha = jnp.exp(m_old - m_new)
    p_new = jnp.exp(s_new - m_new)
    l_s = alpha * l_ref[...][:, 0:1] + jnp.sum(p_new, axis=-1, keepdims=True)
    acc_s = alpha * acc_ref[...] + jnp.dot(p_new.astype(BF16), knew[:, 256:].astype(BF16), preferred_element_type=F32)
    o_s = _own_group_cols(acc_s, grp) / l_s

    kv_w = jnp.concatenate([wc_ref[...], wnew_ref[...]], axis=0)
    s_w = _nt(qbd, kv_w[:, :256].astype(BF16)) + far
    n_w = w_buf + LANES
    idx = lax.broadcasted_iota(jnp.int32, (rows, n_w), 1)
    d_w = w_buf + lax.broadcasted_iota(jnp.int32, (rows, n_w), 0) % nq - idx
    ok_w = (d_w >= 0) & (d_w < WINDOW) & (idx < w_buf + nq)
    corr = [jnp.zeros((rows, n_w - 2 * LANES), F32)]
    for cidx in range(2):
        lo = n_w - 2 * LANES + cidx * LANES
        corr.append(_lut_gather(lut, d_w[:, lo:lo + LANES]))
    s_w = jnp.where(ok_w, s_w + jnp.concatenate(corr, axis=-1), NEG)
    m_w = jnp.max(s_w, axis=-1, keepdims=True)
    p_w = jnp.exp(s_w - m_w)
    l_w = jnp.sum(p_w, axis=-1, keepdims=True)
    acc_w = jnp.dot(p_w.astype(BF16), kv_w[:, 256:].astype(BF16), preferred_element_type=F32)
    o_w = _own_group_cols(acc_w, grp) / l_w

    gt = jax.nn.sigmoid(gr_ref[...])
    o_ref[...] = gt[:, 0:1] * oc_ref[...][:, :NSA_HEAD_DIM] + gt[:, 1:2] * o_s + gt[:, 2:3] * o_w


def _sample_final(qbd, m, l, acc, snew, wcache, wnew, o_c, graw, farcol, lut64, nq):
    b = qbd.shape[0]
    w_buf = wcache.shape[1]
    assert w_buf == WINDOW
    whole = lambda *shape: pl.BlockSpec((None,) + shape, lambda b: (b,) + (0,) * len(shape))
    const = lambda *shape: pl.BlockSpec(shape, lambda b: (0,) * len(shape))
    return pl.pallas_call(
        functools.partial(_sample_final_kernel, nq=nq, w_buf=w_buf),
        grid=(b,),
        in_specs=[whole(SAMPLE_ROWS, 2 * LANES), whole(SAMPLE_ROWS, LANES), whole(SAMPLE_ROWS, LANES),
                  whole(SAMPLE_ROWS, 2 * LANES), whole(LANES, NSA_KV_W), whole(w_buf, NSA_KV_W), whole(LANES, NSA_KV_W),
                  whole(SAMPLE_ROWS, LANES), whole(SAMPLE_ROWS, LANES),
                  const(SAMPLE_ROWS, LANES), const(SAMPLE_ROWS, LANES)],
        out_specs=whole(SAMPLE_ROWS, NSA_HEAD_DIM),
        out_shape=jax.ShapeDtypeStruct((b, SAMPLE_ROWS, NSA_HEAD_DIM), F32),
        compiler_params=pltpu.CompilerParams(dimension_semantics=("parallel",), vmem_limit_bytes=VMEM_LIMIT),
        name="nsa_sample_final",
    )(qbd, m, l, acc, snew, wcache, wnew, o_c, graw, farcol, lut64)


def _ffn(x, mod, gains, w_in, w_out):
    hid = _norm_mod_swiglu(x, gains[2], mod[3], mod[4], w_in)
    return _matmul_rms_residual(hid, w_out, x, mod[5], gains[3])


def _nsa_layout_kernel(pq_ref, pc_ref, ps_ref, pw_ref, bias_ref, q_ref, ks_ref, vs_ref, kw_ref, vw_ref,
                       kvc_ref, kvc_t_ref, kvs_t_ref, kvw_t_ref, *, tm):
    i = pl.program_id(1)
    lane = lax.broadcasted_iota(jnp.int32, (tm, LANES), 1)
    low = lane < NSA_HEAD_DIM
    kvc_ref[...] = pc_ref[...]
    kvc_t_ref[...] = pc_ref[...].T
    kvs_t_ref[...] = ps_ref[...].T
    kvw_t_ref[...] = pw_ref[...].T

    def head_tile(ref, h):
        tile = ref[:, (h // 2) * LANES:(h // 2 + 1) * LANES]
        return pltpu.roll(tile, NSA_HEAD_DIM, axis=1) if h % 2 else tile

    for h in range(NSA_HEADS):
        q_ref[h] = jnp.where(low, head_tile(pq_ref, h), bias_ref[h]).astype(BF16)
    key = jnp.maximum(i - 1, 0) * tm + lax.broadcasted_iota(jnp.int32, (tm, LANES), 0)
    onehot = jnp.where(key // SLC_BLOCK == lane, NEG, 0.0).astype(BF16)
    k_ones = jnp.where((lane == NSA_HEAD_DIM) | (lane == NSA_HEAD_DIM + 1), 1.0, 0.0)
    v_ones = jnp.where(lane == NSA_HEAD_DIM, 1.0, 0.0)
    live = i > 0
    for g in range(NSA_KV_GROUPS):
        ks_ref[g, :, :LANES] = onehot
        ks_ref[g, :, LANES:] = jnp.where(low, head_tile(ps_ref, g), k_ones).astype(BF16)
        vs_ref[g] = jnp.where(low, head_tile(ps_ref, NSA_KV_GROUPS + g), v_ones).astype(BF16)
        kw_ref[g] = jnp.where(live, jnp.where(low, head_tile(pw_ref, g), k_ones), 0.0).astype(BF16)
        vw_ref[g] = jnp.where(live, jnp.where(low, head_tile(pw_ref, NSA_KV_GROUPS + g), v_ones), 0.0).astype(BF16)


def _nsa_layouts(p, bias_cols):
    b, t, _ = p.shape
    tm = WINDOW
    assert t % tm == 0
    prev = lambda i: jnp.maximum(i - 1, 0)
    src = lambda width, col_block: pl.BlockSpec((None, tm, width), lambda b, i: (b, prev(i), col_block))
    same = lambda heads, width: pl.BlockSpec((None, heads, tm, width), lambda b, i: (b, 0, prev(i), 0))
    late = pl.BlockSpec((None, NSA_KV_GROUPS, tm, LANES), lambda b, i: (b, 0, i, 0))
    rows_minor = pl.BlockSpec((None, NSA_KV_W, tm), lambda b, i: (b, 0, prev(i)))
    bias = jnp.pad(bias_cols.astype(F32), ((0, 0), (NSA_HEAD_DIM, 0))).reshape(NSA_HEADS, 1, LANES)
    sds = lambda heads, rows, width: jax.ShapeDtypeStruct((b, heads, rows, width), BF16)
    kv_t = jax.ShapeDtypeStruct((b, NSA_KV_W, t), F32)
    kv_col = lambda n: (NSA_Q_W + n * NSA_KV_W) // NSA_KV_W
    return pl.pallas_call(
        functools.partial(_nsa_layout_kernel, tm=tm),
        grid=(b, t // tm + 1),
        in_specs=[src(NSA_Q_W, 0), src(NSA_KV_W, kv_col(0)), src(NSA_KV_W, kv_col(1)), src(NSA_KV_W, kv_col(2)),
                  pl.BlockSpec((NSA_HEADS, 1, LANES), lambda b, i: (0, 0, 0))],
        out_specs=[same(NSA_HEADS, LANES), same(NSA_KV_GROUPS, 2 * LANES), same(NSA_KV_GROUPS, LANES), late, late,
                   pl.BlockSpec((None, tm, NSA_KV_W), lambda b, i: (b, prev(i), 0)),
                   rows_minor, rows_minor, rows_minor],
        out_shape=[sds(NSA_HEADS, t, LANES), sds(NSA_KV_GROUPS, t, 2 * LANES), sds(NSA_KV_GROUPS, t, LANES),
                   sds(NSA_KV_GROUPS, t + WINDOW, LANES), sds(NSA_KV_GROUPS, t + WINDOW, LANES),
                   jax.ShapeDtypeStruct((b, t, NSA_KV_W), F32), kv_t, kv_t, kv_t],
        compiler_params=pltpu.CompilerParams(
            dimension_semantics=("parallel", "arbitrary"), vmem_limit_bytes=VMEM_LIMIT),
        name="nsa_layouts",
    )(p, p, p, p, bias)


def _nsa_prompt(x, mod, gains, w_in, cmp_w, w_out, tb, lut, bias_cols):
    b, t, _ = x.shape
    p = _norm_mod_linear(x, gains[0], mod[0], mod[1], w_in)
    q128, ks, vs, kw, vw, kvc, kvc_t, kvs_t, kvw_t = _nsa_layouts(p, bias_cols)
    n_pages = t // PAGE_SIZE
    table = jnp.arange(b * n_pages, dtype=jnp.int32).reshape(b, n_pages)
    kc, vc = _compress(kvc.reshape(b * n_pages, SEGS_PER_PAGE, SEG_W), table, *cmp_w)
    o = _nsa_prompt_attention(q128, p, (NSA_Q_W + 3 * NSA_KV_W) // LANES, kc, vc, ks, vs, kw, vw, tb, lut)
    x = _matmul_rms_residual(o, w_out, x, mod[2], gains[1])
    rows_major = lambda a: a.reshape(b, 2, NSA_KV_GROUPS, NSA_HEAD_DIM, -1).transpose(0, 4, 1, 2, 3)
    return x, rows_major(kvc_t), rows_major(kvs_t), rows_major(kvw_t[:, :, t - min(WINDOW, t):])


def _nsa_sample(x, mod, gains, w_in, cmp_w, w_out, lut, rel_bias, bias_cols, cache_cmp, cache_slc, cache_win,
                page_table, db, nq):
    n_pages = page_table.shape[1]
    past_len = n_pages * PAGE_SIZE
    rows = db * nq
    p = _norm_mod_linear(x, gains[0], mod[0], mod[1], w_in)[0]
    kvc, kvs, kvw = (p[:, 1024 + i * NSA_KV_W:1024 + (i + 1) * NSA_KV_W] for i in range(3))
    qh = p[:, :NSA_Q_W].astype(BF16).reshape(db, nq, NSA_KV_GROUPS, NSA_HPG, NSA_HEAD_DIM).transpose(0, 2, 3, 1, 4)
    q16 = jnp.concatenate([qh, jnp.broadcast_to(bias_cols.reshape(1, NSA_KV_GROUPS, NSA_HPG, 1, NSA_HEAD_DIM), qh.shape)],
                          axis=-1).reshape(db, NSA_KV_GROUPS, NSA_HPG * nq, LANES)
    eye_g = jnp.eye(NSA_KV_GROUPS, dtype=BF16)
    qbd = jnp.einsum('bghtd,gj->bghtjd', qh, eye_g).reshape(db, SAMPLE_ROWS, NSA_KV_GROUPS * NSA_HEAD_DIM)
    row_head = np.repeat(np.arange(NSA_HEADS), nq)
    lut64 = lut[:, 0, :][row_head]
    farcol = jnp.broadcast_to(rel_bias[N_BUCKETS - 1][row_head][:, None], (SAMPLE_ROWS, LANES))
    rows_minor = lambda cache: jnp.transpose(cache, (0, 2, 3, 4, 1))
    kc, vc = _compress(rows_minor(cache_cmp).reshape(-1, 2, 2, LANES, PAGE_SIZE), page_table, *cmp_w)
    o_c, unsel = _sample_cmp(q16, kc, vc, lut64.reshape(NSA_KV_GROUPS, NSA_HPG * nq, LANES), past_len, nq)
    npg = min(SEL_PAGES_PER_STEP, n_pages)
    nch = n_pages // npg
    bpc = npg * PAGE_SIZE // SLC_BLOCK
    un = unsel[:, :, :nq, :past_len // SLC_BLOCK].reshape(db, NSA_KV_GROUPS, 1, nq, nch, bpc)
    un = jnp.broadcast_to(un, (db, NSA_KV_GROUPS, NSA_HPG, nq, nch, bpc)).transpose(0, 4, 1, 2, 3, 5)
    un = jnp.pad(un.reshape(db, nch, SAMPLE_ROWS, bpc), ((0, 0), (0, 0), (0, 0), (0, LANES - bpc))).astype(BF16)
    m, l, acc = _sample_sel(rows_minor(cache_slc).reshape(-1, 2, NSA_KV_W // 2, PAGE_SIZE), page_table, qbd, un, farcol,
                            lut64, past_len, nq)
    pad_rows = lambda a: jnp.pad(a.reshape(db, nq, NSA_KV_W), ((0, 0), (0, LANES - nq), (0, 0)))
    wcache = cache_win.reshape(db, -1, NSA_KV_W)
    graw = p[:, NSA_Q_W + 3 * NSA_KV_W:NSA_Q_W + 3 * NSA_KV_W + 3 * NSA_HEADS]
    graw = graw.reshape(db, nq, NSA_HEADS, 3).transpose(0, 2, 1, 3).reshape(db, SAMPLE_ROWS, 3)
    graw = jnp.pad(graw, ((0, 0), (0, 0), (0, LANES - 3)))
    o = _sample_final(qbd, m, l, acc, pad_rows(kvs), wcache, pad_rows(kvw), o_c.reshape(db, SAMPLE_ROWS, LANES), graw,
                      farcol, lut64, nq)
    o = o.reshape(db, NSA_HEADS, nq, NSA_HEAD_DIM).transpose(0, 2, 1, 3).reshape(1, rows, NSA_Q_W)
    x = _matmul_rms_residual(o, w_out, x, mod[2], gains[1])
    shape5 = (db, nq, 2, NSA_KV_GROUPS, NSA_HEAD_DIM)
    kv_win = jnp.concatenate([cache_win, kvw.reshape(shape5)], axis=1)[:, -cache_win.shape[1]:]
    return x, kvc.reshape(shape5), kvs.reshape(shape5), kv_win


def _gdn_prompt(x, mod, gains, w_in, conv_w, a_log, dt_bias, norm_w, w_out):
    b, t, _ = x.shape
    p = _norm_mod_linear(x, gains[0], mod[0], mod[1], w_in)
    act = _gdn_conv(p, conv_w)
    bg = _gdn_gates(p, (C_CONV + GDN_V_W) // LANES, a_log, dt_bias)
    s0 = jnp.zeros((b, GDN_V_HEADS, GDN_HEAD_DIM, GDN_HEAD_DIM), F32)
    o, s_fin = _gdn_delta(act, bg, s0)
    x = _gdn_out(o, p, C_CONV // GDN_V_W, norm_w, w_out, x, mod[2], gains[1])
    return x, p[:, t - (CONV_W - 1):, :C_CONV], s_fin


def _gdn_sample(x, mod, gains, w_in, conv_w, a_log, dt_bias, norm_w, w_out, conv_buf, s0, db, nq):
    p = _norm_mod_linear(x, gains[0], mod[0], mod[1], w_in)
    qkv = p[0, :, :C_CONV].reshape(db, nq, C_CONV)
    xp = jnp.concatenate([conv_buf, qkv], axis=1)
    act = _gdn_conv(jnp.pad(xp, ((0, 0), (0, 8 - xp.shape[1]), (0, 0))), conv_w)[:, CONV_W - 1:CONV_W - 1 + nq]
    bg = _gdn_gates(p, (C_CONV + GDN_V_W) // LANES, a_log, dt_bias).reshape(db, nq, LANES)
    pad_t = ((0, 0), (0, GDN_CHUNK - nq), (0, 0))
    o, s_fin = _gdn_delta(jnp.pad(act, pad_t), jnp.pad(bg, pad_t), s0)
    o = o[:, :nq].reshape(1, db * nq, GDN_V_W)
    x = _gdn_out(o, p, C_CONV // GDN_V_W, norm_w, w_out, x, mod[2], gains[1])
    return x, xp[:, -(CONV_W - 1):], s_fin


def kernel(x_prompt, x_sample, c_prompt, c_sample, cache_kv_cmp, cache_kv_slc, cache_kv_win, state_conv, state_ssm,
           page_table, rel_bias, norm_gains, w_ada, b_ada, w_ffn_in, w_ffn_out, nsa_w_in, nsa_cmp_pe, nsa_cmp_w1,
           nsa_cmp_b1, nsa_cmp_w2, nsa_w_out, gdn_w_in, gdn_conv_w, gdn_a_log, gdn_dt_bias, gdn_norm_w, gdn_w_out):
    depth = w_ada.shape[0]
    bp, t, d = x_prompt.shape
    db, nq, _ = x_sample.shape
    assert nq + CONV_W - 1 <= 8 and nq <= GDN_CHUNK

    c_all = jnp.concatenate([c_prompt, c_sample], axis=0)
    rows_pad = -(-c_all.shape[0] // 8) * 8
    ada = _adaln(jnp.pad(c_all, ((0, rows_pad - c_all.shape[0]), (0, 0))), w_ada, b_ada)
    ada = ada.reshape(depth, rows_pad, 6, d)
    tb, lut = _bias_tables(rel_bias)
    far_hi, far_lo = _split2(rel_bias[N_BUCKETS - 1])
    bias_cols = jnp.zeros((NSA_HEADS, NSA_HEAD_DIM), BF16).at[:, 0].set(far_hi).at[:, 1].set(far_lo)

    xp = x_prompt
    xs = x_sample.reshape(1, db * nq, d)
    kvc_p, kvc_s, kvs_p, kvs_s, kvw_p, kvw_s, cv_p, cv_s, ss_p, ss_s = ([] for _ in range(10))
    for i in range(depth):
        mod_p = [ada[i, :bp, k][:, None, :] for k in range(6)]
        mod_s = [jnp.repeat(ada[i, bp:bp + db, k], nq, axis=0)[None] for k in range(6)]
        gains = norm_gains[i]
        l = i // 2
        if i % 2 == 0:
            w_in = jnp.concatenate([nsa_w_in[l][:, :NSA_Q_W] * (NSA_HEAD_DIM ** -0.5), nsa_w_in[l][:, NSA_Q_W:]], axis=1)
            w_in = jnp.pad(w_in, ((0, 0), (0, -w_in.shape[1] % LANES))).astype(BF16)
            cmp_w = _compress_weights(nsa_cmp_pe[l], nsa_cmp_w1[l], nsa_cmp_b1[l], nsa_cmp_w2[l])
            w_out = nsa_w_out[l].astype(BF16)
            xp, a, bq, cq = _nsa_prompt(xp, mod_p, gains, w_in, cmp_w, w_out, tb, lut, bias_cols)
            kvc_p.append(a); kvs_p.append(bq); kvw_p.append(cq)
            xs, a, bq, cq = _nsa_sample(xs, mod_s, gains, w_in, cmp_w, w_out, lut, rel_bias, bias_cols, cache_kv_cmp[l],
                                        cache_kv_slc[l], cache_kv_win[l], page_table, db, nq)
            kvc_s.append(a); kvs_s.append(bq); kvw_s.append(cq)
        else:
            w_in = jnp.pad(gdn_w_in[l], ((0, 0), (0, -gdn_w_in.shape[2] % (5 * MXU_WIDTH)))).astype(BF16)
            gdn_w = (w_in, gdn_conv_w[l], gdn_a_log[l], gdn_dt_bias[l], gdn_norm_w[l], gdn_w_out[l].astype(BF16))
            xp, a, bq = _gdn_prompt(xp, mod_p, gains, *gdn_w)
            cv_p.append(a); ss_p.append(bq)
            xs, a, bq = _gdn_sample(xs, mod_s, gains, *gdn_w, state_conv[l], state_ssm[l], db, nq)
            cv_s.append(a); ss_s.append(bq)
        w_ffn = (w_ffn_in[i].astype(BF16), w_ffn_out[i].astype(BF16))
        xp = _ffn(xp, mod_p, gains, *w_ffn)
        xs = _ffn(xs, mod_s, gains, *w_ffn)
    return (xp, xs.reshape(db, nq, d), jnp.stack(kvc_p), jnp.stack(kvc_s), jnp.stack(kvs_p), jnp.stack(kvs_s),
            jnp.stack(kvw_p), jnp.stack(kvw_s), jnp.stack(cv_p), jnp.stack(cv_s), jnp.stack(ss_p), jnp.stack(ss_s))
```

```python
import functools
import math

import numpy as np
import jax
import jax.numpy as jnp
from jax import lax
from jax.experimental import pallas as pl
from jax.experimental.pallas import tpu as pltpu

F32 = jnp.float32
BF16 = jnp.bfloat16

D_MODEL = 1024
RMS_EPS = 1e-6
D_FF = 2816
NSA_HEADS = 16
NSA_HEAD_DIM = 64
NSA_KV_GROUPS = 4
NSA_HPG = 4
CMP_BLOCK = 32
CMP_STRIDE = 16
CMP_HID = 256
SLC_BLOCK = 64
N_SELECT = 16
WINDOW = 512
Q_BLOCK = 128
PAGE_SIZE = 128
N_BUCKETS = 32
GDN_QK_HEADS = 8
GDN_V_HEADS = 16
GDN_HEAD_DIM = 128
CONV_W = 4
GDN_CHUNK = 64
NSA_Q_W = 1024
NSA_KV_W = 512
C_CONV = 4096
GDN_V_W = 2048

LANES = 128
SEG_W = CMP_STRIDE * NSA_KV_W
SEGS_PER_PAGE = PAGE_SIZE // CMP_STRIDE
NEG = -1e30
VMEM_LIMIT = 48 * 1024 * 1024

_BUCKET_THR = (19, 21, 24, 27, 31, 35, 40, 46, 52, 59, 67, 77, 87, 99, 113)
FAR_DIST = 128


def _nt(a, b):
    return lax.dot_general(a, b, (((1,), (1,)), ((), ())), preferred_element_type=F32)


def _split2(x):
    hi = x.astype(BF16)
    lo = (x - hi.astype(F32)).astype(BF16)
    return hi, lo


def _split3(x):
    hi = x.astype(BF16)
    r = x - hi.astype(F32)
    mid = r.astype(BF16)
    lo = (r - mid.astype(F32)).astype(BF16)
    return hi, mid, lo


MXU_WIDTH = 256
MAX_TN = 2816


def _pick_tn(n):
    units = n // LANES
    cands = [d * LANES for d in range(1, units + 1) if units % d == 0 and d * LANES <= MAX_TN]
    full = [c for c in cands if c % MXU_WIDTH == 0]
    return max(full) if full and 2 * max(full) >= max(cands) else max(cands)


def _adaln_kernel(c_ref, w_ref, b_ref, o_ref):
    c = c_ref[...]
    a = (c * jax.nn.sigmoid(c)).astype(BF16)
    o_ref[...] = jnp.dot(a, w_ref[...].astype(BF16), preferred_element_type=F32) + b_ref[...]


def _adaln(c_all, w_ada, b_ada):
    depth, d, n = w_ada.shape
    rows = c_all.shape[0]
    tn = 768
    return pl.pallas_call(
        _adaln_kernel,
        grid=(depth, n // tn),
        in_specs=[pl.BlockSpec((rows, d), lambda l, j: (0, 0)),
                  pl.BlockSpec((None, d, tn), lambda l, j: (l, 0, j)),
                  pl.BlockSpec((None, 1, tn), lambda l, j: (l, 0, j))],
        out_specs=pl.BlockSpec((None, rows, tn), lambda l, j: (l, 0, j)),
        out_shape=jax.ShapeDtypeStruct((depth, rows, n), F32),
        compiler_params=pltpu.CompilerParams(dimension_semantics=("parallel", "parallel")),
        name="adaln",
    )(c_all, w_ada, b_ada.reshape(depth, 1, n))


def _mod_norm(x, gain, shift, scale):
    ms = jnp.mean(x * x, axis=-1, keepdims=True)
    y = x * lax.rsqrt(ms + RMS_EPS) * gain
    return y * (1.0 + scale) + shift


def _nml_kernel(x_ref, g_ref, sh_ref, sc_ref, w_ref, o_ref):
    h = _mod_norm(x_ref[...], g_ref[...], sh_ref[...], sc_ref[...]).astype(BF16)
    o_ref[...] = jnp.dot(h, w_ref[...], preferred_element_type=F32).astype(o_ref.dtype)


def _nml_swiglu_kernel(x_ref, g_ref, sh_ref, sc_ref, wg_ref, wu_ref, o_ref, h_ref):
    @pl.when(pl.program_id(2) == 0)
    def _():
        h_ref[...] = _mod_norm(x_ref[...], g_ref[...], sh_ref[...], sc_ref[...]).astype(BF16)

    h = h_ref[...]
    gate = jnp.dot(h, wg_ref[...], preferred_element_type=F32)
    up = jnp.dot(h, wu_ref[...], preferred_element_type=F32)
    o_ref[...] = (gate * jax.nn.sigmoid(gate) * up).astype(o_ref.dtype)


def _mod_spec(mod, tm):
    if mod.shape[1] == 1:
        return pl.BlockSpec((None, 1, mod.shape[2]), lambda b, i, *_: (b, 0, 0))
    return pl.BlockSpec((None, tm, mod.shape[2]), lambda b, i, *_: (b, i, 0))


def _row_tile(t):
    return 512 if t % 512 == 0 else t


def _norm_mod_linear(x, gain, shift, scale, w, out_dtype=F32):
    b, t, d = x.shape
    n = w.shape[1]
    tm, tn = _row_tile(t), _pick_tn(n)
    if tn <= 5 * MXU_WIDTH and t % (2 * tm) == 0:
        tm *= 2

    def mod_spec(mod):
        if mod.shape[1] == 1:
            return pl.BlockSpec((None, 1, d), lambda j, b, i: (b, 0, 0))
        return pl.BlockSpec((None, tm, d), lambda j, b, i: (b, i, 0))

    return pl.pallas_call(
        _nml_kernel,
        grid=(n // tn, b, t // tm),
        in_specs=[pl.BlockSpec((None, tm, d), lambda j, b, i: (b, i, 0)),
                  pl.BlockSpec((1, d), lambda j, b, i: (0, 0)),
                  mod_spec(shift), mod_spec(scale),
                  pl.BlockSpec((d, tn), lambda j, b, i: (0, j))],
        out_specs=pl.BlockSpec((None, tm, tn), lambda j, b, i: (b, i, j)),
        out_shape=jax.ShapeDtypeStruct((b, t, n), out_dtype),
        compiler_params=pltpu.CompilerParams(
            dimension_semantics=("parallel", "parallel", "parallel"), vmem_limit_bytes=VMEM_LIMIT),
        name="norm_mod_linear",
    )(x, gain.reshape(1, d), shift, scale, w)


def _norm_mod_swiglu(x, gain, shift, scale, w_in):
    b, t, d = x.shape
    nf = w_in.shape[1] // 2
    tm, tn = _row_tile(t), _pick_tn(nf)
    nj = nf // tn
    return pl.pallas_call(
        _nml_swiglu_kernel,
        grid=(b, t // tm, nj),
        in_specs=[pl.BlockSpec((None, tm, d), lambda b, i, j: (b, i, 0)),
                  pl.BlockSpec((1, d), lambda b, i, j: (0, 0)),
                  _mod_spec(shift, tm), _mod_spec(scale, tm),
                  pl.BlockSpec((d, tn), lambda b, i, j: (0, j)),
                  pl.BlockSpec((d, tn), lambda b, i, j: (0, j + nj))],
        out_specs=pl.BlockSpec((None, tm, tn), lambda b, i, j: (b, i, j)),
        out_shape=jax.ShapeDtypeStruct((b, t, nf), BF16),
        scratch_shapes=[pltpu.VMEM((tm, d), BF16)],
        compiler_params=pltpu.CompilerParams(
            dimension_semantics=("parallel", "parallel", "arbitrary"), vmem_limit_bytes=VMEM_LIMIT),
        name="norm_mod_swiglu",
    )(x, gain.reshape(1, d), shift, scale, w_in, w_in)


def _rms_gated_residual(y, x, gate, gain):
    ms = jnp.mean(y * y, axis=-1, keepdims=True)
    return x + gate * (y * lax.rsqrt(ms + RMS_EPS) * gain)


def _mrr_kernel(a_ref, w_ref, x_ref, gate_ref, gain_ref, o_ref):
    y = jnp.dot(a_ref[...].astype(BF16), w_ref[...], preferred_element_type=F32)
    o_ref[...] = _rms_gated_residual(y, x_ref[...], gate_ref[...], gain_ref[...])


def _matmul_rms_residual(a, w, x, gate, gain):
    b, t, k = a.shape
    d = w.shape[1]
    tm = _row_tile(t)
    return pl.pallas_call(
        _mrr_kernel,
        grid=(b, t // tm),
        in_specs=[pl.BlockSpec((None, tm, k), lambda b, i: (b, i, 0)),
                  pl.BlockSpec((k, d), lambda b, i: (0, 0)),
                  pl.BlockSpec((None, tm, d), lambda b, i: (b, i, 0)),
                  _mod_spec(gate, tm),
                  pl.BlockSpec((1, d), lambda b, i: (0, 0))],
        out_specs=pl.BlockSpec((None, tm, d), lambda b, i: (b, i, 0)),
        out_shape=jax.ShapeDtypeStruct((b, t, d), F32),
        compiler_params=pltpu.CompilerParams(
            dimension_semantics=("parallel", "parallel"), vmem_limit_bytes=VMEM_LIMIT),
        name="matmul_rms_residual",
    )(a, w, x, gate, gain.reshape(1, d))


def _gdn_out_kernel(o_ref, z_ref, nw_ref, w_ref, x_ref, gate_ref, gain_ref, out_ref, a_ref):
    nw = nw_ref[...]
    for h in range(GDN_V_HEADS):
        sl = slice(h * GDN_HEAD_DIM, (h + 1) * GDN_HEAD_DIM)
        o = o_ref[:, sl]
        z = z_ref[:, sl]
        ms = jnp.mean(o * o, axis=-1, keepdims=True)
        a_ref[:, sl] = ((o * lax.rsqrt(ms + RMS_EPS) * nw) * (z * jax.nn.sigmoid(z))).astype(BF16)
    y = jnp.dot(a_ref[...], w_ref[...], preferred_element_type=F32)
    out_ref[...] = _rms_gated_residual(y, x_ref[...], gate_ref[...], gain_ref[...])


def _gdn_out(o, p, z_col_block, norm_w, w, x, gate, gain):
    b, t, k = o.shape
    d = w.shape[1]
    tm = _row_tile(t)
    return pl.pallas_call(
        _gdn_out_kernel,
        grid=(b, t // tm),
        in_specs=[pl.BlockSpec((None, tm, k), lambda b, i: (b, i, 0)),
                  pl.BlockSpec((None, tm, k), lambda b, i: (b, i, z_col_block)),
                  pl.BlockSpec((1, GDN_HEAD_DIM), lambda b, i: (0, 0)),
                  pl.BlockSpec((k, d), lambda b, i: (0, 0)),
                  pl.BlockSpec((None, tm, d), lambda b, i: (b, i, 0)),
                  _mod_spec(gate, tm),
                  pl.BlockSpec((1, d), lambda b, i: (0, 0))],
        out_specs=pl.BlockSpec((None, tm, d), lambda b, i: (b, i, 0)),
        out_shape=jax.ShapeDtypeStruct((b, t, d), F32),
        scratch_shapes=[pltpu.VMEM((tm, k), BF16)],
        compiler_params=pltpu.CompilerParams(
            dimension_semantics=("parallel", "parallel"), vmem_limit_bytes=VMEM_LIMIT),
        name="gdn_out",
    )(o, p, norm_w.reshape(1, GDN_HEAD_DIM), w, x, gate, gain.reshape(1, d))


def _bucket_of(n):
    big = jnp.full(n.shape, 16, jnp.int32)
    for thr in _BUCKET_THR:
        big = big + (n >= thr).astype(jnp.int32)
    return jnp.where(n < 16, n, big)


def _bias_tab_kernel(tbl_ref, tb_ref, lut_ref):
    h = pl.program_id(0)
    far = tbl_ref[N_BUCKETS - 1, h]

    def lookup(dist):
        bkt = _bucket_of(jnp.maximum(dist, 0))
        out = jnp.zeros(dist.shape, F32)
        for bb in range(N_BUCKETS):
            out = jnp.where(bkt == bb, tbl_ref[bb, h], out)
        return out - far

    qi = lax.broadcasted_iota(jnp.int32, (Q_BLOCK, 2 * Q_BLOCK), 0)
    kj = lax.broadcasted_iota(jnp.int32, (Q_BLOCK, 2 * Q_BLOCK), 1)
    dist = Q_BLOCK + qi - kj
    tb_ref[...] = jnp.where(dist >= 0, lookup(dist), NEG)
    lut_ref[...] = lookup(lax.broadcasted_iota(jnp.int32, (8, LANES), 1))


def _bias_tables(rel_bias):
    return pl.pallas_call(
        _bias_tab_kernel,
        grid=(NSA_HEADS,),
        in_specs=[pl.BlockSpec(memory_space=pltpu.SMEM)],
        out_specs=[pl.BlockSpec((None, Q_BLOCK, 2 * Q_BLOCK), lambda h: (h, 0, 0)),
                   pl.BlockSpec((None, 8, LANES), lambda h: (h, 0, 0))],
        out_shape=[jax.ShapeDtypeStruct((NSA_HEADS, Q_BLOCK, 2 * Q_BLOCK), F32),
                   jax.ShapeDtypeStruct((NSA_HEADS, 8, LANES), F32)],
        compiler_params=pltpu.CompilerParams(dimension_semantics=("parallel",)),
        name="bias_tables",
    )(rel_bias)


def _lut_gather(lut_rows, dist):
    idx = jnp.clip(dist, 0, LANES - 1)
    val = jnp.take_along_axis(lut_rows, idx, axis=1)
    return jnp.where((dist >= 0) & (dist < FAR_DIST), val, 0.0)


def _pe_term_kernel(pe_ref, wbd_ref, b1_ref, o_ref):
    y = jnp.dot(pe_ref[...], wbd_ref[...], preferred_element_type=F32)
    o_ref[...] = y[:, 0:CMP_HID] + y[:, 3 * CMP_HID:4 * CMP_HID] + b1_ref[...]


def _pe_term(pe_x, wbd, b1):
    return pl.pallas_call(
        _pe_term_kernel,
        grid=(2,),
        in_specs=[pl.BlockSpec((None, 8, 2048), lambda k: (k, 0, 0)),
                  pl.BlockSpec((None, 2048, 1024), lambda k: (k, 0, 0)),
                  pl.BlockSpec((None, 1, CMP_HID), lambda k: (k, 0, 0))],
        out_specs=pl.BlockSpec((None, 8, CMP_HID), lambda k: (k, 0, 0)),
        out_shape=jax.ShapeDtypeStruct((2, 8, CMP_HID), F32),
        compiler_params=pltpu.CompilerParams(dimension_semantics=("parallel",), vmem_limit_bytes=VMEM_LIMIT),
        name="cmp_pe_term",
    )(pe_x, wbd, b1.reshape(2, 1, CMP_HID))


def _compress_kernel(tab_ref, *refs, npg, rows_minor):
    del tab_ref
    pages = refs[:npg]
    wbd_ref, w2_ref, pe_ref, cc_ref, kc_ref, vc_ref, xs_ref, carry_ref = refs[npg:npg + 8]
    ts = SEGS_PER_PAGE * npg

    @pl.when(pl.program_id(1) == 0)
    def _():
        carry_ref[...] = jnp.zeros(carry_ref.shape, F32)

    row0 = lax.broadcasted_iota(jnp.int32, (ts, CMP_HID), 0) == 0
    low_half = lax.broadcasted_iota(jnp.int32, (ts, LANES), 1) < NSA_HEAD_DIM
    half = CMP_STRIDE * NSA_HEAD_DIM
    passes = [(k, gp) for k in range(2) for gp in range(2)]

    def transpose_pages(n):
        k, gp = passes[n]
        rt_ref = refs[npg + 8].at[n % 2]
        for j, pg in enumerate(pages):
            rt_ref[j * PAGE_SIZE:(j + 1) * PAGE_SIZE, :] = pg[k, gp].astype(BF16).T.astype(F32)

    def build_features(n):
        k, gp = passes[n]
        xb_ref = xs_ref.at[n % 2]
        if rows_minor:
            rt_ref = refs[npg + 8].at[n % 2]
            for j in range(CMP_STRIDE // 2):
                ra = rt_ref[pl.ds(2 * j, ts, stride=CMP_STRIDE), :]
                rb = rt_ref[pl.ds(2 * j + 1, ts, stride=CMP_STRIDE), :]
                xb_ref[:, j * LANES:(j + 1) * LANES] = jnp.where(
                    low_half, ra, pltpu.roll(rb, NSA_HEAD_DIM, axis=1)).astype(BF16)
                xb_ref[:, half + j * LANES:half + (j + 1) * LANES] = jnp.where(
                    low_half, pltpu.roll(ra, NSA_HEAD_DIM, axis=1), rb).astype(BF16)
        else:
            off = k * 256 + gp * LANES
            for s in range(CMP_STRIDE):
                lo = s * NSA_KV_W + off
                piece = jnp.concatenate([pg[:, lo:lo + LANES] for pg in pages], axis=0)
                xb_ref[:, s * LANES:(s + 1) * LANES] = piece.astype(BF16)

    def first_layer(n):
        k, _ = passes[n]
        xb_ref = xs_ref.at[n % 2]
        if rows_minor:
            return [jnp.dot(xb_ref[:, g2 * half:(g2 + 1) * half], wbd_ref[k], preferred_element_type=F32)
                    for g2 in range(2)]
        y = jnp.dot(xb_ref[...], wbd_ref[k], preferred_element_type=F32)
        return [y[:, :2 * CMP_HID], y[:, 2 * CMP_HID:]]

    def second_layer(n, ys):
        k, gp = passes[n]
        out_ref = kc_ref if k == 0 else vc_ref
        hs = []
        for g2 in range(2):
            pa = ys[g2][:, :CMP_HID]
            pb = ys[g2][:, CMP_HID:]
            ci = (k * 2 + gp) * 2 + g2
            prev = carry_ref[ci]
            pa_prev = jnp.where(row0, prev[7:8, :], pltpu.roll(pa, 1, axis=0))
            carry_ref[ci] = pa[ts - 8:ts, :]
            hs.append(jax.nn.gelu(pa_prev + pb + pe_ref[k, 0:1, :]))
        hid = jnp.concatenate(hs, axis=-1).astype(BF16)
        o = jnp.dot(hid, w2_ref[k], preferred_element_type=F32) + cc_ref[k]
        out_ref[2 * gp] = o[:, :LANES].astype(BF16)
        out_ref[2 * gp + 1] = o[:, LANES:].astype(BF16)

    if rows_minor:
        transpose_pages(0)
    build_features(0)
    if rows_minor:
        transpose_pages(1)
    for n in range(len(passes)):
        ys = first_layer(n)
        if n + 1 < len(passes):
            build_features(n + 1)
        if rows_minor and n + 2 < len(passes):
            transpose_pages(n + 2)
        second_layer(n, ys)


def _compress(pages, table, wbd, w1cat, w2bd, pe_term, ccols):
    bc, n_pages = table.shape
    npg = min(32, n_pages)
    ts = SEGS_PER_PAGE * npg
    nseg = n_pages * SEGS_PER_PAGE
    rows_minor = pages.ndim == 5
    w_first = w1cat if rows_minor else wbd
    page_block = (None,) + pages.shape[1:]

    def page_spec(j):
        return pl.BlockSpec(page_block, lambda b, i, tab: (tab[b, i * npg + j],) + (0,) * (pages.ndim - 1))

    const = lambda *shape: pl.BlockSpec(shape, lambda b, i, tab: (0,) * len(shape), pipeline_mode=pl.Buffered(1))
    out_spec = pl.BlockSpec((None, NSA_KV_GROUPS, ts, LANES), lambda b, i, tab: (b, 0, i, 0))
    grid_spec = pltpu.PrefetchScalarGridSpec(
        num_scalar_prefetch=1,
        grid=(bc, n_pages // npg),
        in_specs=[page_spec(j) for j in range(npg)] + [
            const(*w_first.shape), const(2, 512, 256), const(2, 8, CMP_HID), const(2, 1, 256)],
        out_specs=[out_spec, out_spec],
        scratch_shapes=[pltpu.VMEM((2, ts, 2048), BF16), pltpu.VMEM((8, 8, CMP_HID), F32)] + (
            [pltpu.VMEM((2, npg * PAGE_SIZE, LANES), F32)] if rows_minor else []),
    )
    out_sds = jax.ShapeDtypeStruct((bc, NSA_KV_GROUPS, nseg, LANES), BF16)
    return pl.pallas_call(
        functools.partial(_compress_kernel, npg=npg, rows_minor=rows_minor),
        grid_spec=grid_spec,
        out_shape=[out_sds, out_sds],
        compiler_params=pltpu.CompilerParams(
            dimension_semantics=("parallel", "arbitrary"), vmem_limit_bytes=VMEM_LIMIT),
        name="kv_compress",
    )(table, *([pages] * npg), w_first, w2bd, pe_term, ccols)


def _compress_weights(pe, w1, b1, w2):
    eye2 = jnp.eye(2, dtype=F32)
    w = w1.reshape(2, 2, CMP_STRIDE, NSA_HEAD_DIM, CMP_HID)
    wbd = jnp.einsum('kasdh,gj->ksgdjah', w, eye2).reshape(2, 2048, 1024).astype(BF16)
    w1cat = w.transpose(0, 2, 3, 1, 4).reshape(2, CMP_STRIDE * NSA_HEAD_DIM, 2 * CMP_HID).astype(BF16)
    w2p = jnp.pad(w2, ((0, 0), (0, 0), (0, LANES - NSA_HEAD_DIM)))
    w2bd = jnp.einsum('khd,gj->kghjd', w2p, eye2).reshape(2, 512, 256).astype(BF16)
    pe_x = pe.reshape(2, 2, CMP_STRIDE, NSA_HEAD_DIM).transpose(0, 2, 1, 3).reshape(2, 1, 2048)
    pe_x = jnp.broadcast_to(pe_x, (2, 8, 2048)).astype(BF16)
    pe_term = _pe_term(pe_x, wbd, b1)
    cc = np.zeros((2, 1, 256), np.float32)
    for g2 in range(2):
        cc[0, 0, g2 * LANES + 64] = 1.0
        cc[0, 0, g2 * LANES + 65] = 1.0
        cc[1, 0, g2 * LANES + 64] = 1.0
    return wbd, w1cat, w2bd, pe_term, jnp.asarray(cc)


def _topk_rows_mask(scores, k, fillers=(), preselected=None):
    scores = list(scores)
    shape = scores[0].shape
    blk = lax.broadcasted_iota(jnp.int32, shape, 0).astype(F32)
    sels = [jnp.zeros(shape, F32) if preselected is None else preselected.astype(F32) for _ in scores]
    fillers = list(fillers)
    for it in range(k):
        for n, score in enumerate(scores):
            mx = jnp.max(score, axis=0, keepdims=True)
            idx = jnp.min(jnp.where(score == mx, blk, 1e9), axis=0, keepdims=True)
            hit = blk == idx
            sels[n] = jnp.where(hit, 1.0, sels[n])
            scores[n] = jnp.where(hit, -jnp.inf, score)
        for f in fillers[it * len(fillers) // k:(it + 1) * len(fillers) // k]:
            f()
    return sels


def _nsa_prompt_kernel(q_ref, gt_ref, kc_ref, vc_ref, ks_ref, vs_ref, kw_ref, vw_ref, tb_ref, lut_ref, ovt_ref,
                       o_ref, qaug_ref, sc_ref, pcb_ref, pcs_ref, sw_ref, pw_ref, pfar_ref, *, nseg, n_sb, bb):
    g = pl.program_id(1)
    qb = pl.program_id(2)
    s0 = qb * Q_BLOCK
    rows = NSA_HPG * Q_BLOCK
    bis = range(bb)
    qs = [q_ref[bi].reshape(rows, LANES) for bi in bis]
    tb = tb_ref[...]

    for bi in bis:
        sc_ref[bi] = _nt(qs[bi], kc_ref[bi])
    qi128 = lax.broadcasted_iota(jnp.int32, (Q_BLOCK, LANES), 0)
    li128 = lax.broadcasted_iota(jnp.int32, (Q_BLOCK, LANES), 1)

    def add_near_bias(chunk):
        l0 = pl.multiple_of(chunk * LANES, LANES)
        dist = s0 + qi128 - CMP_STRIDE * (chunk * LANES + li128) - (CMP_STRIDE - 1)
        for hh in range(NSA_HPG):
            lut = jnp.broadcast_to(lut_ref[hh, 0:1, :], (Q_BLOCK, LANES))
            rs = slice(hh * Q_BLOCK, (hh + 1) * Q_BLOCK)
            corr = _lut_gather(lut, dist)
            for bi in bis:
                sc_ref[bi, rs, pl.ds(l0, LANES)] = sc_ref[bi, rs, pl.ds(l0, LANES)] + corr

    chunk_lo = jnp.maximum(8 * qb - 8, 0) // LANES
    chunk_hi = (8 * qb + 7) // LANES
    add_near_bias(chunk_lo)

    @pl.when(chunk_hi != chunk_lo)
    def _():
        add_near_bias(chunk_hi)

    rb_n = 32
    qi_b = lax.broadcasted_iota(jnp.int32, (rb_n, nseg), 0)
    mi_b = lax.broadcasted_iota(jnp.int32, (rb_n, nseg), 1)
    for rb in range(Q_BLOCK // rb_n):
        dist_b = s0 + rb * rb_n + qi_b - CMP_STRIDE * mi_b - (CMP_STRIDE - 1)
        valid_b = (dist_b >= 0) & (mi_b >= 1)
        for bi in bis:
            pcs_b = jnp.zeros((rb_n, nseg), F32)
            for hh in range(NSA_HPG):
                rs = slice(hh * Q_BLOCK + rb * rb_n, hh * Q_BLOCK + (rb + 1) * rb_n)
                s_b = jnp.where(valid_b, sc_ref[bi, rs, :], NEG)
                e = jnp.where(valid_b, jnp.exp(s_b - jnp.max(s_b, axis=-1, keepdims=True)), 0.0)
                ssum = jnp.sum(e, axis=-1, keepdims=True)
                pc = e * (1.0 / jnp.where(ssum > 0, ssum, 1.0))
                pcb_ref[bi, rs, :] = pc.astype(BF16)
                pcs_b = pcs_b + pc
            pcs_ref[bi, rb * rb_n:(rb + 1) * rb_n, :] = pcs_b
    o_cs = [jnp.dot(pcb_ref[bi], vc_ref[bi], preferred_element_type=F32) for bi in bis]

    ovt = ovt_ref[...]
    imp_ts = []
    for bi in bis:
        hi, lo = _split2(pcs_ref[bi])
        imp_ts.append(_nt(ovt, hi) + _nt(ovt, lo))
    jb = lax.broadcasted_iota(jnp.int32, (LANES, Q_BLOCK), 0)
    q_blk = (s0 + lax.broadcasted_iota(jnp.int32, (LANES, Q_BLOCK), 1)) // SLC_BLOCK
    forced = (jb == 0) | (jb == q_blk) | (jb == q_blk - 1)
    scores = []
    for bi in bis:
        score = jnp.where(forced, -jnp.inf, jnp.where(jb <= q_blk, imp_ts[bi], -1e4))
        scores.append(jnp.where(jb < n_sb, score, -3e38))
    n_forced = 3

    w0 = pl.multiple_of(s0, Q_BLOCK)
    n_w = WINDOW + Q_BLOCK
    for bi in bis:
        sw_ref[bi] = _nt(qs[bi], kw_ref[bi, pl.ds(w0, n_w), :])
    qi_w = lax.broadcasted_iota(jnp.int32, (rb_n, n_w), 0)
    kk_w = lax.broadcasted_iota(jnp.int32, (rb_n, n_w), 1)

    def window_block(bi, hh, rb):
        ok_w = (kk_w > qi_w + rb * rb_n) & (s0 + kk_w >= WINDOW)
        rs = slice(hh * Q_BLOCK + rb * rb_n, hh * Q_BLOCK + (rb + 1) * rb_n)
        s_b = jnp.where(ok_w, sw_ref[bi, rs, :], NEG)
        s_b = jnp.concatenate([s_b[:, :WINDOW - Q_BLOCK],
                               s_b[:, WINDOW - Q_BLOCK:] + tb_ref[hh, rb * rb_n:(rb + 1) * rb_n, :]], axis=-1)
        pw_ref[bi, rs, :] = jnp.exp(s_b - jnp.max(s_b, axis=-1, keepdims=True)).astype(BF16)

    gts = [jax.nn.sigmoid(gt_ref[bi]) for bi in bis]
    glane = lax.broadcasted_iota(jnp.int32, (Q_BLOCK, LANES), 1)
    gates = [[[None] * 3 for _ in range(NSA_HPG)] for _ in bis]

    def gate_column(bi, hh, br):
        col = 3 * (NSA_HPG * g + hh) + br
        gates[bi][hh][br] = jnp.sum(jnp.where(glane == col, gts[bi], 0.0), axis=-1, keepdims=True)

    fillers = []
    for hh in range(NSA_HPG):
        for bi in bis:
            fillers += [functools.partial(window_block, bi, hh, rb) for rb in range(Q_BLOCK // rb_n)]
            fillers += [functools.partial(gate_column, bi, hh, br) for br in range(3)]
    sel_ts = _topk_rows_mask(scores, min(N_SELECT, n_sb) - n_forced, fillers, preselected=forced)
    blk_lane = lax.broadcasted_iota(jnp.int32, (Q_BLOCK, LANES), 1)
    near_blk0 = 2 * qb - 2
    a0 = pl.multiple_of(jnp.maximum(s0 - Q_BLOCK, 0), Q_BLOCK)
    b0 = pl.multiple_of(s0, Q_BLOCK)
    kcol = lax.broadcasted_iota(jnp.int32, (Q_BLOCK, 2 * Q_BLOCK), 1)
    no_prev = jnp.where((kcol < Q_BLOCK) & (qb == 0), NEG, 0.0)
    kc_far = 4 * Q_BLOCK
    o_ws, unsel_fars, m_runs, accs = [], [], [], []
    for bi in bis:
        acc_w = jnp.dot(pw_ref[bi], vw_ref[bi, pl.ds(w0, n_w), :], preferred_element_type=F32)
        o_ws.append(acc_w * (1.0 / acc_w[:, NSA_HEAD_DIM:NSA_HEAD_DIM + 1]))
        unsel = 1.0 - sel_ts[bi].T
        unsel_fars.append(jnp.where(blk_lane >= near_blk0, 1.0, unsel))
        qaug_ref[bi, :, LANES:] = qs[bi]
        for hh in range(NSA_HPG):
            qaug_ref[bi, hh * Q_BLOCK:(hh + 1) * Q_BLOCK, :LANES] = unsel.astype(BF16)
    for bi in bis:
        kn = jnp.concatenate([ks_ref[bi, pl.ds(a0, Q_BLOCK), :], ks_ref[bi, pl.ds(b0, Q_BLOCK), :]], axis=0)
        vn = jnp.concatenate([vs_ref[bi, pl.ds(a0, Q_BLOCK), :], vs_ref[bi, pl.ds(b0, Q_BLOCK), :]], axis=0)
        s_n = _nt(qaug_ref[bi], kn).reshape(NSA_HPG, Q_BLOCK, 2 * Q_BLOCK) + (tb + no_prev[None])
        s_n = s_n.reshape(rows, 2 * Q_BLOCK)
        m_run = jnp.max(s_n, axis=-1, keepdims=True)
        m_runs.append(m_run)
        accs.append(jnp.dot(jnp.exp(s_n - m_run).astype(BF16), vn, preferred_element_type=F32))

    for bi in bis:
        for hh in range(NSA_HPG):
            qaug_ref[bi, hh * Q_BLOCK:(hh + 1) * Q_BLOCK, :LANES] = unsel_fars[bi].astype(BF16)
        pfar_ref[bi, 1] = jnp.zeros((rows, kc_far), BF16)
    n_far = (jnp.maximum(qb - 1, 0) + 3) // 4


    def far_logits(bi, c):
        return _nt(qaug_ref[bi], ks_ref[bi, pl.ds(pl.multiple_of(c * kc_far, kc_far), kc_far), :])

    def far_pv(bi, slot, c):
        k0 = pl.multiple_of(jnp.maximum(c, 0) * kc_far, kc_far)
        return jnp.dot(pfar_ref[bi, slot], vs_ref[bi, pl.ds(k0, kc_far), :], preferred_element_type=F32)

    def far_trip(t, carry):
        m_old, acc_old, alpha_prev = carry[0::3], carry[1::3], carry[2::3]
        s_a = [far_logits(bi, 2 * t) for bi in bis]
        acc_1 = [alpha_prev[bi] * acc_old[bi] + far_pv(bi, 1, 2 * t - 1) for bi in bis]
        s_b = [far_logits(bi, 2 * t + 1) for bi in bis]
        m_a = [jnp.maximum(m_old[bi], jnp.max(s_a[bi], axis=-1, keepdims=True)) for bi in bis]
        for bi in bis:
            pfar_ref[bi, 0] = jnp.exp((s_a[bi] - m_a[bi]).astype(BF16))
        acc_2 = [jnp.exp(m_old[bi] - m_a[bi]) * acc_1[bi] + far_pv(bi, 0, 2 * t) for bi in bis]
        m_b = [jnp.maximum(m_a[bi], jnp.max(s_b[bi], axis=-1, keepdims=True)) for bi in bis]
        for bi in bis:
            pfar_ref[bi, 1] = jnp.exp((s_b[bi] - m_b[bi]).astype(BF16))
        out = ()
        for bi in bis:
            out += (m_b[bi], acc_2[bi], jnp.exp(m_a[bi] - m_b[bi]))
        return out

    n_trips = (n_far + 1) // 2
    init = ()
    for bi in bis:
        init += (m_runs[bi], accs[bi], jnp.ones((rows, 1), F32))
    final = lax.fori_loop(0, n_trips, far_trip, init)

    low = lax.broadcasted_iota(jnp.int32, (Q_BLOCK, LANES), 1) < NSA_HEAD_DIM
    for bi in bis:
        acc = final[3 * bi + 2] * final[3 * bi + 1] + far_pv(bi, 1, 2 * n_trips - 1)
        o_s = acc * (1.0 / acc[:, NSA_HEAD_DIM:NSA_HEAD_DIM + 1])
        o_heads = []
        for hh in range(NSA_HPG):
            rs = slice(hh * Q_BLOCK, (hh + 1) * Q_BLOCK)
            o_heads.append(gates[bi][hh][0] * o_cs[bi][rs] + gates[bi][hh][1] * o_s[rs]
                           + gates[bi][hh][2] * o_ws[bi][rs])
        for pair in range(NSA_HPG // 2):
            o_ref[bi, :, pair * LANES:(pair + 1) * LANES] = jnp.where(
                low, o_heads[2 * pair], pltpu.roll(o_heads[2 * pair + 1], NSA_HEAD_DIM, axis=1)).astype(o_ref.dtype)


def _overlap_t(n_blk_pad, nseg):
    m = np.arange(nseg)[None, :]
    j = np.arange(n_blk_pad)[:, None]
    c_start = CMP_STRIDE * m - CMP_STRIDE
    c_end = CMP_STRIDE * m + CMP_STRIDE - 1
    ov = (c_start < j * SLC_BLOCK + SLC_BLOCK) & (c_end >= j * SLC_BLOCK) & (m >= 1)
    return ov.astype(np.float32)


def _nsa_prompt_attention(q128, p, gate_col_block, kc, vc, ks, vs, kw, vw, tb, lut):
    b, _, t, _ = q128.shape
    nseg = kc.shape[2]
    n_sb = t // SLC_BLOCK
    assert n_sb <= LANES and t % (4 * Q_BLOCK) == 0 and nseg % LANES == 0
    ovt = jnp.asarray(_overlap_t(LANES, nseg), dtype=BF16)
    bb = 2 if b % 2 == 0 else 1
    rows = NSA_HPG * Q_BLOCK
    per_bg = lambda nrow, cols: pl.BlockSpec((bb, None, nrow, cols), lambda b, g, i: (b, g, 0, 0),
                                             pipeline_mode=pl.Buffered(1))
    return pl.pallas_call(
        functools.partial(_nsa_prompt_kernel, nseg=nseg, n_sb=n_sb, bb=bb),
        grid=(b // bb, NSA_KV_GROUPS, t // Q_BLOCK),
        in_specs=[pl.BlockSpec((bb, NSA_HPG, Q_BLOCK, LANES), lambda b, g, i: (b, g, i, 0)),
                  pl.BlockSpec((bb, Q_BLOCK, LANES), lambda b, g, i: (b, i, gate_col_block)),
                  per_bg(nseg, LANES), per_bg(nseg, LANES),
                  per_bg(t, 2 * LANES), per_bg(t, LANES),
                  per_bg(t + WINDOW, LANES), per_bg(t + WINDOW, LANES),
                  pl.BlockSpec((NSA_HPG, Q_BLOCK, 2 * Q_BLOCK), lambda b, g, i: (g, 0, 0)),
                  pl.BlockSpec((NSA_HPG, 8, LANES), lambda b, g, i: (g, 0, 0)),
                  pl.BlockSpec((LANES, nseg), lambda b, g, i: (0, 0))],
        out_specs=pl.BlockSpec((bb, Q_BLOCK, NSA_HPG * NSA_HEAD_DIM), lambda b, g, i: (b, i, g)),
        out_shape=jax.ShapeDtypeStruct((b, t, NSA_Q_W), BF16),
        scratch_shapes=[pltpu.VMEM((bb, rows, 2 * LANES), BF16),
                        pltpu.VMEM((bb, rows, nseg), F32), pltpu.VMEM((bb, rows, nseg), BF16),
                        pltpu.VMEM((bb, Q_BLOCK, nseg), F32),
                        pltpu.VMEM((bb, rows, WINDOW + Q_BLOCK), F32),
                        pltpu.VMEM((bb, rows, WINDOW + Q_BLOCK), BF16),
                        pltpu.VMEM((bb, 2, rows, 4 * Q_BLOCK), BF16)],
        compiler_params=pltpu.CompilerParams(
            dimension_semantics=("parallel", "parallel", "arbitrary"), vmem_limit_bytes=VMEM_LIMIT),
        name="nsa_prompt_attention",
    )(q128, p, kc, vc, ks, vs, kw, vw, tb, lut, ovt)


def _gdn_conv_kernel(x_ref, w_ref, o_ref, carry_ref, *, tm, tc):
    j = pl.program_id(1)

    @pl.when(pl.program_id(2) == 0)
    def _():
        carry_ref[...] = jnp.zeros(carry_ref.shape, F32)

    x = x_ref[...]
    w = w_ref[...]
    prev = carry_ref[...]
    row8 = lax.broadcasted_iota(jnp.int32, (8, tc), 0)
    conv = x * w[CONV_W - 1:CONV_W, :]
    for sft in range(1, CONV_W):
        xs = pltpu.roll(x, sft, axis=0)
        top = jnp.where(row8 < sft, pltpu.roll(prev, sft, axis=0), xs[0:8])
        xs = top if tm == 8 else jnp.concatenate([top, xs[8:]], axis=0)
        conv = conv + xs * w[CONV_W - 1 - sft:CONV_W - sft, :]
    carry_ref[...] = x[tm - 8:tm, :]
    act = conv * jax.nn.sigmoid(conv)
    for hd in range(tc // GDN_HEAD_DIM):
        sl = slice(hd * GDN_HEAD_DIM, (hd + 1) * GDN_HEAD_DIM)
        a = act[:, sl]
        col0 = j * tc + hd * GDN_HEAD_DIM
        nrm = a * lax.rsqrt(jnp.sum(a * a, axis=-1, keepdims=True) + 1e-6)
        nrm = nrm * jnp.where(col0 < 1024, GDN_HEAD_DIM ** -0.5, 1.0)
        o_ref[:, sl] = jnp.where(col0 < 2048, nrm, a)


def _gdn_conv(p, conv_w):
    b, t, _ = p.shape
    tm = _row_tile(t)
    tc = 1024
    return pl.pallas_call(
        functools.partial(_gdn_conv_kernel, tm=tm, tc=tc),
        grid=(b, C_CONV // tc, t // tm),
        in_specs=[pl.BlockSpec((None, tm, tc), lambda b, j, i: (b, i, j)),
                  pl.BlockSpec((CONV_W, tc), lambda b, j, i: (0, j))],
        out_specs=pl.BlockSpec((None, tm, tc), lambda b, j, i: (b, i, j)),
        out_shape=jax.ShapeDtypeStruct((b, t, C_CONV), F32),
        scratch_shapes=[pltpu.VMEM((8, tc), F32)],
        compiler_params=pltpu.CompilerParams(
            dimension_semantics=("parallel", "parallel", "arbitrary"), vmem_limit_bytes=VMEM_LIMIT),
        name="gdn_conv",
    )(p, conv_w)


def _gdn_gate_kernel(ba_ref, alog_ref, dtb_ref, o_ref):
    x = ba_ref[...]
    y = x + dtb_ref[...]
    softplus = jnp.maximum(y, 0.0) + jnp.log1p(jnp.exp(-jnp.abs(y)))
    g = -jnp.exp(alog_ref[...]) * softplus
    lane = lax.broadcasted_iota(jnp.int32, x.shape, 1)
    o_ref[...] = jnp.where(lane < GDN_V_HEADS, jax.nn.sigmoid(x), g)


def _gdn_gates(p, ba_col_block, a_log, dt_bias):
    b, t, _ = p.shape
    tm = _row_tile(t)
    pad = lambda v: jnp.pad(v.reshape(1, GDN_V_HEADS), ((0, 0), (GDN_V_HEADS, LANES - 2 * GDN_V_HEADS)))
    return pl.pallas_call(
        _gdn_gate_kernel,
        grid=(b, t // tm),
        in_specs=[pl.BlockSpec((None, tm, LANES), lambda b, i: (b, i, ba_col_block)),
                  pl.BlockSpec((1, LANES), lambda b, i: (0, 0)),
                  pl.BlockSpec((1, LANES), lambda b, i: (0, 0))],
        out_specs=pl.BlockSpec((None, tm, LANES), lambda b, i: (b, i, 0)),
        out_shape=jax.ShapeDtypeStruct((b, t, LANES), F32),
        compiler_params=pltpu.CompilerParams(dimension_semantics=("parallel", "parallel")),
        name="gdn_gates",
    )(p, pad(a_log), pad(dt_bias))


def _bdot(a, b):
    return jnp.dot(a.astype(BF16), b.astype(BF16), preferred_element_type=F32)


GDN_PACK = 4
_PACK_ORDER = (0, 2, 1, 3)
_PACK_HEADS = tuple(GDN_PACK * p + o for p in range(GDN_V_HEADS // GDN_PACK) for o in _PACK_ORDER)


def _iota2(shape, axis):
    return lax.broadcasted_iota(jnp.int32, shape, axis)


def _packed_mm(a_cat, b_cat, bd_mask):
    b_bd = jnp.where(bd_mask, jnp.concatenate([b_cat] * GDN_PACK, axis=0), 0.0)
    return _bdot(a_cat, b_bd)


def _unit_lower_inverse_packed(ls, row, col, bd_mask):
    eye = (row == col).astype(F32)
    same16 = (row // 16) == (col // 16)
    same32 = (row // 32) == (col // 32)
    ms = [jnp.where(same16, -l, 0.0) for l in ls]
    ps = [eye + m for m in ms]
    for _ in range(3):
        ms = [_packed_mm(m, m, bd_mask) for m in ms]
        ps = [p + _packed_mm(p, m, bd_mask) for p, m in zip(ps, ms)]
    for level in (same32 & jnp.logical_not(same16), jnp.logical_not(same32)):
        ts = [_packed_mm(jnp.where(level, l, 0.0), p, bd_mask) for l, p in zip(ls, ps)]
        ps = [p - _packed_mm(p, t, bd_mask) for p, t in zip(ps, ts)]
    return ps


def _gdn_delta_kernel(act_ref, bg_ref, gt_ref, s0_ref, ltri_ref, lbd_ref, o_ref, s_ref, sbd_ref, *, bb):
    c, hd = GDN_CHUNK, GDN_HEAD_DIM
    n_packs = GDN_V_HEADS // GDN_PACK
    n_units = bb * n_packs
    n_pairs = GDN_V_HEADS // 2
    zero_hd = jnp.zeros((hd, hd), F32)

    @pl.when(pl.program_id(1) == 0)
    def _():
        for bi in range(bb):
            for pr in range(n_pairs):
                h0, h1 = _PACK_HEADS[2 * pr], _PACK_HEADS[2 * pr + 1]
                sbd_ref[bi * n_pairs + pr] = jnp.concatenate(
                    [jnp.concatenate([s0_ref[bi, h0], zero_hd], axis=-1),
                     jnp.concatenate([zero_hd, s0_ref[bi, h1]], axis=-1)], axis=0)

    bgs = [bg_ref[bi] for bi in range(bb)]
    cums = [sum(jnp.dot(ltri_ref[...], part, preferred_element_type=F32) for part in _split3(bg)) for bg in bgs]
    gcr_alls = [sum(_nt(part, lbd_ref[...]) for part in _split3(gt_ref[bi])) for bi in range(bb)]
    row = _iota2((c, GDN_PACK * c), 0)
    lane = _iota2((c, GDN_PACK * c), 1)
    col, slot = lane % c, lane // c
    incl, strict = row >= col, row > col
    bd_mask = (_iota2((4 * c, 4 * c), 0) // c) == (_iota2((4 * c, 4 * c), 1) // c)
    pair_mask = (_iota2((2 * hd, 2 * hd), 0) // hd) == (_iota2((2 * hd, 2 * hd), 1) // hd)
    k_mask = (_iota2((2 * hd, hd), 0) // hd) == (_iota2((2 * hd, hd), 1) // c)
    row_pair = _iota2((2 * hd, 1), 0)

    def slot_cat(cols):
        out = jnp.broadcast_to(cols[3], (c, GDN_PACK * c))
        for x in (2, 1, 0):
            out = jnp.where(slot == x, cols[x], out)
        return out

    def side_by_side(a, b):
        return jnp.concatenate([a, b], axis=-1)

    qs, ks, betas, gcs, lmats, a_ins = [], [], [], [], [], []
    for u in range(n_units):
        bi, p = divmod(u, n_packs)
        bg, cum, gcr_all = bgs[bi], cums[bi], gcr_alls[bi]
        heads = _PACK_HEADS[GDN_PACK * p:GDN_PACK * (p + 1)]
        qa, qb = (act_ref[bi, :, (2 * p + i) * hd:(2 * p + i + 1) * hd] for i in (0, 1))
        ka, kb = (act_ref[bi, :, 1024 + (2 * p + i) * hd:1024 + (2 * p + i + 1) * hd] for i in (0, 1))
        kt = jnp.concatenate([ka, kb], axis=0).T
        k_bd = jnp.where(k_mask, jnp.concatenate([kt, kt], axis=0), 0.0)
        kq = _bdot(jnp.concatenate([side_by_side(ka, kb), side_by_side(qa, qb)], axis=0), k_bd)
        kk = side_by_side(kq[:c], kq[:c])
        qk = side_by_side(kq[c:], kq[c:])
        beta = [bg[:, h:h + 1] for h in heads]
        gc = [cum[:, GDN_V_HEADS + h:GDN_V_HEADS + h + 1] for h in heads]
        decay = jnp.where(incl, jnp.exp(jnp.where(incl, slot_cat(gc) - gcr_all[p:p + 1, :], 0.0)), 0.0)
        lmats.append(jnp.where(strict, slot_cat(beta) * kk * decay, 0.0))
        a_ins.append(qk * decay)
        qs.append((qa, qb, qa, qb)); ks.append((ka, kb, ka, kb)); betas.append(beta); gcs.append(gc)

    tinvs = _unit_lower_inverse_packed(lmats, row, col, bd_mask)

    uws, egs = [], []
    for u in range(n_units):
        bi, p = divmod(u, n_packs)
        bands = []
        eg = [jnp.exp(g) for g in gcs[u]]
        for x in range(GDN_PACK):
            h = _PACK_HEADS[GDN_PACK * p + x]
            vh = act_ref[bi, :, 2048 + h * hd:2048 + (h + 1) * hd]
            rhs = betas[u][x] * side_by_side(vh, ks[u][x] * eg[x])
            pieces = [jnp.zeros((c, 2 * hd * x), F32)] * (x > 0) + [rhs] + [jnp.zeros((c, 2 * hd * (3 - x)), F32)] * (x < 3)
            bands.append(jnp.concatenate(pieces, axis=-1))
        uws.append(_bdot(tinvs[u], jnp.concatenate(bands, axis=0)))
        egs.append(eg)

    wss, s_olds = [], []
    for u in range(n_units):
        for pr in range(2):
            x0, x1 = 2 * pr, 2 * pr + 1
            w0, w1 = (uws[u][:, 2 * hd * x + hd:2 * hd * (x + 1)] for x in (x0, x1))
            lhs = jnp.concatenate([side_by_side(w0, w1),
                                   side_by_side(qs[u][x0] * egs[u][x0], qs[u][x1] * egs[u][x1])], axis=0)
            s_old = sbd_ref[2 * u + pr]
            s_olds.append(s_old)
            wss.append(_bdot(lhs, s_old))

    v_news = []
    for u in range(n_units):
        vn = []
        for x in range(GDN_PACK):
            ws = wss[2 * u + x // 2]
            vn.append(uws[u][:, 2 * hd * x:2 * hd * x + hd] - ws[:c, hd * (x % 2):hd * (x % 2 + 1)])
        v_news.append(vn)
    for u in range(n_units):
        bi, p = divmod(u, n_packs)
        bands = []
        for x in range(GDN_PACK):
            pieces = [jnp.zeros((c, hd * x), F32)] * (x > 0) + [v_news[u][x]] + [jnp.zeros((c, hd * (3 - x)), F32)] * (x < 3)
            bands.append(jnp.concatenate(pieces, axis=-1))
        av = _bdot(a_ins[u], jnp.concatenate(bands, axis=0))
        for x in range(GDN_PACK):
            h = _PACK_HEADS[GDN_PACK * p + x]
            ws = wss[2 * u + x // 2]
            o_ref[bi, :, h * hd:(h + 1) * hd] = ws[c:, hd * (x % 2):hd * (x % 2 + 1)] + av[:, hd * x:hd * (x + 1)]
    zrows = jnp.zeros((c, 2 * hd), F32)
    for u in range(n_units):
        for pr in range(2):
            x0, x1 = 2 * pr, 2 * pr + 1
            gl0, gl1 = gcs[u][x0][c - 1:c, :], gcs[u][x1][c - 1:c, :]
            kd = jnp.concatenate([side_by_side(ks[u][x0] * jnp.exp(gl0 - gcs[u][x0]),
                                               ks[u][x1] * jnp.exp(gl1 - gcs[u][x1])), zrows], axis=0)
            kd_t = jnp.concatenate([kd[:, :hd].T, kd[:, hd:].T], axis=0)
            vn = jnp.concatenate([side_by_side(v_news[u][x0], v_news[u][x1]), zrows], axis=0)
            d_last = jnp.where(row_pair < hd, jnp.exp(gl0), jnp.exp(gl1))
            sbd_ref[2 * u + pr] = jnp.where(pair_mask, s_olds[2 * u + pr] * d_last + _bdot(kd_t, vn), 0.0)

    @pl.when(pl.program_id(1) == pl.num_programs(1) - 1)
    def _():
        for bi in range(bb):
            for pr in range(n_pairs):
                s_pair = sbd_ref[bi * n_pairs + pr]
                s_ref[bi, _PACK_HEADS[2 * pr]] = s_pair[:hd, :hd]
                s_ref[bi, _PACK_HEADS[2 * pr + 1]] = s_pair[hd:, hd:]


def _gdn_delta(act, bg, s0):
    b, t, _ = act.shape
    nc = t // GDN_CHUNK
    n_packs = GDN_V_HEADS // GDN_PACK
    wp = GDN_PACK * GDN_CHUNK
    g_rows = bg[:, :, GDN_V_HEADS:2 * GDN_V_HEADS][:, :, np.asarray(_PACK_HEADS)]
    g_rows = g_rows.reshape(b, nc, GDN_CHUNK, n_packs, GDN_PACK).transpose(0, 1, 3, 4, 2).reshape(b, nc, n_packs, wp)
    g_rows = jnp.pad(g_rows, ((0, 0), (0, 0), (0, 8 - n_packs), (0, 0)))
    tri = np.tril(np.ones((GDN_CHUNK, GDN_CHUNK), np.float32))
    ltri = jnp.asarray(tri, dtype=BF16)
    lbd = jnp.asarray(np.kron(np.eye(GDN_PACK, dtype=np.float32), tri), dtype=BF16)
    bb = 2 if b % 2 == 0 else 1
    state_spec = pl.BlockSpec((bb, GDN_V_HEADS, GDN_HEAD_DIM, GDN_HEAD_DIM), lambda b, n: (b, 0, 0, 0))
    return pl.pallas_call(
        functools.partial(_gdn_delta_kernel, bb=bb),
        grid=(b // bb, nc),
        in_specs=[pl.BlockSpec((bb, GDN_CHUNK, C_CONV), lambda b, n: (b, n, 0)),
                  pl.BlockSpec((bb, GDN_CHUNK, LANES), lambda b, n: (b, n, 0)),
                  pl.BlockSpec((bb, None, 8, wp), lambda b, n: (b, n, 0, 0)),
                  state_spec,
                  pl.BlockSpec((GDN_CHUNK, GDN_CHUNK), lambda b, n: (0, 0)),
                  pl.BlockSpec((wp, wp), lambda b, n: (0, 0))],
        out_specs=[pl.BlockSpec((bb, GDN_CHUNK, GDN_V_W), lambda b, n: (b, n, 0)), state_spec],
        out_shape=[jax.ShapeDtypeStruct((b, t, GDN_V_W), F32),
                   jax.ShapeDtypeStruct(s0.shape, F32)],
        scratch_shapes=[pltpu.VMEM((bb * GDN_V_HEADS // 2, 2 * GDN_HEAD_DIM, 2 * GDN_HEAD_DIM), F32)],
        compiler_params=pltpu.CompilerParams(
            dimension_semantics=("parallel", "arbitrary"), vmem_limit_bytes=VMEM_LIMIT),
        name="gdn_delta_rule",
    )(act, bg, g_rows, s0, ltri, lbd)


SAMPLE_ROWS = NSA_HEADS * 4
SEL_PAGES_PER_STEP = 32


def _sample_cmp_kernel(q_ref, kc_ref, vc_ref, lut_ref, ov_ref, oc_ref, un_ref, *, nseg, past_len, n_sb, nq):
    rg = NSA_HPG * nq
    ri = lax.broadcasted_iota(jnp.int32, (rg, nseg), 0)
    mi = lax.broadcasted_iota(jnp.int32, (rg, nseg), 1)
    dist = past_len + ri % nq - CMP_STRIDE * mi - (CMP_STRIDE - 1)
    valid = (dist >= 0) & (mi >= 1)
    jl = lax.broadcasted_iota(jnp.int32, (8, un_ref.shape[-1]), 1)
    q_blk = (past_len + lax.broadcasted_iota(jnp.int32, jl.shape, 0) % nq) // SLC_BLOCK
    forced = (jl == 0) | (jl == q_blk) | (jl == q_blk - 1)
    jf = jl.astype(F32)
    pcs_parts = []
    for g in range(NSA_KV_GROUPS):
        sc = _nt(q_ref[g], kc_ref[g])
        tail = sc[:, nseg - LANES:] + _lut_gather(lut_ref[g], dist[:, nseg - LANES:])
        sc = jnp.where(valid, jnp.concatenate([sc[:, :nseg - LANES], tail], axis=-1), NEG)
        mx = jnp.max(sc, axis=-1, keepdims=True)
        e = jnp.where(valid, jnp.exp(sc - mx), 0.0)
        ssum = jnp.sum(e, axis=-1, keepdims=True)
        pc = e / jnp.where(ssum > 0, ssum, 1.0)
        oc_ref[g] = jnp.dot(pc.astype(BF16), vc_ref[g], preferred_element_type=F32)
        pcs = pc
        for hh in range(1, NSA_HPG):
            pcs = pcs + pltpu.roll(pc, hh * nq, axis=0)
        hi = pcs[0:8].astype(BF16).astype(F32)
        pcs_parts += [hi, pcs[0:8] - hi]
    imp_all = jnp.dot(jnp.concatenate(pcs_parts, axis=0).astype(BF16), ov_ref[...], preferred_element_type=F32)
    scores = []
    for g in range(NSA_KV_GROUPS):
        imp = imp_all[16 * g:16 * g + 8] + imp_all[16 * g + 8:16 * g + 16]
        score = jnp.where(forced, -jnp.inf, jnp.where(jl <= q_blk, imp, -1e4))
        scores.append(jnp.where(jl < n_sb, score, -3e38))
    sels = [forced.astype(F32)] * NSA_KV_GROUPS
    for _ in range(min(N_SELECT, n_sb) - 3):
        for g in range(NSA_KV_GROUPS):
            mxs = jnp.max(scores[g], axis=-1, keepdims=True)
            idx = jnp.min(jnp.where(scores[g] == mxs, jf, 1e9), axis=-1, keepdims=True)
            hit = jf == idx
            sels[g] = jnp.where(hit, 1.0, sels[g])
            scores[g] = jnp.where(hit, -jnp.inf, scores[g])
    for g in range(NSA_KV_GROUPS):
        un_ref[g] = 1.0 - sels[g]


def _sample_cmp(q16, kc, vc, lut16, past_len, nq):
    b = q16.shape[0]
    nseg = kc.shape[2]
    rg = NSA_HPG * nq
    n_sb = past_len // SLC_BLOCK + 1
    n_sb_pad = -(-n_sb // LANES) * LANES
    assert nq == 4 and nseg * CMP_STRIDE == past_len
    m = np.arange(nseg)[:, None]
    j = np.arange(n_sb_pad)[None, :]
    ov = ((CMP_STRIDE * m - CMP_STRIDE < j * SLC_BLOCK + SLC_BLOCK) & (CMP_STRIDE * m + CMP_STRIDE - 1 >= j * SLC_BLOCK)
          & (m >= 1) & (j < n_sb)).astype(np.float32)
    whole = lambda *shape: pl.BlockSpec((None,) + shape, lambda b: (b,) + (0,) * len(shape))
    return pl.pallas_call(
        functools.partial(_sample_cmp_kernel, nseg=nseg, past_len=past_len, n_sb=n_sb, nq=nq),
        grid=(b,),
        in_specs=[whole(NSA_KV_GROUPS, rg, LANES), whole(NSA_KV_GROUPS, nseg, LANES), whole(NSA_KV_GROUPS, nseg, LANES),
                  pl.BlockSpec((NSA_KV_GROUPS, rg, LANES), lambda b: (0, 0, 0)),
                  pl.BlockSpec((nseg, n_sb_pad), lambda b: (0, 0))],
        out_specs=[whole(NSA_KV_GROUPS, rg, LANES), whole(NSA_KV_GROUPS, 8, n_sb_pad)],
        out_shape=[jax.ShapeDtypeStruct((b, NSA_KV_GROUPS, rg, LANES), F32),
                   jax.ShapeDtypeStruct((b, NSA_KV_GROUPS, 8, n_sb_pad), F32)],
        compiler_params=pltpu.CompilerParams(dimension_semantics=("parallel",), vmem_limit_bytes=VMEM_LIMIT),
        name="nsa_sample_cmp_topk",
    )(q16, kc, vc, lut16, jnp.asarray(ov, dtype=BF16))


def _sample_sel_kernel(tab_ref, *refs, npg, past_len, nq):
    del tab_ref
    pages = refs[:npg]
    qbd_ref, un_ref, ee_ref, far_ref, lut_ref, m_ref, l_ref, acc_ref = refs[npg:]
    c = pl.program_id(1)
    kc = npg * PAGE_SIZE

    @pl.when(c == 0)
    def _():
        m_ref[...] = jnp.full(m_ref.shape, NEG, F32)
        l_ref[...] = jnp.zeros(l_ref.shape, F32)
        acc_ref[...] = jnp.zeros(acc_ref.shape, F32)

    kt = jnp.concatenate([pg[0] for pg in pages], axis=1).astype(BF16)
    vt = jnp.concatenate([pg[1] for pg in pages], axis=1).astype(BF16)
    s = (jnp.dot(qbd_ref[...], kt, preferred_element_type=F32) + far_ref[...][:, 0:1]
         + jnp.dot(un_ref[...], ee_ref[...], preferred_element_type=F32))
    ri = lax.broadcasted_iota(jnp.int32, (SAMPLE_ROWS, LANES), 0)
    li = lax.broadcasted_iota(jnp.int32, (SAMPLE_ROWS, LANES), 1)
    dist = past_len + ri % nq - (c * kc + kc - LANES + li)
    s = jnp.concatenate([s[:, :kc - LANES], s[:, kc - LANES:] + _lut_gather(lut_ref[...], dist)], axis=-1)
    m_old = m_ref[...][:, 0:1]
    m_new = jnp.maximum(m_old, jnp.max(s, axis=-1, keepdims=True))
    alpha = jnp.exp(m_old - m_new)
    p = jnp.exp(s - m_new)
    l_ref[...] = alpha * l_ref[...] + jnp.sum(p, axis=-1, keepdims=True)
    acc_ref[...] = alpha * acc_ref[...] + _nt(p.astype(BF16), vt)
    m_ref[...] = jnp.broadcast_to(m_new, m_ref.shape)


def _sample_sel(pages, table, qbd, unsel_c, farcol, lut64, past_len, nq):
    b, n_pages = table.shape
    npg = min(SEL_PAGES_PER_STEP, n_pages)
    kc = npg * PAGE_SIZE
    nch = n_pages // npg
    blk_per_chunk = kc // SLC_BLOCK
    ee = np.zeros((LANES, kc), np.float32)
    ee[np.arange(kc) // SLC_BLOCK, np.arange(kc)] = NEG
    assert blk_per_chunk <= LANES

    def page_spec(j):
        return pl.BlockSpec((None, 2, NSA_KV_W // 2, PAGE_SIZE), lambda b, c, tab: (tab[b, c * npg + j], 0, 0, 0))

    const = lambda *shape: pl.BlockSpec(shape, lambda b, c, tab: (0,) * len(shape))
    acc_spec = lambda cols: pl.BlockSpec((None, SAMPLE_ROWS, cols), lambda b, c, tab: (b, 0, 0))
    grid_spec = pltpu.PrefetchScalarGridSpec(
        num_scalar_prefetch=1,
        grid=(b, nch),
        in_specs=[page_spec(j) for j in range(npg)] + [
            pl.BlockSpec((None, SAMPLE_ROWS, 2 * LANES), lambda b, c, tab: (b, 0, 0)),
            pl.BlockSpec((None, None, SAMPLE_ROWS, LANES), lambda b, c, tab: (b, c, 0, 0)),
            const(LANES, kc), const(SAMPLE_ROWS, LANES), const(SAMPLE_ROWS, LANES)],
        out_specs=[acc_spec(LANES), acc_spec(LANES), acc_spec(2 * LANES)],
    )
    return pl.pallas_call(
        functools.partial(_sample_sel_kernel, npg=npg, past_len=past_len, nq=nq),
        grid_spec=grid_spec,
        out_shape=[jax.ShapeDtypeStruct((b, SAMPLE_ROWS, LANES), F32),
                   jax.ShapeDtypeStruct((b, SAMPLE_ROWS, LANES), F32),
                   jax.ShapeDtypeStruct((b, SAMPLE_ROWS, 2 * LANES), F32)],
        compiler_params=pltpu.CompilerParams(
            dimension_semantics=("parallel", "arbitrary"), vmem_limit_bytes=VMEM_LIMIT),
        name="nsa_sample_selected",
    )(table, *([pages] * npg), qbd, unsel_c, jnp.asarray(ee, dtype=BF16), farcol, lut64)


def _own_group_cols(x, grp):
    out = jnp.zeros((x.shape[0], NSA_HEAD_DIM), F32)
    for g in range(NSA_KV_GROUPS):
        out = jnp.where(grp == g, x[:, g * NSA_HEAD_DIM:(g + 1) * NSA_HEAD_DIM], out)
    return out


def _sample_final_kernel(qbd_ref, m_ref, l_ref, acc_ref, snew_ref, wc_ref, wnew_ref, oc_ref, gr_ref, far_ref, lut_ref,
                         o_ref, *, nq, w_buf):
    rows = SAMPLE_ROWS
    qbd = qbd_ref[...]
    far = far_ref[...][:, 0:1]
    lut = lut_ref[...]
    ri = lax.broadcasted_iota(jnp.int32, (rows, LANES), 0)
    li = lax.broadcasted_iota(jnp.int32, (rows, LANES), 1)
    tok = ri % nq
    grp = lax.broadcasted_iota(jnp.int32, (rows, NSA_HEAD_DIM), 0) // (NSA_HPG * nq)

    knew = snew_ref[...]
    s_new = _nt(qbd, knew[:, :256].astype(BF16)) + far
    d_new = tok - li
    s_new = jnp.where((d_new >= 0) & (li < nq), s_new + _lut_gather(lut, d_new), NEG)
    m_old = m_ref[...][:, 0:1]
    m_new = jnp.maximum(m_old, jnp.max(s_new, axis=-1, keepdims=True))
    alpha = jnp.exp(m_old - m_new)
    p_new = jnp.exp(s_new - m_new)
    l_s = alpha * l_ref[...][:, 0:1] + jnp.sum(p_new, axis=-1, keepdims=True)
    acc_s = alpha * acc_ref[...] + jnp.dot(p_new.astype(BF16), knew[:, 256:].astype(BF16), preferred_element_type=F32)
    o_s = _own_group_cols(acc_s, grp) / l_s

    w_new = wnew_ref[...]
    s_w = jnp.concatenate([jnp.dot(qbd, wc_ref[0].astype(BF16), preferred_element_type=F32),
                           _nt(qbd, w_new[:, :256].astype(BF16))], axis=-1) + far
    n_w = w_buf + LANES
    idx = lax.broadcasted_iota(jnp.int32, (rows, n_w), 1)
    d_w = w_buf + lax.broadcasted_iota(jnp.int32, (rows, n_w), 0) % nq - idx
    ok_w = (d_w >= 0) & (d_w < WINDOW) & (idx < w_buf + nq)
    corr = [jnp.zeros((rows, n_w - 2 * LANES), F32)]
    for cidx in range(2):
        lo = n_w - 2 * LANES + cidx * LANES
        corr.append(_lut_gather(lut, d_w[:, lo:lo + LANES]))
    s_w = jnp.where(ok_w, s_w + jnp.concatenate(corr, axis=-1), NEG)
    m_w = jnp.max(s_w, axis=-1, keepdims=True)
    p_w = jnp.exp(s_w - m_w)
    l_w = jnp.sum(p_w, axis=-1, keepdims=True)
    acc_w = (_nt(p_w[:, :w_buf].astype(BF16), wc_ref[1].astype(BF16))
             + jnp.dot(p_w[:, w_buf:].astype(BF16), w_new[:, 256:].astype(BF16), preferred_element_type=F32))
    o_w = _own_group_cols(acc_w, grp) / l_w

    gt = jax.nn.sigmoid(gr_ref[...])
    o_ref[...] = gt[:, 0:1] * oc_ref[...][:, :NSA_HEAD_DIM] + gt[:, 1:2] * o_s + gt[:, 2:3] * o_w


def _sample_final(qbd, m, l, acc, snew, wcache, wnew, o_c, graw, farcol, lut64, nq):
    b = qbd.shape[0]
    w_buf = wcache.shape[-1]
    assert w_buf == WINDOW
    whole = lambda *shape: pl.BlockSpec((None,) + shape, lambda b: (b,) + (0,) * len(shape))
    const = lambda *shape: pl.BlockSpec(shape, lambda b: (0,) * len(shape))
    return pl.pallas_call(
        functools.partial(_sample_final_kernel, nq=nq, w_buf=w_buf),
        grid=(b,),
        in_specs=[whole(SAMPLE_ROWS, 2 * LANES), whole(SAMPLE_ROWS, LANES), whole(SAMPLE_ROWS, LANES),
                  whole(SAMPLE_ROWS, 2 * LANES), whole(LANES, NSA_KV_W), whole(2, NSA_KV_W // 2, w_buf),
                  whole(LANES, NSA_KV_W),
                  whole(SAMPLE_ROWS, LANES), whole(SAMPLE_ROWS, LANES),
                  const(SAMPLE_ROWS, LANES), const(SAMPLE_ROWS, LANES)],
        out_specs=whole(SAMPLE_ROWS, NSA_HEAD_DIM),
        out_shape=jax.ShapeDtypeStruct((b, SAMPLE_ROWS, NSA_HEAD_DIM), F32),
        compiler_params=pltpu.CompilerParams(dimension_semantics=("parallel",), vmem_limit_bytes=VMEM_LIMIT),
        name="nsa_sample_final",
    )(qbd, m, l, acc, snew, wcache, wnew, o_c, graw, farcol, lut64)


def _ffn(x, mod, gains, w_in, w_out):
    hid = _norm_mod_swiglu(x, gains[2], mod[3], mod[4], w_in)
    return _matmul_rms_residual(hid, w_out, x, mod[5], gains[3])


def _nsa_layout_kernel(pq_ref, pc_ref, ps_ref, pw_ref, bias_ref, q_ref, ks_ref, vs_ref, kw_ref, vw_ref,
                       kvc_ref, kvc_t_ref, kvs_t_ref, kvw_t_ref, *, tm):
    i = pl.program_id(1)
    lane = lax.broadcasted_iota(jnp.int32, (tm, LANES), 1)
    low = lane < NSA_HEAD_DIM
    kvc_ref[...] = pc_ref[...]
    kvc_t_ref[...] = pc_ref[...].T
    kvs_t_ref[...] = ps_ref[...].T
    kvw_t_ref[...] = pw_ref[...].T

    def head_tile(ref, h):
        tile = ref[:, (h // 2) * LANES:(h // 2 + 1) * LANES]
        return pltpu.roll(tile, NSA_HEAD_DIM, axis=1) if h % 2 else tile

    for h in range(NSA_HEADS):
        q_ref[h] = jnp.where(low, head_tile(pq_ref, h), bias_ref[h]).astype(BF16)
    key = jnp.maximum(i - 1, 0) * tm + lax.broadcasted_iota(jnp.int32, (tm, LANES), 0)
    onehot = jnp.where(key // SLC_BLOCK == lane, NEG, 0.0).astype(BF16)
    k_ones = jnp.where((lane == NSA_HEAD_DIM) | (lane == NSA_HEAD_DIM + 1), 1.0, 0.0)
    v_ones = jnp.where(lane == NSA_HEAD_DIM, 1.0, 0.0)
    live = i > 0
    for g in range(NSA_KV_GROUPS):
        ks_ref[g, :, :LANES] = onehot
        ks_ref[g, :, LANES:] = jnp.where(low, head_tile(ps_ref, g), k_ones).astype(BF16)
        vs_ref[g] = jnp.where(low, head_tile(ps_ref, NSA_KV_GROUPS + g), v_ones).astype(BF16)
        kw_ref[g] = jnp.where(live, jnp.where(low, head_tile(pw_ref, g), k_ones), 0.0).astype(BF16)
        vw_ref[g] = jnp.where(live, jnp.where(low, head_tile(pw_ref, NSA_KV_GROUPS + g), v_ones), 0.0).astype(BF16)


def _nsa_layouts(p, bias_cols):
    b, t, _ = p.shape
    tm = WINDOW
    assert t % tm == 0
    prev = lambda i: jnp.maximum(i - 1, 0)
    src = lambda width, col_block: pl.BlockSpec((None, tm, width), lambda b, i: (b, prev(i), col_block))
    same = lambda heads, width: pl.BlockSpec((None, heads, tm, width), lambda b, i: (b, 0, prev(i), 0))
    late = pl.BlockSpec((None, NSA_KV_GROUPS, tm, LANES), lambda b, i: (b, 0, i, 0))
    rows_minor = pl.BlockSpec((None, NSA_KV_W, tm), lambda b, i: (b, 0, prev(i)))
    bias = jnp.pad(bias_cols.astype(F32), ((0, 0), (NSA_HEAD_DIM, 0))).reshape(NSA_HEADS, 1, LANES)
    sds = lambda heads, rows, width: jax.ShapeDtypeStruct((b, heads, rows, width), BF16)
    kv_t = jax.ShapeDtypeStruct((b, NSA_KV_W, t), F32)
    kv_col = lambda n: (NSA_Q_W + n * NSA_KV_W) // NSA_KV_W
    return pl.pallas_call(
        functools.partial(_nsa_layout_kernel, tm=tm),
        grid=(b, t // tm + 1),
        in_specs=[src(NSA_Q_W, 0), src(NSA_KV_W, kv_col(0)), src(NSA_KV_W, kv_col(1)), src(NSA_KV_W, kv_col(2)),
                  pl.BlockSpec((NSA_HEADS, 1, LANES), lambda b, i: (0, 0, 0))],
        out_specs=[same(NSA_HEADS, LANES), same(NSA_KV_GROUPS, 2 * LANES), same(NSA_KV_GROUPS, LANES), late, late,
                   pl.BlockSpec((None, tm, NSA_KV_W), lambda b, i: (b, prev(i), 0)),
                   rows_minor, rows_minor, rows_minor],
        out_shape=[sds(NSA_HEADS, t, LANES), sds(NSA_KV_GROUPS, t, 2 * LANES), sds(NSA_KV_GROUPS, t, LANES),
                   sds(NSA_KV_GROUPS, t + WINDOW, LANES), sds(NSA_KV_GROUPS, t + WINDOW, LANES),
                   jax.ShapeDtypeStruct((b, t, NSA_KV_W), F32), kv_t, kv_t, kv_t],
        compiler_params=pltpu.CompilerParams(
            dimension_semantics=("parallel", "arbitrary"), vmem_limit_bytes=VMEM_LIMIT),
        name="nsa_layouts",
    )(p, p, p, p, bias)


def _nsa_prompt(x, mod, gains, w_in, cmp_w, w_out, tb, lut, bias_cols):
    b, t, _ = x.shape
    p = _norm_mod_linear(x, gains[0], mod[0], mod[1], w_in)
    q128, ks, vs, kw, vw, kvc, kvc_t, kvs_t, kvw_t = _nsa_layouts(p, bias_cols)
    n_pages = t // PAGE_SIZE
    table = jnp.arange(b * n_pages, dtype=jnp.int32).reshape(b, n_pages)
    kc, vc = _compress(kvc.reshape(b * n_pages, SEGS_PER_PAGE, SEG_W), table, *cmp_w)
    o = _nsa_prompt_attention(q128, p, (NSA_Q_W + 3 * NSA_KV_W) // LANES, kc, vc, ks, vs, kw, vw, tb, lut)
    x = _matmul_rms_residual(o, w_out, x, mod[2], gains[1])
    rows_major = lambda a: a.reshape(b, 2, NSA_KV_GROUPS, NSA_HEAD_DIM, -1).transpose(0, 4, 1, 2, 3)
    return x, rows_major(kvc_t), rows_major(kvs_t), rows_major(kvw_t[:, :, t - min(WINDOW, t):])


def _nsa_sample(x, mod, gains, w_in, cmp_w, w_out, lut, rel_bias, bias_cols, cache_cmp, cache_slc, cache_win,
                page_table, db, nq):
    n_pages = page_table.shape[1]
    past_len = n_pages * PAGE_SIZE
    rows = db * nq
    p = _norm_mod_linear(x, gains[0], mod[0], mod[1], w_in)[0]
    kvc, kvs, kvw = (p[:, 1024 + i * NSA_KV_W:1024 + (i + 1) * NSA_KV_W] for i in range(3))
    qh = p[:, :NSA_Q_W].astype(BF16).reshape(db, nq, NSA_KV_GROUPS, NSA_HPG, NSA_HEAD_DIM).transpose(0, 2, 3, 1, 4)
    q16 = jnp.concatenate([qh, jnp.broadcast_to(bias_cols.reshape(1, NSA_KV_GROUPS, NSA_HPG, 1, NSA_HEAD_DIM), qh.shape)],
                          axis=-1).reshape(db, NSA_KV_GROUPS, NSA_HPG * nq, LANES)
    eye_g = jnp.eye(NSA_KV_GROUPS, dtype=BF16)
    qbd = jnp.einsum('bghtd,gj->bghtjd', qh, eye_g).reshape(db, SAMPLE_ROWS, NSA_KV_GROUPS * NSA_HEAD_DIM)
    row_head = np.repeat(np.arange(NSA_HEADS), nq)
    lut64 = lut[:, 0, :][row_head]
    farcol = jnp.broadcast_to(rel_bias[N_BUCKETS - 1][row_head][:, None], (SAMPLE_ROWS, LANES))
    rows_minor = lambda cache: jnp.transpose(cache, (0, 2, 3, 4, 1))
    kc, vc = _compress(rows_minor(cache_cmp).reshape(-1, 2, 2, LANES, PAGE_SIZE), page_table, *cmp_w)
    o_c, unsel = _sample_cmp(q16, kc, vc, lut64.reshape(NSA_KV_GROUPS, NSA_HPG * nq, LANES), past_len, nq)
    npg = min(SEL_PAGES_PER_STEP, n_pages)
    nch = n_pages // npg
    bpc = npg * PAGE_SIZE // SLC_BLOCK
    un = unsel[:, :, :nq, :past_len // SLC_BLOCK].reshape(db, NSA_KV_GROUPS, 1, nq, nch, bpc)
    un = jnp.broadcast_to(un, (db, NSA_KV_GROUPS, NSA_HPG, nq, nch, bpc)).transpose(0, 4, 1, 2, 3, 5)
    un = jnp.pad(un.reshape(db, nch, SAMPLE_ROWS, bpc), ((0, 0), (0, 0), (0, 0), (0, LANES - bpc))).astype(BF16)
    m, l, acc = _sample_sel(rows_minor(cache_slc).reshape(-1, 2, NSA_KV_W // 2, PAGE_SIZE), page_table, qbd, un, farcol,
                            lut64, past_len, nq)
    pad_rows = lambda a: jnp.pad(a.reshape(db, nq, NSA_KV_W), ((0, 0), (0, LANES - nq), (0, 0)))
    wcache = rows_minor(cache_win).reshape(db, 2, NSA_KV_W // 2, -1)
    graw = p[:, NSA_Q_W + 3 * NSA_KV_W:NSA_Q_W + 3 * NSA_KV_W + 3 * NSA_HEADS]
    graw = graw.reshape(db, nq, NSA_HEADS, 3).transpose(0, 2, 1, 3).reshape(db, SAMPLE_ROWS, 3)
    graw = jnp.pad(graw, ((0, 0), (0, 0), (0, LANES - 3)))
    o = _sample_final(qbd, m, l, acc, pad_rows(kvs), wcache, pad_rows(kvw), o_c.reshape(db, SAMPLE_ROWS, LANES), graw,
                      farcol, lut64, nq)
    o = o.reshape(db, NSA_HEADS, nq, NSA_HEAD_DIM).transpose(0, 2, 1, 3).reshape(1, rows, NSA_Q_W)
    x = _matmul_rms_residual(o, w_out, x, mod[2], gains[1])
    shape5 = (db, nq, 2, NSA_KV_GROUPS, NSA_HEAD_DIM)
    kv_win = jnp.concatenate([cache_win, kvw.reshape(shape5)], axis=1)[:, -cache_win.shape[1]:]
    return x, kvc.reshape(shape5), kvs.reshape(shape5), kv_win


def _gdn_prompt(x, mod, gains, w_in, conv_w, a_log, dt_bias, norm_w, w_out):
    b, t, _ = x.shape
    p = _norm_mod_linear(x, gains[0], mod[0], mod[1], w_in)
    act = _gdn_conv(p, conv_w)
    bg = _gdn_gates(p, (C_CONV + GDN_V_W) // LANES, a_log, dt_bias)
    s0 = jnp.zeros((b, GDN_V_HEADS, GDN_HEAD_DIM, GDN_HEAD_DIM), F32)
    o, s_fin = _gdn_delta(act, bg, s0)
    x = _gdn_out(o, p, C_CONV // GDN_V_W, norm_w, w_out, x, mod[2], gains[1])
    return x, p[:, t - (CONV_W - 1):, :C_CONV], s_fin


def _gdn_sample(x, mod, gains, w_in, conv_w, a_log, dt_bias, norm_w, w_out, conv_buf, s0, db, nq):
    p = _norm_mod_linear(x, gains[0], mod[0], mod[1], w_in)
    qkv = p[0, :, :C_CONV].reshape(db, nq, C_CONV)
    xp = jnp.concatenate([conv_buf, qkv], axis=1)
    act = _gdn_conv(jnp.pad(xp, ((0, 0), (0, 8 - xp.shape[1]), (0, 0))), conv_w)[:, CONV_W - 1:CONV_W - 1 + nq]
    bg = _gdn_gates(p, (C_CONV + GDN_V_W) // LANES, a_log, dt_bias).reshape(db, nq, LANES)
    pad_t = ((0, 0), (0, GDN_CHUNK - nq), (0, 0))
    o, s_fin = _gdn_delta(jnp.pad(act, pad_t), jnp.pad(bg, pad_t), s0)
    o = o[:, :nq].reshape(1, db * nq, GDN_V_W)
    x = _gdn_out(o, p, C_CONV // GDN_V_W, norm_w, w_out, x, mod[2], gains[1])
    return x, xp[:, -(CONV_W - 1):], s_fin


def kernel(x_prompt, x_sample, c_prompt, c_sample, cache_kv_cmp, cache_kv_slc, cache_kv_win, state_conv, state_ssm,
           page_table, rel_bias, norm_gains, w_ada, b_ada, w_ffn_in, w_ffn_out, nsa_w_in, nsa_cmp_pe, nsa_cmp_w1,
           nsa_cmp_b1, nsa_cmp_w2, nsa_w_out, gdn_w_in, gdn_conv_w, gdn_a_log, gdn_dt_bias, gdn_norm_w, gdn_w_out):
    depth = w_ada.shape[0]
    bp, t, d = x_prompt.shape
    db, nq, _ = x_sample.shape
    assert nq + CONV_W - 1 <= 8 and nq <= GDN_CHUNK

    c_all = jnp.concatenate([c_prompt, c_sample], axis=0)
    rows_pad = -(-c_all.shape[0] // 8) * 8
    ada = _adaln(jnp.pad(c_all, ((0, rows_pad - c_all.shape[0]), (0, 0))), w_ada, b_ada)
    ada = ada.reshape(depth, rows_pad, 6, d)
    tb, lut = _bias_tables(rel_bias)
    far_hi, far_lo = _split2(rel_bias[N_BUCKETS - 1])
    bias_cols = jnp.zeros((NSA_HEADS, NSA_HEAD_DIM), BF16).at[:, 0].set(far_hi).at[:, 1].set(far_lo)

    xp = x_prompt
    xs = x_sample.reshape(1, db * nq, d)
    kvc_p, kvc_s, kvs_p, kvs_s, kvw_p, kvw_s, cv_p, cv_s, ss_p, ss_s = ([] for _ in range(10))
    for i in range(depth):
        mod_p = [ada[i, :bp, k][:, None, :] for k in range(6)]
        mod_s = [jnp.repeat(ada[i, bp:bp + db, k], nq, axis=0)[None] for k in range(6)]
        gains = norm_gains[i]
        l = i // 2
        if i % 2 == 0:
            w_in = jnp.concatenate([nsa_w_in[l][:, :NSA_Q_W] * (NSA_HEAD_DIM ** -0.5), nsa_w_in[l][:, NSA_Q_W:]], axis=1)
            w_in = jnp.pad(w_in, ((0, 0), (0, -w_in.shape[1] % LANES))).astype(BF16)
            cmp_w = _compress_weights(nsa_cmp_pe[l], nsa_cmp_w1[l], nsa_cmp_b1[l], nsa_cmp_w2[l])
            w_out = nsa_w_out[l].astype(BF16)
            xp, a, bq, cq = _nsa_prompt(xp, mod_p, gains, w_in, cmp_w, w_out, tb, lut, bias_cols)
            kvc_p.append(a); kvs_p.append(bq); kvw_p.append(cq)
            xs, a, bq, cq = _nsa_sample(xs, mod_s, gains, w_in, cmp_w, w_out, lut, rel_bias, bias_cols, cache_kv_cmp[l],
                                        cache_kv_slc[l], cache_kv_win[l], page_table, db, nq)
            kvc_s.append(a); kvs_s.append(bq); kvw_s.append(cq)
        else:
            w_in = jnp.pad(gdn_w_in[l], ((0, 0), (0, -gdn_w_in.shape[2] % (5 * MXU_WIDTH)))).astype(BF16)
            gdn_w = (w_in, gdn_conv_w[l], gdn_a_log[l], gdn_dt_bias[l], gdn_norm_w[l], gdn_w_out[l].astype(BF16))
            xp, a, bq = _gdn_prompt(xp, mod_p, gains, *gdn_w)
            cv_p.append(a); ss_p.append(bq)
            xs, a, bq = _gdn_sample(xs, mod_s, gains, *gdn_w, state_conv[l], state_ssm[l], db, nq)
            cv_s.append(a); ss_s.append(bq)
        w_ffn = (w_ffn_in[i].astype(BF16), w_ffn_out[i].astype(BF16))
        xp = _ffn(xp, mod_p, gains, *w_ffn)
        xs = _ffn(xs, mod_s, gains, *w_ffn)
    return (xp, xs.reshape(db, nq, d), jnp.stack(kvc_p), jnp.stack(kvc_s), jnp.stack(kvs_p), jnp.stack(kvs_s),
            jnp.stack(kvw_p), jnp.stack(kvw_s), jnp.stack(cv_p), jnp.stack(cv_s), jnp.stack(ss_p), jnp.stack(ss_s))
```
